```python
import math
import jax, jax.numpy as jnp
from jax import lax
import numpy as np


D_MODEL = 4096
BATCH = 1
SEQ = 8192
DEPTH = 2
DEC_BATCH = 4
DEC_SEQ = 4096
PAST_LEN = 128

N_META = 16
GRID_W = 64
Q_BLOCK = 128
HEAD_DIM = 128
ROPE_THETA = 10000.0
NORM_EPS = 1e-6
A_HEADS = 16
A_KV_HEADS = 4
A_WIDTH = A_HEADS * HEAD_DIM
A_KV_WIDTH = A_KV_HEADS * HEAD_DIM
B_HEADS = 8
B_Q_RANK = 1024
B_KV_RANK = 512
B_NOPE = 128
B_ROPE = 64
B_VDIM = 128
B_WIDTH = B_HEADS * B_VDIM
C_HEADS = 8
C_QK_DIM = 64
C_VDIM = 2 * C_QK_DIM
C_QK_WIDTH = C_HEADS * 2 * C_QK_DIM
C_WIDTH = C_HEADS * C_VDIM
MIX_WIDTH = A_WIDTH + B_WIDTH + C_WIDTH
IN_SPLITS = (A_WIDTH, A_KV_WIDTH, A_KV_WIDTH, A_WIDTH,
             B_Q_RANK, B_KV_RANK, B_ROPE, B_WIDTH,
             C_QK_WIDTH, C_QK_WIDTH, C_WIDTH, C_WIDTH)
IN_WIDTH = 11840

kernel_name = 'hymba_hybrid_encoder'


def rms_norm(x, g):
    xf = x.astype(jnp.float32)
    y = xf * lax.rsqrt(jnp.mean(xf * xf, axis=-1, keepdims=True) + NORM_EPS)
    return (y * g.astype(jnp.float32)).astype(x.dtype)


def rope_angles(pos_f, n_freq):
    inv = ROPE_THETA ** (-jnp.arange(n_freq, dtype=jnp.float32) / n_freq)
    return pos_f[:, None] * inv[None, :]


def apply_rotary(x, ang):
    cos = jnp.cos(ang)[None, :, None, :].astype(x.dtype)
    sin = jnp.sin(ang)[None, :, None, :].astype(x.dtype)
    x1, x2 = jnp.split(x, 2, axis=-1)
    return jnp.concatenate([x1 * cos - x2 * sin, x2 * cos + x1 * sin], axis=-1)


def sweep_queries(block_fn, q, pos):
    bsz, L = q.shape[:2]
    n_tok = L - N_META
    n_blk = n_tok // Q_BLOCK
    meta_out = block_fn(q[:, :N_META], pos[:N_META])
    qb = jnp.moveaxis(q[:, N_META:].reshape((bsz, n_blk, Q_BLOCK) + q.shape[2:]), 1, 0)
    pb = pos[N_META:].reshape(n_blk, Q_BLOCK)
    outs = lax.map(lambda a: block_fn(a[0], a[1]), (qb, pb))
    outs = jnp.moveaxis(outs, 0, 1).reshape((bsz, n_tok) + outs.shape[3:])
    return jnp.concatenate([meta_out, outs], axis=1)


def mixer_a(q, k, v, g_qn, g_kn, ang, pos):
    bsz, L = q.shape[:2]
    q = apply_rotary(rms_norm(q.reshape(bsz, L, A_HEADS, HEAD_DIM), g_qn), ang)
    k = apply_rotary(rms_norm(k.reshape(bsz, L, A_KV_HEADS, HEAD_DIM), g_kn), ang)
    v = v.reshape(bsz, L, A_KV_HEADS, HEAD_DIM)
    group = A_HEADS // A_KV_HEADS
    scale = HEAD_DIM ** -0.5

    def block(qb, qpos):
        nq = qb.shape[1]
        qb = qb.reshape(bsz, nq, A_KV_HEADS, group, HEAD_DIM)
        s = jnp.einsum('bqkgd,bskd->bkgqs', qb, k).astype(jnp.float32) * scale
        p = jax.nn.softmax(s, axis=-1).astype(v.dtype)
        o = jnp.einsum('bkgqs,bskd->bqkgd', p, v)
        return o.reshape(bsz, nq, A_WIDTH)

    return sweep_queries(block, q, pos)


def mixer_b(c_q, c_kv, k_pe, g_q_a, w_q_b, g_kv_a, w_kv_b, ang, pos):
    bsz, L = c_q.shape[:2]
    q = (rms_norm(c_q, g_q_a) @ w_q_b).reshape(bsz, L, B_HEADS, B_NOPE + B_ROPE)
    q = jnp.concatenate([q[..., :B_NOPE], apply_rotary(q[..., B_NOPE:], ang)], axis=-1)
    kv = (rms_norm(c_kv, g_kv_a) @ w_kv_b).reshape(bsz, L, B_HEADS, B_NOPE + B_VDIM)
    k_nope, v = kv[..., :B_NOPE], kv[..., B_NOPE:]
    k_pe = apply_rotary(k_pe[:, :, None, :], ang)
    k = jnp.concatenate([k_nope, jnp.broadcast_to(k_pe, (bsz, L, B_HEADS, B_ROPE))], axis=-1)
    scale = (B_NOPE + B_ROPE) ** -0.5

    def block(qb, qpos):
        nq = qb.shape[1]
        s = jnp.einsum('bqhd,bshd->bhqs', qb, k).astype(jnp.float32) * scale
        p = jax.nn.softmax(s, axis=-1).astype(v.dtype)
        o = jnp.einsum('bhqs,bshd->bqhd', p, v)
        return o.reshape(bsz, nq, B_WIDTH)

    return sweep_queries(block, q, pos)


def mixer_c(q, k, v, lam_q1, lam_k1, lam_q2, lam_k2, g_sub, lambda_init, pos):
    bsz, L = q.shape[:2]
    q = q.reshape(bsz, L, C_HEADS, 2, C_QK_DIM)
    k = k.reshape(bsz, L, C_HEADS, 2, C_QK_DIM)
    v = v.reshape(bsz, L, C_HEADS, C_VDIM)
    f32 = jnp.float32
    lam = (jnp.exp(jnp.sum(lam_q1.astype(f32) * lam_k1.astype(f32)))
           - jnp.exp(jnp.sum(lam_q2.astype(f32) * lam_k2.astype(f32))) + lambda_init)
    slopes = 2.0 ** (-(jnp.arange(C_HEADS, dtype=f32) + 1.0) * 8.0 / C_HEADS)
    pos_f = pos.astype(f32)
    scale = C_QK_DIM ** -0.5

    def block(qb, qpos):
        nq = qb.shape[1]
        s = jnp.einsum('bqhcd,bshcd->bchqs', qb, k).astype(f32) * scale
        dist = jnp.abs(qpos.astype(f32)[:, None] - pos_f[None, :])
        s = s - slopes[None, None, :, None, None] * dist[None, None, None]
        p = jax.nn.softmax(s, axis=-1)
        a = (p[:, 0] - lam * p[:, 1]).astype(v.dtype)
        o = jnp.einsum('bhqs,bshd->bqhd', a, v)
        o = rms_norm(o, g_sub) * (1.0 - lambda_init)
        return o.reshape(bsz, nq, C_WIDTH)

    return sweep_queries(block, q, pos)


def encode(x, meta, g_attn, w_in, g_qn, g_kn, g_q_a, w_q_b, g_kv_a, w_kv_b,
           lam_q1, lam_k1, lam_q2, lam_k2, g_sub, w_out, g_final):
    bsz, n_tok, _ = x.shape
    rows = n_tok // GRID_W
    L = N_META + n_tok
    h = jnp.concatenate([jnp.broadcast_to(meta[None].astype(x.dtype), (bsz, N_META, D_MODEL)), x], axis=1)
    pos = jnp.arange(L, dtype=jnp.int32)
    zeros_meta = jnp.zeros((N_META,), jnp.float32)
    row_f = jnp.concatenate([zeros_meta, jnp.repeat(jnp.arange(rows, dtype=jnp.float32), GRID_W)])
    col_f = jnp.concatenate([zeros_meta, jnp.tile(jnp.arange(GRID_W, dtype=jnp.float32), rows)])
    ang_a = jnp.concatenate([rope_angles(row_f, HEAD_DIM // 4), rope_angles(col_f, HEAD_DIM // 4)], axis=-1)
    ang_b = rope_angles(pos.astype(jnp.float32), B_ROPE // 2)
    split_at = np.cumsum(IN_SPLITS)[:-1].tolist()
    for l in range(DEPTH):
        lambda_init = 0.8 - 0.6 * math.exp(-0.3 * l)
        u = rms_norm(h, g_attn[l])
        proj = u @ w_in[l]
        aq, ak, av, ag, bq, bkv, bpe, bg, cq, ck, cv, cg = jnp.split(proj, split_at, axis=-1)
        oa = mixer_a(aq, ak, av, g_qn[l], g_kn[l], ang_a, pos) * jax.nn.silu(ag)
        ob = mixer_b(bq, bkv, bpe, g_q_a[l], w_q_b[l], g_kv_a[l], w_kv_b[l], ang_b, pos) * jax.nn.silu(bg)
        oc = mixer_c(cq, ck, cv, lam_q1[l], lam_k1[l], lam_q2[l], lam_k2[l], g_sub[l], lambda_init, pos) * jax.nn.silu(cg)
        h = h + jnp.concatenate([oa, ob, oc], axis=-1) @ w_out[l]
    return rms_norm(h[:, N_META:], g_final)


def setup_inputs(seed: int = 0) -> dict:
    key = jax.random.key(seed)
    ks = jax.random.split(key, 20)
    f32 = jnp.float32

    def nrm(k, shape, scale):
        return jax.random.normal(k, shape, f32) * scale

    def gain(k, shape):
        return 1.0 + 0.02 * jax.random.normal(k, shape, f32)

    return {
        'x_prompt': nrm(ks[0], (BATCH, SEQ, D_MODEL), 1.0),
        'x_sample': nrm(ks[1], (DEC_BATCH, DEC_SEQ, D_MODEL), 1.0),
        'meta': nrm(ks[2], (N_META, D_MODEL), 1.0),
        'g_attn': gain(ks[3], (DEPTH, D_MODEL)),
        'w_in': nrm(ks[4], (DEPTH, D_MODEL, IN_WIDTH), D_MODEL ** -0.5),
        'g_qn': gain(ks[5], (DEPTH, HEAD_DIM)),
        'g_kn': gain(ks[6], (DEPTH, HEAD_DIM)),
        'g_q_a': gain(ks[7], (DEPTH, B_Q_RANK)),
        'w_q_b': nrm(ks[8], (DEPTH, B_Q_RANK, B_HEADS * (B_NOPE + B_ROPE)), B_Q_RANK ** -0.5),
        'g_kv_a': gain(ks[9], (DEPTH, B_KV_RANK)),
        'w_kv_b': nrm(ks[10], (DEPTH, B_KV_RANK, B_HEADS * (B_NOPE + B_VDIM)), B_KV_RANK ** -0.5),
        'lam_q1': nrm(ks[11], (DEPTH, C_QK_DIM), 0.1),
        'lam_k1': nrm(ks[12], (DEPTH, C_QK_DIM), 0.1),
        'lam_q2': nrm(ks[13], (DEPTH, C_QK_DIM), 0.1),
        'lam_k2': nrm(ks[14], (DEPTH, C_QK_DIM), 0.1),
        'g_sub': gain(ks[15], (DEPTH, C_VDIM)),
        'w_out': nrm(ks[16], (DEPTH, MIX_WIDTH, D_MODEL), MIX_WIDTH ** -0.5),
        'g_final': gain(ks[17], (D_MODEL,)),
    }


def reference(x_prompt, x_sample, meta, g_attn, w_in, g_qn, g_kn, g_q_a, w_q_b, g_kv_a, w_kv_b,
              lam_q1, lam_k1, lam_q2, lam_k2, g_sub, w_out, g_final):
    y_prompt = encode(x_prompt, meta, g_attn, w_in, g_qn, g_kn, g_q_a, w_q_b, g_kv_a, w_kv_b,
                      lam_q1, lam_k1, lam_q2, lam_k2, g_sub, w_out, g_final)
    y_sample = encode(x_sample, meta, g_attn, w_in, g_qn, g_kn, g_q_a, w_q_b, g_kv_a, w_kv_b,
                      lam_q1, lam_k1, lam_q2, lam_k2, g_sub, w_out, g_final)
    return (y_prompt, y_sample)
```

```python
import functools
import math

import jax
import jax.numpy as jnp
from jax import lax
from jax.experimental import pallas as pl
from jax.experimental.pallas import tpu as pltpu

D_MODEL = 4096
DEPTH = 2
N_META = 16
GRID_W = 64
HEAD_DIM = 128
ROPE_THETA = 10000.0
NORM_EPS = 1e-6
A_HEADS = 16
A_KV_HEADS = 4
A_GROUP = A_HEADS // A_KV_HEADS
A_WIDTH = A_HEADS * HEAD_DIM
A_KV_WIDTH = A_KV_HEADS * HEAD_DIM
B_HEADS = 8
B_Q_RANK = 1024
B_KV_RANK = 512
B_NOPE = 128
B_ROPE = 64
B_VDIM = 128
B_WIDTH = B_HEADS * B_VDIM
C_HEADS = 8
C_QK_DIM = 64
C_VDIM = 2 * C_QK_DIM
C_QK_WIDTH = C_HEADS * 2 * C_QK_DIM
C_WIDTH = C_HEADS * C_VDIM
IN_SPLITS = (A_WIDTH, A_KV_WIDTH, A_KV_WIDTH, A_WIDTH,
             B_Q_RANK, B_KV_RANK, B_ROPE, B_WIDTH,
             C_QK_WIDTH, C_QK_WIDTH, C_WIDTH, C_WIDTH)

LANES = 128
TAIL = LANES
KV_CHUNK = 512
NEG_BIG = -1e30
BF16 = jnp.bfloat16
F32 = jnp.float32
VMEM_LIMIT = 56 * 1024 * 1024

_NT = (((1,), (1,)), ((), ()))


def _cparams(n_axes):
    return pltpu.CompilerParams(dimension_semantics=("arbitrary",) * n_axes,
                                vmem_limit_bytes=VMEM_LIMIT)


def _row_tile(rows):
    for t in (512, 640, 256, 128):
        if rows % t == 0:
            return t
    raise ValueError(f"no row tile for {rows}")


def _rmsnorm_body(x_ref, g_ref, o_ref):
    x = x_ref[...]
    ms = jnp.mean(x * x, axis=-1, keepdims=True)
    o_ref[...] = ((x * lax.rsqrt(ms + NORM_EPS)) * g_ref[...]).astype(o_ref.dtype)


def _rmsnorm_rows(h2d, g, out_dtype):
    rows, d = h2d.shape
    tm = 256 if rows % 256 == 0 else 128
    return pl.pallas_call(
        _rmsnorm_body,
        grid=(rows // tm,),
        in_specs=[pl.BlockSpec((tm, d), lambda i: (i, 0)),
                  pl.BlockSpec((1, d), lambda i: (0, 0))],
        out_specs=pl.BlockSpec((tm, d), lambda i: (i, 0)),
        out_shape=jax.ShapeDtypeStruct((rows, d), out_dtype),
        compiler_params=_cparams(1),
        name="rmsnorm",
    )(h2d, g.reshape(1, d))


def _final_norm(h3d, g, n_tok):
    bsz, _, d = h3d.shape
    tm = 256
    return pl.pallas_call(
        _rmsnorm_body,
        grid=(bsz, n_tok // tm),
        in_specs=[pl.BlockSpec((None, tm, d), lambda b, i: (b, i, 0)),
                  pl.BlockSpec((1, d), lambda b, i: (0, 0))],
        out_specs=pl.BlockSpec((None, tm, d), lambda b, i: (b, i, 0)),
        out_shape=jax.ShapeDtypeStruct((bsz, n_tok, d), F32),
        compiler_params=_cparams(2),
        name="final_norm",
    )(h3d, g.reshape(1, d))


def _rope_pairs_64(y, cos, sin_signed):
    return y * cos + pltpu.roll(y, 64, 1) * sin_signed


def _rope_pairs_32(x, cos, sin_lo, sin_hi):
    return x * cos + pltpu.roll(x, 96, 1) * sin_lo + pltpu.roll(x, 32, 1) * sin_hi


def _proj_aqk_body(x_ref, w_ref, g_ref, cos_ref, sin_ref, o_ref):
    acc = jnp.dot(x_ref[...], w_ref[...], preferred_element_type=F32)
    cos = cos_ref[...]
    sin = sin_ref[...]
    for h in range(acc.shape[1] // HEAD_DIM):
        sl = slice(h * HEAD_DIM, (h + 1) * HEAD_DIM)
        a = acc[:, sl]
        ms = jnp.mean(a * a, axis=-1, keepdims=True)
        y = (a * lax.rsqrt(ms + NORM_EPS)) * g_ref[:, sl]
        o_ref[:, sl] = _rope_pairs_64(y, cos, sin).astype(o_ref.dtype)


def _proj_plain_body(x_ref, w_ref, o_ref):
    o_ref[...] = jnp.dot(x_ref[...], w_ref[...], preferred_element_type=F32).astype(o_ref.dtype)


def _proj_gate_body(x_ref, w_ref, o_ref):
    acc = jnp.dot(x_ref[...], w_ref[...], preferred_element_type=F32)
    o_ref[...] = acc * (1.0 / (1.0 + jnp.exp(-acc)))


def _proj_cqk_body(x_ref, w_ref, mult_ref, o_ref):
    acc = jnp.dot(x_ref[...], w_ref[...], preferred_element_type=F32) * mult_ref[...]
    lane = lax.broadcasted_iota(jnp.int32, (acc.shape[0], LANES), 1)
    low = lane < C_QK_DIM
    for t in range(acc.shape[1] // LANES):
        a = acc[:, t * LANES:(t + 1) * LANES]
        o_ref[:, (2 * t) * LANES:(2 * t + 1) * LANES] = jnp.where(low, a, 0.0).astype(o_ref.dtype)
        o_ref[:, (2 * t + 1) * LANES:(2 * t + 2) * LANES] = (
            jnp.where(low, pltpu.roll(a, 64, 1), 0.0).astype(o_ref.dtype))


def _proj_bpe_body(x_ref, w_ref, cos_ref, slo_ref, shi_ref, o_ref):
    acc = jnp.dot(x_ref[...], w_ref[...], preferred_element_type=F32)
    o_ref[...] = _rope_pairs_32(acc, cos_ref[...], slo_ref[...], shi_ref[...]).astype(o_ref.dtype)


def _proj(body, u, w, extras, extra_specs, n_out, out_dtype, tn, out_tn=None, name="proj"):
    rows, k = u.shape
    n = w.shape[1]
    tm = _row_tile(rows)
    out_tn = tn if out_tn is None else out_tn
    return pl.pallas_call(
        body,
        grid=(rows // tm, n // tn),
        in_specs=[pl.BlockSpec((tm, k), lambda i, j: (i, 0)),
                  pl.BlockSpec((k, tn), lambda i, j: (0, j))] + extra_specs(tm, tn),
        out_specs=pl.BlockSpec((tm, out_tn), lambda i, j: (i, j)),
        out_shape=jax.ShapeDtypeStruct((rows, n_out), out_dtype),
        compiler_params=_cparams(2),
        name=name,
    )(u, w, *extras)


def _row_table_spec(tm, tn):
    return pl.BlockSpec((tm, LANES), lambda i, j: (i, 0))


def _col_vec_spec(tm, tn):
    return pl.BlockSpec((1, tn), lambda i, j: (0, j))


def _bq_up_body(c_ref, g_ref, w_ref, cos_ref, slo_ref, shi_ref, o_ref):
    c = c_ref[...]
    ms = jnp.mean(c * c, axis=-1, keepdims=True)
    u = ((c * lax.rsqrt(ms + NORM_EPS)) * g_ref[...]).astype(BF16)
    acc = jnp.dot(u, w_ref[...], preferred_element_type=F32)
    cos, slo, shi = cos_ref[...], slo_ref[...], shi_ref[...]
    for h in range(B_HEADS):
        base = h * 2 * LANES
        o_ref[:, base:base + LANES] = acc[:, base:base + LANES].astype(o_ref.dtype)
        x = acc[:, base + LANES:base + 2 * LANES]
        o_ref[:, base + LANES:base + 2 * LANES] = _rope_pairs_32(x, cos, slo, shi).astype(o_ref.dtype)


def _bkv_up_body(c_ref, g_ref, w_ref, kpe_ref, k_ref, v_ref):
    c = c_ref[...]
    ms = jnp.mean(c * c, axis=-1, keepdims=True)
    u = ((c * lax.rsqrt(ms + NORM_EPS)) * g_ref[...]).astype(BF16)
    acc = jnp.dot(u, w_ref[...], preferred_element_type=F32)
    kpe = kpe_ref[...]
    for h in range(B_HEADS):
        base = h * 2 * LANES
        k_ref[:, base:base + LANES] = acc[:, h * LANES:(h + 1) * LANES].astype(k_ref.dtype)
        k_ref[:, base + LANES:base + 2 * LANES] = kpe
    v_ref[...] = acc[:, B_HEADS * LANES:].astype(v_ref.dtype)


def _bq_up(cb, g, w, cos, slo, shi):
    rows = cb.shape[0]
    tm = _row_tile(rows)
    n = w.shape[1]
    tab = pl.BlockSpec((tm, LANES), lambda i: (i, 0))
    return pl.pallas_call(
        _bq_up_body,
        grid=(rows // tm,),
        in_specs=[pl.BlockSpec((tm, B_Q_RANK), lambda i: (i, 0)),
                  pl.BlockSpec((1, B_Q_RANK), lambda i: (0, 0)),
                  pl.BlockSpec((B_Q_RANK, n), lambda i: (0, 0)),
                  tab, tab, tab],
        out_specs=pl.BlockSpec((tm, n), lambda i: (i, 0)),
        out_shape=jax.ShapeDtypeStruct((rows, n), BF16),
        compiler_params=_cparams(1),
        name="mla_q_up",
    )(cb, g.reshape(1, -1), w, cos, slo, shi)


def _bkv_up(cb, g, w, kpe):
    rows = cb.shape[0]
    tm = _row_tile(rows)
    n = w.shape[1]
    return pl.pallas_call(
        _bkv_up_body,
        grid=(rows // tm,),
        in_specs=[pl.BlockSpec((tm, B_KV_RANK), lambda i: (i, B_Q_RANK // B_KV_RANK)),
                  pl.BlockSpec((1, B_KV_RANK), lambda i: (0, 0)),
                  pl.BlockSpec((B_KV_RANK, n), lambda i: (0, 0)),
                  pl.BlockSpec((tm, LANES), lambda i: (i, 0))],
        out_specs=[pl.BlockSpec((tm, B_HEADS * 2 * LANES), lambda i: (i, 0)),
                   pl.BlockSpec((tm, B_HEADS * B_VDIM), lambda i: (i, 0))],
        out_shape=[jax.ShapeDtypeStruct((rows, B_HEADS * 2 * LANES), BF16),
                   jax.ShapeDtypeStruct((rows, B_HEADS * B_VDIM), BF16)],
        compiler_params=_cparams(1),
        name="mla_kv_up",
    )(cb, g.reshape(1, -1), w, kpe)


def _online_softmax_step(s, v, m_ref, l_ref, acc_ref):
    m_prev = m_ref[...]
    m_new = jnp.maximum(m_prev, jnp.max(s, axis=1, keepdims=True))
    alpha = jnp.exp(m_prev - m_new)
    p = jnp.exp(s - m_new)
    l_ref[...] = alpha * l_ref[...] + jnp.sum(p, axis=1, keepdims=True)
    acc_ref[...] = alpha * acc_ref[...] + jnp.dot(p.astype(BF16), v, preferred_element_type=F32)
    m_ref[...] = m_new


def _attn_ab_body(q_ref, k_ref, v_ref, gate_ref, *rest, group, tq, dk, n_tok, scale):
    if len(rest) == 6:
        _, o_ref, qs_ref, m_ref, l_ref, acc_ref = rest
    else:
        o_ref, qs_ref, m_ref, l_ref, acc_ref = rest
    for g in range(group):
        qs_ref[g * tq:(g + 1) * tq, :] = q_ref[:, g * dk:(g + 1) * dk]
    m_ref[...] = jnp.full(m_ref.shape, NEG_BIG, F32)
    l_ref[...] = jnp.zeros(l_ref.shape, F32)
    acc_ref[...] = jnp.zeros(acc_ref.shape, F32)

    def chunk_step(c, carry):
        off = pl.multiple_of(c * KV_CHUNK, KV_CHUNK)
        k = k_ref[pl.ds(off, KV_CHUNK), :]
        v = v_ref[pl.ds(off, KV_CHUNK), :]
        s = lax.dot_general(qs_ref[...], k, _NT, preferred_element_type=F32) * scale
        _online_softmax_step(s, v, m_ref, l_ref, acc_ref)
        return carry

    lax.fori_loop(0, n_tok // KV_CHUNK, chunk_step, 0)

    k = k_ref[n_tok:n_tok + TAIL, :]
    v = v_ref[n_tok:n_tok + TAIL, :]
    s = lax.dot_general(qs_ref[...], k, _NT, preferred_element_type=F32) * scale
    col = lax.broadcasted_iota(jnp.int32, s.shape, 1)
    s = jnp.where(col < N_META, s, NEG_BIG)
    _online_softmax_step(s, v, m_ref, l_ref, acc_ref)

    inv_l = 1.0 / l_ref[...]
    for g in range(group):
        rows = slice(g * tq, (g + 1) * tq)
        cols = slice(g * HEAD_DIM, (g + 1) * HEAD_DIM)
        o = acc_ref[rows, :] * inv_l[rows, :]
        o_ref[:, cols] = (o * gate_ref[:, cols]).astype(o_ref.dtype)


def _attn_c_body(q_ref, k_ref, v_ref, gate_ref, lam_ref, slope_ref, gsub_ref, *rest,
                 tq, n_tok, q_is_meta, lambda_init):
    if len(rest) == 5:
        _, o_ref, m_ref, l_ref, acc_ref = rest
    else:
        o_ref, m_ref, l_ref, acc_ref = rest
    m_ref[...] = jnp.full(m_ref.shape, NEG_BIG, F32)
    l_ref[...] = jnp.zeros(l_ref.shape, F32)
    acc_ref[...] = jnp.zeros(acc_ref.shape, F32)
    slope = slope_ref[:, :1]
    row = lax.broadcasted_iota(jnp.int32, (tq, 1), 0)
    if q_is_meta:
        qpos = row.astype(F32)
    else:
        qpos = (row + (N_META + pl.program_id(2) * tq)).astype(F32)

    def two_maps(k, v, kpos, n_valid):
        bias = slope * jnp.abs(qpos - kpos)
        for c in range(2):
            s = lax.dot_general(q_ref[:, c * LANES:(c + 1) * LANES], k[:, c * LANES:(c + 1) * LANES],
                                _NT, preferred_element_type=F32) - bias
            if n_valid is not None:
                col = lax.broadcasted_iota(jnp.int32, s.shape, 1)
                s = jnp.where(col < n_valid, s, NEG_BIG)
            _online_softmax_step(s, v, m_ref.at[c], l_ref.at[c], acc_ref.at[c])

    def chunk_step(c, carry):
        off = pl.multiple_of(c * KV_CHUNK, KV_CHUNK)
        col = lax.broadcasted_iota(jnp.int32, (1, KV_CHUNK), 1)
        kpos = (col + (N_META + off)).astype(F32)
        two_maps(k_ref[pl.ds(off, KV_CHUNK), :], v_ref[pl.ds(off, KV_CHUNK), :], kpos, None)
        return carry

    lax.fori_loop(0, n_tok // KV_CHUNK, chunk_step, 0)

    kpos = lax.broadcasted_iota(jnp.int32, (1, TAIL), 1).astype(F32)
    two_maps(k_ref[n_tok:n_tok + TAIL, :], v_ref[n_tok:n_tok + TAIL, :], kpos, N_META)

    lv = lam_ref[...]
    lam = (jnp.exp(jnp.sum(lv[0:1] * lv[1:2], axis=1, keepdims=True))
           - jnp.exp(jnp.sum(lv[2:3] * lv[3:4], axis=1, keepdims=True)) + lambda_init)
    o = acc_ref[0] * (1.0 / l_ref[0]) - lam * (acc_ref[1] * (1.0 / l_ref[1]))
    ms = jnp.mean(o * o, axis=-1, keepdims=True)
    o = ((o * lax.rsqrt(ms + NORM_EPS)) * gsub_ref[...]) * (1.0 - lambda_init)
    o_ref[...] = (o * gate_ref[...]).astype(o_ref.dtype)


def _attention(body, q, q_col0, q_w, k, k_col0, k_w, v, v_col0, gate, gate_col0, extras, extra_specs,
               out_w_total, heads, group, tq, n_tok, scratch, name):
    bsz, lp, _ = q.shape
    out_w = group * HEAD_DIM

    def call(tq_, n_qblk, qblk0, prev):
        in_specs = [
            pl.BlockSpec((None, tq_, q_w), lambda b, h, i: (b, qblk0 + i, q_col0 + h)),
            pl.BlockSpec((None, lp, k_w), lambda b, h, i: (b, 0, k_col0 + h)),
            pl.BlockSpec((None, lp, HEAD_DIM), lambda b, h, i: (b, 0, v_col0 + h)),
            pl.BlockSpec((None, tq_, out_w), lambda b, h, i: (b, qblk0 + i, gate_col0 + h)),
        ] + extra_specs
        args = [q, k, v, gate] + extras
        aliases = {}
        if prev is not None:
            in_specs.append(pl.BlockSpec(memory_space=pl.ANY))
            args.append(prev)
            aliases = {len(args) - 1: 0}
        return pl.pallas_call(
            body(tq_, prev is not None),
            grid=(bsz, heads, n_qblk),
            in_specs=in_specs,
            out_specs=pl.BlockSpec((None, tq_, out_w), lambda b, h, i: (b, qblk0 + i, h)),
            out_shape=jax.ShapeDtypeStruct((bsz, lp, out_w_total), BF16),
            scratch_shapes=scratch(tq_),
            input_output_aliases=aliases,
            compiler_params=_cparams(3),
            name=name + ("_meta" if prev is not None else ""),
        )(*args)

    main = call(tq, n_tok // tq, 0, None)
    return call(TAIL, 1, n_tok // TAIL, main)


def _attn_a(qk, vplain, gates, n_tok):
    tq = 256

    def body(tq_, is_meta):
        return functools.partial(_attn_ab_body, group=A_GROUP, tq=tq_, dk=HEAD_DIM, n_tok=n_tok,
                                 scale=HEAD_DIM ** -0.5)

    def scratch(tq_):
        m = A_GROUP * tq_
        return [pltpu.VMEM((m, HEAD_DIM), BF16), pltpu.VMEM((m, 1), F32), pltpu.VMEM((m, 1), F32),
                pltpu.VMEM((m, HEAD_DIM), F32)]

    return _attention(body, qk, 0, A_GROUP * HEAD_DIM, qk, A_WIDTH // HEAD_DIM, HEAD_DIM, vplain, 0,
                      gates, 0, [], [], A_WIDTH, A_KV_HEADS, A_GROUP, tq, n_tok, scratch, "attn_gqa")


def _attn_b(qb, kb, vb, gates, n_tok):
    tq = 512
    dk = 2 * LANES

    def body(tq_, is_meta):
        return functools.partial(_attn_ab_body, group=1, tq=tq_, dk=dk, n_tok=n_tok,
                                 scale=(B_NOPE + B_ROPE) ** -0.5)

    def scratch(tq_):
        return [pltpu.VMEM((tq_, dk), BF16), pltpu.VMEM((tq_, 1), F32), pltpu.VMEM((tq_, 1), F32),
                pltpu.VMEM((tq_, HEAD_DIM), F32)]

    return _attention(body, qb, 0, dk, kb, 0, dk, vb, 0, gates, A_WIDTH // HEAD_DIM, [], [],
                      B_WIDTH, B_HEADS, 1, tq, n_tok, scratch, "attn_mla")


def _attn_c(cqk, vplain, gates, lamvec, slopes, g_sub, lambda_init, n_tok):
    tq = 512
    dk = 2 * LANES

    def body(tq_, is_meta):
        return functools.partial(_attn_c_body, tq=tq_, n_tok=n_tok, q_is_meta=is_meta,
                                 lambda_init=lambda_init)

    def scratch(tq_):
        return [pltpu.VMEM((2, tq_, 1), F32), pltpu.VMEM((2, tq_, 1), F32),
                pltpu.VMEM((2, tq_, HEAD_DIM), F32)]

    extras = [lamvec, slopes, g_sub.reshape(1, C_VDIM)]
    extra_specs = [pl.BlockSpec((4, C_QK_DIM), lambda b, h, i: (0, 0)),
                   pl.BlockSpec((None, 1, LANES), lambda b, h, i: (h, 0, 0)),
                   pl.BlockSpec((1, C_VDIM), lambda b, h, i: (0, 0))]
    return _attention(body, cqk, 0, dk, cqk, C_HEADS, dk, vplain, A_KV_HEADS, gates,
                      (A_WIDTH + B_WIDTH) // HEAD_DIM, extras, extra_specs,
                      C_WIDTH, C_HEADS, 1, tq, n_tok, scratch, "attn_diff")


def _out_proj_body(a_ref, b_ref, c_ref, wa_ref, wb_ref, wc_ref, h_ref, o_ref):
    acc = jnp.dot(a_ref[...], wa_ref[...], preferred_element_type=F32)
    acc += jnp.dot(b_ref[...], wb_ref[...], preferred_element_type=F32)
    acc += jnp.dot(c_ref[...], wc_ref[...], preferred_element_type=F32)
    o_ref[...] = h_ref[...] + acc


def _out_proj(oa, ob, oc, w, h2d):
    rows = h2d.shape[0]
    tm = _row_tile(rows)
    tn = 512
    return pl.pallas_call(
        _out_proj_body,
        grid=(rows // tm, D_MODEL // tn),
        in_specs=[pl.BlockSpec((tm, A_WIDTH), lambda i, j: (i, 0)),
                  pl.BlockSpec((tm, B_WIDTH), lambda i, j: (i, 0)),
                  pl.BlockSpec((tm, C_WIDTH), lambda i, j: (i, 0)),
                  pl.BlockSpec((A_WIDTH, tn), lambda i, j: (0, j)),
                  pl.BlockSpec((B_WIDTH, tn), lambda i, j: (A_WIDTH // B_WIDTH, j)),
                  pl.BlockSpec((C_WIDTH, tn), lambda i, j: ((A_WIDTH + B_WIDTH) // C_WIDTH, j)),
                  pl.BlockSpec((tm, tn), lambda i, j: (i, j))],
        out_specs=pl.BlockSpec((tm, tn), lambda i, j: (i, j)),
        out_shape=jax.ShapeDtypeStruct((rows, D_MODEL), F32),
        compiler_params=_cparams(2),
        name="out_proj",
    )(oa, ob, oc, w, w, w, h2d)


def _rope_angles(pos_f, n_freq):
    inv = ROPE_THETA ** (-jnp.arange(n_freq, dtype=F32) / n_freq)
    return pos_f[:, None] * inv[None, :]


def _position_tables(n_tok, bsz):
    rows = n_tok // GRID_W
    z = jnp.zeros((TAIL,), F32)
    row_f = jnp.concatenate([jnp.repeat(jnp.arange(rows, dtype=F32), GRID_W), z])
    col_f = jnp.concatenate([jnp.tile(jnp.arange(GRID_W, dtype=F32), rows), z])
    pos_f = jnp.concatenate([jnp.arange(n_tok, dtype=F32) + N_META,
                             jnp.arange(N_META, dtype=F32), jnp.zeros((TAIL - N_META,), F32)])
    ang_a = jnp.concatenate([_rope_angles(row_f, HEAD_DIM // 4), _rope_angles(col_f, HEAD_DIM // 4)], axis=-1)
    cos_a, sin_a = jnp.cos(ang_a), jnp.sin(ang_a)
    cos_a = jnp.concatenate([cos_a, cos_a], axis=-1)
    sin_a = jnp.concatenate([-sin_a, sin_a], axis=-1)
    ang_b = _rope_angles(pos_f, B_ROPE // 2)
    cos_b, sin_b = jnp.cos(ang_b), jnp.sin(ang_b)
    zb = jnp.zeros_like(cos_b)
    cos_b128 = jnp.concatenate([cos_b, cos_b, zb, zb], axis=-1)
    sin_lo = jnp.concatenate([-sin_b, zb, zb, zb], axis=-1)
    sin_hi = jnp.concatenate([zb, sin_b, zb, zb], axis=-1)
    return tuple(jnp.tile(t, (bsz, 1)) for t in (cos_a, sin_a, cos_b128, sin_lo, sin_hi))


def _prep_weights(w_in, w_q_b, w_kv_b, w_out, g_qn, g_kn):
    bounds = [0]
    for s in IN_SPLITS:
        bounds.append(bounds[-1] + s)
    col = lambda i: slice(bounds[i], bounds[i + 1])
    w = w_in.astype(BF16)
    aq, ak, av, ag, bq, bkv, bpe, bg, cq, ck, cv, cg = (w[:, col(i)] for i in range(12))
    w_aqk = jnp.concatenate([aq, ak], axis=1)
    w_plain = jnp.concatenate([av, cv], axis=1)
    w_gate = jnp.concatenate([ag, bg, cg], axis=1)
    w_cqk = jnp.concatenate([cq, ck], axis=1)
    w_blow = jnp.concatenate([bq, bkv], axis=1)
    w_bpe = jnp.concatenate([bpe, jnp.zeros((D_MODEL, LANES - B_ROPE), BF16)], axis=1)
    g_aqk = jnp.concatenate([jnp.tile(g_qn, A_HEADS), jnp.tile(g_kn, A_KV_HEADS)]).reshape(1, -1)
    wq = w_q_b.astype(BF16).reshape(B_Q_RANK, B_HEADS, B_NOPE + B_ROPE)
    wq = jnp.concatenate([wq, jnp.zeros((B_Q_RANK, B_HEADS, 2 * LANES - B_NOPE - B_ROPE), BF16)], axis=-1)
    wq = wq.reshape(B_Q_RANK, B_HEADS * 2 * LANES)
    wkv = w_kv_b.astype(BF16).reshape(B_KV_RANK, B_HEADS, B_NOPE + B_VDIM)
    wkv = jnp.concatenate([wkv[:, :, :B_NOPE].reshape(B_KV_RANK, -1),
                           wkv[:, :, B_NOPE:].reshape(B_KV_RANK, -1)], axis=1)
    return dict(aqk=w_aqk, plain=w_plain, gate=w_gate, cqk=w_cqk, blow=w_blow, bpe=w_bpe,
                g_aqk=g_aqk, q_up=wq, kv_up=wkv, out=w_out.astype(BF16))


def _encode(x, meta, layers, g_final, slopes, cq_mult):
    bsz, n_tok, _ = x.shape
    lp = n_tok + TAIL
    rows = bsz * lp
    tail = jnp.concatenate([meta.astype(x.dtype), jnp.zeros((TAIL - N_META, D_MODEL), x.dtype)], axis=0)
    h = jnp.concatenate([x, jnp.broadcast_to(tail[None], (bsz, TAIL, D_MODEL))], axis=1).reshape(rows, D_MODEL)
    cos_a, sin_a, cos_b, sin_lo, sin_hi = _position_tables(n_tok, bsz)
    three = lambda a: a.reshape(bsz, lp, a.shape[-1])

    for l, p in enumerate(layers):
        lambda_init = 0.8 - 0.6 * math.exp(-0.3 * l)
        u = _rmsnorm_rows(h, p["g_attn"], BF16)
        aqk = _proj(_proj_aqk_body, u, p["aqk"], [p["g_aqk"], cos_a, sin_a],
                    lambda tm, tn: [_col_vec_spec(tm, tn), _row_table_spec(tm, tn), _row_table_spec(tm, tn)],
                    A_WIDTH + A_KV_WIDTH, BF16, 512, name="proj_gqa_qk")
        plain = _proj(_proj_plain_body, u, p["plain"], [], lambda tm, tn: [],
                      A_KV_WIDTH + C_WIDTH, BF16, 512, name="proj_v")
        gates = _proj(_proj_gate_body, u, p["gate"], [], lambda tm, tn: [],
                      A_WIDTH + B_WIDTH + C_WIDTH, F32, 512, name="proj_gate")
        cqk = _proj(_proj_cqk_body, u, p["cqk"], [cq_mult], lambda tm, tn: [_col_vec_spec(tm, tn)],
                    4 * C_QK_WIDTH, BF16, 512, out_tn=1024, name="proj_diff_qk")
        blow = _proj(_proj_plain_body, u, p["blow"], [], lambda tm, tn: [],
                     B_Q_RANK + B_KV_RANK, F32, 512, name="proj_mla_low")
        kpe = _proj(_proj_bpe_body, u, p["bpe"], [cos_b, sin_lo, sin_hi],
                    lambda tm, tn: [_row_table_spec(tm, tn)] * 3, LANES, BF16, LANES, name="proj_mla_pe")
        qb = _bq_up(blow, p["g_q_a"], p["q_up"], cos_b, sin_lo, sin_hi)
        kb, vb = _bkv_up(blow, p["g_kv_a"], p["kv_up"], kpe)

        gates3 = three(gates)
        oa = _attn_a(three(aqk), three(plain), gates3, n_tok)
        ob = _attn_b(three(qb), three(kb), three(vb), gates3, n_tok)
        oc = _attn_c(three(cqk), three(plain), gates3, p["lamvec"], slopes, p["g_sub"], lambda_init, n_tok)
        h = _out_proj(oa.reshape(rows, -1), ob.reshape(rows, -1), oc.reshape(rows, -1), p["out"], h)

    return _final_norm(h.reshape(bsz, lp, D_MODEL), g_final, n_tok)


def kernel(x_prompt, x_sample, meta, g_attn, w_in, g_qn, g_kn, g_q_a, w_q_b, g_kv_a, w_kv_b,
           lam_q1, lam_k1, lam_q2, lam_k2, g_sub, w_out, g_final):
    layers = []
    for l in range(DEPTH):
        p = _prep_weights(w_in[l], w_q_b[l], w_kv_b[l], w_out[l], g_qn[l], g_kn[l])
        p.update(g_attn=g_attn[l], g_q_a=g_q_a[l], g_kv_a=g_kv_a[l], g_sub=g_sub[l],
                 lamvec=jnp.stack([lam_q1[l], lam_k1[l], lam_q2[l], lam_k2[l]]).astype(F32))
        layers.append(p)
    slopes = 2.0 ** (-(jnp.arange(C_HEADS, dtype=F32) + 1.0) * 8.0 / C_HEADS)
    slopes = jnp.broadcast_to(slopes[:, None, None], (C_HEADS, 1, LANES))
    cq_mult = jnp.concatenate([jnp.full((C_QK_WIDTH,), C_QK_DIM ** -0.5, F32),
                               jnp.ones((C_QK_WIDTH,), F32)]).reshape(1, -1)
    y_prompt = _encode(x_prompt, meta, layers, g_final, slopes, cq_mult)
    y_sample = _encode(x_sample, meta, layers, g_final, slopes, cq_mult)
    return (y_prompt, y_sample)
```

```python
import functools
import math

import jax
import jax.numpy as jnp
from jax import lax
from jax.experimental import pallas as pl
from jax.experimental.pallas import tpu as pltpu

D_MODEL = 4096
DEPTH = 2
N_META = 16
GRID_W = 64
HEAD_DIM = 128
ROPE_THETA = 10000.0
NORM_EPS = 1e-6
A_HEADS = 16
A_KV_HEADS = 4
A_GROUP = A_HEADS // A_KV_HEADS
A_WIDTH = A_HEADS * HEAD_DIM
A_KV_WIDTH = A_KV_HEADS * HEAD_DIM
B_HEADS = 8
B_Q_RANK = 1024
B_KV_RANK = 512
B_NOPE = 128
B_ROPE = 64
B_VDIM = 128
B_WIDTH = B_HEADS * B_VDIM
C_HEADS = 8
C_QK_DIM = 64
C_VDIM = 2 * C_QK_DIM
C_QK_WIDTH = C_HEADS * 2 * C_QK_DIM
C_WIDTH = C_HEADS * C_VDIM
IN_SPLITS = (A_WIDTH, A_KV_WIDTH, A_KV_WIDTH, A_WIDTH,
             B_Q_RANK, B_KV_RANK, B_ROPE, B_WIDTH,
             C_QK_WIDTH, C_QK_WIDTH, C_WIDTH, C_WIDTH)

LANES = 128
TAIL = LANES
KV_CHUNK = 512
Q_SUB = 256
NEG_BIG = -1e30
LOG2E = 1.4426950408889634
BF16 = jnp.bfloat16
F32 = jnp.float32
VMEM_LIMIT = 56 * 1024 * 1024

_NT = (((1,), (1,)), ((), ()))


def _cparams(n_axes):
    return pltpu.CompilerParams(dimension_semantics=("arbitrary",) * n_axes,
                                vmem_limit_bytes=VMEM_LIMIT)


def _row_tile(rows):
    for t in (512, 640, 256, 128):
        if rows % t == 0:
            return t
    raise ValueError(f"no row tile for {rows}")


def _rmsnorm_body(x_ref, g_ref, o_ref):
    x = x_ref[...]
    ms = jnp.mean(x * x, axis=-1, keepdims=True)
    o_ref[...] = ((x * lax.rsqrt(ms + NORM_EPS)) * g_ref[...]).astype(o_ref.dtype)


def _rmsnorm_rows(h2d, g, out_dtype):
    rows, d = h2d.shape
    tm = 256 if rows % 256 == 0 else 128
    return pl.pallas_call(
        _rmsnorm_body,
        grid=(rows // tm,),
        in_specs=[pl.BlockSpec((tm, d), lambda i: (i, 0)),
                  pl.BlockSpec((1, d), lambda i: (0, 0))],
        out_specs=pl.BlockSpec((tm, d), lambda i: (i, 0)),
        out_shape=jax.ShapeDtypeStruct((rows, d), out_dtype),
        compiler_params=_cparams(1),
        name="rmsnorm",
    )(h2d, g.reshape(1, d))


def _final_norm(h3d, g, n_tok):
    bsz, _, d = h3d.shape
    tm = 256
    return pl.pallas_call(
        _rmsnorm_body,
        grid=(bsz, n_tok // tm),
        in_specs=[pl.BlockSpec((None, tm, d), lambda b, i: (b, i, 0)),
                  pl.BlockSpec((1, d), lambda b, i: (0, 0))],
        out_specs=pl.BlockSpec((None, tm, d), lambda b, i: (b, i, 0)),
        out_shape=jax.ShapeDtypeStruct((bsz, n_tok, d), F32),
        compiler_params=_cparams(2),
        name="final_norm",
    )(h3d, g.reshape(1, d))


def _rope_pairs_64(y, cos, sin_signed):
    return y * cos + pltpu.roll(y, 64, 1) * sin_signed


def _rope_pairs_32(x, cos, sin_lo, sin_hi):
    return x * cos + pltpu.roll(x, 96, 1) * sin_lo + pltpu.roll(x, 32, 1) * sin_hi


def _proj_aqk_body(x_ref, w_ref, g_ref, cos_ref, sin_ref, o_ref):
    acc = jnp.dot(x_ref[...], w_ref[...], preferred_element_type=F32)
    cos = cos_ref[...]
    sin = sin_ref[...]
    for h in range(acc.shape[1] // HEAD_DIM):
        sl = slice(h * HEAD_DIM, (h + 1) * HEAD_DIM)
        a = acc[:, sl]
        ms = jnp.mean(a * a, axis=-1, keepdims=True)
        y = (a * lax.rsqrt(ms + NORM_EPS)) * g_ref[:, sl]
        o_ref[:, sl] = _rope_pairs_64(y, cos, sin).astype(o_ref.dtype)


def _proj_plain_body(x_ref, w_ref, o_ref):
    o_ref[...] = jnp.dot(x_ref[...], w_ref[...], preferred_element_type=F32).astype(o_ref.dtype)


def _proj_transposed_body(x_ref, w_ref, o_ref):
    o_ref[...] = jnp.dot(x_ref[...], w_ref[...], preferred_element_type=F32).T.astype(o_ref.dtype)


def _proj_gate_body(x_ref, w_ref, o_ref):
    acc = jnp.dot(x_ref[...], w_ref[...], preferred_element_type=F32)
    o_ref[...] = acc * (1.0 / (1.0 + jnp.exp(-acc)))


def _proj_cqk_body(x_ref, w_ref, mult_ref, o_ref):
    acc = jnp.dot(x_ref[...], w_ref[...], preferred_element_type=F32) * mult_ref[...]
    lane = lax.broadcasted_iota(jnp.int32, (acc.shape[0], LANES), 1)
    low = lane < C_QK_DIM
    for t in range(acc.shape[1] // LANES):
        a = acc[:, t * LANES:(t + 1) * LANES]
        o_ref[:, (2 * t) * LANES:(2 * t + 1) * LANES] = jnp.where(low, a, 0.0).astype(o_ref.dtype)
        o_ref[:, (2 * t + 1) * LANES:(2 * t + 2) * LANES] = (
            jnp.where(low, pltpu.roll(a, 64, 1), 0.0).astype(o_ref.dtype))


def _proj_bpe_body(x_ref, w_ref, cos_ref, slo_ref, shi_ref, o_ref):
    acc = jnp.dot(x_ref[...], w_ref[...], preferred_element_type=F32)
    o_ref[...] = _rope_pairs_32(acc, cos_ref[...], slo_ref[...], shi_ref[...]).astype(o_ref.dtype)


def _proj(body, u, w, extras, extra_specs, n_out, out_dtype, tn, out_tn=None, transposed=False, name="proj"):
    rows, k = u.shape
    n = w.shape[1]
    tm = _row_tile(rows)
    out_tn = tn if out_tn is None else out_tn
    if transposed:
        out_specs = pl.BlockSpec((out_tn, tm), lambda i, j: (j, i))
        out_shape = jax.ShapeDtypeStruct((n_out, rows), out_dtype)
    else:
        out_specs = pl.BlockSpec((tm, out_tn), lambda i, j: (i, j))
        out_shape = jax.ShapeDtypeStruct((rows, n_out), out_dtype)
    return pl.pallas_call(
        body,
        grid=(rows // tm, n // tn),
        in_specs=[pl.BlockSpec((tm, k), lambda i, j: (i, 0)),
                  pl.BlockSpec((k, tn), lambda i, j: (0, j))] + extra_specs(tm, tn),
        out_specs=out_specs,
        out_shape=out_shape,
        compiler_params=_cparams(2),
        name=name,
    )(u, w, *extras)


def _row_table_spec(tm, tn):
    return pl.BlockSpec((tm, LANES), lambda i, j: (i, 0))


def _col_vec_spec(tm, tn):
    return pl.BlockSpec((1, tn), lambda i, j: (0, j))


def _bq_up_body(c_ref, g_ref, w_ref, cos_ref, slo_ref, shi_ref, o_ref):
    c = c_ref[...]
    ms = jnp.mean(c * c, axis=-1, keepdims=True)
    u = ((c * lax.rsqrt(ms + NORM_EPS)) * g_ref[...]).astype(BF16)
    acc = jnp.dot(u, w_ref[...], preferred_element_type=F32)
    cos, slo, shi = cos_ref[...], slo_ref[...], shi_ref[...]
    for h in range(B_HEADS):
        base = h * 2 * LANES
        o_ref[:, base:base + LANES] = acc[:, base:base + LANES].astype(o_ref.dtype)
        x = acc[:, base + LANES:base + 2 * LANES]
        o_ref[:, base + LANES:base + 2 * LANES] = _rope_pairs_32(x, cos, slo, shi).astype(o_ref.dtype)


def _bkv_up_body(c_ref, g_ref, w_ref, kpe_ref, k_ref, v_ref):
    c = c_ref[...]
    ms = jnp.mean(c * c, axis=-1, keepdims=True)
    u = ((c * lax.rsqrt(ms + NORM_EPS)) * g_ref[...]).astype(BF16)
    acc = jnp.dot(u, w_ref[...], preferred_element_type=F32)
    kpe = kpe_ref[...]
    for h in range(B_HEADS):
        base = h * 2 * LANES
        k_ref[:, base:base + LANES] = acc[:, h * LANES:(h + 1) * LANES].astype(k_ref.dtype)
        k_ref[:, base + LANES:base + 2 * LANES] = kpe
    v_ref[...] = acc[:, B_HEADS * LANES:].T.astype(v_ref.dtype)


def _bq_up(cb, g, w, cos, slo, shi):
    rows = cb.shape[0]
    tm = _row_tile(rows)
    n = w.shape[1]
    tab = pl.BlockSpec((tm, LANES), lambda i: (i, 0))
    return pl.pallas_call(
        _bq_up_body,
        grid=(rows // tm,),
        in_specs=[pl.BlockSpec((tm, B_Q_RANK), lambda i: (i, 0)),
                  pl.BlockSpec((1, B_Q_RANK), lambda i: (0, 0)),
                  pl.BlockSpec((B_Q_RANK, n), lambda i: (0, 0)),
                  tab, tab, tab],
        out_specs=pl.BlockSpec((tm, n), lambda i: (i, 0)),
        out_shape=jax.ShapeDtypeStruct((rows, n), BF16),
        compiler_params=_cparams(1),
        name="mla_q_up",
    )(cb, g.reshape(1, -1), w, cos, slo, shi)


def _bkv_up(cb, g, w, kpe):
    rows = cb.shape[0]
    tm = _row_tile(rows)
    n = w.shape[1]
    return pl.pallas_call(
        _bkv_up_body,
        grid=(rows // tm,),
        in_specs=[pl.BlockSpec((tm, B_KV_RANK), lambda i: (i, B_Q_RANK // B_KV_RANK)),
                  pl.BlockSpec((1, B_KV_RANK), lambda i: (0, 0)),
                  pl.BlockSpec((B_KV_RANK, n), lambda i: (0, 0)),
                  pl.BlockSpec((tm, LANES), lambda i: (i, 0))],
        out_specs=[pl.BlockSpec((tm, B_HEADS * 2 * LANES), lambda i: (i, 0)),
                   pl.BlockSpec((B_HEADS * B_VDIM, tm), lambda i: (0, i))],
        out_shape=[jax.ShapeDtypeStruct((rows, B_HEADS * 2 * LANES), BF16),
                   jax.ShapeDtypeStruct((B_HEADS * B_VDIM, rows), BF16)],
        compiler_params=_cparams(1),
        name="mla_kv_up",
    )(cb, g.reshape(1, -1), w, kpe)


def _online_softmax_step(s, vt, m_ref, l_ref, acc_ref, coef):
    m_prev = m_ref[...]
    m_new = jnp.maximum(m_prev, jnp.max(s, axis=0, keepdims=True))
    alpha = jnp.exp2((m_prev - m_new) * coef)
    p = jnp.exp2((s - m_new) * coef)
    l_ref[...] = alpha * l_ref[...] + jnp.sum(p, axis=0, keepdims=True)
    acc_ref[...] = alpha * acc_ref[...] + jnp.dot(vt, p.astype(BF16), preferred_element_type=F32)
    m_ref[...] = m_new


def _init_softmax_state(m_ref, l_ref, acc_ref):
    m_ref[...] = jnp.full(m_ref.shape, NEG_BIG, F32)
    l_ref[...] = jnp.zeros(l_ref.shape, F32)
    acc_ref[...] = jnp.zeros(acc_ref.shape, F32)


def _attn_ab_body(q_ref, k_ref, vt_ref, gate_ref, *rest, group, qsub, tqs, dk, n_tok, coef):
    o_ref, m_ref, l_ref, acc_ref = rest[-4:]
    _init_softmax_state(m_ref, l_ref, acc_ref)
    units = [(g, u) for g in range(group) for u in range(qsub)]

    def all_units(kc, vt, n_valid):
        for idx, (g, u) in enumerate(units):
            q = q_ref[u * tqs:(u + 1) * tqs, g * dk:(g + 1) * dk]
            s = lax.dot_general(kc, q, _NT, preferred_element_type=F32)
            if n_valid is not None:
                row = lax.broadcasted_iota(jnp.int32, s.shape, 0)
                s = jnp.where(row < n_valid, s, NEG_BIG)
            _online_softmax_step(s, vt, m_ref.at[idx], l_ref.at[idx], acc_ref.at[idx], coef)

    def chunk_step(c, carry):
        off = pl.multiple_of(c * KV_CHUNK, KV_CHUNK)
        all_units(k_ref[pl.ds(off, KV_CHUNK), :], vt_ref[:, pl.ds(off, KV_CHUNK)], None)
        return carry

    lax.fori_loop(0, n_tok // KV_CHUNK, chunk_step, 0)
    all_units(k_ref[n_tok:n_tok + TAIL, :], vt_ref[:, n_tok:n_tok + TAIL], N_META)

    for idx, (g, u) in enumerate(units):
        rows = slice(u * tqs, (u + 1) * tqs)
        cols = slice(g * HEAD_DIM, (g + 1) * HEAD_DIM)
        o = (acc_ref[idx] * (1.0 / l_ref[idx])).T
        o_ref[rows, cols] = (o * gate_ref[rows, cols]).astype(o_ref.dtype)


def _attn_c_body(q_ref, k_ref, vt_ref, gate_ref, lam_ref, slope_ref, gsub_ref, *rest,
                 qsub, tqs, n_tok, q_is_meta, lambda_init):
    o_ref, m_ref, l_ref, acc_ref = rest[-4:]
    _init_softmax_state(m_ref, l_ref, acc_ref)
    slope = slope_ref[:, :1]
    q_off = 0 if q_is_meta else N_META + pl.program_id(2) * (qsub * tqs)

    def all_units(kc, vt, k_off, n_valid):
        n_keys = kc.shape[0]
        key = lax.broadcasted_iota(jnp.int32, (n_keys, tqs), 0)
        qry = lax.broadcasted_iota(jnp.int32, (n_keys, tqs), 1)
        for u in range(qsub):
            dist = jnp.abs(((key - qry) + (k_off - q_off - u * tqs)).astype(F32))
            bias = slope * dist
            for c in range(2):
                q = q_ref[u * tqs:(u + 1) * tqs, c * LANES:(c + 1) * LANES]
                s = lax.dot_general(kc[:, c * LANES:(c + 1) * LANES], q, _NT, preferred_element_type=F32) - bias
                if n_valid is not None:
                    s = jnp.where(key < n_valid, s, NEG_BIG)
                idx = 2 * u + c
                _online_softmax_step(s, vt, m_ref.at[idx], l_ref.at[idx], acc_ref.at[idx], LOG2E)

    def chunk_step(c, carry):
        off = pl.multiple_of(c * KV_CHUNK, KV_CHUNK)
        all_units(k_ref[pl.ds(off, KV_CHUNK), :], vt_ref[:, pl.ds(off, KV_CHUNK)], N_META + off, None)
        return carry

    lax.fori_loop(0, n_tok // KV_CHUNK, chunk_step, 0)
    all_units(k_ref[n_tok:n_tok + TAIL, :], vt_ref[:, n_tok:n_tok + TAIL], 0, N_META)

    lv = lam_ref[...]
    lam = (jnp.exp(jnp.sum(lv[0:1] * lv[1:2], axis=1, keepdims=True))
           - jnp.exp(jnp.sum(lv[2:3] * lv[3:4], axis=1, keepdims=True)) + lambda_init)
    for u in range(qsub):
        rows = slice(u * tqs, (u + 1) * tqs)
        o0 = acc_ref[2 * u] * (1.0 / l_ref[2 * u])
        o1 = acc_ref[2 * u + 1] * (1.0 / l_ref[2 * u + 1])
        o = (o0 - lam * o1).T
        ms = jnp.mean(o * o, axis=-1, keepdims=True)
        o = ((o * lax.rsqrt(ms + NORM_EPS)) * gsub_ref[...]) * (1.0 - lambda_init)
        o_ref[rows, :] = (o * gate_ref[rows, :]).astype(o_ref.dtype)


def _attention(body, q, q_col0, q_w, k, k_col0, k_w, vt, v_row0, gate, gate_col0, extras, extra_specs,
               out_w_total, heads, group, qsub, n_units, n_tok, name):
    bsz, lp, _ = q.shape
    out_w = group * HEAD_DIM

    def call(qsub_, tqs, n_qblk, qblk0, prev):
        tq = qsub_ * tqs
        in_specs = [
            pl.BlockSpec((None, tq, q_w), lambda b, h, i: (b, qblk0 + i, q_col0 + h)),
            pl.BlockSpec((None, lp, k_w), lambda b, h, i: (b, 0, k_col0 + h)),
            pl.BlockSpec((HEAD_DIM, lp), lambda b, h, i: (v_row0 + h, b)),
            pl.BlockSpec((None, tq, out_w), lambda b, h, i: (b, qblk0 + i, gate_col0 + h)),
        ] + extra_specs
        args = [q, k, vt, gate] + extras
        aliases = {}
        if prev is not None:
            in_specs.append(pl.BlockSpec(memory_space=pl.ANY))
            args.append(prev)
            aliases = {len(args) - 1: 0}
        units = n_units * qsub_
        return pl.pallas_call(
            body(qsub_, tqs, prev is not None),
            grid=(bsz, heads, n_qblk),
            in_specs=in_specs,
            out_specs=pl.BlockSpec((None, tq, out_w), lambda b, h, i: (b, qblk0 + i, h)),
            out_shape=jax.ShapeDtypeStruct((bsz, lp, out_w_total), BF16),
            scratch_shapes=[pltpu.VMEM((units, 1, tqs), F32), pltpu.VMEM((units, 1, tqs), F32),
                            pltpu.VMEM((units, HEAD_DIM, tqs), F32)],
            input_output_aliases=aliases,
            compiler_params=_cparams(3),
            name=name + ("_meta" if prev is not None else ""),
        )(*args)

    main = call(qsub, Q_SUB, n_tok // (qsub * Q_SUB), 0, None)
    return call(1, TAIL, 1, n_tok // TAIL, main)


def _attn_a(qk, vt_plain, gates, n_tok):
    def body(qsub_, tqs, is_meta):
        return functools.partial(_attn_ab_body, group=A_GROUP, qsub=qsub_, tqs=tqs, dk=HEAD_DIM, n_tok=n_tok,
                                 coef=HEAD_DIM ** -0.5 * LOG2E)

    return _attention(body, qk, 0, A_GROUP * HEAD_DIM, qk, A_WIDTH // HEAD_DIM, HEAD_DIM, vt_plain, 0,
                      gates, 0, [], [], A_WIDTH, A_KV_HEADS, A_GROUP, 1, A_GROUP, n_tok, "attn_gqa")


def _attn_b(qb, kb, vt_b, gates, n_tok):
    dk = 2 * LANES

    def body(qsub_, tqs, is_meta):
        return functools.partial(_attn_ab_body, group=1, qsub=qsub_, tqs=tqs, dk=dk, n_tok=n_tok,
                                 coef=(B_NOPE + B_ROPE) ** -0.5 * LOG2E)

    return _attention(body, qb, 0, dk, kb, 0, dk, vt_b, 0, gates, A_WIDTH // HEAD_DIM, [], [],
                      B_WIDTH, B_HEADS, 1, 2, 1, n_tok, "attn_mla")


def _attn_c(cqk, vt_plain, gates, lamvec, slopes, g_sub, lambda_init, n_tok):
    dk = 2 * LANES

    def body(qsub_, tqs, is_meta):
        return functools.partial(_attn_c_body, qsub=qsub_, tqs=tqs, n_tok=n_tok, q_is_meta=is_meta,
                                 lambda_init=lambda_init)

    extras = [lamvec, slopes, g_sub.reshape(1, C_VDIM)]
    extra_specs = [pl.BlockSpec((4, C_QK_DIM), lambda b, h, i: (0, 0)),
                   pl.BlockSpec((None, 1, LANES), lambda b, h, i: (h, 0, 0)),
                   pl.BlockSpec((1, C_VDIM), lambda b, h, i: (0, 0))]
    return _attention(body, cqk, 0, dk, cqk, C_HEADS, dk, vt_plain, A_KV_HEADS, gates,
                      (A_WIDTH + B_WIDTH) // HEAD_DIM, extras, extra_specs,
                      C_WIDTH, C_HEADS, 1, 2, 2, n_tok, "attn_diff")


def _out_proj_body(a_ref, b_ref, c_ref, wa_ref, wb_ref, wc_ref, h_ref, o_ref):
    acc = jnp.dot(a_ref[...], wa_ref[...], preferred_element_type=F32)
    acc += jnp.dot(b_ref[...], wb_ref[...], preferred_element_type=F32)
    acc += jnp.dot(c_ref[...], wc_ref[...], preferred_element_type=F32)
    o_ref[...] = h_ref[...] + acc


def _out_proj(oa, ob, oc, w, h2d):
    rows = h2d.shape[0]
    tm = _row_tile(rows)
    tn = 512
    return pl.pallas_call(
        _out_proj_body,
        grid=(rows // tm, D_MODEL // tn),
        in_specs=[pl.BlockSpec((tm, A_WIDTH), lambda i, j: (i, 0)),
                  pl.BlockSpec((tm, B_WIDTH), lambda i, j: (i, 0)),
                  pl.BlockSpec((tm, C_WIDTH), lambda i, j: (i, 0)),
                  pl.BlockSpec((A_WIDTH, tn), lambda i, j: (0, j)),
                  pl.BlockSpec((B_WIDTH, tn), lambda i, j: (A_WIDTH // B_WIDTH, j)),
                  pl.BlockSpec((C_WIDTH, tn), lambda i, j: ((A_WIDTH + B_WIDTH) // C_WIDTH, j)),
                  pl.BlockSpec((tm, tn), lambda i, j: (i, j))],
        out_specs=pl.BlockSpec((tm, tn), lambda i, j: (i, j)),
        out_shape=jax.ShapeDtypeStruct((rows, D_MODEL), F32),
        compiler_params=_cparams(2),
        name="out_proj",
    )(oa, ob, oc, w, w, w, h2d)


def _rope_angles(pos_f, n_freq):
    inv = ROPE_THETA ** (-jnp.arange(n_freq, dtype=F32) / n_freq)
    return pos_f[:, None] * inv[None, :]


def _position_tables(n_tok, bsz):
    rows = n_tok // GRID_W
    z = jnp.zeros((TAIL,), F32)
    row_f = jnp.concatenate([jnp.repeat(jnp.arange(rows, dtype=F32), GRID_W), z])
    col_f = jnp.concatenate([jnp.tile(jnp.arange(GRID_W, dtype=F32), rows), z])
    pos_f = jnp.concatenate([jnp.arange(n_tok, dtype=F32) + N_META,
                             jnp.arange(N_META, dtype=F32), jnp.zeros((TAIL - N_META,), F32)])
    ang_a = jnp.concatenate([_rope_angles(row_f, HEAD_DIM // 4), _rope_angles(col_f, HEAD_DIM // 4)], axis=-1)
    cos_a, sin_a = jnp.cos(ang_a), jnp.sin(ang_a)
    cos_a = jnp.concatenate([cos_a, cos_a], axis=-1)
    sin_a = jnp.concatenate([-sin_a, sin_a], axis=-1)
    ang_b = _rope_angles(pos_f, B_ROPE // 2)
    cos_b, sin_b = jnp.cos(ang_b), jnp.sin(ang_b)
    zb = jnp.zeros_like(cos_b)
    cos_b128 = jnp.concatenate([cos_b, cos_b, zb, zb], axis=-1)
    sin_lo = jnp.concatenate([-sin_b, zb, zb, zb], axis=-1)
    sin_hi = jnp.concatenate([zb, sin_b, zb, zb], axis=-1)
    return tuple(jnp.tile(t, (bsz, 1)) for t in (cos_a, sin_a, cos_b128, sin_lo, sin_hi))


def _prep_weights(w_in, w_q_b, w_kv_b, w_out, g_qn, g_kn):
    bounds = [0]
    for s in IN_SPLITS:
        bounds.append(bounds[-1] + s)
    col = lambda i: slice(bounds[i], bounds[i + 1])
    w = w_in.astype(BF16)
    aq, ak, av, ag, bq, bkv, bpe, bg, cq, ck, cv, cg = (w[:, col(i)] for i in range(12))
    w_aqk = jnp.concatenate([aq, ak], axis=1)
    w_plain = jnp.concatenate([av, cv], axis=1)
    w_gate = jnp.concatenate([ag, bg, cg], axis=1)
    w_cqk = jnp.concatenate([cq, ck], axis=1)
    w_blow = jnp.concatenate([bq, bkv], axis=1)
    w_bpe = jnp.concatenate([bpe, jnp.zeros((D_MODEL, LANES - B_ROPE), BF16)], axis=1)
    g_aqk = jnp.concatenate([jnp.tile(g_qn, A_HEADS), jnp.tile(g_kn, A_KV_HEADS)]).reshape(1, -1)
    wq = w_q_b.astype(BF16).reshape(B_Q_RANK, B_HEADS, B_NOPE + B_ROPE)
    wq = jnp.concatenate([wq, jnp.zeros((B_Q_RANK, B_HEADS, 2 * LANES - B_NOPE - B_ROPE), BF16)], axis=-1)
    wq = wq.reshape(B_Q_RANK, B_HEADS * 2 * LANES)
    wkv = w_kv_b.astype(BF16).reshape(B_KV_RANK, B_HEADS, B_NOPE + B_VDIM)
    wkv = jnp.concatenate([wkv[:, :, :B_NOPE].reshape(B_KV_RANK, -1),
                           wkv[:, :, B_NOPE:].reshape(B_KV_RANK, -1)], axis=1)
    return dict(aqk=w_aqk, plain=w_plain, gate=w_gate, cqk=w_cqk, blow=w_blow, bpe=w_bpe,
                g_aqk=g_aqk, q_up=wq, kv_up=wkv, out=w_out.astype(BF16))


def _encode(x, meta, layers, g_final, slopes, cq_mult):
    bsz, n_tok, _ = x.shape
    lp = n_tok + TAIL
    rows = bsz * lp
    tail = jnp.concatenate([meta.astype(x.dtype), jnp.zeros((TAIL - N_META, D_MODEL), x.dtype)], axis=0)
    h = jnp.concatenate([x, jnp.broadcast_to(tail[None], (bsz, TAIL, D_MODEL))], axis=1).reshape(rows, D_MODEL)
    cos_a, sin_a, cos_b, sin_lo, sin_hi = _position_tables(n_tok, bsz)
    three = lambda a: a.reshape(bsz, lp, a.shape[-1])

    for l, p in enumerate(layers):
        lambda_init = 0.8 - 0.6 * math.exp(-0.3 * l)
        u = _rmsnorm_rows(h, p["g_attn"], BF16)
        aqk = _proj(_proj_aqk_body, u, p["aqk"], [p["g_aqk"], cos_a, sin_a],
                    lambda tm, tn: [_col_vec_spec(tm, tn), _row_table_spec(tm, tn), _row_table_spec(tm, tn)],
                    A_WIDTH + A_KV_WIDTH, BF16, 512, name="proj_gqa_qk")
        vt_plain = _proj(_proj_transposed_body, u, p["plain"], [], lambda tm, tn: [],
                         A_KV_WIDTH + C_WIDTH, BF16, 512, transposed=True, name="proj_v")
        gates = _proj(_proj_gate_body, u, p["gate"], [], lambda tm, tn: [],
                      A_WIDTH + B_WIDTH + C_WIDTH, F32, 512, name="proj_gate")
        cqk = _proj(_proj_cqk_body, u, p["cqk"], [cq_mult], lambda tm, tn: [_col_vec_spec(tm, tn)],
                    4 * C_QK_WIDTH, BF16, 512, out_tn=1024, name="proj_diff_qk")
        blow = _proj(_proj_plain_body, u, p["blow"], [], lambda tm, tn: [],
                     B_Q_RANK + B_KV_RANK, F32, 512, name="proj_mla_low")
        kpe = _proj(_proj_bpe_body, u, p["bpe"], [cos_b, sin_lo, sin_hi],
                    lambda tm, tn: [_row_table_spec(tm, tn)] * 3, LANES, BF16, LANES, name="proj_mla_pe")
        qb = _bq_up(blow, p["g_q_a"], p["q_up"], cos_b, sin_lo, sin_hi)
        kb, vt_b = _bkv_up(blow, p["g_kv_a"], p["kv_up"], kpe)

        gates3 = three(gates)
        oa = _attn_a(three(aqk), vt_plain, gates3, n_tok)
        ob = _attn_b(three(qb), three(kb), vt_b, gates3, n_tok)
        oc = _attn_c(three(cqk), vt_plain, gates3, p["lamvec"], slopes, p["g_sub"], lambda_init, n_tok)
        h = _out_proj(oa.reshape(rows, -1), ob.reshape(rows, -1), oc.reshape(rows, -1), p["out"], h)

    return _final_norm(h.reshape(bsz, lp, D_MODEL), g_final, n_tok)


def kernel(x_prompt, x_sample, meta, g_attn, w_in, g_qn, g_kn, g_q_a, w_q_b, g_kv_a, w_kv_b,
           lam_q1, lam_k1, lam_q2, lam_k2, g_sub, w_out, g_final):
    layers = []
    for l in range(DEPTH):
        p = _prep_weights(w_in[l], w_q_b[l], w_kv_b[l], w_out[l], g_qn[l], g_kn[l])
        p.update(g_attn=g_attn[l], g_q_a=g_q_a[l], g_kv_a=g_kv_a[l], g_sub=g_sub[l],
                 lamvec=jnp.stack([lam_q1[l], lam_k1[l], lam_q2[l], lam_k2[l]]).astype(F32))
        layers.append(p)
    slopes = 2.0 ** (-(jnp.arange(C_HEADS, dtype=F32) + 1.0) * 8.0 / C_HEADS)
    slopes = jnp.broadcast_to(slopes[:, None, None], (C_HEADS, 1, LANES))
    cq_mult = jnp.concatenate([jnp.full((C_QK_WIDTH,), C_QK_DIM ** -0.5, F32),
                               jnp.ones((C_QK_WIDTH,), F32)]).reshape(1, -1)
    y_prompt = _encode(x_prompt, meta, layers, g_final, slopes, cq_mult)
    y_sample = _encode(x_sample, meta, layers, g_final, slopes, cq_mult)
    return (y_prompt, y_sample)
```

```python
import functools
import math

import jax
import jax.numpy as jnp
from jax import lax
from jax.experimental import pallas as pl
from jax.experimental.pallas import tpu as pltpu

D_MODEL = 4096
DEPTH = 2
N_META = 16
GRID_W = 64
HEAD_DIM = 128
ROPE_THETA = 10000.0
NORM_EPS = 1e-6
A_HEADS = 16
A_KV_HEADS = 4
A_GROUP = A_HEADS // A_KV_HEADS
A_WIDTH = A_HEADS * HEAD_DIM
A_KV_WIDTH = A_KV_HEADS * HEAD_DIM
B_HEADS = 8
B_Q_RANK = 1024
B_KV_RANK = 512
B_NOPE = 128
B_ROPE = 64
B_VDIM = 128
B_WIDTH = B_HEADS * B_VDIM
C_HEADS = 8
C_QK_DIM = 64
C_VDIM = 2 * C_QK_DIM
C_QK_WIDTH = C_HEADS * 2 * C_QK_DIM
C_WIDTH = C_HEADS * C_VDIM
IN_SPLITS = (A_WIDTH, A_KV_WIDTH, A_KV_WIDTH, A_WIDTH,
             B_Q_RANK, B_KV_RANK, B_ROPE, B_WIDTH,
             C_QK_WIDTH, C_QK_WIDTH, C_WIDTH, C_WIDTH)

LANES = 128
TAIL = LANES
KV_CHUNK = 512
Q_SUB = 256
NEG_BIG = -1e30
LOG2E = 1.4426950408889634
LOOKAHEAD = 2
ONES_ROWS = 16
BF16 = jnp.bfloat16
F32 = jnp.float32
VMEM_LIMIT = 56 * 1024 * 1024

_NT = (((1,), (1,)), ((), ()))


def _cparams(n_axes):
    return pltpu.CompilerParams(dimension_semantics=("arbitrary",) * n_axes,
                                vmem_limit_bytes=VMEM_LIMIT)


def _row_tile(rows):
    for t in (512, 640, 256, 128):
        if rows % t == 0:
            return t
    raise ValueError(f"no row tile for {rows}")


def _rmsnorm_body(x_ref, g_ref, o_ref):
    x = x_ref[...]
    ms = jnp.mean(x * x, axis=-1, keepdims=True)
    o_ref[...] = ((x * lax.rsqrt(ms + NORM_EPS)) * g_ref[...]).astype(o_ref.dtype)


def _rmsnorm_rows(h2d, g, out_dtype):
    rows, d = h2d.shape
    tm = 256 if rows % 256 == 0 else 128
    return pl.pallas_call(
        _rmsnorm_body,
        grid=(rows // tm,),
        in_specs=[pl.BlockSpec((tm, d), lambda i: (i, 0)),
                  pl.BlockSpec((1, d), lambda i: (0, 0))],
        out_specs=pl.BlockSpec((tm, d), lambda i: (i, 0)),
        out_shape=jax.ShapeDtypeStruct((rows, d), out_dtype),
        compiler_params=_cparams(1),
        name="rmsnorm",
    )(h2d, g.reshape(1, d))


def _final_norm(h3d, g, n_tok):
    bsz, _, d = h3d.shape
    tm = 256
    return pl.pallas_call(
        _rmsnorm_body,
        grid=(bsz, n_tok // tm),
        in_specs=[pl.BlockSpec((None, tm, d), lambda b, i: (b, i, 0)),
                  pl.BlockSpec((1, d), lambda b, i: (0, 0))],
        out_specs=pl.BlockSpec((None, tm, d), lambda b, i: (b, i, 0)),
        out_shape=jax.ShapeDtypeStruct((bsz, n_tok, d), F32),
        compiler_params=_cparams(2),
        name="final_norm",
    )(h3d, g.reshape(1, d))


def _rope_pairs_64(y, cos, sin_signed):
    return y * cos + pltpu.roll(y, 64, 1) * sin_signed


def _rope_pairs_32(x, cos, sin_lo, sin_hi):
    return x * cos + pltpu.roll(x, 96, 1) * sin_lo + pltpu.roll(x, 32, 1) * sin_hi


def _proj_aqk_body(x_ref, w_ref, g_ref, mult_ref, cos_ref, sin_ref, o_ref):
    acc = jnp.dot(x_ref[...], w_ref[...], preferred_element_type=F32)
    cos = cos_ref[...]
    sin = sin_ref[...]
    for h in range(acc.shape[1] // HEAD_DIM):
        sl = slice(h * HEAD_DIM, (h + 1) * HEAD_DIM)
        a = acc[:, sl]
        ms = jnp.mean(a * a, axis=-1, keepdims=True)
        y = (a * lax.rsqrt(ms + NORM_EPS)) * g_ref[:, sl]
        o_ref[:, sl] = (_rope_pairs_64(y, cos, sin) * mult_ref[:, sl]).astype(o_ref.dtype)


def _proj_plain_body(x_ref, w_ref, o_ref):
    o_ref[...] = jnp.dot(x_ref[...], w_ref[...], preferred_element_type=F32).astype(o_ref.dtype)


def _proj_transposed_body(x_ref, w_ref, o_ref):
    o_ref[...] = jnp.dot(x_ref[...], w_ref[...], preferred_element_type=F32).T.astype(o_ref.dtype)


def _proj_gate_body(x_ref, w_ref, o_ref):
    acc = jnp.dot(x_ref[...], w_ref[...], preferred_element_type=F32)
    o_ref[...] = acc * (1.0 / (1.0 + jnp.exp(-acc)))


def _proj_cqk_body(x_ref, w_ref, mult_ref, o_ref):
    acc = jnp.dot(x_ref[...], w_ref[...], preferred_element_type=F32) * mult_ref[...]
    lane = lax.broadcasted_iota(jnp.int32, (acc.shape[0], LANES), 1)
    low = lane < C_QK_DIM
    for t in range(acc.shape[1] // LANES):
        a = acc[:, t * LANES:(t + 1) * LANES]
        o_ref[:, (2 * t) * LANES:(2 * t + 1) * LANES] = jnp.where(low, a, 0.0).astype(o_ref.dtype)
        o_ref[:, (2 * t + 1) * LANES:(2 * t + 2) * LANES] = (
            jnp.where(low, pltpu.roll(a, 64, 1), 0.0).astype(o_ref.dtype))


def _proj_bpe_body(x_ref, w_ref, cos_ref, slo_ref, shi_ref, o_ref):
    acc = jnp.dot(x_ref[...], w_ref[...], preferred_element_type=F32)
    o_ref[...] = _rope_pairs_32(acc, cos_ref[...], slo_ref[...], shi_ref[...]).astype(o_ref.dtype)


def _proj(body, u, w, extras, extra_specs, n_out, out_dtype, tn, out_tn=None, transposed=False, name="proj"):
    rows, k = u.shape
    n = w.shape[1]
    tm = _row_tile(rows)
    out_tn = tn if out_tn is None else out_tn
    if transposed:
        out_specs = pl.BlockSpec((out_tn, tm), lambda i, j: (j, i))
        out_shape = jax.ShapeDtypeStruct((n_out, rows), out_dtype)
    else:
        out_specs = pl.BlockSpec((tm, out_tn), lambda i, j: (i, j))
        out_shape = jax.ShapeDtypeStruct((rows, n_out), out_dtype)
    return pl.pallas_call(
        body,
        grid=(rows // tm, n // tn),
        in_specs=[pl.BlockSpec((tm, k), lambda i, j: (i, 0)),
                  pl.BlockSpec((k, tn), lambda i, j: (0, j))] + extra_specs(tm, tn),
        out_specs=out_specs,
        out_shape=out_shape,
        compiler_params=_cparams(2),
        name=name,
    )(u, w, *extras)


def _row_table_spec(tm, tn):
    return pl.BlockSpec((tm, LANES), lambda i, j: (i, 0))


def _col_vec_spec(tm, tn):
    return pl.BlockSpec((1, tn), lambda i, j: (0, j))


def _bq_up_body(c_ref, g_ref, w_ref, cos_ref, slo_ref, shi_ref, o_ref):
    c = c_ref[...]
    ms = jnp.mean(c * c, axis=-1, keepdims=True)
    u = ((c * lax.rsqrt(ms + NORM_EPS)) * g_ref[...]).astype(BF16)
    acc = jnp.dot(u, w_ref[...], preferred_element_type=F32) * ((B_NOPE + B_ROPE) ** -0.5 * LOG2E)
    cos, slo, shi = cos_ref[...], slo_ref[...], shi_ref[...]
    for h in range(B_HEADS):
        base = h * 2 * LANES
        o_ref[:, base:base + LANES] = acc[:, base:base + LANES].astype(o_ref.dtype)
        x = acc[:, base + LANES:base + 2 * LANES]
        o_ref[:, base + LANES:base + 2 * LANES] = _rope_pairs_32(x, cos, slo, shi).astype(o_ref.dtype)


def _bkv_up_body(c_ref, g_ref, w_ref, kpe_ref, k_ref, v_ref):
    c = c_ref[...]
    ms = jnp.mean(c * c, axis=-1, keepdims=True)
    u = ((c * lax.rsqrt(ms + NORM_EPS)) * g_ref[...]).astype(BF16)
    acc = jnp.dot(u, w_ref[...], preferred_element_type=F32)
    kpe = kpe_ref[...]
    for h in range(B_HEADS):
        base = h * 2 * LANES
        k_ref[:, base:base + LANES] = acc[:, h * LANES:(h + 1) * LANES].astype(k_ref.dtype)
        k_ref[:, base + LANES:base + 2 * LANES] = kpe
    v_ref[...] = acc[:, B_HEADS * LANES:].T.astype(v_ref.dtype)


def _bq_up(cb, g, w, cos, slo, shi):
    rows = cb.shape[0]
    tm = _row_tile(rows)
    n = w.shape[1]
    tab = pl.BlockSpec((tm, LANES), lambda i: (i, 0))
    return pl.pallas_call(
        _bq_up_body,
        grid=(rows // tm,),
        in_specs=[pl.BlockSpec((tm, B_Q_RANK), lambda i: (i, 0)),
                  pl.BlockSpec((1, B_Q_RANK), lambda i: (0, 0)),
                  pl.BlockSpec((B_Q_RANK, n), lambda i: (0, 0)),
                  tab, tab, tab],
        out_specs=pl.BlockSpec((tm, n), lambda i: (i, 0)),
        out_shape=jax.ShapeDtypeStruct((rows, n), BF16),
        compiler_params=_cparams(1),
        name="mla_q_up",
    )(cb, g.reshape(1, -1), w, cos, slo, shi)


def _bkv_up(cb, g, w, kpe):
    rows = cb.shape[0]
    tm = _row_tile(rows)
    n = w.shape[1]
    return pl.pallas_call(
        _bkv_up_body,
        grid=(rows // tm,),
        in_specs=[pl.BlockSpec((tm, B_KV_RANK), lambda i: (i, B_Q_RANK // B_KV_RANK)),
                  pl.BlockSpec((1, B_KV_RANK), lambda i: (0, 0)),
                  pl.BlockSpec((B_KV_RANK, n), lambda i: (0, 0)),
                  pl.BlockSpec((tm, LANES), lambda i: (i, 0))],
        out_specs=[pl.BlockSpec((tm, B_HEADS * 2 * LANES), lambda i: (i, 0)),
                   pl.BlockSpec((B_HEADS * B_VDIM, tm), lambda i: (0, i))],
        out_shape=[jax.ShapeDtypeStruct((rows, B_HEADS * 2 * LANES), BF16),
                   jax.ShapeDtypeStruct((B_HEADS * B_VDIM, rows), BF16)],
        compiler_params=_cparams(1),
        name="mla_kv_up",
    )(cb, g.reshape(1, -1), w, kpe)


def _online_softmax_step(s, vt_ones, m_ref, acc_ref):
    m_prev = m_ref[...]
    m_new = jnp.maximum(m_prev, jnp.max(s, axis=0, keepdims=True))
    alpha = jnp.exp2(m_prev - m_new)
    p = jnp.exp2(s - m_new)
    acc_ref[...] = alpha * acc_ref[...] + jnp.dot(vt_ones, p.astype(BF16), preferred_element_type=F32)
    m_ref[...] = m_new


def _with_ones_rows(vt):
    return jnp.concatenate([vt, jnp.ones((ONES_ROWS, vt.shape[1]), vt.dtype)], axis=0)


def _normalized(acc):
    return acc[:HEAD_DIM] * (1.0 / acc[HEAD_DIM:HEAD_DIM + 1])


def _init_softmax_state(m_ref, acc_ref):
    m_ref[...] = jnp.full(m_ref.shape, NEG_BIG, F32)
    acc_ref[...] = jnp.zeros(acc_ref.shape, F32)


def _attn_ab_body(q_ref, k_ref, vt_ref, gate_ref, *rest, group, qsub, tqs, dk, n_tok):
    o_ref, m_ref, acc_ref, s_ref = rest[-4:]
    _init_softmax_state(m_ref, acc_ref)
    units = [(g, u) for g in range(group) for u in range(qsub)]
    look = min(LOOKAHEAD, len(units))

    def scores(kc, g, u):
        q = q_ref[u * tqs:(u + 1) * tqs, g * dk:(g + 1) * dk]
        return lax.dot_general(kc, q, _NT, preferred_element_type=F32)

    def all_units(s_first, kc, vt, n_valid, kc_after):
        pending = list(s_first)
        vt_ones = _with_ones_rows(vt)
        n = len(units)
        for idx in range(n):
            s = pending.pop(0)
            ahead = idx + look
            if ahead < n:
                pending.append(scores(kc, *units[ahead]))
            elif kc_after is not None:
                s_ref[ahead - n] = scores(kc_after, *units[ahead - n])
            if n_valid is not None:
                row = lax.broadcasted_iota(jnp.int32, s.shape, 0)
                s = jnp.where(row < n_valid, s, NEG_BIG)
            _online_softmax_step(s, vt_ones, m_ref.at[idx], acc_ref.at[idx])

    n_chunks = n_tok // KV_CHUNK

    def chunk_step(c, carry):
        off = pl.multiple_of(c * KV_CHUNK, KV_CHUNK)
        off_after = pl.multiple_of(jnp.minimum(c + 1, n_chunks - 1) * KV_CHUNK, KV_CHUNK)
        all_units([s_ref[j] for j in range(look)], k_ref[pl.ds(off, KV_CHUNK), :],
                  vt_ref[:, pl.ds(off, KV_CHUNK)], None, k_ref[pl.ds(off_after, KV_CHUNK), :])
        return carry

    for j in range(look):
        s_ref[j] = scores(k_ref[0:KV_CHUNK, :], *units[j])
    lax.fori_loop(0, n_chunks, chunk_step, 0)
    k_tail = k_ref[n_tok:n_tok + TAIL, :]
    all_units([scores(k_tail, *units[j]) for j in range(look)], k_tail, vt_ref[:, n_tok:n_tok + TAIL],
              N_META, None)

    for idx, (g, u) in enumerate(units):
        rows = slice(u * tqs, (u + 1) * tqs)
        cols = slice(g * HEAD_DIM, (g + 1) * HEAD_DIM)
        o = _normalized(acc_ref[idx]).T
        o_ref[rows, cols] = (o * gate_ref[rows, cols]).astype(o_ref.dtype)


def _attn_c_body(q_ref, k_ref, vt_ref, gate_ref, lam_ref, slope_ref, gsub_ref, *rest,
                 qsub, tqs, n_tok, q_is_meta, lambda_init):
    o_ref, m_ref, acc_ref, s_ref = rest[-4:]
    _init_softmax_state(m_ref, acc_ref)
    slope = slope_ref[:, :1]
    q_off = 0 if q_is_meta else N_META + pl.program_id(2) * (qsub * tqs)
    units = [(u, c) for u in range(qsub) for c in range(2)]
    look = min(LOOKAHEAD, len(units))

    def scores(kc, u, c):
        q = q_ref[u * tqs:(u + 1) * tqs, c * LANES:(c + 1) * LANES]
        return lax.dot_general(kc[:, c * LANES:(c + 1) * LANES], q, _NT, preferred_element_type=F32)

    def all_units(s_first, kc, vt, k_off, n_valid, kc_after):
        n_keys = kc.shape[0]
        key = lax.broadcasted_iota(jnp.int32, (n_keys, tqs), 0)
        qry = lax.broadcasted_iota(jnp.int32, (n_keys, tqs), 1)
        pending = list(s_first)
        vt_ones = _with_ones_rows(vt)
        bias = None
        n = len(units)
        for idx, (u, c) in enumerate(units):
            s = pending.pop(0)
            ahead = idx + look
            if ahead < n:
                pending.append(scores(kc, *units[ahead]))
            elif kc_after is not None:
                s_ref[ahead - n] = scores(kc_after, *units[ahead - n])
            if c == 0:
                bias = slope * jnp.abs(((key - qry) + (k_off - q_off - u * tqs)).astype(F32))
            s = s - bias
            if n_valid is not None:
                s = jnp.where(key < n_valid, s, NEG_BIG)
            _online_softmax_step(s, vt_ones, m_ref.at[idx], acc_ref.at[idx])

    n_chunks = n_tok // KV_CHUNK

    def chunk_step(c, carry):
        off = pl.multiple_of(c * KV_CHUNK, KV_CHUNK)
        off_after = pl.multiple_of(jnp.minimum(c + 1, n_chunks - 1) * KV_CHUNK, KV_CHUNK)
        all_units([s_ref[j] for j in range(look)], k_ref[pl.ds(off, KV_CHUNK), :],
                  vt_ref[:, pl.ds(off, KV_CHUNK)], N_META + off, None, k_ref[pl.ds(off_after, KV_CHUNK), :])
        return carry

    for j in range(look):
        s_ref[j] = scores(k_ref[0:KV_CHUNK, :], *units[j])
    lax.fori_loop(0, n_chunks, chunk_step, 0)
    k_tail = k_ref[n_tok:n_tok + TAIL, :]
    all_units([scores(k_tail, *units[j]) for j in range(look)], k_tail, vt_ref[:, n_tok:n_tok + TAIL],
              0, N_META, None)

    lv = lam_ref[...]
    lam = (jnp.exp(jnp.sum(lv[0:1] * lv[1:2], axis=1, keepdims=True))
           - jnp.exp(jnp.sum(lv[2:3] * lv[3:4], axis=1, keepdims=True)) + lambda_init)
    for u in range(qsub):
        rows = slice(u * tqs, (u + 1) * tqs)
        o = (_normalized(acc_ref[2 * u]) - lam * _normalized(acc_ref[2 * u + 1])).T
        ms = jnp.mean(o * o, axis=-1, keepdims=True)
        o = ((o * lax.rsqrt(ms + NORM_EPS)) * gsub_ref[...]) * (1.0 - lambda_init)
        o_ref[rows, :] = (o * gate_ref[rows, :]).astype(o_ref.dtype)


def _attention(body, q, q_col0, q_w, k, k_col0, k_w, vt, v_row0, gate, gate_col0, extras, extra_specs,
               out_w_total, heads, group, qsub, n_units, n_tok, name):
    bsz, lp, _ = q.shape
    out_w = group * HEAD_DIM

    def call(qsub_, tqs, n_qblk, qblk0, prev):
        tq = qsub_ * tqs
        in_specs = [
            pl.BlockSpec((None, tq, q_w), lambda b, h, i: (b, qblk0 + i, q_col0 + h)),
            pl.BlockSpec((None, lp, k_w), lambda b, h, i: (b, 0, k_col0 + h)),
            pl.BlockSpec((HEAD_DIM, lp), lambda b, h, i: (v_row0 + h, b)),
            pl.BlockSpec((None, tq, out_w), lambda b, h, i: (b, qblk0 + i, gate_col0 + h)),
        ] + extra_specs
        args = [q, k, vt, gate] + extras
        aliases = {}
        if prev is not None:
            in_specs.append(pl.BlockSpec(memory_space=pl.ANY))
            args.append(prev)
            aliases = {len(args) - 1: 0}
        units = n_units * qsub_
        return pl.pallas_call(
            body(qsub_, tqs, prev is not None),
            grid=(bsz, heads, n_qblk),
            in_specs=in_specs,
            out_specs=pl.BlockSpec((None, tq, out_w), lambda b, h, i: (b, qblk0 + i, h)),
            out_shape=jax.ShapeDtypeStruct((bsz, lp, out_w_total), BF16),
            scratch_shapes=[pltpu.VMEM((units, 1, tqs), F32),
                            pltpu.VMEM((units, HEAD_DIM + ONES_ROWS, tqs), F32),
                            pltpu.VMEM((LOOKAHEAD, KV_CHUNK, tqs), F32)],
            input_output_aliases=aliases,
            compiler_params=_cparams(3),
            name=name + ("_meta" if prev is not None else ""),
        )(*args)

    main = call(qsub, Q_SUB, n_tok // (qsub * Q_SUB), 0, None)
    return call(1, TAIL, 1, n_tok // TAIL, main)


def _attn_a(qk, vt_plain, gates, n_tok):
    def body(qsub_, tqs, is_meta):
        return functools.partial(_attn_ab_body, group=A_GROUP, qsub=qsub_, tqs=tqs, dk=HEAD_DIM, n_tok=n_tok)

    return _attention(body, qk, 0, A_GROUP * HEAD_DIM, qk, A_WIDTH // HEAD_DIM, HEAD_DIM, vt_plain, 0,
                      gates, 0, [], [], A_WIDTH, A_KV_HEADS, A_GROUP, 1, A_GROUP, n_tok, "attn_gqa")


def _attn_b(qb, kb, vt_b, gates, n_tok):
    dk = 2 * LANES

    def body(qsub_, tqs, is_meta):
        return functools.partial(_attn_ab_body, group=1, qsub=qsub_, tqs=tqs, dk=dk, n_tok=n_tok)

    return _attention(body, qb, 0, dk, kb, 0, dk, vt_b, 0, gates, A_WIDTH // HEAD_DIM, [], [],
                      B_WIDTH, B_HEADS, 1, 2, 1, n_tok, "attn_mla")


def _attn_c(cqk, vt_plain, gates, lamvec, slopes, g_sub, lambda_init, n_tok):
    dk = 2 * LANES

    def body(qsub_, tqs, is_meta):
        return functools.partial(_attn_c_body, qsub=qsub_, tqs=tqs, n_tok=n_tok, q_is_meta=is_meta,
                                 lambda_init=lambda_init)

    extras = [lamvec, slopes, g_sub.reshape(1, C_VDIM)]
    extra_specs = [pl.BlockSpec((4, C_QK_DIM), lambda b, h, i: (0, 0)),
                   pl.BlockSpec((None, 1, LANES), lambda b, h, i: (h, 0, 0)),
                   pl.BlockSpec((1, C_VDIM), lambda b, h, i: (0, 0))]
    return _attention(body, cqk, 0, dk, cqk, C_HEADS, dk, vt_plain, A_KV_HEADS, gates,
                      (A_WIDTH + B_WIDTH) // HEAD_DIM, extras, extra_specs,
                      C_WIDTH, C_HEADS, 1, 2, 2, n_tok, "attn_diff")


def _out_proj_body(a_ref, b_ref, c_ref, wa_ref, wb_ref, wc_ref, h_ref, o_ref):
    acc = jnp.dot(a_ref[...], wa_ref[...], preferred_element_type=F32)
    acc += jnp.dot(b_ref[...], wb_ref[...], preferred_element_type=F32)
    acc += jnp.dot(c_ref[...], wc_ref[...], preferred_element_type=F32)
    o_ref[...] = h_ref[...] + acc


def _out_proj(oa, ob, oc, w, h2d):
    rows = h2d.shape[0]
    tm = _row_tile(rows)
    tn = 512
    return pl.pallas_call(
        _out_proj_body,
        grid=(rows // tm, D_MODEL // tn),
        in_specs=[pl.BlockSpec((tm, A_WIDTH), lambda i, j: (i, 0)),
                  pl.BlockSpec((tm, B_WIDTH), lambda i, j: (i, 0)),
                  pl.BlockSpec((tm, C_WIDTH), lambda i, j: (i, 0)),
                  pl.BlockSpec((A_WIDTH, tn), lambda i, j: (0, j)),
                  pl.BlockSpec((B_WIDTH, tn), lambda i, j: (A_WIDTH // B_WIDTH, j)),
                  pl.BlockSpec((C_WIDTH, tn), lambda i, j: ((A_WIDTH + B_WIDTH) // C_WIDTH, j)),
                  pl.BlockSpec((tm, tn), lambda i, j: (i, j))],
        out_specs=pl.BlockSpec((tm, tn), lambda i, j: (i, j)),
        out_shape=jax.ShapeDtypeStruct((rows, D_MODEL), F32),
        compiler_params=_cparams(2),
        name="out_proj",
    )(oa, ob, oc, w, w, w, h2d)


def _rope_angles(pos_f, n_freq):
    inv = ROPE_THETA ** (-jnp.arange(n_freq, dtype=F32) / n_freq)
    return pos_f[:, None] * inv[None, :]


def _position_tables(n_tok, bsz):
    rows = n_tok // GRID_W
    z = jnp.zeros((TAIL,), F32)
    row_f = jnp.concatenate([jnp.repeat(jnp.arange(rows, dtype=F32), GRID_W), z])
    col_f = jnp.concatenate([jnp.tile(jnp.arange(GRID_W, dtype=F32), rows), z])
    pos_f = jnp.concatenate([jnp.arange(n_tok, dtype=F32) + N_META,
                             jnp.arange(N_META, dtype=F32), jnp.zeros((TAIL - N_META,), F32)])
    ang_a = jnp.concatenate([_rope_angles(row_f, HEAD_DIM // 4), _rope_angles(col_f, HEAD_DIM // 4)], axis=-1)
    cos_a, sin_a = jnp.cos(ang_a), jnp.sin(ang_a)
    cos_a = jnp.concatenate([cos_a, cos_a], axis=-1)
    sin_a = jnp.concatenate([-sin_a, sin_a], axis=-1)
    ang_b = _rope_angles(pos_f, B_ROPE // 2)
    cos_b, sin_b = jnp.cos(ang_b), jnp.sin(ang_b)
    zb = jnp.zeros_like(cos_b)
    cos_b128 = jnp.concatenate([cos_b, cos_b, zb, zb], axis=-1)
    sin_lo = jnp.concatenate([-sin_b, zb, zb, zb], axis=-1)
    sin_hi = jnp.concatenate([zb, sin_b, zb, zb], axis=-1)
    return tuple(jnp.tile(t, (bsz, 1)) for t in (cos_a, sin_a, cos_b128, sin_lo, sin_hi))


def _prep_weights(w_in, w_q_b, w_kv_b, w_out, g_qn, g_kn):
    bounds = [0]
    for s in IN_SPLITS:
        bounds.append(bounds[-1] + s)
    col = lambda i: slice(bounds[i], bounds[i + 1])
    w = w_in.astype(BF16)
    aq, ak, av, ag, bq, bkv, bpe, bg, cq, ck, cv, cg = (w[:, col(i)] for i in range(12))
    w_aqk = jnp.concatenate([aq, ak], axis=1)
    w_plain = jnp.concatenate([av, cv], axis=1)
    w_gate = jnp.concatenate([ag, bg, cg], axis=1)
    w_cqk = jnp.concatenate([cq, ck], axis=1)
    w_blow = jnp.concatenate([bq, bkv], axis=1)
    w_bpe = jnp.concatenate([bpe, jnp.zeros((D_MODEL, LANES - B_ROPE), BF16)], axis=1)
    g_aqk = jnp.concatenate([jnp.tile(g_qn, A_HEADS), jnp.tile(g_kn, A_KV_HEADS)]).reshape(1, -1)
    m_aqk = jnp.concatenate([jnp.full((A_WIDTH,), HEAD_DIM ** -0.5 * LOG2E, F32),
                             jnp.ones((A_KV_WIDTH,), F32)]).reshape(1, -1)
    wq = w_q_b.astype(BF16).reshape(B_Q_RANK, B_HEADS, B_NOPE + B_ROPE)
    wq = jnp.concatenate([wq, jnp.zeros((B_Q_RANK, B_HEADS, 2 * LANES - B_NOPE - B_ROPE), BF16)], axis=-1)
    wq = wq.reshape(B_Q_RANK, B_HEADS * 2 * LANES)
    wkv = w_kv_b.astype(BF16).reshape(B_KV_RANK, B_HEADS, B_NOPE + B_VDIM)
    wkv = jnp.concatenate([wkv[:, :, :B_NOPE].reshape(B_KV_RANK, -1),
                           wkv[:, :, B_NOPE:].reshape(B_KV_RANK, -1)], axis=1)
    return dict(aqk=w_aqk, plain=w_plain, gate=w_gate, cqk=w_cqk, blow=w_blow, bpe=w_bpe,
                g_aqk=g_aqk, m_aqk=m_aqk, q_up=wq, kv_up=wkv, out=w_out.astype(BF16))


def _encode(x, meta, layers, g_final, slopes, cq_mult):
    bsz, n_tok, _ = x.shape
    lp = n_tok + TAIL
    rows = bsz * lp
    tail = jnp.concatenate([meta.astype(x.dtype), jnp.zeros((TAIL - N_META, D_MODEL), x.dtype)], axis=0)
    h = jnp.concatenate([x, jnp.broadcast_to(tail[None], (bsz, TAIL, D_MODEL))], axis=1).reshape(rows, D_MODEL)
    cos_a, sin_a, cos_b, sin_lo, sin_hi = _position_tables(n_tok, bsz)
    three = lambda a: a.reshape(bsz, lp, a.shape[-1])

    for l, p in enumerate(layers):
        lambda_init = 0.8 - 0.6 * math.exp(-0.3 * l)
        u = _rmsnorm_rows(h, p["g_attn"], BF16)
        aqk = _proj(_proj_aqk_body, u, p["aqk"], [p["g_aqk"], p["m_aqk"], cos_a, sin_a],
                    lambda tm, tn: [_col_vec_spec(tm, tn), _col_vec_spec(tm, tn),
                                    _row_table_spec(tm, tn), _row_table_spec(tm, tn)],
                    A_WIDTH + A_KV_WIDTH, BF16, 512, name="proj_gqa_qk")
        vt_plain = _proj(_proj_transposed_body, u, p["plain"], [], lambda tm, tn: [],
                         A_KV_WIDTH + C_WIDTH, BF16, 512, transposed=True, name="proj_v")
        gates = _proj(_proj_gate_body, u, p["gate"], [], lambda tm, tn: [],
                      A_WIDTH + B_WIDTH + C_WIDTH, F32, 512, name="proj_gate")
        cqk = _proj(_proj_cqk_body, u, p["cqk"], [cq_mult], lambda tm, tn: [_col_vec_spec(tm, tn)],
                    4 * C_QK_WIDTH, BF16, 512, out_tn=1024, name="proj_diff_qk")
        blow = _proj(_proj_plain_body, u, p["blow"], [], lambda tm, tn: [],
                     B_Q_RANK + B_KV_RANK, F32, 512, name="proj_mla_low")
        kpe = _proj(_proj_bpe_body, u, p["bpe"], [cos_b, sin_lo, sin_hi],
                    lambda tm, tn: [_row_table_spec(tm, tn)] * 3, LANES, BF16, LANES, name="proj_mla_pe")
        qb = _bq_up(blow, p["g_q_a"], p["q_up"], cos_b, sin_lo, sin_hi)
        kb, vt_b = _bkv_up(blow, p["g_kv_a"], p["kv_up"], kpe)

        gates3 = three(gates)
        oa = _attn_a(three(aqk), vt_plain, gates3, n_tok)
        ob = _attn_b(three(qb), three(kb), vt_b, gates3, n_tok)
        oc = _attn_c(three(cqk), vt_plain, gates3, p["lamvec"], slopes, p["g_sub"], lambda_init, n_tok)
        h = _out_proj(oa.reshape(rows, -1), ob.reshape(rows, -1), oc.reshape(rows, -1), p["out"], h)

    return _final_norm(h.reshape(bsz, lp, D_MODEL), g_final, n_tok)


def kernel(x_prompt, x_sample, meta, g_attn, w_in, g_qn, g_kn, g_q_a, w_q_b, g_kv_a, w_kv_b,
           lam_q1, lam_k1, lam_q2, lam_k2, g_sub, w_out, g_final):
    layers = []
    for l in range(DEPTH):
        p = _prep_weights(w_in[l], w_q_b[l], w_kv_b[l], w_out[l], g_qn[l], g_kn[l])
        p.update(g_attn=g_attn[l], g_q_a=g_q_a[l], g_kv_a=g_kv_a[l], g_sub=g_sub[l],
                 lamvec=jnp.stack([lam_q1[l], lam_k1[l], lam_q2[l], lam_k2[l]]).astype(F32))
        layers.append(p)
    slopes = 2.0 ** (-(jnp.arange(C_HEADS, dtype=F32) + 1.0) * 8.0 / C_HEADS) * LOG2E
    slopes = jnp.broadcast_to(slopes[:, None, None], (C_HEADS, 1, LANES))
    cq_mult = jnp.concatenate([jnp.full((C_QK_WIDTH,), C_QK_DIM ** -0.5 * LOG2E, F32),
                               jnp.ones((C_QK_WIDTH,), F32)]).reshape(1, -1)
    y_prompt = _encode(x_prompt, meta, layers, g_final, slopes, cq_mult)
    y_sample = _encode(x_sample, meta, layers, g_final, slopes, cq_mult)
    return (y_prompt, y_sample)
```

```python
import functools
import math

import jax
import jax.numpy as jnp
from jax import lax
from jax.experimental import pallas as pl
from jax.experimental.pallas import tpu as pltpu

D_MODEL = 4096
DEPTH = 2
N_META = 16
GRID_W = 64
HEAD_DIM = 128
ROPE_THETA = 10000.0
NORM_EPS = 1e-6
A_HEADS = 16
A_KV_HEADS = 4
A_GROUP = A_HEADS // A_KV_HEADS
A_WIDTH = A_HEADS * HEAD_DIM
A_KV_WIDTH = A_KV_HEADS * HEAD_DIM
B_HEADS = 8
B_Q_RANK = 1024
B_KV_RANK = 512
B_NOPE = 128
B_ROPE = 64
B_VDIM = 128
B_WIDTH = B_HEADS * B_VDIM
C_HEADS = 8
C_QK_DIM = 64
C_VDIM = 2 * C_QK_DIM
C_QK_WIDTH = C_HEADS * 2 * C_QK_DIM
C_WIDTH = C_HEADS * C_VDIM
IN_SPLITS = (A_WIDTH, A_KV_WIDTH, A_KV_WIDTH, A_WIDTH,
             B_Q_RANK, B_KV_RANK, B_ROPE, B_WIDTH,
             C_QK_WIDTH, C_QK_WIDTH, C_WIDTH, C_WIDTH)

LANES = 128
TAIL = LANES
KV_CHUNK = 512
Q_SUB = 256
NEG_BIG = -1e30
LOG2E = 1.4426950408889634
LOOKAHEAD = 2
ONES_ROWS = 16
BF16 = jnp.bfloat16
F32 = jnp.float32
VMEM_LIMIT = 56 * 1024 * 1024

_NT = (((1,), (1,)), ((), ()))


def _cparams(n_axes):
    return pltpu.CompilerParams(dimension_semantics=("arbitrary",) * n_axes,
                                vmem_limit_bytes=VMEM_LIMIT)


def _row_tile(rows):
    for t in (512, 640, 256, 128):
        if rows % t == 0:
            return t
    raise ValueError(f"no row tile for {rows}")


def _rmsnorm_body(x_ref, g_ref, o_ref):
    x = x_ref[...]
    ms = jnp.mean(x * x, axis=-1, keepdims=True)
    o_ref[...] = ((x * lax.rsqrt(ms + NORM_EPS)) * g_ref[...]).astype(o_ref.dtype)


def _rmsnorm_rows(h2d, g, out_dtype):
    rows, d = h2d.shape
    tm = 256 if rows % 256 == 0 else 128
    return pl.pallas_call(
        _rmsnorm_body,
        grid=(rows // tm,),
        in_specs=[pl.BlockSpec((tm, d), lambda i: (i, 0)),
                  pl.BlockSpec((1, d), lambda i: (0, 0))],
        out_specs=pl.BlockSpec((tm, d), lambda i: (i, 0)),
        out_shape=jax.ShapeDtypeStruct((rows, d), out_dtype),
        compiler_params=_cparams(1),
        name="rmsnorm",
    )(h2d, g.reshape(1, d))


def _final_norm(h3d, g, n_tok):
    bsz, _, d = h3d.shape
    tm = 256
    return pl.pallas_call(
        _rmsnorm_body,
        grid=(bsz, n_tok // tm),
        in_specs=[pl.BlockSpec((None, tm, d), lambda b, i: (b, i, 0)),
                  pl.BlockSpec((1, d), lambda b, i: (0, 0))],
        out_specs=pl.BlockSpec((None, tm, d), lambda b, i: (b, i, 0)),
        out_shape=jax.ShapeDtypeStruct((bsz, n_tok, d), F32),
        compiler_params=_cparams(2),
        name="final_norm",
    )(h3d, g.reshape(1, d))


def _rope_pairs_64(y, cos, sin_signed):
    return y * cos + pltpu.roll(y, 64, 1) * sin_signed


def _rope_pairs_32(x, cos, sin_lo, sin_hi):
    return x * cos + pltpu.roll(x, 96, 1) * sin_lo + pltpu.roll(x, 32, 1) * sin_hi


def _proj_aqk_body(x_ref, w_ref, g_ref, mult_ref, cos_ref, sin_ref, o_ref):
    acc = jnp.dot(x_ref[...], w_ref[...], preferred_element_type=F32)
    cos = cos_ref[...]
    sin = sin_ref[...]
    for h in range(acc.shape[1] // HEAD_DIM):
        sl = slice(h * HEAD_DIM, (h + 1) * HEAD_DIM)
        a = acc[:, sl]
        ms = jnp.mean(a * a, axis=-1, keepdims=True)
        y = (a * lax.rsqrt(ms + NORM_EPS)) * g_ref[:, sl]
        o_ref[:, sl] = (_rope_pairs_64(y, cos, sin) * mult_ref[:, sl]).astype(o_ref.dtype)


def _proj_plain_body(x_ref, w_ref, o_ref):
    o_ref[...] = jnp.dot(x_ref[...], w_ref[...], preferred_element_type=F32).astype(o_ref.dtype)


def _proj_transposed_body(x_ref, w_ref, o_ref):
    o_ref[...] = jnp.dot(x_ref[...], w_ref[...], preferred_element_type=F32).T.astype(o_ref.dtype)


def _proj_gate_body(x_ref, w_ref, o_ref):
    acc = jnp.dot(x_ref[...], w_ref[...], preferred_element_type=F32)
    o_ref[...] = acc * (1.0 / (1.0 + jnp.exp(-acc)))


def _proj_cqk_body(x_ref, w_ref, mult_ref, o_ref):
    acc = jnp.dot(x_ref[...], w_ref[...], preferred_element_type=F32) * mult_ref[...]
    lane = lax.broadcasted_iota(jnp.int32, (acc.shape[0], LANES), 1)
    low = lane < C_QK_DIM
    for t in range(acc.shape[1] // LANES):
        a = acc[:, t * LANES:(t + 1) * LANES]
        o_ref[:, (2 * t) * LANES:(2 * t + 1) * LANES] = jnp.where(low, a, 0.0).astype(o_ref.dtype)
        o_ref[:, (2 * t + 1) * LANES:(2 * t + 2) * LANES] = (
            jnp.where(low, pltpu.roll(a, 64, 1), 0.0).astype(o_ref.dtype))


def _proj_bpe_body(x_ref, w_ref, cos_ref, slo_ref, shi_ref, o_ref):
    acc = jnp.dot(x_ref[...], w_ref[...], preferred_element_type=F32)
    o_ref[...] = _rope_pairs_32(acc, cos_ref[...], slo_ref[...], shi_ref[...]).astype(o_ref.dtype)


def _proj(body, u, w, extras, extra_specs, n_out, out_dtype, tn, out_tn=None, transposed=False, name="proj"):
    rows, k = u.shape
    n = w.shape[1]
    tm = _row_tile(rows)
    out_tn = tn if out_tn is None else out_tn
    if transposed:
        out_specs = pl.BlockSpec((out_tn, tm), lambda i, j: (j, i))
        out_shape = jax.ShapeDtypeStruct((n_out, rows), out_dtype)
    else:
        out_specs = pl.BlockSpec((tm, out_tn), lambda i, j: (i, j))
        out_shape = jax.ShapeDtypeStruct((rows, n_out), out_dtype)
    return pl.pallas_call(
        body,
        grid=(rows // tm, n // tn),
        in_specs=[pl.BlockSpec((tm, k), lambda i, j: (i, 0)),
                  pl.BlockSpec((k, tn), lambda i, j: (0, j))] + extra_specs(tm, tn),
        out_specs=out_specs,
        out_shape=out_shape,
        compiler_params=_cparams(2),
        name=name,
    )(u, w, *extras)


def _row_table_spec(tm, tn):
    return pl.BlockSpec((tm, LANES), lambda i, j: (i, 0))


def _col_vec_spec(tm, tn):
    return pl.BlockSpec((1, tn), lambda i, j: (0, j))


def _bq_up_body(c_ref, g_ref, w_ref, cos_ref, slo_ref, shi_ref, o_ref):
    c = c_ref[...]
    ms = jnp.mean(c * c, axis=-1, keepdims=True)
    u = ((c * lax.rsqrt(ms + NORM_EPS)) * g_ref[...]).astype(BF16)
    acc = jnp.dot(u, w_ref[...], preferred_element_type=F32) * ((B_NOPE + B_ROPE) ** -0.5 * LOG2E)
    cos, slo, shi = cos_ref[...], slo_ref[...], shi_ref[...]
    for h in range(B_HEADS):
        base = h * 2 * LANES
        o_ref[:, base:base + LANES] = acc[:, base:base + LANES].astype(o_ref.dtype)
        x = acc[:, base + LANES:base + 2 * LANES]
        o_ref[:, base + LANES:base + 2 * LANES] = _rope_pairs_32(x, cos, slo, shi).astype(o_ref.dtype)


def _bkv_up_body(c_ref, g_ref, w_ref, kpe_ref, k_ref, v_ref):
    c = c_ref[...]
    ms = jnp.mean(c * c, axis=-1, keepdims=True)
    u = ((c * lax.rsqrt(ms + NORM_EPS)) * g_ref[...]).astype(BF16)
    acc = jnp.dot(u, w_ref[...], preferred_element_type=F32)
    kpe = kpe_ref[...]
    for h in range(B_HEADS):
        base = h * 2 * LANES
        k_ref[:, base:base + LANES] = acc[:, h * LANES:(h + 1) * LANES].astype(k_ref.dtype)
        k_ref[:, base + LANES:base + 2 * LANES] = kpe
    v_ref[...] = acc[:, B_HEADS * LANES:].T.astype(v_ref.dtype)


def _bq_up(cb, g, w, cos, slo, shi):
    rows = cb.shape[0]
    tm = _row_tile(rows)
    n = w.shape[1]
    tab = pl.BlockSpec((tm, LANES), lambda i: (i, 0))
    return pl.pallas_call(
        _bq_up_body,
        grid=(rows // tm,),
        in_specs=[pl.BlockSpec((tm, B_Q_RANK), lambda i: (i, 0)),
                  pl.BlockSpec((1, B_Q_RANK), lambda i: (0, 0)),
                  pl.BlockSpec((B_Q_RANK, n), lambda i: (0, 0)),
                  tab, tab, tab],
        out_specs=pl.BlockSpec((tm, n), lambda i: (i, 0)),
        out_shape=jax.ShapeDtypeStruct((rows, n), BF16),
        compiler_params=_cparams(1),
        name="mla_q_up",
    )(cb, g.reshape(1, -1), w, cos, slo, shi)


def _bkv_up(cb, g, w, kpe):
    rows = cb.shape[0]
    tm = _row_tile(rows)
    n = w.shape[1]
    return pl.pallas_call(
        _bkv_up_body,
        grid=(rows // tm,),
        in_specs=[pl.BlockSpec((tm, B_KV_RANK), lambda i: (i, B_Q_RANK // B_KV_RANK)),
                  pl.BlockSpec((1, B_KV_RANK), lambda i: (0, 0)),
                  pl.BlockSpec((B_KV_RANK, n), lambda i: (0, 0)),
                  pl.BlockSpec((tm, LANES), lambda i: (i, 0))],
        out_specs=[pl.BlockSpec((tm, B_HEADS * 2 * LANES), lambda i: (i, 0)),
                   pl.BlockSpec((B_HEADS * B_VDIM, tm), lambda i: (0, i))],
        out_shape=[jax.ShapeDtypeStruct((rows, B_HEADS * 2 * LANES), BF16),
                   jax.ShapeDtypeStruct((B_HEADS * B_VDIM, rows), BF16)],
        compiler_params=_cparams(1),
        name="mla_kv_up",
    )(cb, g.reshape(1, -1), w, kpe)


def _online_softmax_sweep(k_ref, vt_ref, m_ref, acc_ref, s_ref, p_ref, a_ref, n_units, n_tok, scores, make_adjust):
    look = min(LOOKAHEAD, n_units)
    n_chunks = n_tok // KV_CHUNK

    def probabilities(s, idx):
        m_prev = m_ref[idx]
        m_new = jnp.maximum(m_prev, jnp.max(s, axis=0, keepdims=True))
        m_ref[idx] = m_new
        return jnp.exp2(s - m_new).astype(BF16), jnp.exp2(m_prev - m_new)

    def accumulate(idx, p, alpha, vt_ones):
        acc_ref[idx] = alpha * acc_ref[idx] + jnp.dot(vt_ones, p, preferred_element_type=F32)

    def chunk(s_first, kc, vt, k_off, n_valid, kc_after, behind):
        pending = list(s_first)
        vt_ones = _with_ones_rows(vt)
        adjust = make_adjust(k_off, kc.shape[0], n_valid)
        for idx in range(n_units):
            s = pending.pop(0)
            ahead = idx + look
            if ahead < n_units:
                pending.append(scores(kc, ahead))
            elif kc_after is not None:
                s_ref[ahead - n_units] = scores(kc_after, ahead - n_units)
            p, alpha = probabilities(adjust(s, idx), idx)
            if behind is not None:
                accumulate(*behind)
            behind = (idx, p, alpha, vt_ones)
        return behind

    def chunk_step(c, carry):
        off = pl.multiple_of(c * KV_CHUNK, KV_CHUNK)
        off_before = pl.multiple_of(jnp.maximum(c - 1, 0) * KV_CHUNK, KV_CHUNK)
        off_after = pl.multiple_of(jnp.minimum(c + 1, n_chunks - 1) * KV_CHUNK, KV_CHUNK)
        behind = (n_units - 1, p_ref[...], a_ref[...], _with_ones_rows(vt_ref[:, pl.ds(off_before, KV_CHUNK)]))
        _, p, alpha, _ = chunk([s_ref[j] for j in range(look)], k_ref[pl.ds(off, KV_CHUNK), :],
                               vt_ref[:, pl.ds(off, KV_CHUNK)], N_META + off, None,
                               k_ref[pl.ds(off_after, KV_CHUNK), :], behind)
        p_ref[...] = p
        a_ref[...] = alpha
        return carry

    m_ref[...] = jnp.full(m_ref.shape, NEG_BIG, F32)
    acc_ref[...] = jnp.zeros(acc_ref.shape, F32)
    p_ref[...] = jnp.zeros(p_ref.shape, BF16)
    a_ref[...] = jnp.ones(a_ref.shape, F32)
    for j in range(look):
        s_ref[j] = scores(k_ref[0:KV_CHUNK, :], j)
    lax.fori_loop(0, n_chunks, chunk_step, 0)
    accumulate(n_units - 1, p_ref[...], a_ref[...], _with_ones_rows(vt_ref[:, n_tok - KV_CHUNK:n_tok]))
    k_tail = k_ref[n_tok:n_tok + TAIL, :]
    accumulate(*chunk([scores(k_tail, j) for j in range(look)], k_tail, vt_ref[:, n_tok:n_tok + TAIL],
                      0, N_META, None, None))


def _with_ones_rows(vt):
    return jnp.concatenate([vt, jnp.ones((ONES_ROWS, vt.shape[1]), vt.dtype)], axis=0)


def _normalized(acc):
    return acc[:HEAD_DIM] * (1.0 / acc[HEAD_DIM:HEAD_DIM + 1])


def _attn_ab_body(q_ref, k_ref, vt_ref, gate_ref, *rest, group, qsub, tqs, dk, n_tok):
    o_ref, m_ref, acc_ref, s_ref, p_ref, a_ref = rest[-6:]
    units = [(g, u) for g in range(group) for u in range(qsub)]

    def scores(kc, idx):
        g, u = units[idx]
        q = q_ref[u * tqs:(u + 1) * tqs, g * dk:(g + 1) * dk]
        return lax.dot_general(kc, q, _NT, preferred_element_type=F32)

    def make_adjust(k_off, n_keys, n_valid):
        def adjust(s, idx):
            if n_valid is None:
                return s
            row = lax.broadcasted_iota(jnp.int32, s.shape, 0)
            return jnp.where(row < n_valid, s, NEG_BIG)
        return adjust

    _online_softmax_sweep(k_ref, vt_ref, m_ref, acc_ref, s_ref, p_ref, a_ref, len(units), n_tok, scores, make_adjust)

    for idx, (g, u) in enumerate(units):
        rows = slice(u * tqs, (u + 1) * tqs)
        cols = slice(g * HEAD_DIM, (g + 1) * HEAD_DIM)
        o = _normalized(acc_ref[idx]).T
        o_ref[rows, cols] = (o * gate_ref[rows, cols]).astype(o_ref.dtype)


def _attn_c_body(q_ref, k_ref, vt_ref, gate_ref, lam_ref, slope_ref, gsub_ref, *rest,
                 qsub, tqs, n_tok, q_is_meta, lambda_init):
    o_ref, m_ref, acc_ref, s_ref, p_ref, a_ref = rest[-6:]
    slope = slope_ref[:, :1]
    q_off = 0 if q_is_meta else N_META + pl.program_id(2) * (qsub * tqs)
    units = [(u, c) for u in range(qsub) for c in range(2)]

    def scores(kc, idx):
        u, c = units[idx]
        q = q_ref[u * tqs:(u + 1) * tqs, c * LANES:(c + 1) * LANES]
        return lax.dot_general(kc[:, c * LANES:(c + 1) * LANES], q, _NT, preferred_element_type=F32)

    def make_adjust(k_off, n_keys, n_valid):
        key = lax.broadcasted_iota(jnp.int32, (n_keys, tqs), 0)
        qry = lax.broadcasted_iota(jnp.int32, (n_keys, tqs), 1)
        bias = {}

        def adjust(s, idx):
            u, _ = units[idx]
            if u not in bias:
                bias[u] = slope * jnp.abs(((key - qry) + (k_off - q_off - u * tqs)).astype(F32))
            s = s - bias[u]
            if n_valid is not None:
                s = jnp.where(key < n_valid, s, NEG_BIG)
            return s
        return adjust

    _online_softmax_sweep(k_ref, vt_ref, m_ref, acc_ref, s_ref, p_ref, a_ref, len(units), n_tok, scores, make_adjust)

    lv = lam_ref[...]
    lam = (jnp.exp(jnp.sum(lv[0:1] * lv[1:2], axis=1, keepdims=True))
           - jnp.exp(jnp.sum(lv[2:3] * lv[3:4], axis=1, keepdims=True)) + lambda_init)
    for u in range(qsub):
        rows = slice(u * tqs, (u + 1) * tqs)
        o = (_normalized(acc_ref[2 * u]) - lam * _normalized(acc_ref[2 * u + 1])).T
        ms = jnp.mean(o * o, axis=-1, keepdims=True)
        o = ((o * lax.rsqrt(ms + NORM_EPS)) * gsub_ref[...]) * (1.0 - lambda_init)
        o_ref[rows, :] = (o * gate_ref[rows, :]).astype(o_ref.dtype)


def _attention(body, q, q_col0, q_w, k, k_col0, k_w, vt, v_row0, gate, gate_col0, extras, extra_specs,
               out_w_total, heads, group, qsub, n_units, n_tok, name):
    bsz, lp, _ = q.shape
    out_w = group * HEAD_DIM

    def call(qsub_, tqs, n_qblk, qblk0, prev):
        tq = qsub_ * tqs
        in_specs = [
            pl.BlockSpec((None, tq, q_w), lambda b, h, i: (b, qblk0 + i, q_col0 + h)),
            pl.BlockSpec((None, lp, k_w), lambda b, h, i: (b, 0, k_col0 + h)),
            pl.BlockSpec((HEAD_DIM, lp), lambda b, h, i: (v_row0 + h, b)),
            pl.BlockSpec((None, tq, out_w), lambda b, h, i: (b, qblk0 + i, gate_col0 + h)),
        ] + extra_specs
        args = [q, k, vt, gate] + extras
        aliases = {}
        if prev is not None:
            in_specs.append(pl.BlockSpec(memory_space=pl.ANY))
            args.append(prev)
            aliases = {len(args) - 1: 0}
        units = n_units * qsub_
        return pl.pallas_call(
            body(qsub_, tqs, prev is not None),
            grid=(bsz, heads, n_qblk),
            in_specs=in_specs,
            out_specs=pl.BlockSpec((None, tq, out_w), lambda b, h, i: (b, qblk0 + i, h)),
            out_shape=jax.ShapeDtypeStruct((bsz, lp, out_w_total), BF16),
            scratch_shapes=[pltpu.VMEM((units, 1, tqs), F32),
                            pltpu.VMEM((units, HEAD_DIM + ONES_ROWS, tqs), F32),
                            pltpu.VMEM((LOOKAHEAD, KV_CHUNK, tqs), F32),
                            pltpu.VMEM((KV_CHUNK, tqs), BF16), pltpu.VMEM((1, tqs), F32)],
            input_output_aliases=aliases,
            compiler_params=_cparams(3),
            name=name + ("_meta" if prev is not None else ""),
        )(*args)

    main = call(qsub, Q_SUB, n_tok // (qsub * Q_SUB), 0, None)
    return call(1, TAIL, 1, n_tok // TAIL, main)


def _attn_a(qk, vt_plain, gates, n_tok):
    def body(qsub_, tqs, is_meta):
        return functools.partial(_attn_ab_body, group=A_GROUP, qsub=qsub_, tqs=tqs, dk=HEAD_DIM, n_tok=n_tok)

    return _attention(body, qk, 0, A_GROUP * HEAD_DIM, qk, A_WIDTH // HEAD_DIM, HEAD_DIM, vt_plain, 0,
                      gates, 0, [], [], A_WIDTH, A_KV_HEADS, A_GROUP, 1, A_GROUP, n_tok, "attn_gqa")


def _attn_b(qb, kb, vt_b, gates, n_tok):
    dk = 2 * LANES

    def body(qsub_, tqs, is_meta):
        return functools.partial(_attn_ab_body, group=1, qsub=qsub_, tqs=tqs, dk=dk, n_tok=n_tok)

    return _attention(body, qb, 0, dk, kb, 0, dk, vt_b, 0, gates, A_WIDTH // HEAD_DIM, [], [],
                      B_WIDTH, B_HEADS, 1, 2, 1, n_tok, "attn_mla")


def _attn_c(cqk, vt_plain, gates, lamvec, slopes, g_sub, lambda_init, n_tok):
    dk = 2 * LANES

    def body(qsub_, tqs, is_meta):
        return functools.partial(_attn_c_body, qsub=qsub_, tqs=tqs, n_tok=n_tok, q_is_meta=is_meta,
                                 lambda_init=lambda_init)

    extras = [lamvec, slopes, g_sub.reshape(1, C_VDIM)]
    extra_specs = [pl.BlockSpec((4, C_QK_DIM), lambda b, h, i: (0, 0)),
                   pl.BlockSpec((None, 1, LANES), lambda b, h, i: (h, 0, 0)),
                   pl.BlockSpec((1, C_VDIM), lambda b, h, i: (0, 0))]
    return _attention(body, cqk, 0, dk, cqk, C_HEADS, dk, vt_plain, A_KV_HEADS, gates,
                      (A_WIDTH + B_WIDTH) // HEAD_DIM, extras, extra_specs,
                      C_WIDTH, C_HEADS, 1, 2, 2, n_tok, "attn_diff")


def _out_proj_body(a_ref, b_ref, c_ref, wa_ref, wb_ref, wc_ref, h_ref, o_ref):
    acc = jnp.dot(a_ref[...], wa_ref[...], preferred_element_type=F32)
    acc += jnp.dot(b_ref[...], wb_ref[...], preferred_element_type=F32)
    acc += jnp.dot(c_ref[...], wc_ref[...], preferred_element_type=F32)
    o_ref[...] = h_ref[...] + acc


def _out_proj(oa, ob, oc, w, h2d):
    rows = h2d.shape[0]
    tm = _row_tile(rows)
    tn = 512
    return pl.pallas_call(
        _out_proj_body,
        grid=(rows // tm, D_MODEL // tn),
        in_specs=[pl.BlockSpec((tm, A_WIDTH), lambda i, j: (i, 0)),
                  pl.BlockSpec((tm, B_WIDTH), lambda i, j: (i, 0)),
                  pl.BlockSpec((tm, C_WIDTH), lambda i, j: (i, 0)),
                  pl.BlockSpec((A_WIDTH, tn), lambda i, j: (0, j)),
                  pl.BlockSpec((B_WIDTH, tn), lambda i, j: (A_WIDTH // B_WIDTH, j)),
                  pl.BlockSpec((C_WIDTH, tn), lambda i, j: ((A_WIDTH + B_WIDTH) // C_WIDTH, j)),
                  pl.BlockSpec((tm, tn), lambda i, j: (i, j))],
        out_specs=pl.BlockSpec((tm, tn), lambda i, j: (i, j)),
        out_shape=jax.ShapeDtypeStruct((rows, D_MODEL), F32),
        compiler_params=_cparams(2),
        name="out_proj",
    )(oa, ob, oc, w, w, w, h2d)


def _rope_angles(pos_f, n_freq):
    inv = ROPE_THETA ** (-jnp.arange(n_freq, dtype=F32) / n_freq)
    return pos_f[:, None] * inv[None, :]


def _position_tables(n_tok, bsz):
    rows = n_tok // GRID_W
    z = jnp.zeros((TAIL,), F32)
    row_f = jnp.concatenate([jnp.repeat(jnp.arange(rows, dtype=F32), GRID_W), z])
    col_f = jnp.concatenate([jnp.tile(jnp.arange(GRID_W, dtype=F32), rows), z])
    pos_f = jnp.concatenate([jnp.arange(n_tok, dtype=F32) + N_META,
                             jnp.arange(N_META, dtype=F32), jnp.zeros((TAIL - N_META,), F32)])
    ang_a = jnp.concatenate([_rope_angles(row_f, HEAD_DIM // 4), _rope_angles(col_f, HEAD_DIM // 4)], axis=-1)
    cos_a, sin_a = jnp.cos(ang_a), jnp.sin(ang_a)
    cos_a = jnp.concatenate([cos_a, cos_a], axis=-1)
    sin_a = jnp.concatenate([-sin_a, sin_a], axis=-1)
    ang_b = _rope_angles(pos_f, B_ROPE // 2)
    cos_b, sin_b = jnp.cos(ang_b), jnp.sin(ang_b)
    zb = jnp.zeros_like(cos_b)
    cos_b128 = jnp.concatenate([cos_b, cos_b, zb, zb], axis=-1)
    sin_lo = jnp.concatenate([-sin_b, zb, zb, zb], axis=-1)
    sin_hi = jnp.concatenate([zb, sin_b, zb, zb], axis=-1)
    return tuple(jnp.tile(t, (bsz, 1)) for t in (cos_a, sin_a, cos_b128, sin_lo, sin_hi))


def _prep_weights(w_in, w_q_b, w_kv_b, w_out, g_qn, g_kn):
    bounds = [0]
    for s in IN_SPLITS:
        bounds.append(bounds[-1] + s)
    col = lambda i: slice(bounds[i], bounds[i + 1])
    w = w_in.astype(BF16)
    aq, ak, av, ag, bq, bkv, bpe, bg, cq, ck, cv, cg = (w[:, col(i)] for i in range(12))
    w_aqk = jnp.concatenate([aq, ak], axis=1)
    w_plain = jnp.concatenate([av, cv], axis=1)
    w_gate = jnp.concatenate([ag, bg, cg], axis=1)
    w_cqk = jnp.concatenate([cq, ck], axis=1)
    w_blow = jnp.concatenate([bq, bkv], axis=1)
    w_bpe = jnp.concatenate([bpe, jnp.zeros((D_MODEL, LANES - B_ROPE), BF16)], axis=1)
    g_aqk = jnp.concatenate([jnp.tile(g_qn, A_HEADS), jnp.tile(g_kn, A_KV_HEADS)]).reshape(1, -1)
    m_aqk = jnp.concatenate([jnp.full((A_WIDTH,), HEAD_DIM ** -0.5 * LOG2E, F32),
                             jnp.ones((A_KV_WIDTH,), F32)]).reshape(1, -1)
    wq = w_q_b.astype(BF16).reshape(B_Q_RANK, B_HEADS, B_NOPE + B_ROPE)
    wq = jnp.concatenate([wq, jnp.zeros((B_Q_RANK, B_HEADS, 2 * LANES - B_NOPE - B_ROPE), BF16)], axis=-1)
    wq = wq.reshape(B_Q_RANK, B_HEADS * 2 * LANES)
    wkv = w_kv_b.astype(BF16).reshape(B_KV_RANK, B_HEADS, B_NOPE + B_VDIM)
    wkv = jnp.concatenate([wkv[:, :, :B_NOPE].reshape(B_KV_RANK, -1),
                           wkv[:, :, B_NOPE:].reshape(B_KV_RANK, -1)], axis=1)
    return dict(aqk=w_aqk, plain=w_plain, gate=w_gate, cqk=w_cqk, blow=w_blow, bpe=w_bpe,
                g_aqk=g_aqk, m_aqk=m_aqk, q_up=wq, kv_up=wkv, out=w_out.astype(BF16))


def _encode(x, meta, layers, g_final, slopes, cq_mult):
    bsz, n_tok, _ = x.shape
    lp = n_tok + TAIL
    rows = bsz * lp
    tail = jnp.concatenate([meta.astype(x.dtype), jnp.zeros((TAIL - N_META, D_MODEL), x.dtype)], axis=0)
    h = jnp.concatenate([x, jnp.broadcast_to(tail[None], (bsz, TAIL, D_MODEL))], axis=1).reshape(rows, D_MODEL)
    cos_a, sin_a, cos_b, sin_lo, sin_hi = _position_tables(n_tok, bsz)
    three = lambda a: a.reshape(bsz, lp, a.shape[-1])

    for l, p in enumerate(layers):
        lambda_init = 0.8 - 0.6 * math.exp(-0.3 * l)
        u = _rmsnorm_rows(h, p["g_attn"], BF16)
        aqk = _proj(_proj_aqk_body, u, p["aqk"], [p["g_aqk"], p["m_aqk"], cos_a, sin_a],
                    lambda tm, tn: [_col_vec_spec(tm, tn), _col_vec_spec(tm, tn),
                                    _row_table_spec(tm, tn), _row_table_spec(tm, tn)],
                    A_WIDTH + A_KV_WIDTH, BF16, 512, name="proj_gqa_qk")
        vt_plain = _proj(_proj_transposed_body, u, p["plain"], [], lambda tm, tn: [],
                         A_KV_WIDTH + C_WIDTH, BF16, 512, transposed=True, name="proj_v")
        gates = _proj(_proj_gate_body, u, p["gate"], [], lambda tm, tn: [],
                      A_WIDTH + B_WIDTH + C_WIDTH, F32, 512, name="proj_gate")
        cqk = _proj(_proj_cqk_body, u, p["cqk"], [cq_mult], lambda tm, tn: [_col_vec_spec(tm, tn)],
                    4 * C_QK_WIDTH, BF16, 512, out_tn=1024, name="proj_diff_qk")
        blow = _proj(_proj_plain_body, u, p["blow"], [], lambda tm, tn: [],
                     B_Q_RANK + B_KV_RANK, F32, 512, name="proj_mla_low")
        kpe = _proj(_proj_bpe_body, u, p["bpe"], [cos_b, sin_lo, sin_hi],
                    lambda tm, tn: [_row_table_spec(tm, tn)] * 3, LANES, BF16, LANES, name="proj_mla_pe")
        qb = _bq_up(blow, p["g_q_a"], p["q_up"], cos_b, sin_lo, sin_hi)
        kb, vt_b = _bkv_up(blow, p["g_kv_a"], p["kv_up"], kpe)

        gates3 = three(gates)
        oa = _attn_a(three(aqk), vt_plain, gates3, n_tok)
        ob = _attn_b(three(qb), three(kb), vt_b, gates3, n_tok)
        oc = _attn_c(three(cqk), vt_plain, gates3, p["lamvec"], slopes, p["g_sub"], lambda_init, n_tok)
        h = _out_proj(oa.reshape(rows, -1), ob.reshape(rows, -1), oc.reshape(rows, -1), p["out"], h)

    return _final_norm(h.reshape(bsz, lp, D_MODEL), g_final, n_tok)


def kernel(x_prompt, x_sample, meta, g_attn, w_in, g_qn, g_kn, g_q_a, w_q_b, g_kv_a, w_kv_b,
           lam_q1, lam_k1, lam_q2, lam_k2, g_sub, w_out, g_final):
    layers = []
    for l in range(DEPTH):
        p = _prep_weights(w_in[l], w_q_b[l], w_kv_b[l], w_out[l], g_qn[l], g_kn[l])
        p.update(g_attn=g_attn[l], g_q_a=g_q_a[l], g_kv_a=g_kv_a[l], g_sub=g_sub[l],
                 lamvec=jnp.stack([lam_q1[l], lam_k1[l], lam_q2[l], lam_k2[l]]).astype(F32))
        layers.append(p)
    slopes = 2.0 ** (-(jnp.arange(C_HEADS, dtype=F32) + 1.0) * 8.0 / C_HEADS) * LOG2E
    slopes = jnp.broadcast_to(slopes[:, None, None], (C_HEADS, 1, LANES))
    cq_mult = jnp.concatenate([jnp.full((C_QK_WIDTH,), C_QK_DIM ** -0.5 * LOG2E, F32),
                               jnp.ones((C_QK_WIDTH,), F32)]).reshape(1, -1)
    y_prompt = _encode(x_prompt, meta, layers, g_final, slopes, cq_mult)
    y_sample = _encode(x_sample, meta, layers, g_final, slopes, cq_mult)
    return (y_prompt, y_sample)
```

```python
import functools
import math

import jax
import jax.numpy as jnp
from jax import lax
from jax.experimental import pallas as pl
from jax.experimental.pallas import tpu as pltpu

D_MODEL = 4096
DEPTH = 2
N_META = 16
GRID_W = 64
HEAD_DIM = 128
ROPE_THETA = 10000.0
NORM_EPS = 1e-6
A_HEADS = 16
A_KV_HEADS = 4
A_GROUP = A_HEADS // A_KV_HEADS
A_WIDTH = A_HEADS * HEAD_DIM
A_KV_WIDTH = A_KV_HEADS * HEAD_DIM
B_HEADS = 8
B_Q_RANK = 1024
B_KV_RANK = 512
B_NOPE = 128
B_ROPE = 64
B_VDIM = 128
B_WIDTH = B_HEADS * B_VDIM
C_HEADS = 8
C_QK_DIM = 64
C_VDIM = 2 * C_QK_DIM
C_QK_WIDTH = C_HEADS * 2 * C_QK_DIM
C_WIDTH = C_HEADS * C_VDIM
IN_SPLITS = (A_WIDTH, A_KV_WIDTH, A_KV_WIDTH, A_WIDTH,
             B_Q_RANK, B_KV_RANK, B_ROPE, B_WIDTH,
             C_QK_WIDTH, C_QK_WIDTH, C_WIDTH, C_WIDTH)

LANES = 128
TAIL = LANES
KV_CHUNK = 512
Q_SUB = 256
NEG_BIG = -1e30
LOG2E = 1.4426950408889634
LOOKAHEAD = 2
ONES_ROWS = 16
BF16 = jnp.bfloat16
F32 = jnp.float32
VMEM_LIMIT = 56 * 1024 * 1024

_NT = (((1,), (1,)), ((), ()))


def _cparams(n_axes):
    return pltpu.CompilerParams(dimension_semantics=("arbitrary",) * n_axes,
                                vmem_limit_bytes=VMEM_LIMIT)


def _row_tile(rows):
    for t in (512, 640, 256, 128):
        if rows % t == 0:
            return t
    raise ValueError(f"no row tile for {rows}")


def _rmsnorm_body(x_ref, g_ref, o_ref):
    x = x_ref[...]
    ms = jnp.mean(x * x, axis=-1, keepdims=True)
    o_ref[...] = ((x * lax.rsqrt(ms + NORM_EPS)) * g_ref[...]).astype(o_ref.dtype)


def _rmsnorm_rows(h2d, g, out_dtype):
    rows, d = h2d.shape
    tm = 256 if rows % 256 == 0 else 128
    return pl.pallas_call(
        _rmsnorm_body,
        grid=(rows // tm,),
        in_specs=[pl.BlockSpec((tm, d), lambda i: (i, 0)),
                  pl.BlockSpec((1, d), lambda i: (0, 0))],
        out_specs=pl.BlockSpec((tm, d), lambda i: (i, 0)),
        out_shape=jax.ShapeDtypeStruct((rows, d), out_dtype),
        compiler_params=_cparams(1),
        name="rmsnorm",
    )(h2d, g.reshape(1, d))


def _final_norm(h3d, g, n_tok):
    bsz, _, d = h3d.shape
    tm = 256
    return pl.pallas_call(
        _rmsnorm_body,
        grid=(bsz, n_tok // tm),
        in_specs=[pl.BlockSpec((None, tm, d), lambda b, i: (b, i, 0)),
                  pl.BlockSpec((1, d), lambda b, i: (0, 0))],
        out_specs=pl.BlockSpec((None, tm, d), lambda b, i: (b, i, 0)),
        out_shape=jax.ShapeDtypeStruct((bsz, n_tok, d), F32),
        compiler_params=_cparams(2),
        name="final_norm",
    )(h3d, g.reshape(1, d))


def _rope_pairs_64(y, cos, sin_signed):
    return y * cos + pltpu.roll(y, 64, 1) * sin_signed


def _rope_pairs_32(x, cos, sin_lo, sin_hi):
    return x * cos + pltpu.roll(x, 96, 1) * sin_lo + pltpu.roll(x, 32, 1) * sin_hi


def _proj_aqk_body(x_ref, w_ref, g_ref, mult_ref, cos_ref, sin_ref, o_ref):
    acc = jnp.dot(x_ref[...], w_ref[...], preferred_element_type=F32)
    cos = cos_ref[...]
    sin = sin_ref[...]
    for h in range(acc.shape[1] // HEAD_DIM):
        sl = slice(h * HEAD_DIM, (h + 1) * HEAD_DIM)
        a = acc[:, sl]
        ms = jnp.mean(a * a, axis=-1, keepdims=True)
        y = (a * lax.rsqrt(ms + NORM_EPS)) * g_ref[:, sl]
        o_ref[:, sl] = (_rope_pairs_64(y, cos, sin) * mult_ref[:, sl]).astype(o_ref.dtype)


def _proj_plain_body(x_ref, w_ref, o_ref):
    o_ref[...] = jnp.dot(x_ref[...], w_ref[...], preferred_element_type=F32).astype(o_ref.dtype)


def _proj_transposed_body(x_ref, w_ref, o_ref):
    o_ref[...] = jnp.dot(x_ref[...], w_ref[...], preferred_element_type=F32).T.astype(o_ref.dtype)


def _proj_gate_body(x_ref, w_ref, o_ref):
    acc = jnp.dot(x_ref[...], w_ref[...], preferred_element_type=F32)
    o_ref[...] = acc * (1.0 / (1.0 + jnp.exp(-acc)))


def _proj_cqk_body(x_ref, w_ref, mult_ref, qcoef_ref, kcoef_ref, qtab_ref, ktab_ref, o_ref):
    acc = jnp.dot(x_ref[...], w_ref[...], preferred_element_type=F32) * mult_ref[...]
    lane = lax.broadcasted_iota(jnp.int32, (acc.shape[0], LANES), 1)
    low = lane < C_QK_DIM
    qtab = qtab_ref[...]
    ktab = ktab_ref[...]
    for t in range(acc.shape[1] // LANES):
        sl = slice(t * LANES, (t + 1) * LANES)
        a = acc[:, sl]
        pos_terms = qtab * qcoef_ref[:, sl] + ktab * kcoef_ref[:, sl]
        o_ref[:, (2 * t) * LANES:(2 * t + 1) * LANES] = jnp.where(low, a, pos_terms).astype(o_ref.dtype)
        o_ref[:, (2 * t + 1) * LANES:(2 * t + 2) * LANES] = (
            jnp.where(low, pltpu.roll(a, 64, 1), pos_terms).astype(o_ref.dtype))


def _proj_bpe_body(x_ref, w_ref, cos_ref, slo_ref, shi_ref, o_ref):
    acc = jnp.dot(x_ref[...], w_ref[...], preferred_element_type=F32)
    o_ref[...] = _rope_pairs_32(acc, cos_ref[...], slo_ref[...], shi_ref[...]).astype(o_ref.dtype)


def _proj(body, u, w, extras, extra_specs, n_out, out_dtype, tn, out_tn=None, transposed=False, name="proj"):
    rows, k = u.shape
    n = w.shape[1]
    tm = _row_tile(rows)
    out_tn = tn if out_tn is None else out_tn
    if transposed:
        out_specs = pl.BlockSpec((out_tn, tm), lambda i, j: (j, i))
        out_shape = jax.ShapeDtypeStruct((n_out, rows), out_dtype)
    else:
        out_specs = pl.BlockSpec((tm, out_tn), lambda i, j: (i, j))
        out_shape = jax.ShapeDtypeStruct((rows, n_out), out_dtype)
    return pl.pallas_call(
        body,
        grid=(rows // tm, n // tn),
        in_specs=[pl.BlockSpec((tm, k), lambda i, j: (i, 0)),
                  pl.BlockSpec((k, tn), lambda i, j: (0, j))] + extra_specs(tm, tn),
        out_specs=out_specs,
        out_shape=out_shape,
        compiler_params=_cparams(2),
        name=name,
    )(u, w, *extras)


def _row_table_spec(tm, tn):
    return pl.BlockSpec((tm, LANES), lambda i, j: (i, 0))


def _col_vec_spec(tm, tn):
    return pl.BlockSpec((1, tn), lambda i, j: (0, j))


def _bq_up_body(c_ref, g_ref, w_ref, cos_ref, slo_ref, shi_ref, o_ref):
    c = c_ref[...]
    ms = jnp.mean(c * c, axis=-1, keepdims=True)
    u = ((c * lax.rsqrt(ms + NORM_EPS)) * g_ref[...]).astype(BF16)
    acc = jnp.dot(u, w_ref[...], preferred_element_type=F32) * ((B_NOPE + B_ROPE) ** -0.5 * LOG2E)
    cos, slo, shi = cos_ref[...], slo_ref[...], shi_ref[...]
    for h in range(B_HEADS):
        base = h * 2 * LANES
        o_ref[:, base:base + LANES] = acc[:, base:base + LANES].astype(o_ref.dtype)
        x = acc[:, base + LANES:base + 2 * LANES]
        o_ref[:, base + LANES:base + 2 * LANES] = _rope_pairs_32(x, cos, slo, shi).astype(o_ref.dtype)


def _bkv_up_body(c_ref, g_ref, w_ref, kpe_ref, k_ref, v_ref):
    c = c_ref[...]
    ms = jnp.mean(c * c, axis=-1, keepdims=True)
    u = ((c * lax.rsqrt(ms + NORM_EPS)) * g_ref[...]).astype(BF16)
    acc = jnp.dot(u, w_ref[...], preferred_element_type=F32)
    kpe = kpe_ref[...]
    for h in range(B_HEADS):
        base = h * 2 * LANES
        k_ref[:, base:base + LANES] = acc[:, h * LANES:(h + 1) * LANES].astype(k_ref.dtype)
        k_ref[:, base + LANES:base + 2 * LANES] = kpe
    v_ref[...] = acc[:, B_HEADS * LANES:].T.astype(v_ref.dtype)


def _bq_up(cb, g, w, cos, slo, shi):
    rows = cb.shape[0]
    tm = _row_tile(rows)
    n = w.shape[1]
    tab = pl.BlockSpec((tm, LANES), lambda i: (i, 0))
    return pl.pallas_call(
        _bq_up_body,
        grid=(rows // tm,),
        in_specs=[pl.BlockSpec((tm, B_Q_RANK), lambda i: (i, 0)),
                  pl.BlockSpec((1, B_Q_RANK), lambda i: (0, 0)),
                  pl.BlockSpec((B_Q_RANK, n), lambda i: (0, 0)),
                  tab, tab, tab],
        out_specs=pl.BlockSpec((tm, n), lambda i: (i, 0)),
        out_shape=jax.ShapeDtypeStruct((rows, n), BF16),
        compiler_params=_cparams(1),
        name="mla_q_up",
    )(cb, g.reshape(1, -1), w, cos, slo, shi)


def _bkv_up(cb, g, w, kpe):
    rows = cb.shape[0]
    tm = _row_tile(rows)
    n = w.shape[1]
    return pl.pallas_call(
        _bkv_up_body,
        grid=(rows // tm,),
        in_specs=[pl.BlockSpec((tm, B_KV_RANK), lambda i: (i, B_Q_RANK // B_KV_RANK)),
                  pl.BlockSpec((1, B_KV_RANK), lambda i: (0, 0)),
                  pl.BlockSpec((B_KV_RANK, n), lambda i: (0, 0)),
                  pl.BlockSpec((tm, LANES), lambda i: (i, 0))],
        out_specs=[pl.BlockSpec((tm, B_HEADS * 2 * LANES), lambda i: (i, 0)),
                   pl.BlockSpec((B_HEADS * B_VDIM, tm), lambda i: (0, i))],
        out_shape=[jax.ShapeDtypeStruct((rows, B_HEADS * 2 * LANES), BF16),
                   jax.ShapeDtypeStruct((B_HEADS * B_VDIM, rows), BF16)],
        compiler_params=_cparams(1),
        name="mla_kv_up",
    )(cb, g.reshape(1, -1), w, kpe)


def _online_softmax_sweep(k_ref, vt_ref, m_ref, acc_ref, s_ref, p_ref, a_ref, n_units, n_tok, scores, make_adjust,
                          phases=None, first_variant=0, tail_variant=0, tail_fix=False, coef=None):
    look = min(LOOKAHEAD, n_units)
    n_chunks = n_tok // KV_CHUNK
    if phases is None:
        phases = [(0, n_chunks, 0, 0, False)]

    def probabilities(s, idx):
        m_prev = m_ref[idx]
        m_new = jnp.maximum(m_prev, jnp.max(s, axis=0, keepdims=True))
        m_ref[idx] = m_new
        if coef is None:
            return jnp.exp2(s - m_new).astype(BF16), jnp.exp2(m_prev - m_new)
        return jnp.exp2((s - m_new) * coef).astype(BF16), jnp.exp2((m_prev - m_new) * coef)

    def accumulate(idx, p, alpha, vt_ones):
        acc_ref[idx] = alpha * acc_ref[idx] + jnp.dot(vt_ones, p, preferred_element_type=F32)

    def chunk(s_first, kc, vt, k_off, n_valid, kc_after, behind, variant, variant_after, fix):
        pending = list(s_first)
        vt_ones = _with_ones_rows(vt)
        adjust = make_adjust(k_off, kc.shape[0], n_valid, fix)
        for idx in range(n_units):
            s = pending.pop(0)
            ahead = idx + look
            if ahead < n_units:
                pending.append(scores(kc, ahead, variant))
            elif kc_after is not None:
                s_ref[ahead - n_units] = scores(kc_after, ahead - n_units, variant_after)
            p, alpha = probabilities(adjust(s, idx), idx)
            if behind is not None:
                accumulate(*behind)
            behind = (idx, p, alpha, vt_ones)
        return behind

    def chunk_step(c, carry, variant, variant_after, fix):
        off = pl.multiple_of(c * KV_CHUNK, KV_CHUNK)
        off_before = pl.multiple_of(jnp.maximum(c - 1, 0) * KV_CHUNK, KV_CHUNK)
        off_after = pl.multiple_of(jnp.minimum(c + 1, n_chunks - 1) * KV_CHUNK, KV_CHUNK)
        behind = (n_units - 1, p_ref[...], a_ref[...], _with_ones_rows(vt_ref[:, pl.ds(off_before, KV_CHUNK)]))
        _, p, alpha, _ = chunk([s_ref[j] for j in range(look)], k_ref[pl.ds(off, KV_CHUNK), :],
                               vt_ref[:, pl.ds(off, KV_CHUNK)], N_META + off, None,
                               k_ref[pl.ds(off_after, KV_CHUNK), :], behind, variant, variant_after, fix)
        p_ref[...] = p
        a_ref[...] = alpha
        return carry

    m_ref[...] = jnp.full(m_ref.shape, NEG_BIG, F32)
    acc_ref[...] = jnp.zeros(acc_ref.shape, F32)
    p_ref[...] = jnp.zeros(p_ref.shape, BF16)
    a_ref[...] = jnp.ones(a_ref.shape, F32)
    for j in range(look):
        s_ref[j] = scores(k_ref[0:KV_CHUNK, :], j, first_variant)
    for lo, hi, variant, variant_after, fix in phases:
        lax.fori_loop(lo, hi, functools.partial(chunk_step, variant=variant, variant_after=variant_after, fix=fix), 0)
    accumulate(n_units - 1, p_ref[...], a_ref[...], _with_ones_rows(vt_ref[:, n_tok - KV_CHUNK:n_tok]))
    k_tail = k_ref[n_tok:n_tok + TAIL, :]
    accumulate(*chunk([scores(k_tail, j, tail_variant) for j in range(look)], k_tail, vt_ref[:, n_tok:n_tok + TAIL],
                      0, N_META, None, None, tail_variant, tail_variant, tail_fix))


def _with_ones_rows(vt):
    return jnp.concatenate([vt, jnp.ones((ONES_ROWS, vt.shape[1]), vt.dtype)], axis=0)


def _normalized(acc):
    return acc[:HEAD_DIM] * (1.0 / acc[HEAD_DIM:HEAD_DIM + 1])


def _attn_ab_body(q_ref, k_ref, vt_ref, gate_ref, *rest, group, qsub, tqs, dk, n_tok):
    o_ref, m_ref, acc_ref, s_ref, p_ref, a_ref = rest[-6:]
    units = [(g, u) for g in range(group) for u in range(qsub)]

    def scores(kc, idx, variant):
        g, u = units[idx]
        q = q_ref[u * tqs:(u + 1) * tqs, g * dk:(g + 1) * dk]
        return lax.dot_general(kc, q, _NT, preferred_element_type=F32)

    def make_adjust(k_off, n_keys, n_valid, fix):
        def adjust(s, idx):
            if n_valid is None:
                return s
            row = lax.broadcasted_iota(jnp.int32, s.shape, 0)
            return jnp.where(row < n_valid, s, NEG_BIG)
        return adjust

    _online_softmax_sweep(k_ref, vt_ref, m_ref, acc_ref, s_ref, p_ref, a_ref, len(units), n_tok, scores, make_adjust)

    for idx, (g, u) in enumerate(units):
        rows = slice(u * tqs, (u + 1) * tqs)
        cols = slice(g * HEAD_DIM, (g + 1) * HEAD_DIM)
        o = _normalized(acc_ref[idx]).T
        o_ref[rows, cols] = (o * gate_ref[rows, cols]).astype(o_ref.dtype)


def _attn_c_body(q_ref, k_ref, vt_ref, gate_ref, lam_ref, slope_ref, gsub_ref, *rest,
                 qsub, tqs, n_tok, q_is_meta, lambda_init):
    o_ref, m_ref, acc_ref, s_ref, p_ref, a_ref, q2_ref = rest[-7:]
    lane = lax.broadcasted_iota(jnp.int32, (1, 2 * LANES), 1)
    flip = jnp.where((lane & (LANES - 1)) < C_QK_DIM, 1.0, -1.0)
    q2_ref[0] = q_ref[...]
    q2_ref[1] = (q_ref[...].astype(F32) * flip).astype(BF16)
    slope2 = 2.0 * slope_ref[:, :1]
    tq = qsub * tqs
    n_chunks = n_tok // KV_CHUNK
    if q_is_meta:
        q_off = 0
        phases = [(0, n_chunks, 1, 1, False)]
        first_variant, tail_fix = 1, True
    else:
        assert tq == KV_CHUNK
        i = pl.program_id(2)
        q_off = N_META + i * tq
        phases = [(0, i, 0, 0, False), (i, i + 1, 0, 1, True), (i + 1, n_chunks, 1, 1, False)]
        first_variant, tail_fix = 0, False
    units = [(u, c) for u in range(qsub) for c in range(2)]

    def scores(kc, idx, variant):
        u, c = units[idx]
        q = q2_ref[variant, u * tqs:(u + 1) * tqs, c * LANES:(c + 1) * LANES]
        return lax.dot_general(kc[:, c * LANES:(c + 1) * LANES], q, _NT, preferred_element_type=F32)

    def make_adjust(k_off, n_keys, n_valid, fix):
        key = lax.broadcasted_iota(jnp.int32, (n_keys, tqs), 0)
        qry = lax.broadcasted_iota(jnp.int32, (n_keys, tqs), 1)
        late = {}

        def adjust(s, idx):
            u, _ = units[idx]
            if fix:
                if u not in late:
                    ahead = ((key - qry) + (k_off - q_off - u * tqs)).astype(F32)
                    late[u] = slope2 * jnp.maximum(ahead, 0.0)
                s = s - late[u]
            if n_valid is not None:
                s = jnp.where(key < n_valid, s, NEG_BIG)
            return s
        return adjust

    _online_softmax_sweep(k_ref, vt_ref, m_ref, acc_ref, s_ref, p_ref, a_ref, len(units), n_tok, scores, make_adjust,
                          phases=phases, first_variant=first_variant, tail_variant=0, tail_fix=tail_fix, coef=LOG2E)

    lv = lam_ref[...]
    lam = (jnp.exp(jnp.sum(lv[0:1] * lv[1:2], axis=1, keepdims=True))
           - jnp.exp(jnp.sum(lv[2:3] * lv[3:4], axis=1, keepdims=True)) + lambda_init)
    for u in range(qsub):
        rows = slice(u * tqs, (u + 1) * tqs)
        o = (_normalized(acc_ref[2 * u]) - lam * _normalized(acc_ref[2 * u + 1])).T
        ms = jnp.mean(o * o, axis=-1, keepdims=True)
        o = ((o * lax.rsqrt(ms + NORM_EPS)) * gsub_ref[...]) * (1.0 - lambda_init)
        o_ref[rows, :] = (o * gate_ref[rows, :]).astype(o_ref.dtype)


def _attention(body, q, q_col0, q_w, k, k_col0, k_w, vt, v_row0, gate, gate_col0, extras, extra_specs,
               out_w_total, heads, group, qsub, n_units, n_tok, name, extra_scratch=lambda tq: []):
    bsz, lp, _ = q.shape
    out_w = group * HEAD_DIM

    def call(qsub_, tqs, n_qblk, qblk0, prev):
        tq = qsub_ * tqs
        in_specs = [
            pl.BlockSpec((None, tq, q_w), lambda b, h, i: (b, qblk0 + i, q_col0 + h)),
            pl.BlockSpec((None, lp, k_w), lambda b, h, i: (b, 0, k_col0 + h)),
            pl.BlockSpec((HEAD_DIM, lp), lambda b, h, i: (v_row0 + h, b)),
            pl.BlockSpec((None, tq, out_w), lambda b, h, i: (b, qblk0 + i, gate_col0 + h)),
        ] + extra_specs
        args = [q, k, vt, gate] + extras
        aliases = {}
        if prev is not None:
            in_specs.append(pl.BlockSpec(memory_space=pl.ANY))
            args.append(prev)
            aliases = {len(args) - 1: 0}
        units = n_units * qsub_
        return pl.pallas_call(
            body(qsub_, tqs, prev is not None),
            grid=(bsz, heads, n_qblk),
            in_specs=in_specs,
            out_specs=pl.BlockSpec((None, tq, out_w), lambda b, h, i: (b, qblk0 + i, h)),
            out_shape=jax.ShapeDtypeStruct((bsz, lp, out_w_total), BF16),
            scratch_shapes=[pltpu.VMEM((units, 1, tqs), F32),
                            pltpu.VMEM((units, HEAD_DIM + ONES_ROWS, tqs), F32),
                            pltpu.VMEM((LOOKAHEAD, KV_CHUNK, tqs), F32),
                            pltpu.VMEM((KV_CHUNK, tqs), BF16), pltpu.VMEM((1, tqs), F32)] + extra_scratch(tq),
            input_output_aliases=aliases,
            compiler_params=_cparams(3),
            name=name + ("_meta" if prev is not None else ""),
        )(*args)

    main = call(qsub, Q_SUB, n_tok // (qsub * Q_SUB), 0, None)
    return call(1, TAIL, 1, n_tok // TAIL, main)


def _attn_a(qk, vt_plain, gates, n_tok):
    def body(qsub_, tqs, is_meta):
        return functools.partial(_attn_ab_body, group=A_GROUP, qsub=qsub_, tqs=tqs, dk=HEAD_DIM, n_tok=n_tok)

    return _attention(body, qk, 0, A_GROUP * HEAD_DIM, qk, A_WIDTH // HEAD_DIM, HEAD_DIM, vt_plain, 0,
                      gates, 0, [], [], A_WIDTH, A_KV_HEADS, A_GROUP, 1, A_GROUP, n_tok, "attn_gqa")


def _attn_b(qb, kb, vt_b, gates, n_tok):
    dk = 2 * LANES

    def body(qsub_, tqs, is_meta):
        return functools.partial(_attn_ab_body, group=1, qsub=qsub_, tqs=tqs, dk=dk, n_tok=n_tok)

    return _attention(body, qb, 0, dk, kb, 0, dk, vt_b, 0, gates, A_WIDTH // HEAD_DIM, [], [],
                      B_WIDTH, B_HEADS, 1, 2, 1, n_tok, "attn_mla")


def _attn_c(cqk, vt_plain, gates, lamvec, slopes, g_sub, lambda_init, n_tok):
    dk = 2 * LANES

    def body(qsub_, tqs, is_meta):
        return functools.partial(_attn_c_body, qsub=qsub_, tqs=tqs, n_tok=n_tok, q_is_meta=is_meta,
                                 lambda_init=lambda_init)

    extras = [lamvec, slopes, g_sub.reshape(1, C_VDIM)]
    extra_specs = [pl.BlockSpec((4, C_QK_DIM), lambda b, h, i: (0, 0)),
                   pl.BlockSpec((None, 1, LANES), lambda b, h, i: (h, 0, 0)),
                   pl.BlockSpec((1, C_VDIM), lambda b, h, i: (0, 0))]
    return _attention(body, cqk, 0, dk, cqk, C_HEADS, dk, vt_plain, A_KV_HEADS, gates,
                      (A_WIDTH + B_WIDTH) // HEAD_DIM, extras, extra_specs,
                      C_WIDTH, C_HEADS, 1, 2, 2, n_tok, "attn_diff",
                      extra_scratch=lambda tq: [pltpu.VMEM((2, tq, dk), BF16)])


def _out_proj_body(a_ref, b_ref, c_ref, wa_ref, wb_ref, wc_ref, h_ref, o_ref):
    acc = jnp.dot(a_ref[...], wa_ref[...], preferred_element_type=F32)
    acc += jnp.dot(b_ref[...], wb_ref[...], preferred_element_type=F32)
    acc += jnp.dot(c_ref[...], wc_ref[...], preferred_element_type=F32)
    o_ref[...] = h_ref[...] + acc


def _out_proj(oa, ob, oc, w, h2d):
    rows = h2d.shape[0]
    tm = _row_tile(rows)
    tn = 512
    return pl.pallas_call(
        _out_proj_body,
        grid=(rows // tm, D_MODEL // tn),
        in_specs=[pl.BlockSpec((tm, A_WIDTH), lambda i, j: (i, 0)),
                  pl.BlockSpec((tm, B_WIDTH), lambda i, j: (i, 0)),
                  pl.BlockSpec((tm, C_WIDTH), lambda i, j: (i, 0)),
                  pl.BlockSpec((A_WIDTH, tn), lambda i, j: (0, j)),
                  pl.BlockSpec((B_WIDTH, tn), lambda i, j: (A_WIDTH // B_WIDTH, j)),
                  pl.BlockSpec((C_WIDTH, tn), lambda i, j: ((A_WIDTH + B_WIDTH) // C_WIDTH, j)),
                  pl.BlockSpec((tm, tn), lambda i, j: (i, j))],
        out_specs=pl.BlockSpec((tm, tn), lambda i, j: (i, j)),
        out_shape=jax.ShapeDtypeStruct((rows, D_MODEL), F32),
        compiler_params=_cparams(2),
        name="out_proj",
    )(oa, ob, oc, w, w, w, h2d)


def _rope_angles(pos_f, n_freq):
    inv = ROPE_THETA ** (-jnp.arange(n_freq, dtype=F32) / n_freq)
    return pos_f[:, None] * inv[None, :]


def _position_tables(n_tok, bsz):
    rows = n_tok // GRID_W
    z = jnp.zeros((TAIL,), F32)
    row_f = jnp.concatenate([jnp.repeat(jnp.arange(rows, dtype=F32), GRID_W), z])
    col_f = jnp.concatenate([jnp.tile(jnp.arange(GRID_W, dtype=F32), rows), z])
    pos_f = jnp.concatenate([jnp.arange(n_tok, dtype=F32) + N_META,
                             jnp.arange(N_META, dtype=F32), jnp.zeros((TAIL - N_META,), F32)])
    ang_a = jnp.concatenate([_rope_angles(row_f, HEAD_DIM // 4), _rope_angles(col_f, HEAD_DIM // 4)], axis=-1)
    cos_a, sin_a = jnp.cos(ang_a), jnp.sin(ang_a)
    cos_a = jnp.concatenate([cos_a, cos_a], axis=-1)
    sin_a = jnp.concatenate([-sin_a, sin_a], axis=-1)
    ang_b = _rope_angles(pos_f, B_ROPE // 2)
    cos_b, sin_b = jnp.cos(ang_b), jnp.sin(ang_b)
    zb = jnp.zeros_like(cos_b)
    cos_b128 = jnp.concatenate([cos_b, cos_b, zb, zb], axis=-1)
    sin_lo = jnp.concatenate([-sin_b, zb, zb, zb], axis=-1)
    sin_hi = jnp.concatenate([zb, sin_b, zb, zb], axis=-1)
    hi = jnp.floor(pos_f / LANES)
    lo = pos_f - hi * LANES
    one = jnp.ones_like(pos_f)

    def slot_lanes(cols):
        body = jnp.stack(cols, axis=-1)
        return jnp.concatenate([jnp.zeros((lp, C_QK_DIM), F32), body,
                                jnp.zeros((lp, LANES - C_QK_DIM - len(cols)), F32)], axis=-1)

    lp = n_tok + TAIL
    q_terms = slot_lanes([LANES * one, one, -LANES * hi, -lo])
    k_terms = slot_lanes([hi, lo, one, one])
    return tuple(jnp.tile(t, (bsz, 1)) for t in (cos_a, sin_a, cos_b128, sin_lo, sin_hi, q_terms, k_terms))


def _prep_weights(w_in, w_q_b, w_kv_b, w_out, g_qn, g_kn):
    bounds = [0]
    for s in IN_SPLITS:
        bounds.append(bounds[-1] + s)
    col = lambda i: slice(bounds[i], bounds[i + 1])
    w = w_in.astype(BF16)
    aq, ak, av, ag, bq, bkv, bpe, bg, cq, ck, cv, cg = (w[:, col(i)] for i in range(12))
    w_aqk = jnp.concatenate([aq, ak], axis=1)
    w_plain = jnp.concatenate([av, cv], axis=1)
    w_gate = jnp.concatenate([ag, bg, cg], axis=1)
    w_cqk = jnp.concatenate([cq, ck], axis=1)
    w_blow = jnp.concatenate([bq, bkv], axis=1)
    w_bpe = jnp.concatenate([bpe, jnp.zeros((D_MODEL, LANES - B_ROPE), BF16)], axis=1)
    g_aqk = jnp.concatenate([jnp.tile(g_qn, A_HEADS), jnp.tile(g_kn, A_KV_HEADS)]).reshape(1, -1)
    m_aqk = jnp.concatenate([jnp.full((A_WIDTH,), HEAD_DIM ** -0.5 * LOG2E, F32),
                             jnp.ones((A_KV_WIDTH,), F32)]).reshape(1, -1)
    wq = w_q_b.astype(BF16).reshape(B_Q_RANK, B_HEADS, B_NOPE + B_ROPE)
    wq = jnp.concatenate([wq, jnp.zeros((B_Q_RANK, B_HEADS, 2 * LANES - B_NOPE - B_ROPE), BF16)], axis=-1)
    wq = wq.reshape(B_Q_RANK, B_HEADS * 2 * LANES)
    wkv = w_kv_b.astype(BF16).reshape(B_KV_RANK, B_HEADS, B_NOPE + B_VDIM)
    wkv = jnp.concatenate([wkv[:, :, :B_NOPE].reshape(B_KV_RANK, -1),
                           wkv[:, :, B_NOPE:].reshape(B_KV_RANK, -1)], axis=1)
    return dict(aqk=w_aqk, plain=w_plain, gate=w_gate, cqk=w_cqk, blow=w_blow, bpe=w_bpe,
                g_aqk=g_aqk, m_aqk=m_aqk, q_up=wq, kv_up=wkv, out=w_out.astype(BF16))


def _encode(x, meta, layers, g_final, slopes, cq_cols):
    bsz, n_tok, _ = x.shape
    lp = n_tok + TAIL
    rows = bsz * lp
    tail = jnp.concatenate([meta.astype(x.dtype), jnp.zeros((TAIL - N_META, D_MODEL), x.dtype)], axis=0)
    h = jnp.concatenate([x, jnp.broadcast_to(tail[None], (bsz, TAIL, D_MODEL))], axis=1).reshape(rows, D_MODEL)
    cos_a, sin_a, cos_b, sin_lo, sin_hi, q_terms, k_terms = _position_tables(n_tok, bsz)
    three = lambda a: a.reshape(bsz, lp, a.shape[-1])

    for l, p in enumerate(layers):
        lambda_init = 0.8 - 0.6 * math.exp(-0.3 * l)
        u = _rmsnorm_rows(h, p["g_attn"], BF16)
        aqk = _proj(_proj_aqk_body, u, p["aqk"], [p["g_aqk"], p["m_aqk"], cos_a, sin_a],
                    lambda tm, tn: [_col_vec_spec(tm, tn), _col_vec_spec(tm, tn),
                                    _row_table_spec(tm, tn), _row_table_spec(tm, tn)],
                    A_WIDTH + A_KV_WIDTH, BF16, 512, name="proj_gqa_qk")
        vt_plain = _proj(_proj_transposed_body, u, p["plain"], [], lambda tm, tn: [],
                         A_KV_WIDTH + C_WIDTH, BF16, 512, transposed=True, name="proj_v")
        gates = _proj(_proj_gate_body, u, p["gate"], [], lambda tm, tn: [],
                      A_WIDTH + B_WIDTH + C_WIDTH, F32, 512, name="proj_gate")
        cqk = _proj(_proj_cqk_body, u, p["cqk"], list(cq_cols) + [q_terms, k_terms],
                    lambda tm, tn: [_col_vec_spec(tm, tn)] * 3 + [_row_table_spec(tm, tn)] * 2,
                    4 * C_QK_WIDTH, BF16, 512, out_tn=1024, name="proj_diff_qk")
        blow = _proj(_proj_plain_body, u, p["blow"], [], lambda tm, tn: [],
                     B_Q_RANK + B_KV_RANK, F32, 512, name="proj_mla_low")
        kpe = _proj(_proj_bpe_body, u, p["bpe"], [cos_b, sin_lo, sin_hi],
                    lambda tm, tn: [_row_table_spec(tm, tn)] * 3, LANES, BF16, LANES, name="proj_mla_pe")
        qb = _bq_up(blow, p["g_q_a"], p["q_up"], cos_b, sin_lo, sin_hi)
        kb, vt_b = _bkv_up(blow, p["g_kv_a"], p["kv_up"], kpe)

        gates3 = three(gates)
        oa = _attn_a(three(aqk), vt_plain, gates3, n_tok)
        ob = _attn_b(three(qb), three(kb), vt_b, gates3, n_tok)
        oc = _attn_c(three(cqk), vt_plain, gates3, p["lamvec"], slopes, p["g_sub"], lambda_init, n_tok)
        h = _out_proj(oa.reshape(rows, -1), ob.reshape(rows, -1), oc.reshape(rows, -1), p["out"], h)

    return _final_norm(h.reshape(bsz, lp, D_MODEL), g_final, n_tok)


def kernel(x_prompt, x_sample, meta, g_attn, w_in, g_qn, g_kn, g_q_a, w_q_b, g_kv_a, w_kv_b,
           lam_q1, lam_k1, lam_q2, lam_k2, g_sub, w_out, g_final):
    layers = []
    for l in range(DEPTH):
        p = _prep_weights(w_in[l], w_q_b[l], w_kv_b[l], w_out[l], g_qn[l], g_kn[l])
        p.update(g_attn=g_attn[l], g_q_a=g_q_a[l], g_kv_a=g_kv_a[l], g_sub=g_sub[l],
                 lamvec=jnp.stack([lam_q1[l], lam_k1[l], lam_q2[l], lam_k2[l]]).astype(F32))
        layers.append(p)
    slope_h = 2.0 ** (-(jnp.arange(C_HEADS, dtype=F32) + 1.0) * 8.0 / C_HEADS)
    slopes = jnp.broadcast_to(slope_h[:, None, None], (C_HEADS, 1, LANES))
    zeros_w, ones_w = jnp.zeros((C_QK_WIDTH,), F32), jnp.ones((C_QK_WIDTH,), F32)
    cq_cols = [jnp.concatenate([C_QK_DIM ** -0.5 * ones_w, ones_w]).reshape(1, -1),
               jnp.concatenate([jnp.repeat(slope_h, 2 * C_QK_DIM), zeros_w]).reshape(1, -1),
               jnp.concatenate([zeros_w, ones_w]).reshape(1, -1)]
    y_prompt = _encode(x_prompt, meta, layers, g_final, slopes, cq_cols)
    y_sample = _encode(x_sample, meta, layers, g_final, slopes, cq_cols)
    return (y_prompt, y_sample)
```

```python
import functools
import math

import jax
import jax.numpy as jnp
from jax import lax
from jax.experimental import pallas as pl
from jax.experimental.pallas import tpu as pltpu

D_MODEL = 4096
DEPTH = 2
N_META = 16
GRID_W = 64
HEAD_DIM = 128
ROPE_THETA = 10000.0
NORM_EPS = 1e-6
A_HEADS = 16
A_KV_HEADS = 4
A_GROUP = A_HEADS // A_KV_HEADS
A_WIDTH = A_HEADS * HEAD_DIM
A_KV_WIDTH = A_KV_HEADS * HEAD_DIM
B_HEADS = 8
B_Q_RANK = 1024
B_KV_RANK = 512
B_NOPE = 128
B_ROPE = 64
B_VDIM = 128
B_WIDTH = B_HEADS * B_VDIM
C_HEADS = 8
C_QK_DIM = 64
C_VDIM = 2 * C_QK_DIM
C_QK_WIDTH = C_HEADS * 2 * C_QK_DIM
C_WIDTH = C_HEADS * C_VDIM
IN_SPLITS = (A_WIDTH, A_KV_WIDTH, A_KV_WIDTH, A_WIDTH,
             B_Q_RANK, B_KV_RANK, B_ROPE, B_WIDTH,
             C_QK_WIDTH, C_QK_WIDTH, C_WIDTH, C_WIDTH)

LANES = 128
TAIL = LANES
KV_CHUNK = 512
Q_SUB = 256
NEG_BIG = -1e30
LOG2E = 1.4426950408889634
LOOKAHEAD = 2
ONES_ROWS = 16
BF16 = jnp.bfloat16
F32 = jnp.float32
VMEM_LIMIT = 56 * 1024 * 1024

_NT = (((1,), (1,)), ((), ()))


def _cparams(n_axes):
    return pltpu.CompilerParams(dimension_semantics=("arbitrary",) * n_axes,
                                vmem_limit_bytes=VMEM_LIMIT)


def _row_tile(rows):
    for t in (512, 640, 256, 128):
        if rows % t == 0:
            return t
    raise ValueError(f"no row tile for {rows}")


def _tall_row_tile(rows):
    for t in (1536, 1664):
        if rows % t == 0:
            return t
    return _row_tile(rows)


def _rmsnorm_body(x_ref, g_ref, o_ref):
    x = x_ref[...]
    ms = jnp.mean(x * x, axis=-1, keepdims=True)
    o_ref[...] = ((x * lax.rsqrt(ms + NORM_EPS)) * g_ref[...]).astype(o_ref.dtype)


def _rmsnorm_rows(h2d, g, out_dtype):
    rows, d = h2d.shape
    tm = 256 if rows % 256 == 0 else 128
    return pl.pallas_call(
        _rmsnorm_body,
        grid=(rows // tm,),
        in_specs=[pl.BlockSpec((tm, d), lambda i: (i, 0)),
                  pl.BlockSpec((1, d), lambda i: (0, 0))],
        out_specs=pl.BlockSpec((tm, d), lambda i: (i, 0)),
        out_shape=jax.ShapeDtypeStruct((rows, d), out_dtype),
        compiler_params=_cparams(1),
        name="rmsnorm",
    )(h2d, g.reshape(1, d))


def _final_norm(h3d, g, n_tok):
    bsz, _, d = h3d.shape
    tm = 256
    return pl.pallas_call(
        _rmsnorm_body,
        grid=(bsz, n_tok // tm),
        in_specs=[pl.BlockSpec((None, tm, d), lambda b, i: (b, i, 0)),
                  pl.BlockSpec((1, d), lambda b, i: (0, 0))],
        out_specs=pl.BlockSpec((None, tm, d), lambda b, i: (b, i, 0)),
        out_shape=jax.ShapeDtypeStruct((bsz, n_tok, d), F32),
        compiler_params=_cparams(2),
        name="final_norm",
    )(h3d, g.reshape(1, d))


def _rope_pairs_64(y, cos, sin_signed):
    return y * cos + pltpu.roll(y, 64, 1) * sin_signed


def _rope_pairs_32(x, cos, sin_lo, sin_hi):
    return x * cos + pltpu.roll(x, 96, 1) * sin_lo + pltpu.roll(x, 32, 1) * sin_hi


def _proj_aqk_body(x_ref, w_ref, g_ref, mult_ref, cos_ref, sin_ref, o_ref):
    acc = jnp.dot(x_ref[...], w_ref[...], preferred_element_type=F32)
    cos = cos_ref[...]
    sin = sin_ref[...]
    for h in range(acc.shape[1] // HEAD_DIM):
        sl = slice(h * HEAD_DIM, (h + 1) * HEAD_DIM)
        a = acc[:, sl]
        ms = jnp.mean(a * a, axis=-1, keepdims=True)
        y = (a * lax.rsqrt(ms + NORM_EPS)) * g_ref[:, sl]
        o_ref[:, sl] = (_rope_pairs_64(y, cos, sin) * mult_ref[:, sl]).astype(o_ref.dtype)


def _proj_plain_body(x_ref, w_ref, o_ref):
    o_ref[...] = jnp.dot(x_ref[...], w_ref[...], preferred_element_type=F32).astype(o_ref.dtype)


def _proj_transposed_body(x_ref, w_ref, o_ref):
    o_ref[...] = jnp.dot(x_ref[...], w_ref[...], preferred_element_type=F32).T.astype(o_ref.dtype)


def _proj_gate_body(x_ref, w_ref, o_ref):
    acc = jnp.dot(x_ref[...], w_ref[...], preferred_element_type=F32)
    o_ref[...] = acc * (1.0 / (1.0 + jnp.exp(-acc)))


def _proj_cqk_body(x_ref, w_ref, mult_ref, qcoef_ref, kcoef_ref, qtab_ref, ktab_ref, o_ref):
    acc = jnp.dot(x_ref[...], w_ref[...], preferred_element_type=F32) * mult_ref[...]
    lane = lax.broadcasted_iota(jnp.int32, (acc.shape[0], LANES), 1)
    low = lane < C_QK_DIM
    qtab = qtab_ref[...]
    ktab = ktab_ref[...]
    for t in range(acc.shape[1] // LANES):
        sl = slice(t * LANES, (t + 1) * LANES)
        a = acc[:, sl]
        pos_terms = qtab * qcoef_ref[:, sl] + ktab * kcoef_ref[:, sl]
        o_ref[:, (2 * t) * LANES:(2 * t + 1) * LANES] = jnp.where(low, a, pos_terms).astype(o_ref.dtype)
        o_ref[:, (2 * t + 1) * LANES:(2 * t + 2) * LANES] = (
            jnp.where(low, pltpu.roll(a, 64, 1), pos_terms).astype(o_ref.dtype))


def _proj_bpe_body(x_ref, w_ref, cos_ref, slo_ref, shi_ref, o_ref):
    acc = jnp.dot(x_ref[...], w_ref[...], preferred_element_type=F32)
    o_ref[...] = _rope_pairs_32(acc, cos_ref[...], slo_ref[...], shi_ref[...]).astype(o_ref.dtype)


def _proj(body, u, w, extras, extra_specs, n_out, out_dtype, tn, out_tn=None, transposed=False, name="proj"):
    rows, k = u.shape
    n = w.shape[1]
    tm = _tall_row_tile(rows)
    out_tn = tn if out_tn is None else out_tn
    if transposed:
        out_specs = pl.BlockSpec((out_tn, tm), lambda i, j: (j, i))
        out_shape = jax.ShapeDtypeStruct((n_out, rows), out_dtype)
    else:
        out_specs = pl.BlockSpec((tm, out_tn), lambda i, j: (i, j))
        out_shape = jax.ShapeDtypeStruct((rows, n_out), out_dtype)
    return pl.pallas_call(
        body,
        grid=(rows // tm, n // tn),
        in_specs=[pl.BlockSpec((tm, k), lambda i, j: (i, 0)),
                  pl.BlockSpec((k, tn), lambda i, j: (0, j))] + extra_specs(tm, tn),
        out_specs=out_specs,
        out_shape=out_shape,
        compiler_params=_cparams(2),
        name=name,
    )(u, w, *extras)


def _row_table_spec(tm, tn):
    return pl.BlockSpec((tm, LANES), lambda i, j: (i, 0))


def _col_vec_spec(tm, tn):
    return pl.BlockSpec((1, tn), lambda i, j: (0, j))


def _bq_up_body(c_ref, g_ref, w_ref, cos_ref, slo_ref, shi_ref, o_ref):
    c = c_ref[...]
    ms = jnp.mean(c * c, axis=-1, keepdims=True)
    u = ((c * lax.rsqrt(ms + NORM_EPS)) * g_ref[...]).astype(BF16)
    acc = jnp.dot(u, w_ref[...], preferred_element_type=F32) * ((B_NOPE + B_ROPE) ** -0.5 * LOG2E)
    cos, slo, shi = cos_ref[...], slo_ref[...], shi_ref[...]
    for h in range(B_HEADS):
        base = h * 2 * LANES
        o_ref[:, base:base + LANES] = acc[:, base:base + LANES].astype(o_ref.dtype)
        x = acc[:, base + LANES:base + 2 * LANES]
        o_ref[:, base + LANES:base + 2 * LANES] = _rope_pairs_32(x, cos, slo, shi).astype(o_ref.dtype)


def _bkv_up_body(c_ref, g_ref, w_ref, kpe_ref, k_ref, v_ref):
    c = c_ref[...]
    ms = jnp.mean(c * c, axis=-1, keepdims=True)
    u = ((c * lax.rsqrt(ms + NORM_EPS)) * g_ref[...]).astype(BF16)
    acc = jnp.dot(u, w_ref[...], preferred_element_type=F32)
    kpe = kpe_ref[...]
    for h in range(B_HEADS):
        base = h * 2 * LANES
        k_ref[:, base:base + LANES] = acc[:, h * LANES:(h + 1) * LANES].astype(k_ref.dtype)
        k_ref[:, base + LANES:base + 2 * LANES] = kpe
    v_ref[...] = acc[:, B_HEADS * LANES:].T.astype(v_ref.dtype)


def _bq_up(cb, g, w, cos, slo, shi):
    rows = cb.shape[0]
    tm = _row_tile(rows)
    n = w.shape[1]
    tab = pl.BlockSpec((tm, LANES), lambda i: (i, 0))
    return pl.pallas_call(
        _bq_up_body,
        grid=(rows // tm,),
        in_specs=[pl.BlockSpec((tm, B_Q_RANK), lambda i: (i, 0)),
                  pl.BlockSpec((1, B_Q_RANK), lambda i: (0, 0)),
                  pl.BlockSpec((B_Q_RANK, n), lambda i: (0, 0)),
                  tab, tab, tab],
        out_specs=pl.BlockSpec((tm, n), lambda i: (i, 0)),
        out_shape=jax.ShapeDtypeStruct((rows, n), BF16),
        compiler_params=_cparams(1),
        name="mla_q_up",
    )(cb, g.reshape(1, -1), w, cos, slo, shi)


def _bkv_up(cb, g, w, kpe):
    rows = cb.shape[0]
    tm = _row_tile(rows)
    n = w.shape[1]
    return pl.pallas_call(
        _bkv_up_body,
        grid=(rows // tm,),
        in_specs=[pl.BlockSpec((tm, B_KV_RANK), lambda i: (i, B_Q_RANK // B_KV_RANK)),
                  pl.BlockSpec((1, B_KV_RANK), lambda i: (0, 0)),
                  pl.BlockSpec((B_KV_RANK, n), lambda i: (0, 0)),
                  pl.BlockSpec((tm, LANES), lambda i: (i, 0))],
        out_specs=[pl.BlockSpec((tm, B_HEADS * 2 * LANES), lambda i: (i, 0)),
                   pl.BlockSpec((B_HEADS * B_VDIM, tm), lambda i: (0, i))],
        out_shape=[jax.ShapeDtypeStruct((rows, B_HEADS * 2 * LANES), BF16),
                   jax.ShapeDtypeStruct((B_HEADS * B_VDIM, rows), BF16)],
        compiler_params=_cparams(1),
        name="mla_kv_up",
    )(cb, g.reshape(1, -1), w, kpe)


def _online_softmax_sweep(k_ref, vt_ref, m_ref, acc_ref, s_ref, p_ref, a_ref, n_units, n_tok, scores, make_adjust,
                          phases=None, first_variant=0, tail_variant=0, tail_fix=False, coef=None):
    look = min(LOOKAHEAD, n_units)
    n_chunks = n_tok // KV_CHUNK
    if phases is None:
        phases = [(0, n_chunks, 0, 0, False)]

    def probabilities(s, idx):
        m_prev = m_ref[idx]
        m_new = jnp.maximum(m_prev, jnp.max(s, axis=0, keepdims=True))
        m_ref[idx] = m_new
        if coef is None:
            return jnp.exp2(s - m_new).astype(BF16), jnp.exp2(m_prev - m_new)
        return jnp.exp2((s - m_new) * coef).astype(BF16), jnp.exp2((m_prev - m_new) * coef)

    def accumulate(idx, p, alpha, vt_ones):
        acc_ref[idx] = alpha * acc_ref[idx] + jnp.dot(vt_ones, p, preferred_element_type=F32)

    def chunk(s_first, kc, vt, k_off, n_valid, kc_after, behind, variant, variant_after, fix):
        pending = list(s_first)
        vt_ones = _with_ones_rows(vt)
        adjust = make_adjust(k_off, kc.shape[0], n_valid, fix)
        for idx in range(n_units):
            s = pending.pop(0)
            ahead = idx + look
            if ahead < n_units:
                pending.append(scores(kc, ahead, variant))
            elif kc_after is not None:
                s_ref[ahead - n_units] = scores(kc_after, ahead - n_units, variant_after)
            p, alpha = probabilities(adjust(s, idx), idx)
            if behind is not None:
                accumulate(*behind)
            behind = (idx, p, alpha, vt_ones)
        return behind

    def chunk_step(c, carry, variant, variant_after, fix):
        off = pl.multiple_of(c * KV_CHUNK, KV_CHUNK)
        off_before = pl.multiple_of(jnp.maximum(c - 1, 0) * KV_CHUNK, KV_CHUNK)
        off_after = pl.multiple_of(jnp.minimum(c + 1, n_chunks - 1) * KV_CHUNK, KV_CHUNK)
        behind = (n_units - 1, p_ref[...], a_ref[...], _with_ones_rows(vt_ref[:, pl.ds(off_before, KV_CHUNK)]))
        _, p, alpha, _ = chunk([s_ref[j] for j in range(look)], k_ref[pl.ds(off, KV_CHUNK), :],
                               vt_ref[:, pl.ds(off, KV_CHUNK)], N_META + off, None,
                               k_ref[pl.ds(off_after, KV_CHUNK), :], behind, variant, variant_after, fix)
        p_ref[...] = p
        a_ref[...] = alpha
        return carry

    m_ref[...] = jnp.full(m_ref.shape, NEG_BIG, F32)
    acc_ref[...] = jnp.zeros(acc_ref.shape, F32)
    p_ref[...] = jnp.zeros(p_ref.shape, BF16)
    a_ref[...] = jnp.ones(a_ref.shape, F32)
    for j in range(look):
        s_ref[j] = scores(k_ref[0:KV_CHUNK, :], j, first_variant)
    for lo, hi, variant, variant_after, fix in phases:
        lax.fori_loop(lo, hi, functools.partial(chunk_step, variant=variant, variant_after=variant_after, fix=fix), 0)
    accumulate(n_units - 1, p_ref[...], a_ref[...], _with_ones_rows(vt_ref[:, n_tok - KV_CHUNK:n_tok]))
    k_tail = k_ref[n_tok:n_tok + TAIL, :]
    accumulate(*chunk([scores(k_tail, j, tail_variant) for j in range(look)], k_tail, vt_ref[:, n_tok:n_tok + TAIL],
                      0, N_META, None, None, tail_variant, tail_variant, tail_fix))


def _with_ones_rows(vt):
    return jnp.concatenate([vt, jnp.ones((ONES_ROWS, vt.shape[1]), vt.dtype)], axis=0)


def _normalized(acc):
    return acc[:HEAD_DIM] * (1.0 / acc[HEAD_DIM:HEAD_DIM + 1])


def _attn_ab_body(q_ref, k_ref, vt_ref, gate_ref, *rest, group, qsub, tqs, dk, n_tok):
    o_ref, m_ref, acc_ref, s_ref, p_ref, a_ref, qt_ref = rest[-7:]
    units = [(g, u) for g in range(group) for u in range(qsub)]
    for idx, (g, u) in enumerate(units):
        qt_ref[idx] = q_ref[u * tqs:(u + 1) * tqs, g * dk:(g + 1) * dk].astype(F32).T.astype(BF16)

    def scores(kc, idx, variant):
        return jnp.dot(kc, qt_ref[idx], preferred_element_type=F32)

    def make_adjust(k_off, n_keys, n_valid, fix):
        def adjust(s, idx):
            if n_valid is None:
                return s
            row = lax.broadcasted_iota(jnp.int32, s.shape, 0)
            return jnp.where(row < n_valid, s, NEG_BIG)
        return adjust

    _online_softmax_sweep(k_ref, vt_ref, m_ref, acc_ref, s_ref, p_ref, a_ref, len(units), n_tok, scores, make_adjust)

    for idx, (g, u) in enumerate(units):
        rows = slice(u * tqs, (u + 1) * tqs)
        cols = slice(g * HEAD_DIM, (g + 1) * HEAD_DIM)
        o = _normalized(acc_ref[idx]).T
        o_ref[rows, cols] = (o * gate_ref[rows, cols]).astype(o_ref.dtype)


def _attn_c_body(q_ref, k_ref, vt_ref, gate_ref, lam_ref, slope_ref, gsub_ref, *rest,
                 qsub, tqs, n_tok, q_is_meta, lambda_init):
    o_ref, m_ref, acc_ref, s_ref, p_ref, a_ref, qt_ref = rest[-7:]
    units = [(u, c) for u in range(qsub) for c in range(2)]
    lane = lax.broadcasted_iota(jnp.int32, (1, LANES), 1)
    flip = jnp.where(lane < C_QK_DIM, 1.0, -1.0)
    for idx, (u, c) in enumerate(units):
        q = q_ref[u * tqs:(u + 1) * tqs, c * LANES:(c + 1) * LANES].astype(F32)
        qt_ref[0, idx] = q.T.astype(BF16)
        qt_ref[1, idx] = (q * flip).T.astype(BF16)
    slope2 = 2.0 * slope_ref[:, :1]
    tq = qsub * tqs
    n_chunks = n_tok // KV_CHUNK
    if q_is_meta:
        q_off = 0
        phases = [(0, n_chunks, 1, 1, False)]
        first_variant, tail_fix = 1, True
    else:
        assert tq == KV_CHUNK
        i = pl.program_id(2)
        q_off = N_META + i * tq
        phases = [(0, i, 0, 0, False), (i, i + 1, 0, 1, True), (i + 1, n_chunks, 1, 1, False)]
        first_variant, tail_fix = 0, False

    def scores(kc, idx, variant):
        _, c = units[idx]
        return jnp.dot(kc[:, c * LANES:(c + 1) * LANES], qt_ref[variant, idx], preferred_element_type=F32)

    def make_adjust(k_off, n_keys, n_valid, fix):
        key = lax.broadcasted_iota(jnp.int32, (n_keys, tqs), 0)
        qry = lax.broadcasted_iota(jnp.int32, (n_keys, tqs), 1)
        late = {}

        def adjust(s, idx):
            u, _ = units[idx]
            if fix:
                if u not in late:
                    ahead = ((key - qry) + (k_off - q_off - u * tqs)).astype(F32)
                    late[u] = slope2 * jnp.maximum(ahead, 0.0)
                s = s - late[u]
            if n_valid is not None:
                s = jnp.where(key < n_valid, s, NEG_BIG)
            return s
        return adjust

    _online_softmax_sweep(k_ref, vt_ref, m_ref, acc_ref, s_ref, p_ref, a_ref, len(units), n_tok, scores, make_adjust,
                          phases=phases, first_variant=first_variant, tail_variant=0, tail_fix=tail_fix, coef=LOG2E)

    lv = lam_ref[...]
    lam = (jnp.exp(jnp.sum(lv[0:1] * lv[1:2], axis=1, keepdims=True))
           - jnp.exp(jnp.sum(lv[2:3] * lv[3:4], axis=1, keepdims=True)) + lambda_init)
    for u in range(qsub):
        rows = slice(u * tqs, (u + 1) * tqs)
        o = (_normalized(acc_ref[2 * u]) - lam * _normalized(acc_ref[2 * u + 1])).T
        ms = jnp.mean(o * o, axis=-1, keepdims=True)
        o = ((o * lax.rsqrt(ms + NORM_EPS)) * gsub_ref[...]) * (1.0 - lambda_init)
        o_ref[rows, :] = (o * gate_ref[rows, :]).astype(o_ref.dtype)


def _attention(body, q, q_col0, q_w, k, k_col0, k_w, vt, v_row0, gate, gate_col0, extras, extra_specs,
               out_w_total, heads, group, qsub, n_units, n_tok, name, qt_shape):
    bsz, lp, _ = q.shape
    out_w = group * HEAD_DIM

    def call(qsub_, tqs, n_qblk, qblk0, prev):
        tq = qsub_ * tqs
        in_specs = [
            pl.BlockSpec((None, tq, q_w), lambda b, h, i: (b, qblk0 + i, q_col0 + h)),
            pl.BlockSpec((None, lp, k_w), lambda b, h, i: (b, 0, k_col0 + h)),
            pl.BlockSpec((HEAD_DIM, lp), lambda b, h, i: (v_row0 + h, b)),
            pl.BlockSpec((None, tq, out_w), lambda b, h, i: (b, qblk0 + i, gate_col0 + h)),
        ] + extra_specs
        args = [q, k, vt, gate] + extras
        aliases = {}
        if prev is not None:
            in_specs.append(pl.BlockSpec(memory_space=pl.ANY))
            args.append(prev)
            aliases = {len(args) - 1: 0}
        units = n_units * qsub_
        return pl.pallas_call(
            body(qsub_, tqs, prev is not None),
            grid=(bsz, heads, n_qblk),
            in_specs=in_specs,
            out_specs=pl.BlockSpec((None, tq, out_w), lambda b, h, i: (b, qblk0 + i, h)),
            out_shape=jax.ShapeDtypeStruct((bsz, lp, out_w_total), BF16),
            scratch_shapes=[pltpu.VMEM((units, 1, tqs), F32),
                            pltpu.VMEM((units, HEAD_DIM + ONES_ROWS, tqs), F32),
                            pltpu.VMEM((LOOKAHEAD, KV_CHUNK, tqs), F32),
                            pltpu.VMEM((KV_CHUNK, tqs), BF16), pltpu.VMEM((1, tqs), F32),
                            pltpu.VMEM(qt_shape(units, tqs), BF16)],
            input_output_aliases=aliases,
            compiler_params=_cparams(3),
            name=name + ("_meta" if prev is not None else ""),
        )(*args)

    main = call(qsub, Q_SUB, n_tok // (qsub * Q_SUB), 0, None)
    return call(1, TAIL, 1, n_tok // TAIL, main)


def _attn_a(qk, vt_plain, gates, n_tok):
    def body(qsub_, tqs, is_meta):
        return functools.partial(_attn_ab_body, group=A_GROUP, qsub=qsub_, tqs=tqs, dk=HEAD_DIM, n_tok=n_tok)

    return _attention(body, qk, 0, A_GROUP * HEAD_DIM, qk, A_WIDTH // HEAD_DIM, HEAD_DIM, vt_plain, 0,
                      gates, 0, [], [], A_WIDTH, A_KV_HEADS, A_GROUP, 1, A_GROUP, n_tok, "attn_gqa",
                      lambda units, tqs: (units, HEAD_DIM, tqs))


def _attn_b(qb, kb, vt_b, gates, n_tok):
    dk = 2 * LANES

    def body(qsub_, tqs, is_meta):
        return functools.partial(_attn_ab_body, group=1, qsub=qsub_, tqs=tqs, dk=dk, n_tok=n_tok)

    return _attention(body, qb, 0, dk, kb, 0, dk, vt_b, 0, gates, A_WIDTH // HEAD_DIM, [], [],
                      B_WIDTH, B_HEADS, 1, 2, 1, n_tok, "attn_mla", lambda units, tqs: (units, dk, tqs))


def _attn_c(cqk, vt_plain, gates, lamvec, slopes, g_sub, lambda_init, n_tok):
    dk = 2 * LANES

    def body(qsub_, tqs, is_meta):
        return functools.partial(_attn_c_body, qsub=qsub_, tqs=tqs, n_tok=n_tok, q_is_meta=is_meta,
                                 lambda_init=lambda_init)

    extras = [lamvec, slopes, g_sub.reshape(1, C_VDIM)]
    extra_specs = [pl.BlockSpec((4, C_QK_DIM), lambda b, h, i: (0, 0)),
                   pl.BlockSpec((None, 1, LANES), lambda b, h, i: (h, 0, 0)),
                   pl.BlockSpec((1, C_VDIM), lambda b, h, i: (0, 0))]
    return _attention(body, cqk, 0, dk, cqk, C_HEADS, dk, vt_plain, A_KV_HEADS, gates,
                      (A_WIDTH + B_WIDTH) // HEAD_DIM, extras, extra_specs,
                      C_WIDTH, C_HEADS, 1, 2, 2, n_tok, "attn_diff", lambda units, tqs: (2, units, LANES, tqs))


def _out_proj_body(a_ref, b_ref, c_ref, wa_ref, wb_ref, wc_ref, h_ref, o_ref):
    acc = jnp.dot(a_ref[...], wa_ref[...], preferred_element_type=F32)
    acc += jnp.dot(b_ref[...], wb_ref[...], preferred_element_type=F32)
    acc += jnp.dot(c_ref[...], wc_ref[...], preferred_element_type=F32)
    o_ref[...] = h_ref[...] + acc


def _out_proj(oa, ob, oc, w, h2d):
    rows = h2d.shape[0]
    tm = _tall_row_tile(rows)
    tn = 512
    return pl.pallas_call(
        _out_proj_body,
        grid=(rows // tm, D_MODEL // tn),
        in_specs=[pl.BlockSpec((tm, A_WIDTH), lambda i, j: (i, 0)),
                  pl.BlockSpec((tm, B_WIDTH), lambda i, j: (i, 0)),
                  pl.BlockSpec((tm, C_WIDTH), lambda i, j: (i, 0)),
                  pl.BlockSpec((A_WIDTH, tn), lambda i, j: (0, j)),
                  pl.BlockSpec((B_WIDTH, tn), lambda i, j: (A_WIDTH // B_WIDTH, j)),
                  pl.BlockSpec((C_WIDTH, tn), lambda i, j: ((A_WIDTH + B_WIDTH) // C_WIDTH, j)),
                  pl.BlockSpec((tm, tn), lambda i, j: (i, j))],
        out_specs=pl.BlockSpec((tm, tn), lambda i, j: (i, j)),
        out_shape=jax.ShapeDtypeStruct((rows, D_MODEL), F32),
        compiler_params=_cparams(2),
        name="out_proj",
    )(oa, ob, oc, w, w, w, h2d)


def _rope_angles(pos_f, n_freq):
    inv = ROPE_THETA ** (-jnp.arange(n_freq, dtype=F32) / n_freq)
    return pos_f[:, None] * inv[None, :]


def _position_tables(n_tok, bsz):
    rows = n_tok // GRID_W
    z = jnp.zeros((TAIL,), F32)
    row_f = jnp.concatenate([jnp.repeat(jnp.arange(rows, dtype=F32), GRID_W), z])
    col_f = jnp.concatenate([jnp.tile(jnp.arange(GRID_W, dtype=F32), rows), z])
    pos_f = jnp.concatenate([jnp.arange(n_tok, dtype=F32) + N_META,
                             jnp.arange(N_META, dtype=F32), jnp.zeros((TAIL - N_META,), F32)])
    ang_a = jnp.concatenate([_rope_angles(row_f, HEAD_DIM // 4), _rope_angles(col_f, HEAD_DIM // 4)], axis=-1)
    cos_a, sin_a = jnp.cos(ang_a), jnp.sin(ang_a)
    cos_a = jnp.concatenate([cos_a, cos_a], axis=-1)
    sin_a = jnp.concatenate([-sin_a, sin_a], axis=-1)
    ang_b = _rope_angles(pos_f, B_ROPE // 2)
    cos_b, sin_b = jnp.cos(ang_b), jnp.sin(ang_b)
    zb = jnp.zeros_like(cos_b)
    cos_b128 = jnp.concatenate([cos_b, cos_b, zb, zb], axis=-1)
    sin_lo = jnp.concatenate([-sin_b, zb, zb, zb], axis=-1)
    sin_hi = jnp.concatenate([zb, sin_b, zb, zb], axis=-1)
    hi = jnp.floor(pos_f / LANES)
    lo = pos_f - hi * LANES
    one = jnp.ones_like(pos_f)

    def slot_lanes(cols):
        body = jnp.stack(cols, axis=-1)
        return jnp.concatenate([jnp.zeros((lp, C_QK_DIM), F32), body,
                                jnp.zeros((lp, LANES - C_QK_DIM - len(cols)), F32)], axis=-1)

    lp = n_tok + TAIL
    q_terms = slot_lanes([LANES * one, one, -LANES * hi, -lo])
    k_terms = slot_lanes([hi, lo, one, one])
    return tuple(jnp.tile(t, (bsz, 1)) for t in (cos_a, sin_a, cos_b128, sin_lo, sin_hi, q_terms, k_terms))


def _prep_weights(w_in, w_q_b, w_kv_b, w_out, g_qn, g_kn):
    bounds = [0]
    for s in IN_SPLITS:
        bounds.append(bounds[-1] + s)
    col = lambda i: slice(bounds[i], bounds[i + 1])
    w = w_in.astype(BF16)
    aq, ak, av, ag, bq, bkv, bpe, bg, cq, ck, cv, cg = (w[:, col(i)] for i in range(12))
    w_aqk = jnp.concatenate([aq, ak], axis=1)
    w_plain = jnp.concatenate([av, cv], axis=1)
    w_gate = jnp.concatenate([ag, bg, cg], axis=1)
    w_cqk = jnp.concatenate([cq, ck], axis=1)
    w_blow = jnp.concatenate([bq, bkv], axis=1)
    w_bpe = jnp.concatenate([bpe, jnp.zeros((D_MODEL, LANES - B_ROPE), BF16)], axis=1)
    g_aqk = jnp.concatenate([jnp.tile(g_qn, A_HEADS), jnp.tile(g_kn, A_KV_HEADS)]).reshape(1, -1)
    m_aqk = jnp.concatenate([jnp.full((A_WIDTH,), HEAD_DIM ** -0.5 * LOG2E, F32),
                             jnp.ones((A_KV_WIDTH,), F32)]).reshape(1, -1)
    wq = w_q_b.astype(BF16).reshape(B_Q_RANK, B_HEADS, B_NOPE + B_ROPE)
    wq = jnp.concatenate([wq, jnp.zeros((B_Q_RANK, B_HEADS, 2 * LANES - B_NOPE - B_ROPE), BF16)], axis=-1)
    wq = wq.reshape(B_Q_RANK, B_HEADS * 2 * LANES)
    wkv = w_kv_b.astype(BF16).reshape(B_KV_RANK, B_HEADS, B_NOPE + B_VDIM)
    wkv = jnp.concatenate([wkv[:, :, :B_NOPE].reshape(B_KV_RANK, -1),
                           wkv[:, :, B_NOPE:].reshape(B_KV_RANK, -1)], axis=1)
    return dict(aqk=w_aqk, plain=w_plain, gate=w_gate, cqk=w_cqk, blow=w_blow, bpe=w_bpe,
                g_aqk=g_aqk, m_aqk=m_aqk, q_up=wq, kv_up=wkv, out=w_out.astype(BF16))


def _encode(x, meta, layers, g_final, slopes, cq_cols):
    bsz, n_tok, _ = x.shape
    lp = n_tok + TAIL
    rows = bsz * lp
    tail = jnp.concatenate([meta.astype(x.dtype), jnp.zeros((TAIL - N_META, D_MODEL), x.dtype)], axis=0)
    h = jnp.concatenate([x, jnp.broadcast_to(tail[None], (bsz, TAIL, D_MODEL))], axis=1).reshape(rows, D_MODEL)
    cos_a, sin_a, cos_b, sin_lo, sin_hi, q_terms, k_terms = _position_tables(n_tok, bsz)
    three = lambda a: a.reshape(bsz, lp, a.shape[-1])

    for l, p in enumerate(layers):
        lambda_init = 0.8 - 0.6 * math.exp(-0.3 * l)
        u = _rmsnorm_rows(h, p["g_attn"], BF16)
        aqk = _proj(_proj_aqk_body, u, p["aqk"], [p["g_aqk"], p["m_aqk"], cos_a, sin_a],
                    lambda tm, tn: [_col_vec_spec(tm, tn), _col_vec_spec(tm, tn),
                                    _row_table_spec(tm, tn), _row_table_spec(tm, tn)],
                    A_WIDTH + A_KV_WIDTH, BF16, 512, name="proj_gqa_qk")
        vt_plain = _proj(_proj_transposed_body, u, p["plain"], [], lambda tm, tn: [],
                         A_KV_WIDTH + C_WIDTH, BF16, 512, transposed=True, name="proj_v")
        gates = _proj(_proj_gate_body, u, p["gate"], [], lambda tm, tn: [],
                      A_WIDTH + B_WIDTH + C_WIDTH, F32, 512, name="proj_gate")
        cqk = _proj(_proj_cqk_body, u, p["cqk"], list(cq_cols) + [q_terms, k_terms],
                    lambda tm, tn: [_col_vec_spec(tm, tn)] * 3 + [_row_table_spec(tm, tn)] * 2,
                    4 * C_QK_WIDTH, BF16, 512, out_tn=1024, name="proj_diff_qk")
        blow = _proj(_proj_plain_body, u, p["blow"], [], lambda tm, tn: [],
                     B_Q_RANK + B_KV_RANK, F32, 512, name="proj_mla_low")
        kpe = _proj(_proj_bpe_body, u, p["bpe"], [cos_b, sin_lo, sin_hi],
                    lambda tm, tn: [_row_table_spec(tm, tn)] * 3, LANES, BF16, LANES, name="proj_mla_pe")
        qb = _bq_up(blow, p["g_q_a"], p["q_up"], cos_b, sin_lo, sin_hi)
        kb, vt_b = _bkv_up(blow, p["g_kv_a"], p["kv_up"], kpe)

        gates3 = three(gates)
        oa = _attn_a(three(aqk), vt_plain, gates3, n_tok)
        ob = _attn_b(three(qb), three(kb), vt_b, gates3, n_tok)
        oc = _attn_c(three(cqk), vt_plain, gates3, p["lamvec"], slopes, p["g_sub"], lambda_init, n_tok)
        h = _out_proj(oa.reshape(rows, -1), ob.reshape(rows, -1), oc.reshape(rows, -1), p["out"], h)

    return _final_norm(h.reshape(bsz, lp, D_MODEL), g_final, n_tok)


def kernel(x_prompt, x_sample, meta, g_attn, w_in, g_qn, g_kn, g_q_a, w_q_b, g_kv_a, w_kv_b,
           lam_q1, lam_k1, lam_q2, lam_k2, g_sub, w_out, g_final):
    layers = []
    for l in range(DEPTH):
        p = _prep_weights(w_in[l], w_q_b[l], w_kv_b[l], w_out[l], g_qn[l], g_kn[l])
        p.update(g_attn=g_attn[l], g_q_a=g_q_a[l], g_kv_a=g_kv_a[l], g_sub=g_sub[l],
                 lamvec=jnp.stack([lam_q1[l], lam_k1[l], lam_q2[l], lam_k2[l]]).astype(F32))
        layers.append(p)
    slope_h = 2.0 ** (-(jnp.arange(C_HEADS, dtype=F32) + 1.0) * 8.0 / C_HEADS)
    slopes = jnp.broadcast_to(slope_h[:, None, None], (C_HEADS, 1, LANES))
    zeros_w, ones_w = jnp.zeros((C_QK_WIDTH,), F32), jnp.ones((C_QK_WIDTH,), F32)
    cq_cols = [jnp.concatenate([C_QK_DIM ** -0.5 * ones_w, ones_w]).reshape(1, -1),
               jnp.concatenate([jnp.repeat(slope_h, 2 * C_QK_DIM), zeros_w]).reshape(1, -1),
               jnp.concatenate([zeros_w, ones_w]).reshape(1, -1)]
    y_prompt = _encode(x_prompt, meta, layers, g_final, slopes, cq_cols)
    y_sample = _encode(x_sample, meta, layers, g_final, slopes, cq_cols)
    return (y_prompt, y_sample)
```

```python
import functools
import math

import jax
import jax.numpy as jnp
from jax import lax
from jax.experimental import pallas as pl
from jax.experimental.pallas import tpu as pltpu

D_MODEL = 4096
DEPTH = 2
N_META = 16
GRID_W = 64
HEAD_DIM = 128
ROPE_THETA = 10000.0
NORM_EPS = 1e-6
A_HEADS = 16
A_KV_HEADS = 4
A_GROUP = A_HEADS // A_KV_HEADS
A_WIDTH = A_HEADS * HEAD_DIM
A_KV_WIDTH = A_KV_HEADS * HEAD_DIM
B_HEADS = 8
B_Q_RANK = 1024
B_KV_RANK = 512
B_NOPE = 128
B_ROPE = 64
B_VDIM = 128
B_WIDTH = B_HEADS * B_VDIM
C_HEADS = 8
C_QK_DIM = 64
C_VDIM = 2 * C_QK_DIM
C_QK_WIDTH = C_HEADS * 2 * C_QK_DIM
C_WIDTH = C_HEADS * C_VDIM
IN_SPLITS = (A_WIDTH, A_KV_WIDTH, A_KV_WIDTH, A_WIDTH,
             B_Q_RANK, B_KV_RANK, B_ROPE, B_WIDTH,
             C_QK_WIDTH, C_QK_WIDTH, C_WIDTH, C_WIDTH)

LANES = 128
TAIL = LANES
KV_CHUNK = 512
Q_SUB = 256
NEG_BIG = -1e30
LOG2E = 1.4426950408889634
A_LOOKAHEAD = 2
B_LOOKAHEAD = 2
C_LOOKAHEAD = 2
ONES_ROWS = 16
BF16 = jnp.bfloat16
F32 = jnp.float32
VMEM_LIMIT = 56 * 1024 * 1024

_NT = (((1,), (1,)), ((), ()))


def _cparams(n_axes):
    return pltpu.CompilerParams(dimension_semantics=("arbitrary",) * n_axes,
                                vmem_limit_bytes=VMEM_LIMIT)


def _row_tile(rows):
    for t in (512, 640, 256, 128):
        if rows % t == 0:
            return t
    raise ValueError(f"no row tile for {rows}")


def _tall_row_tile(rows):
    for t in (1536, 1664):
        if rows % t == 0:
            return t
    return _row_tile(rows)


def _rmsnorm_body(x_ref, g_ref, o_ref):
    x = x_ref[...]
    ms = jnp.mean(x * x, axis=-1, keepdims=True)
    o_ref[...] = ((x * lax.rsqrt(ms + NORM_EPS)) * g_ref[...]).astype(o_ref.dtype)


def _rmsnorm_rows(h2d, g, out_dtype):
    rows, d = h2d.shape
    tm = 256 if rows % 256 == 0 else 128
    return pl.pallas_call(
        _rmsnorm_body,
        grid=(rows // tm,),
        in_specs=[pl.BlockSpec((tm, d), lambda i: (i, 0)),
                  pl.BlockSpec((1, d), lambda i: (0, 0))],
        out_specs=pl.BlockSpec((tm, d), lambda i: (i, 0)),
        out_shape=jax.ShapeDtypeStruct((rows, d), out_dtype),
        compiler_params=_cparams(1),
        name="rmsnorm",
    )(h2d, g.reshape(1, d))


def _final_norm(h3d, g, n_tok):
    bsz, _, d = h3d.shape
    tm = 256
    return pl.pallas_call(
        _rmsnorm_body,
        grid=(bsz, n_tok // tm),
        in_specs=[pl.BlockSpec((None, tm, d), lambda b, i: (b, i, 0)),
                  pl.BlockSpec((1, d), lambda b, i: (0, 0))],
        out_specs=pl.BlockSpec((None, tm, d), lambda b, i: (b, i, 0)),
        out_shape=jax.ShapeDtypeStruct((bsz, n_tok, d), F32),
        compiler_params=_cparams(2),
        name="final_norm",
    )(h3d, g.reshape(1, d))


def _rope_pairs_64(y, cos, sin_signed):
    return y * cos + pltpu.roll(y, 64, 1) * sin_signed


def _rope_pairs_32(x, cos, sin_lo, sin_hi):
    return x * cos + pltpu.roll(x, 96, 1) * sin_lo + pltpu.roll(x, 32, 1) * sin_hi


def _proj_aqk_body(x_ref, w_ref, g_ref, mult_ref, cos_ref, sin_ref, o_ref):
    acc = jnp.dot(x_ref[...], w_ref[...], preferred_element_type=F32)
    cos = cos_ref[...]
    sin = sin_ref[...]
    for h in range(acc.shape[1] // HEAD_DIM):
        sl = slice(h * HEAD_DIM, (h + 1) * HEAD_DIM)
        a = acc[:, sl]
        ms = jnp.mean(a * a, axis=-1, keepdims=True)
        y = (a * lax.rsqrt(ms + NORM_EPS)) * g_ref[:, sl]
        o_ref[:, sl] = (_rope_pairs_64(y, cos, sin) * mult_ref[:, sl]).astype(o_ref.dtype)


def _proj_plain_body(x_ref, w_ref, o_ref):
    o_ref[...] = jnp.dot(x_ref[...], w_ref[...], preferred_element_type=F32).astype(o_ref.dtype)


def _proj_transposed_body(x_ref, w_ref, o_ref):
    o_ref[...] = jnp.dot(x_ref[...], w_ref[...], preferred_element_type=F32).T.astype(o_ref.dtype)


def _proj_gate_body(x_ref, w_ref, o_ref):
    acc = jnp.dot(x_ref[...], w_ref[...], preferred_element_type=F32)
    o_ref[...] = acc * (1.0 / (1.0 + jnp.exp(-acc)))


def _proj_cqk_body(x_ref, w_ref, mult_ref, qcoef_ref, kcoef_ref, qtab_ref, ktab_ref, o_ref):
    acc = jnp.dot(x_ref[...], w_ref[...], preferred_element_type=F32) * mult_ref[...]
    lane = lax.broadcasted_iota(jnp.int32, (acc.shape[0], LANES), 1)
    low = lane < C_QK_DIM
    qtab = qtab_ref[...]
    ktab = ktab_ref[...]
    for t in range(acc.shape[1] // LANES):
        sl = slice(t * LANES, (t + 1) * LANES)
        a = acc[:, sl]
        pos_terms = qtab * qcoef_ref[:, sl] + ktab * kcoef_ref[:, sl]
        o_ref[:, (2 * t) * LANES:(2 * t + 1) * LANES] = jnp.where(low, a, pos_terms).astype(o_ref.dtype)
        o_ref[:, (2 * t + 1) * LANES:(2 * t + 2) * LANES] = (
            jnp.where(low, pltpu.roll(a, 64, 1), pos_terms).astype(o_ref.dtype))


def _proj_bpe_body(x_ref, w_ref, cos_ref, slo_ref, shi_ref, o_ref):
    acc = jnp.dot(x_ref[...], w_ref[...], preferred_element_type=F32)
    o_ref[...] = _rope_pairs_32(acc, cos_ref[...], slo_ref[...], shi_ref[...]).astype(o_ref.dtype)


def _proj(body, u, w, extras, extra_specs, n_out, out_dtype, tn, out_tn=None, transposed=False, tall=True,
          name="proj"):
    rows, k = u.shape
    n = w.shape[1]
    tm = _tall_row_tile(rows) if tall else _row_tile(rows)
    out_tn = tn if out_tn is None else out_tn
    if transposed:
        out_specs = pl.BlockSpec((out_tn, tm), lambda i, j: (j, i))
        out_shape = jax.ShapeDtypeStruct((n_out, rows), out_dtype)
    else:
        out_specs = pl.BlockSpec((tm, out_tn), lambda i, j: (i, j))
        out_shape = jax.ShapeDtypeStruct((rows, n_out), out_dtype)
    return pl.pallas_call(
        body,
        grid=(rows // tm, n // tn),
        in_specs=[pl.BlockSpec((tm, k), lambda i, j: (i, 0)),
                  pl.BlockSpec((k, tn), lambda i, j: (0, j))] + extra_specs(tm, tn),
        out_specs=out_specs,
        out_shape=out_shape,
        compiler_params=_cparams(2),
        name=name,
    )(u, w, *extras)


def _row_table_spec(tm, tn):
    return pl.BlockSpec((tm, LANES), lambda i, j: (i, 0))


def _col_vec_spec(tm, tn):
    return pl.BlockSpec((1, tn), lambda i, j: (0, j))


def _bq_up_body(c_ref, g_ref, w_ref, cos_ref, slo_ref, shi_ref, o_ref):
    c = c_ref[...]
    ms = jnp.mean(c * c, axis=-1, keepdims=True)
    u = ((c * lax.rsqrt(ms + NORM_EPS)) * g_ref[...]).astype(BF16)
    acc = jnp.dot(u, w_ref[...], preferred_element_type=F32) * ((B_NOPE + B_ROPE) ** -0.5 * LOG2E)
    cos, slo, shi = cos_ref[...], slo_ref[...], shi_ref[...]
    for h in range(B_HEADS):
        base = h * 2 * LANES
        o_ref[:, base:base + LANES] = acc[:, base:base + LANES].astype(o_ref.dtype)
        x = acc[:, base + LANES:base + 2 * LANES]
        o_ref[:, base + LANES:base + 2 * LANES] = _rope_pairs_32(x, cos, slo, shi).astype(o_ref.dtype)


def _bkv_up_body(c_ref, g_ref, w_ref, kpe_ref, k_ref, v_ref):
    c = c_ref[...]
    ms = jnp.mean(c * c, axis=-1, keepdims=True)
    u = ((c * lax.rsqrt(ms + NORM_EPS)) * g_ref[...]).astype(BF16)
    acc = jnp.dot(u, w_ref[...], preferred_element_type=F32)
    kpe = kpe_ref[...]
    for h in range(B_HEADS):
        base = h * 2 * LANES
        k_ref[:, base:base + LANES] = acc[:, h * LANES:(h + 1) * LANES].astype(k_ref.dtype)
        k_ref[:, base + LANES:base + 2 * LANES] = kpe
    v_ref[...] = acc[:, B_HEADS * LANES:].T.astype(v_ref.dtype)


def _bq_up(cb, g, w, cos, slo, shi):
    rows = cb.shape[0]
    tm = _row_tile(rows)
    n = w.shape[1]
    tab = pl.BlockSpec((tm, LANES), lambda i: (i, 0))
    return pl.pallas_call(
        _bq_up_body,
        grid=(rows // tm,),
        in_specs=[pl.BlockSpec((tm, B_Q_RANK), lambda i: (i, 0)),
                  pl.BlockSpec((1, B_Q_RANK), lambda i: (0, 0)),
                  pl.BlockSpec((B_Q_RANK, n), lambda i: (0, 0)),
                  tab, tab, tab],
        out_specs=pl.BlockSpec((tm, n), lambda i: (i, 0)),
        out_shape=jax.ShapeDtypeStruct((rows, n), BF16),
        compiler_params=_cparams(1),
        name="mla_q_up",
    )(cb, g.reshape(1, -1), w, cos, slo, shi)


def _bkv_up(cb, g, w, kpe):
    rows = cb.shape[0]
    tm = _row_tile(rows)
    n = w.shape[1]
    return pl.pallas_call(
        _bkv_up_body,
        grid=(rows // tm,),
        in_specs=[pl.BlockSpec((tm, B_KV_RANK), lambda i: (i, B_Q_RANK // B_KV_RANK)),
                  pl.BlockSpec((1, B_KV_RANK), lambda i: (0, 0)),
                  pl.BlockSpec((B_KV_RANK, n), lambda i: (0, 0)),
                  pl.BlockSpec((tm, LANES), lambda i: (i, 0))],
        out_specs=[pl.BlockSpec((tm, B_HEADS * 2 * LANES), lambda i: (i, 0)),
                   pl.BlockSpec((B_HEADS * B_VDIM, tm), lambda i: (0, i))],
        out_shape=[jax.ShapeDtypeStruct((rows, B_HEADS * 2 * LANES), BF16),
                   jax.ShapeDtypeStruct((B_HEADS * B_VDIM, rows), BF16)],
        compiler_params=_cparams(1),
        name="mla_kv_up",
    )(cb, g.reshape(1, -1), w, kpe)


def _online_softmax_sweep(k_ref, vt_ref, m_ref, acc_ref, s_ref, p_ref, a_ref, n_units, n_tok, scores, make_adjust,
                          lookahead, phases=None, first_variant=0, tail_variant=0, tail_fix=False, coef=None):
    look = min(lookahead, n_units)
    n_chunks = n_tok // KV_CHUNK
    if phases is None:
        phases = [(0, n_chunks, 0, 0, False)]

    def probabilities(s, idx):
        m_prev = m_ref[idx]
        m_new = jnp.maximum(m_prev, jnp.max(s, axis=0, keepdims=True))
        m_ref[idx] = m_new
        if coef is None:
            return jnp.exp2(s - m_new).astype(BF16), jnp.exp2(m_prev - m_new)
        return jnp.exp2((s - m_new) * coef).astype(BF16), jnp.exp2((m_prev - m_new) * coef)

    def accumulate(idx, p, alpha, vt_ones):
        acc_ref[idx] = alpha * acc_ref[idx] + jnp.dot(vt_ones, p, preferred_element_type=F32)

    def chunk(s_first, kc, vt, k_off, n_valid, kc_after, behind, variant, variant_after, fix):
        pending = list(s_first)
        vt_ones = _with_ones_rows(vt)
        adjust = make_adjust(k_off, kc.shape[0], n_valid, fix)
        for idx in range(n_units):
            s = pending.pop(0)
            ahead = idx + look
            if ahead < n_units:
                pending.append(scores(kc, ahead, variant))
            elif kc_after is not None:
                s_ref[ahead - n_units] = scores(kc_after, ahead - n_units, variant_after)
            p, alpha = probabilities(adjust(s, idx), idx)
            if behind is not None:
                accumulate(*behind)
            behind = (idx, p, alpha, vt_ones)
        return behind

    def chunk_step(c, carry, variant, variant_after, fix):
        off = pl.multiple_of(c * KV_CHUNK, KV_CHUNK)
        off_before = pl.multiple_of(jnp.maximum(c - 1, 0) * KV_CHUNK, KV_CHUNK)
        off_after = pl.multiple_of(jnp.minimum(c + 1, n_chunks - 1) * KV_CHUNK, KV_CHUNK)
        behind = (n_units - 1, p_ref[...], a_ref[...], _with_ones_rows(vt_ref[:, pl.ds(off_before, KV_CHUNK)]))
        _, p, alpha, _ = chunk([s_ref[j] for j in range(look)], k_ref[pl.ds(off, KV_CHUNK), :],
                               vt_ref[:, pl.ds(off, KV_CHUNK)], N_META + off, None,
                               k_ref[pl.ds(off_after, KV_CHUNK), :], behind, variant, variant_after, fix)
        p_ref[...] = p
        a_ref[...] = alpha
        return carry

    m_ref[...] = jnp.full(m_ref.shape, NEG_BIG, F32)
    acc_ref[...] = jnp.zeros(acc_ref.shape, F32)
    p_ref[...] = jnp.zeros(p_ref.shape, BF16)
    a_ref[...] = jnp.ones(a_ref.shape, F32)
    for j in range(look):
        s_ref[j] = scores(k_ref[0:KV_CHUNK, :], j, first_variant)
    for lo, hi, variant, variant_after, fix in phases:
        lax.fori_loop(lo, hi, functools.partial(chunk_step, variant=variant, variant_after=variant_after, fix=fix), 0)
    accumulate(n_units - 1, p_ref[...], a_ref[...], _with_ones_rows(vt_ref[:, n_tok - KV_CHUNK:n_tok]))
    k_tail = k_ref[n_tok:n_tok + TAIL, :]
    accumulate(*chunk([scores(k_tail, j, tail_variant) for j in range(look)], k_tail, vt_ref[:, n_tok:n_tok + TAIL],
                      0, N_META, None, None, tail_variant, tail_variant, tail_fix))


def _with_ones_rows(vt):
    return jnp.concatenate([vt, jnp.ones((ONES_ROWS, vt.shape[1]), vt.dtype)], axis=0)


def _normalized(acc):
    return acc[:HEAD_DIM] * (1.0 / acc[HEAD_DIM:HEAD_DIM + 1])


def _attn_ab_body(q_ref, k_ref, vt_ref, gate_ref, *rest, group, qsub, tqs, dk, n_tok, pre_transpose, lookahead):
    units = [(g, u) for g in range(group) for u in range(qsub)]
    if pre_transpose:
        o_ref, m_ref, acc_ref, s_ref, p_ref, a_ref, qt_ref = rest[-7:]
        for idx, (g, u) in enumerate(units):
            qt_ref[idx] = q_ref[u * tqs:(u + 1) * tqs, g * dk:(g + 1) * dk].astype(F32).T.astype(BF16)
    else:
        o_ref, m_ref, acc_ref, s_ref, p_ref, a_ref = rest[-6:]

    def scores(kc, idx, variant):
        if pre_transpose:
            return jnp.dot(kc, qt_ref[idx], preferred_element_type=F32)
        g, u = units[idx]
        return lax.dot_general(kc, q_ref[u * tqs:(u + 1) * tqs, g * dk:(g + 1) * dk], _NT,
                               preferred_element_type=F32)

    def make_adjust(k_off, n_keys, n_valid, fix):
        def adjust(s, idx):
            if n_valid is None:
                return s
            row = lax.broadcasted_iota(jnp.int32, s.shape, 0)
            return jnp.where(row < n_valid, s, NEG_BIG)
        return adjust

    _online_softmax_sweep(k_ref, vt_ref, m_ref, acc_ref, s_ref, p_ref, a_ref, len(units), n_tok, scores, make_adjust,
                          lookahead)

    for idx, (g, u) in enumerate(units):
        rows = slice(u * tqs, (u + 1) * tqs)
        cols = slice(g * HEAD_DIM, (g + 1) * HEAD_DIM)
        o = _normalized(acc_ref[idx]).T
        o_ref[rows, cols] = (o * gate_ref[rows, cols]).astype(o_ref.dtype)


def _attn_c_body(q_ref, k_ref, vt_ref, gate_ref, lam_ref, slope_ref, gsub_ref, *rest,
                 qsub, tqs, n_tok, q_is_meta, lambda_init, lookahead):
    o_ref, m_ref, acc_ref, s_ref, p_ref, a_ref, qt_ref = rest[-7:]
    units = [(u, c) for u in range(qsub) for c in range(2)]
    lane = lax.broadcasted_iota(jnp.int32, (1, LANES), 1)
    flip = jnp.where(lane < C_QK_DIM, 1.0, -1.0)
    for idx, (u, c) in enumerate(units):
        q = q_ref[u * tqs:(u + 1) * tqs, c * LANES:(c + 1) * LANES].astype(F32)
        qt_ref[0, idx] = q.T.astype(BF16)
        qt_ref[1, idx] = (q * flip).T.astype(BF16)
    slope2 = 2.0 * slope_ref[:, :1]
    tq = qsub * tqs
    n_chunks = n_tok // KV_CHUNK
    if q_is_meta:
        q_off = 0
        phases = [(0, n_chunks, 1, 1, False)]
        first_variant, tail_fix = 1, True
    else:
        assert tq % KV_CHUNK == 0
        span = tq // KV_CHUNK
        i = pl.program_id(2)
        q_off = N_META + i * tq
        d = i * span
        phases = ([(0, d, 0, 0, False)]
                  + [(d + j, d + j + 1, 0, 0 if j + 1 < span else 1, True) for j in range(span)]
                  + [(d + span, n_chunks, 1, 1, False)])
        first_variant, tail_fix = 0, False

    def scores(kc, idx, variant):
        _, c = units[idx]
        return jnp.dot(kc[:, c * LANES:(c + 1) * LANES], qt_ref[variant, idx], preferred_element_type=F32)

    def make_adjust(k_off, n_keys, n_valid, fix):
        key = lax.broadcasted_iota(jnp.int32, (n_keys, tqs), 0)
        qry = lax.broadcasted_iota(jnp.int32, (n_keys, tqs), 1)
        late = {}

        def adjust(s, idx):
            u, _ = units[idx]
            if fix:
                if u not in late:
                    ahead = ((key - qry) + (k_off - q_off - u * tqs)).astype(F32)
                    late[u] = slope2 * jnp.maximum(ahead, 0.0)
                s = s - late[u]
            if n_valid is not None:
                s = jnp.where(key < n_valid, s, NEG_BIG)
            return s
        return adjust

    _online_softmax_sweep(k_ref, vt_ref, m_ref, acc_ref, s_ref, p_ref, a_ref, len(units), n_tok, scores, make_adjust,
                          lookahead, phases=phases, first_variant=first_variant, tail_variant=0, tail_fix=tail_fix,
                          coef=LOG2E)

    lv = lam_ref[...]
    lam = (jnp.exp(jnp.sum(lv[0:1] * lv[1:2], axis=1, keepdims=True))
           - jnp.exp(jnp.sum(lv[2:3] * lv[3:4], axis=1, keepdims=True)) + lambda_init)
    for u in range(qsub):
        rows = slice(u * tqs, (u + 1) * tqs)
        o = (_normalized(acc_ref[2 * u]) - lam * _normalized(acc_ref[2 * u + 1])).T
        ms = jnp.mean(o * o, axis=-1, keepdims=True)
        o = ((o * lax.rsqrt(ms + NORM_EPS)) * gsub_ref[...]) * (1.0 - lambda_init)
        o_ref[rows, :] = (o * gate_ref[rows, :]).astype(o_ref.dtype)


def _attention(body, q, q_col0, q_w, k, k_col0, k_w, vt, v_row0, gate, gate_col0, extras, extra_specs,
               out_w_total, heads, group, qsub, n_units, n_tok, name, lookahead, qt_shape):
    bsz, lp, _ = q.shape
    out_w = group * HEAD_DIM

    def call(qsub_, tqs, n_qblk, qblk0, prev):
        tq = qsub_ * tqs
        in_specs = [
            pl.BlockSpec((None, tq, q_w), lambda b, h, i: (b, qblk0 + i, q_col0 + h)),
            pl.BlockSpec((None, lp, k_w), lambda b, h, i: (b, 0, k_col0 + h)),
            pl.BlockSpec((HEAD_DIM, lp), lambda b, h, i: (v_row0 + h, b)),
            pl.BlockSpec((None, tq, out_w), lambda b, h, i: (b, qblk0 + i, gate_col0 + h)),
        ] + extra_specs
        args = [q, k, vt, gate] + extras
        aliases = {}
        if prev is not None:
            in_specs.append(pl.BlockSpec(memory_space=pl.ANY))
            args.append(prev)
            aliases = {len(args) - 1: 0}
        units = n_units * qsub_
        qt = qt_shape(units, tqs)
        return pl.pallas_call(
            body(qsub_, tqs, prev is not None),
            grid=(bsz, heads, n_qblk),
            in_specs=in_specs,
            out_specs=pl.BlockSpec((None, tq, out_w), lambda b, h, i: (b, qblk0 + i, h)),
            out_shape=jax.ShapeDtypeStruct((bsz, lp, out_w_total), BF16),
            scratch_shapes=[pltpu.VMEM((units, 1, tqs), F32),
                            pltpu.VMEM((units, HEAD_DIM + ONES_ROWS, tqs), F32),
                            pltpu.VMEM((lookahead, KV_CHUNK, tqs), F32),
                            pltpu.VMEM((KV_CHUNK, tqs), BF16), pltpu.VMEM((1, tqs), F32)]
                           + ([] if qt is None else [pltpu.VMEM(qt, BF16)]),
            input_output_aliases=aliases,
            compiler_params=_cparams(3),
            name=name + ("_meta" if prev is not None else ""),
        )(*args)

    main = call(qsub, Q_SUB, n_tok // (qsub * Q_SUB), 0, None)
    return call(1, TAIL, 1, n_tok // TAIL, main)


def _attn_a(qk, vt_plain, gates, n_tok):
    def body(qsub_, tqs, is_meta):
        return functools.partial(_attn_ab_body, group=A_GROUP, qsub=qsub_, tqs=tqs, dk=HEAD_DIM, n_tok=n_tok,
                                 pre_transpose=False, lookahead=A_LOOKAHEAD)

    return _attention(body, qk, 0, A_GROUP * HEAD_DIM, qk, A_WIDTH // HEAD_DIM, HEAD_DIM, vt_plain, 0,
                      gates, 0, [], [], A_WIDTH, A_KV_HEADS, A_GROUP, 2, A_GROUP, n_tok, "attn_gqa",
                      A_LOOKAHEAD, lambda units, tqs: None)


def _attn_b(qb, kb, vt_b, gates, n_tok):
    dk = 2 * LANES

    def body(qsub_, tqs, is_meta):
        return functools.partial(_attn_ab_body, group=1, qsub=qsub_, tqs=tqs, dk=dk, n_tok=n_tok,
                                 pre_transpose=True, lookahead=B_LOOKAHEAD)

    return _attention(body, qb, 0, dk, kb, 0, dk, vt_b, 0, gates, A_WIDTH // HEAD_DIM, [], [],
                      B_WIDTH, B_HEADS, 1, 4, 1, n_tok, "attn_mla", B_LOOKAHEAD,
                      lambda units, tqs: (units, dk, tqs))


def _attn_c(cqk, vt_plain, gates, lamvec, slopes, g_sub, lambda_init, n_tok):
    dk = 2 * LANES

    def body(qsub_, tqs, is_meta):
        return functools.partial(_attn_c_body, qsub=qsub_, tqs=tqs, n_tok=n_tok, q_is_meta=is_meta,
                                 lambda_init=lambda_init, lookahead=C_LOOKAHEAD)

    extras = [lamvec, slopes, g_sub.reshape(1, C_VDIM)]
    extra_specs = [pl.BlockSpec((4, C_QK_DIM), lambda b, h, i: (0, 0)),
                   pl.BlockSpec((None, 1, LANES), lambda b, h, i: (h, 0, 0)),
                   pl.BlockSpec((1, C_VDIM), lambda b, h, i: (0, 0))]
    return _attention(body, cqk, 0, dk, cqk, C_HEADS, dk, vt_plain, A_KV_HEADS, gates,
                      (A_WIDTH + B_WIDTH) // HEAD_DIM, extras, extra_specs,
                      C_WIDTH, C_HEADS, 1, 4, 2, n_tok, "attn_diff", C_LOOKAHEAD,
                      lambda units, tqs: (2, units, LANES, tqs))


def _out_proj_body(a_ref, b_ref, c_ref, wa_ref, wb_ref, wc_ref, h_ref, o_ref):
    acc = jnp.dot(a_ref[...], wa_ref[...], preferred_element_type=F32)
    acc += jnp.dot(b_ref[...], wb_ref[...], preferred_element_type=F32)
    acc += jnp.dot(c_ref[...], wc_ref[...], preferred_element_type=F32)
    o_ref[...] = h_ref[...] + acc


def _out_proj(oa, ob, oc, w, h2d):
    rows = h2d.shape[0]
    tm = _tall_row_tile(rows)
    tn = 512
    return pl.pallas_call(
        _out_proj_body,
        grid=(rows // tm, D_MODEL // tn),
        in_specs=[pl.BlockSpec((tm, A_WIDTH), lambda i, j: (i, 0)),
                  pl.BlockSpec((tm, B_WIDTH), lambda i, j: (i, 0)),
                  pl.BlockSpec((tm, C_WIDTH), lambda i, j: (i, 0)),
                  pl.BlockSpec((A_WIDTH, tn), lambda i, j: (0, j)),
                  pl.BlockSpec((B_WIDTH, tn), lambda i, j: (A_WIDTH // B_WIDTH, j)),
                  pl.BlockSpec((C_WIDTH, tn), lambda i, j: ((A_WIDTH + B_WIDTH) // C_WIDTH, j)),
                  pl.BlockSpec((tm, tn), lambda i, j: (i, j))],
        out_specs=pl.BlockSpec((tm, tn), lambda i, j: (i, j)),
        out_shape=jax.ShapeDtypeStruct((rows, D_MODEL), F32),
        compiler_params=_cparams(2),
        name="out_proj",
    )(oa, ob, oc, w, w, w, h2d)


def _rope_angles(pos_f, n_freq):
    inv = ROPE_THETA ** (-jnp.arange(n_freq, dtype=F32) / n_freq)
    return pos_f[:, None] * inv[None, :]


def _position_tables(n_tok, bsz):
    rows = n_tok // GRID_W
    z = jnp.zeros((TAIL,), F32)
    row_f = jnp.concatenate([jnp.repeat(jnp.arange(rows, dtype=F32), GRID_W), z])
    col_f = jnp.concatenate([jnp.tile(jnp.arange(GRID_W, dtype=F32), rows), z])
    pos_f = jnp.concatenate([jnp.arange(n_tok, dtype=F32) + N_META,
                             jnp.arange(N_META, dtype=F32), jnp.zeros((TAIL - N_META,), F32)])
    ang_a = jnp.concatenate([_rope_angles(row_f, HEAD_DIM // 4), _rope_angles(col_f, HEAD_DIM // 4)], axis=-1)
    cos_a, sin_a = jnp.cos(ang_a), jnp.sin(ang_a)
    cos_a = jnp.concatenate([cos_a, cos_a], axis=-1)
    sin_a = jnp.concatenate([-sin_a, sin_a], axis=-1)
    ang_b = _rope_angles(pos_f, B_ROPE // 2)
    cos_b, sin_b = jnp.cos(ang_b), jnp.sin(ang_b)
    zb = jnp.zeros_like(cos_b)
    cos_b128 = jnp.concatenate([cos_b, cos_b, zb, zb], axis=-1)
    sin_lo = jnp.concatenate([-sin_b, zb, zb, zb], axis=-1)
    sin_hi = jnp.concatenate([zb, sin_b, zb, zb], axis=-1)
    hi = jnp.floor(pos_f / LANES)
    lo = pos_f - hi * LANES
    one = jnp.ones_like(pos_f)

    def slot_lanes(cols):
        body = jnp.stack(cols, axis=-1)
        return jnp.concatenate([jnp.zeros((lp, C_QK_DIM), F32), body,
                                jnp.zeros((lp, LANES - C_QK_DIM - len(cols)), F32)], axis=-1)

    lp = n_tok + TAIL
    q_terms = slot_lanes([LANES * one, one, -LANES * hi, -lo])
    k_terms = slot_lanes([hi, lo, one, one])
    return tuple(jnp.tile(t, (bsz, 1)) for t in (cos_a, sin_a, cos_b128, sin_lo, sin_hi, q_terms, k_terms))


def _prep_weights(w_in, w_q_b, w_kv_b, w_out, g_qn, g_kn):
    bounds = [0]
    for s in IN_SPLITS:
        bounds.append(bounds[-1] + s)
    col = lambda i: slice(bounds[i], bounds[i + 1])
    w = w_in.astype(BF16)
    aq, ak, av, ag, bq, bkv, bpe, bg, cq, ck, cv, cg = (w[:, col(i)] for i in range(12))
    w_aqk = jnp.concatenate([aq, ak], axis=1)
    w_plain = jnp.concatenate([av, cv], axis=1)
    w_gate = jnp.concatenate([ag, bg, cg], axis=1)
    w_cqk = jnp.concatenate([cq, ck], axis=1)
    w_blow = jnp.concatenate([bq, bkv], axis=1)
    w_bpe = jnp.concatenate([bpe, jnp.zeros((D_MODEL, LANES - B_ROPE), BF16)], axis=1)
    g_aqk = jnp.concatenate([jnp.tile(g_qn, A_HEADS), jnp.tile(g_kn, A_KV_HEADS)]).reshape(1, -1)
    m_aqk = jnp.concatenate([jnp.full((A_WIDTH,), HEAD_DIM ** -0.5 * LOG2E, F32),
                             jnp.ones((A_KV_WIDTH,), F32)]).reshape(1, -1)
    wq = w_q_b.astype(BF16).reshape(B_Q_RANK, B_HEADS, B_NOPE + B_ROPE)
    wq = jnp.concatenate([wq, jnp.zeros((B_Q_RANK, B_HEADS, 2 * LANES - B_NOPE - B_ROPE), BF16)], axis=-1)
    wq = wq.reshape(B_Q_RANK, B_HEADS * 2 * LANES)
    wkv = w_kv_b.astype(BF16).reshape(B_KV_RANK, B_HEADS, B_NOPE + B_VDIM)
    wkv = jnp.concatenate([wkv[:, :, :B_NOPE].reshape(B_KV_RANK, -1),
                           wkv[:, :, B_NOPE:].reshape(B_KV_RANK, -1)], axis=1)
    return dict(aqk=w_aqk, plain=w_plain, gate=w_gate, cqk=w_cqk, blow=w_blow, bpe=w_bpe,
                g_aqk=g_aqk, m_aqk=m_aqk, q_up=wq, kv_up=wkv, out=w_out.astype(BF16))


def _encode(x, meta, layers, g_final, slopes, cq_cols):
    bsz, n_tok, _ = x.shape
    lp = n_tok + TAIL
    rows = bsz * lp
    tail = jnp.concatenate([meta.astype(x.dtype), jnp.zeros((TAIL - N_META, D_MODEL), x.dtype)], axis=0)
    h = jnp.concatenate([x, jnp.broadcast_to(tail[None], (bsz, TAIL, D_MODEL))], axis=1).reshape(rows, D_MODEL)
    cos_a, sin_a, cos_b, sin_lo, sin_hi, q_terms, k_terms = _position_tables(n_tok, bsz)
    three = lambda a: a.reshape(bsz, lp, a.shape[-1])

    for l, p in enumerate(layers):
        lambda_init = 0.8 - 0.6 * math.exp(-0.3 * l)
        u = _rmsnorm_rows(h, p["g_attn"], BF16)
        aqk = _proj(_proj_aqk_body, u, p["aqk"], [p["g_aqk"], p["m_aqk"], cos_a, sin_a],
                    lambda tm, tn: [_col_vec_spec(tm, tn), _col_vec_spec(tm, tn),
                                    _row_table_spec(tm, tn), _row_table_spec(tm, tn)],
                    A_WIDTH + A_KV_WIDTH, BF16, 512, tall=False, name="proj_gqa_qk")
        vt_plain = _proj(_proj_transposed_body, u, p["plain"], [], lambda tm, tn: [],
                         A_KV_WIDTH + C_WIDTH, BF16, 512, transposed=True, name="proj_v")
        gates = _proj(_proj_gate_body, u, p["gate"], [], lambda tm, tn: [],
                      A_WIDTH + B_WIDTH + C_WIDTH, F32, 512, name="proj_gate")
        cqk = _proj(_proj_cqk_body, u, p["cqk"], list(cq_cols) + [q_terms, k_terms],
                    lambda tm, tn: [_col_vec_spec(tm, tn)] * 3 + [_row_table_spec(tm, tn)] * 2,
                    4 * C_QK_WIDTH, BF16, 512, out_tn=1024, name="proj_diff_qk")
        blow = _proj(_proj_plain_body, u, p["blow"], [], lambda tm, tn: [],
                     B_Q_RANK + B_KV_RANK, F32, 512, name="proj_mla_low")
        kpe = _proj(_proj_bpe_body, u, p["bpe"], [cos_b, sin_lo, sin_hi],
                    lambda tm, tn: [_row_table_spec(tm, tn)] * 3, LANES, BF16, LANES, name="proj_mla_pe")
        qb = _bq_up(blow, p["g_q_a"], p["q_up"], cos_b, sin_lo, sin_hi)
        kb, vt_b = _bkv_up(blow, p["g_kv_a"], p["kv_up"], kpe)

        gates3 = three(gates)
        oa = _attn_a(three(aqk), vt_plain, gates3, n_tok)
        ob = _attn_b(three(qb), three(kb), vt_b, gates3, n_tok)
        oc = _attn_c(three(cqk), vt_plain, gates3, p["lamvec"], slopes, p["g_sub"], lambda_init, n_tok)
        h = _out_proj(oa.reshape(rows, -1), ob.reshape(rows, -1), oc.reshape(rows, -1), p["out"], h)

    return _final_norm(h.reshape(bsz, lp, D_MODEL), g_final, n_tok)


def kernel(x_prompt, x_sample, meta, g_attn, w_in, g_qn, g_kn, g_q_a, w_q_b, g_kv_a, w_kv_b,
           lam_q1, lam_k1, lam_q2, lam_k2, g_sub, w_out, g_final):
    layers = []
    for l in range(DEPTH):
        p = _prep_weights(w_in[l], w_q_b[l], w_kv_b[l], w_out[l], g_qn[l], g_kn[l])
        p.update(g_attn=g_attn[l], g_q_a=g_q_a[l], g_kv_a=g_kv_a[l], g_sub=g_sub[l],
                 lamvec=jnp.stack([lam_q1[l], lam_k1[l], lam_q2[l], lam_k2[l]]).astype(F32))
        layers.append(p)
    slope_h = 2.0 ** (-(jnp.arange(C_HEADS, dtype=F32) + 1.0) * 8.0 / C_HEADS)
    slopes = jnp.broadcast_to(slope_h[:, None, None], (C_HEADS, 1, LANES))
    zeros_w, ones_w = jnp.zeros((C_QK_WIDTH,), F32), jnp.ones((C_QK_WIDTH,), F32)
    cq_cols = [jnp.concatenate([C_QK_DIM ** -0.5 * ones_w, ones_w]).reshape(1, -1),
               jnp.concatenate([jnp.repeat(slope_h, 2 * C_QK_DIM), zeros_w]).reshape(1, -1),
               jnp.concatenate([zeros_w, ones_w]).reshape(1, -1)]
    y_prompt = _encode(x_prompt, meta, layers, g_final, slopes, cq_cols)
    y_sample = _encode(x_sample, meta, layers, g_final, slopes, cq_cols)
    return (y_prompt, y_sample)
```

```python
import functools
import math

import jax
import jax.numpy as jnp
from jax import lax
from jax.experimental import pallas as pl
from jax.experimental.pallas import tpu as pltpu

D_MODEL = 4096
DEPTH = 2
N_META = 16
GRID_W = 64
HEAD_DIM = 128
ROPE_THETA = 10000.0
NORM_EPS = 1e-6
A_HEADS = 16
A_KV_HEADS = 4
A_GROUP = A_HEADS // A_KV_HEADS
A_WIDTH = A_HEADS * HEAD_DIM
A_KV_WIDTH = A_KV_HEADS * HEAD_DIM
B_HEADS = 8
B_Q_RANK = 1024
B_KV_RANK = 512
B_NOPE = 128
B_ROPE = 64
B_VDIM = 128
B_WIDTH = B_HEADS * B_VDIM
C_HEADS = 8
C_QK_DIM = 64
C_VDIM = 2 * C_QK_DIM
C_QK_WIDTH = C_HEADS * 2 * C_QK_DIM
C_WIDTH = C_HEADS * C_VDIM
IN_SPLITS = (A_WIDTH, A_KV_WIDTH, A_KV_WIDTH, A_WIDTH,
             B_Q_RANK, B_KV_RANK, B_ROPE, B_WIDTH,
             C_QK_WIDTH, C_QK_WIDTH, C_WIDTH, C_WIDTH)

LANES = 128
TAIL = LANES
KV_CHUNK = 512
Q_SUB = 256
NEG_BIG = -1e30
LOG2E = 1.4426950408889634
A_LOOKAHEAD = 2
B_LOOKAHEAD = 2
C_LOOKAHEAD = 2
ONES_ROWS = 16
BF16 = jnp.bfloat16
F32 = jnp.float32
VMEM_LIMIT = 56 * 1024 * 1024

_NT = (((1,), (1,)), ((), ()))


def _cparams(n_axes):
    return pltpu.CompilerParams(dimension_semantics=("arbitrary",) * n_axes,
                                vmem_limit_bytes=VMEM_LIMIT)


def _row_tile(rows):
    for t in (512, 640, 256, 128):
        if rows % t == 0:
            return t
    raise ValueError(f"no row tile for {rows}")


def _tall_row_tile(rows):
    for t in (1536, 1664):
        if rows % t == 0:
            return t
    return _row_tile(rows)


def _rmsnorm_body(x_ref, g_ref, o_ref):
    x = x_ref[...]
    ms = jnp.mean(x * x, axis=-1, keepdims=True)
    o_ref[...] = ((x * lax.rsqrt(ms + NORM_EPS)) * g_ref[...]).astype(o_ref.dtype)


def _rmsnorm_rows(h2d, g, out_dtype):
    rows, d = h2d.shape
    tm = 256 if rows % 256 == 0 else 128
    return pl.pallas_call(
        _rmsnorm_body,
        grid=(rows // tm,),
        in_specs=[pl.BlockSpec((tm, d), lambda i: (i, 0)),
                  pl.BlockSpec((1, d), lambda i: (0, 0))],
        out_specs=pl.BlockSpec((tm, d), lambda i: (i, 0)),
        out_shape=jax.ShapeDtypeStruct((rows, d), out_dtype),
        compiler_params=_cparams(1),
        name="rmsnorm",
    )(h2d, g.reshape(1, d))


def _final_norm(h3d, g, n_tok):
    bsz, _, d = h3d.shape
    tm = 256
    return pl.pallas_call(
        _rmsnorm_body,
        grid=(bsz, n_tok // tm),
        in_specs=[pl.BlockSpec((None, tm, d), lambda b, i: (b, i, 0)),
                  pl.BlockSpec((1, d), lambda b, i: (0, 0))],
        out_specs=pl.BlockSpec((None, tm, d), lambda b, i: (b, i, 0)),
        out_shape=jax.ShapeDtypeStruct((bsz, n_tok, d), F32),
        compiler_params=_cparams(2),
        name="final_norm",
    )(h3d, g.reshape(1, d))


def _rope_pairs_64(y, cos, sin_signed):
    return y * cos + pltpu.roll(y, 64, 1) * sin_signed


def _rope_pairs_32(x, cos, sin_lo, sin_hi):
    return x * cos + pltpu.roll(x, 96, 1) * sin_lo + pltpu.roll(x, 32, 1) * sin_hi


def _proj_aqk_body(x_ref, w_ref, g_ref, mult_ref, cos_ref, sin_ref, o_ref):
    acc = jnp.dot(x_ref[...], w_ref[...], preferred_element_type=F32)
    cos = cos_ref[...]
    sin = sin_ref[...]
    for h in range(acc.shape[1] // HEAD_DIM):
        sl = slice(h * HEAD_DIM, (h + 1) * HEAD_DIM)
        a = acc[:, sl]
        ms = jnp.mean(a * a, axis=-1, keepdims=True)
        y = (a * lax.rsqrt(ms + NORM_EPS)) * g_ref[:, sl]
        o_ref[:, sl] = (_rope_pairs_64(y, cos, sin) * mult_ref[:, sl]).astype(o_ref.dtype)


def _proj_plain_body(x_ref, w_ref, o_ref):
    o_ref[...] = jnp.dot(x_ref[...], w_ref[...], preferred_element_type=F32).astype(o_ref.dtype)


def _proj_transposed_body(x_ref, w_ref, o_ref):
    o_ref[...] = jnp.dot(x_ref[...], w_ref[...], preferred_element_type=F32).T.astype(o_ref.dtype)


def _proj_gate_body(x_ref, w_ref, o_ref):
    acc = jnp.dot(x_ref[...], w_ref[...], preferred_element_type=F32)
    o_ref[...] = acc * (1.0 / (1.0 + jnp.exp(-acc)))


def _proj_cqk_body(x_ref, w_ref, mult_ref, qcoef_ref, kcoef_ref, qtab_ref, ktab_ref, o_ref):
    acc = jnp.dot(x_ref[...], w_ref[...], preferred_element_type=F32) * mult_ref[...]
    lane = lax.broadcasted_iota(jnp.int32, (acc.shape[0], LANES), 1)
    low = lane < C_QK_DIM
    qtab = qtab_ref[...]
    ktab = ktab_ref[...]
    for t in range(acc.shape[1] // LANES):
        sl = slice(t * LANES, (t + 1) * LANES)
        a = acc[:, sl]
        pos_terms = qtab * qcoef_ref[:, sl] + ktab * kcoef_ref[:, sl]
        o_ref[:, (2 * t) * LANES:(2 * t + 1) * LANES] = jnp.where(low, a, pos_terms).astype(o_ref.dtype)
        o_ref[:, (2 * t + 1) * LANES:(2 * t + 2) * LANES] = (
            jnp.where(low, pltpu.roll(a, 64, 1), pos_terms).astype(o_ref.dtype))


def _proj_bpe_body(x_ref, w_ref, cos_ref, slo_ref, shi_ref, o_ref):
    acc = jnp.dot(x_ref[...], w_ref[...], preferred_element_type=F32)
    o_ref[...] = _rope_pairs_32(acc, cos_ref[...], slo_ref[...], shi_ref[...]).astype(o_ref.dtype)


def _proj(body, u, w, extras, extra_specs, n_out, out_dtype, tn, out_tn=None, transposed=False, tall=True,
          name="proj"):
    rows, k = u.shape
    n = w.shape[1]
    tm = _tall_row_tile(rows) if tall else _row_tile(rows)
    out_tn = tn if out_tn is None else out_tn
    if transposed:
        out_specs = pl.BlockSpec((out_tn, tm), lambda i, j: (j, i))
        out_shape = jax.ShapeDtypeStruct((n_out, rows), out_dtype)
    else:
        out_specs = pl.BlockSpec((tm, out_tn), lambda i, j: (i, j))
        out_shape = jax.ShapeDtypeStruct((rows, n_out), out_dtype)
    return pl.pallas_call(
        body,
        grid=(rows // tm, n // tn),
        in_specs=[pl.BlockSpec((tm, k), lambda i, j: (i, 0)),
                  pl.BlockSpec((k, tn), lambda i, j: (0, j))] + extra_specs(tm, tn),
        out_specs=out_specs,
        out_shape=out_shape,
        compiler_params=_cparams(2),
        name=name,
    )(u, w, *extras)


def _row_table_spec(tm, tn):
    return pl.BlockSpec((tm, LANES), lambda i, j: (i, 0))


def _col_vec_spec(tm, tn):
    return pl.BlockSpec((1, tn), lambda i, j: (0, j))


def _bq_up_body(c_ref, g_ref, w_ref, cos_ref, slo_ref, shi_ref, o_ref):
    c = c_ref[...]
    ms = jnp.mean(c * c, axis=-1, keepdims=True)
    u = ((c * lax.rsqrt(ms + NORM_EPS)) * g_ref[...]).astype(BF16)
    acc = jnp.dot(u, w_ref[...], preferred_element_type=F32) * ((B_NOPE + B_ROPE) ** -0.5 * LOG2E)
    cos, slo, shi = cos_ref[...], slo_ref[...], shi_ref[...]
    for h in range(B_HEADS):
        base = h * 2 * LANES
        o_ref[:, base:base + LANES] = acc[:, base:base + LANES].astype(o_ref.dtype)
        x = acc[:, base + LANES:base + 2 * LANES]
        o_ref[:, base + LANES:base + 2 * LANES] = _rope_pairs_32(x, cos, slo, shi).astype(o_ref.dtype)


def _bkv_up_body(c_ref, g_ref, w_ref, kpe_ref, k_ref, v_ref):
    c = c_ref[...]
    ms = jnp.mean(c * c, axis=-1, keepdims=True)
    u = ((c * lax.rsqrt(ms + NORM_EPS)) * g_ref[...]).astype(BF16)
    acc = jnp.dot(u, w_ref[...], preferred_element_type=F32)
    kpe = kpe_ref[...]
    for h in range(B_HEADS):
        base = h * 2 * LANES
        k_ref[:, base:base + LANES] = acc[:, h * LANES:(h + 1) * LANES].astype(k_ref.dtype)
        k_ref[:, base + LANES:base + 2 * LANES] = kpe
    v_ref[...] = acc[:, B_HEADS * LANES:].T.astype(v_ref.dtype)


def _bq_up(cb, g, w, cos, slo, shi):
    rows = cb.shape[0]
    tm = _row_tile(rows)
    n = w.shape[1]
    tab = pl.BlockSpec((tm, LANES), lambda i: (i, 0))
    return pl.pallas_call(
        _bq_up_body,
        grid=(rows // tm,),
        in_specs=[pl.BlockSpec((tm, B_Q_RANK), lambda i: (i, 0)),
                  pl.BlockSpec((1, B_Q_RANK), lambda i: (0, 0)),
                  pl.BlockSpec((B_Q_RANK, n), lambda i: (0, 0)),
                  tab, tab, tab],
        out_specs=pl.BlockSpec((tm, n), lambda i: (i, 0)),
        out_shape=jax.ShapeDtypeStruct((rows, n), BF16),
        compiler_params=_cparams(1),
        name="mla_q_up",
    )(cb, g.reshape(1, -1), w, cos, slo, shi)


def _bkv_up(cb, g, w, kpe):
    rows = cb.shape[0]
    tm = _row_tile(rows)
    n = w.shape[1]
    return pl.pallas_call(
        _bkv_up_body,
        grid=(rows // tm,),
        in_specs=[pl.BlockSpec((tm, B_KV_RANK), lambda i: (i, B_Q_RANK // B_KV_RANK)),
                  pl.BlockSpec((1, B_KV_RANK), lambda i: (0, 0)),
                  pl.BlockSpec((B_KV_RANK, n), lambda i: (0, 0)),
                  pl.BlockSpec((tm, LANES), lambda i: (i, 0))],
        out_specs=[pl.BlockSpec((tm, B_HEADS * 2 * LANES), lambda i: (i, 0)),
                   pl.BlockSpec((B_HEADS * B_VDIM, tm), lambda i: (0, i))],
        out_shape=[jax.ShapeDtypeStruct((rows, B_HEADS * 2 * LANES), BF16),
                   jax.ShapeDtypeStruct((B_HEADS * B_VDIM, rows), BF16)],
        compiler_params=_cparams(1),
        name="mla_kv_up",
    )(cb, g.reshape(1, -1), w, kpe)


def _online_softmax_sweep(k_ref, vt_ref, m_ref, acc_ref, s_ref, p_ref, a_ref, n_units, n_tok, scores, make_adjust,
                          lookahead, phases=None, first_variant=0, tail_variant=0, tail_fix=False, coef=None):
    look = min(lookahead, n_units)
    n_chunks = n_tok // KV_CHUNK
    if phases is None:
        phases = [(0, n_chunks, 0, 0, False)]

    def probabilities(s, idx):
        m_prev = m_ref[idx]
        m_new = jnp.maximum(m_prev, jnp.max(s, axis=0, keepdims=True))
        m_ref[idx] = m_new
        if coef is None:
            return jnp.exp2(s - m_new).astype(BF16), jnp.exp2(m_prev - m_new)
        return jnp.exp2((s - m_new) * coef).astype(BF16), jnp.exp2((m_prev - m_new) * coef)

    def accumulate(idx, p, alpha, vt_ones):
        acc_ref[idx] = alpha * acc_ref[idx] + jnp.dot(vt_ones, p, preferred_element_type=F32)

    def chunk(s_first, kc, vt, k_off, n_valid, kc_after, behind, variant, variant_after, fix):
        pending = list(s_first)
        vt_ones = _with_ones_rows(vt)
        adjust = make_adjust(k_off, kc.shape[0], n_valid, fix)
        for idx in range(n_units):
            s = pending.pop(0)
            ahead = idx + look
            if ahead < n_units:
                pending.append(scores(kc, ahead, variant))
            elif kc_after is not None:
                s_ref[ahead - n_units] = scores(kc_after, ahead - n_units, variant_after)
            p, alpha = probabilities(adjust(s, idx), idx)
            if behind is not None:
                accumulate(*behind)
            behind = (idx, p, alpha, vt_ones)
        return behind

    def chunk_step(c, carry, variant, variant_after, fix):
        off = pl.multiple_of(c * KV_CHUNK, KV_CHUNK)
        off_before = pl.multiple_of(jnp.maximum(c - 1, 0) * KV_CHUNK, KV_CHUNK)
        off_after = pl.multiple_of(jnp.minimum(c + 1, n_chunks - 1) * KV_CHUNK, KV_CHUNK)
        behind = (n_units - 1, p_ref[...], a_ref[...], _with_ones_rows(vt_ref[:, pl.ds(off_before, KV_CHUNK)]))
        _, p, alpha, _ = chunk([s_ref[j] for j in range(look)], k_ref[pl.ds(off, KV_CHUNK), :],
                               vt_ref[:, pl.ds(off, KV_CHUNK)], N_META + off, None,
                               k_ref[pl.ds(off_after, KV_CHUNK), :], behind, variant, variant_after, fix)
        p_ref[...] = p
        a_ref[...] = alpha
        return carry

    m_ref[...] = jnp.full(m_ref.shape, NEG_BIG, F32)
    acc_ref[...] = jnp.zeros(acc_ref.shape, F32)
    p_ref[...] = jnp.zeros(p_ref.shape, BF16)
    a_ref[...] = jnp.ones(a_ref.shape, F32)
    for j in range(look):
        s_ref[j] = scores(k_ref[0:KV_CHUNK, :], j, first_variant)
    for lo, hi, variant, variant_after, fix in phases:
        lax.fori_loop(lo, hi, functools.partial(chunk_step, variant=variant, variant_after=variant_after, fix=fix), 0)
    accumulate(n_units - 1, p_ref[...], a_ref[...], _with_ones_rows(vt_ref[:, n_tok - KV_CHUNK:n_tok]))
    k_tail = k_ref[n_tok:n_tok + TAIL, :]
    accumulate(*chunk([scores(k_tail, j, tail_variant) for j in range(look)], k_tail, vt_ref[:, n_tok:n_tok + TAIL],
                      0, N_META, None, None, tail_variant, tail_variant, tail_fix))


def _with_ones_rows(vt):
    return jnp.concatenate([vt, jnp.ones((ONES_ROWS, vt.shape[1]), vt.dtype)], axis=0)


def _normalized(acc):
    return acc[:HEAD_DIM] * (1.0 / acc[HEAD_DIM:HEAD_DIM + 1])


def _attn_ab_body(q_ref, k_ref, vt_ref, gate_ref, *rest, group, qsub, tqs, dk, n_tok, pre_transpose, lookahead):
    units = [(g, u) for g in range(group) for u in range(qsub)]
    if pre_transpose:
        o_ref, m_ref, acc_ref, s_ref, p_ref, a_ref, qt_ref = rest[-7:]
        for idx, (g, u) in enumerate(units):
            qt_ref[idx] = q_ref[u * tqs:(u + 1) * tqs, g * dk:(g + 1) * dk].astype(F32).T.astype(BF16)
    else:
        o_ref, m_ref, acc_ref, s_ref, p_ref, a_ref = rest[-6:]

    def scores(kc, idx, variant):
        if pre_transpose:
            return jnp.dot(kc, qt_ref[idx], preferred_element_type=F32)
        g, u = units[idx]
        return lax.dot_general(kc, q_ref[u * tqs:(u + 1) * tqs, g * dk:(g + 1) * dk], _NT,
                               preferred_element_type=F32)

    def make_adjust(k_off, n_keys, n_valid, fix):
        def adjust(s, idx):
            if n_valid is None:
                return s
            row = lax.broadcasted_iota(jnp.int32, s.shape, 0)
            return jnp.where(row < n_valid, s, NEG_BIG)
        return adjust

    _online_softmax_sweep(k_ref, vt_ref, m_ref, acc_ref, s_ref, p_ref, a_ref, len(units), n_tok, scores, make_adjust,
                          lookahead)

    for idx, (g, u) in enumerate(units):
        rows = slice(u * tqs, (u + 1) * tqs)
        cols = slice(g * HEAD_DIM, (g + 1) * HEAD_DIM)
        o = _normalized(acc_ref[idx]).T
        o_ref[rows, cols] = (o * gate_ref[rows, cols]).astype(o_ref.dtype)


def _attn_c_body(q_ref, k_ref, vt_ref, gate_ref, lam_ref, slope_ref, gsub_ref, *rest,
                 qsub, tqs, n_tok, q_is_meta, lambda_init, lookahead):
    o_ref, m_ref, acc_ref, s_ref, p_ref, a_ref, qt_ref = rest[-7:]
    units = [(u, c) for u in range(qsub) for c in range(2)]
    lane = lax.broadcasted_iota(jnp.int32, (1, LANES), 1)
    flip = jnp.where(lane < C_QK_DIM, 1.0, -1.0)
    for idx, (u, c) in enumerate(units):
        q = q_ref[u * tqs:(u + 1) * tqs, c * LANES:(c + 1) * LANES].astype(F32)
        qt_ref[0, idx] = q.T.astype(BF16)
        qt_ref[1, idx] = (q * flip).T.astype(BF16)
    slope2 = 2.0 * slope_ref[:, :1]
    tq = qsub * tqs
    n_chunks = n_tok // KV_CHUNK
    if q_is_meta:
        q_off = 0
        phases = [(0, n_chunks, 1, 1, False)]
        first_variant, tail_fix = 1, True
    else:
        assert tq % KV_CHUNK == 0
        span = tq // KV_CHUNK
        i = pl.program_id(2)
        q_off = N_META + i * tq
        d = i * span
        phases = ([(0, d, 0, 0, False)]
                  + [(d + j, d + j + 1, 0, 0 if j + 1 < span else 1, True) for j in range(span)]
                  + [(d + span, n_chunks, 1, 1, False)])
        first_variant, tail_fix = 0, False

    def scores(kc, idx, variant):
        _, c = units[idx]
        return jnp.dot(kc[:, c * LANES:(c + 1) * LANES], qt_ref[variant, idx], preferred_element_type=F32)

    def make_adjust(k_off, n_keys, n_valid, fix):
        key = lax.broadcasted_iota(jnp.int32, (n_keys, tqs), 0)
        qry = lax.broadcasted_iota(jnp.int32, (n_keys, tqs), 1)
        late = {}

        def adjust(s, idx):
            u, _ = units[idx]
            if fix:
                if u not in late:
                    ahead = ((key - qry) + (k_off - q_off - u * tqs)).astype(F32)
                    late[u] = slope2 * jnp.maximum(ahead, 0.0)
                s = s - late[u]
            if n_valid is not None:
                s = jnp.where(key < n_valid, s, NEG_BIG)
            return s
        return adjust

    _online_softmax_sweep(k_ref, vt_ref, m_ref, acc_ref, s_ref, p_ref, a_ref, len(units), n_tok, scores, make_adjust,
                          lookahead, phases=phases, first_variant=first_variant, tail_variant=0, tail_fix=tail_fix,
                          coef=LOG2E)

    lv = lam_ref[...]
    lam = (jnp.exp(jnp.sum(lv[0:1] * lv[1:2], axis=1, keepdims=True))
           - jnp.exp(jnp.sum(lv[2:3] * lv[3:4], axis=1, keepdims=True)) + lambda_init)
    for u in range(qsub):
        rows = slice(u * tqs, (u + 1) * tqs)
        o = (_normalized(acc_ref[2 * u]) - lam * _normalized(acc_ref[2 * u + 1])).T
        ms = jnp.mean(o * o, axis=-1, keepdims=True)
        o = ((o * lax.rsqrt(ms + NORM_EPS)) * gsub_ref[...]) * (1.0 - lambda_init)
        o_ref[rows, :] = (o * gate_ref[rows, :]).astype(o_ref.dtype)


def _attention(body, q, q_col0, q_w, k, k_col0, k_w, vt, v_row0, gate, gate_col0, extras, extra_specs,
               out_w_total, heads, group, qsub, n_units, n_tok, name, lookahead, qt_shape):
    bsz, lp, _ = q.shape
    out_w = group * HEAD_DIM

    def call(qsub_, tqs, n_qblk, qblk0, prev):
        tq = qsub_ * tqs
        in_specs = [
            pl.BlockSpec((None, tq, q_w), lambda b, h, i: (b, qblk0 + i, q_col0 + h)),
            pl.BlockSpec((None, lp, k_w), lambda b, h, i: (b, 0, k_col0 + h)),
            pl.BlockSpec((HEAD_DIM, lp), lambda b, h, i: (v_row0 + h, b)),
            pl.BlockSpec((None, tq, out_w), lambda b, h, i: (b, qblk0 + i, gate_col0 + h)),
        ] + extra_specs
        args = [q, k, vt, gate] + extras
        aliases = {}
        if prev is not None:
            in_specs.append(pl.BlockSpec(memory_space=pl.ANY))
            args.append(prev)
            aliases = {len(args) - 1: 0}
        units = n_units * qsub_
        qt = qt_shape(units, tqs)
        return pl.pallas_call(
            body(qsub_, tqs, prev is not None),
            grid=(bsz, heads, n_qblk),
            in_specs=in_specs,
            out_specs=pl.BlockSpec((None, tq, out_w), lambda b, h, i: (b, qblk0 + i, h)),
            out_shape=jax.ShapeDtypeStruct((bsz, lp, out_w_total), BF16),
            scratch_shapes=[pltpu.VMEM((units, 1, tqs), F32),
                            pltpu.VMEM((units, HEAD_DIM + ONES_ROWS, tqs), F32),
                            pltpu.VMEM((lookahead, KV_CHUNK, tqs), F32),
                            pltpu.VMEM((KV_CHUNK, tqs), BF16), pltpu.VMEM((1, tqs), F32)]
                           + ([] if qt is None else [pltpu.VMEM(qt, BF16)]),
            input_output_aliases=aliases,
            compiler_params=_cparams(3),
            name=name + ("_meta" if prev is not None else ""),
        )(*args)

    main = call(qsub, Q_SUB, n_tok // (qsub * Q_SUB), 0, None)
    return call(1, TAIL, 1, n_tok // TAIL, main)


def _attn_a(qk, vt_plain, gates, n_tok):
    def body(qsub_, tqs, is_meta):
        return functools.partial(_attn_ab_body, group=A_GROUP, qsub=qsub_, tqs=tqs, dk=HEAD_DIM, n_tok=n_tok,
                                 pre_transpose=False, lookahead=A_LOOKAHEAD)

    return _attention(body, qk, 0, A_GROUP * HEAD_DIM, qk, A_WIDTH // HEAD_DIM, HEAD_DIM, vt_plain, 0,
                      gates, 0, [], [], A_WIDTH, A_KV_HEADS, A_GROUP, 4, A_GROUP, n_tok, "attn_gqa",
                      A_LOOKAHEAD, lambda units, tqs: None)


def _attn_b(qb, kb, vt_b, gates, n_tok):
    dk = 2 * LANES

    def body(qsub_, tqs, is_meta):
        return functools.partial(_attn_ab_body, group=1, qsub=qsub_, tqs=tqs, dk=dk, n_tok=n_tok,
                                 pre_transpose=True, lookahead=B_LOOKAHEAD)

    return _attention(body, qb, 0, dk, kb, 0, dk, vt_b, 0, gates, A_WIDTH // HEAD_DIM, [], [],
                      B_WIDTH, B_HEADS, 1, 8, 1, n_tok, "attn_mla", B_LOOKAHEAD,
                      lambda units, tqs: (units, dk, tqs))


def _attn_c(cqk, vt_plain, gates, lamvec, slopes, g_sub, lambda_init, n_tok):
    dk = 2 * LANES

    def body(qsub_, tqs, is_meta):
        return functools.partial(_attn_c_body, qsub=qsub_, tqs=tqs, n_tok=n_tok, q_is_meta=is_meta,
                                 lambda_init=lambda_init, lookahead=C_LOOKAHEAD)

    extras = [lamvec, slopes, g_sub.reshape(1, C_VDIM)]
    extra_specs = [pl.BlockSpec((4, C_QK_DIM), lambda b, h, i: (0, 0)),
                   pl.BlockSpec((None, 1, LANES), lambda b, h, i: (h, 0, 0)),
                   pl.BlockSpec((1, C_VDIM), lambda b, h, i: (0, 0))]
    return _attention(body, cqk, 0, dk, cqk, C_HEADS, dk, vt_plain, A_KV_HEADS, gates,
                      (A_WIDTH + B_WIDTH) // HEAD_DIM, extras, extra_specs,
                      C_WIDTH, C_HEADS, 1, 4, 2, n_tok, "attn_diff", C_LOOKAHEAD,
                      lambda units, tqs: (2, units, LANES, tqs))


def _out_proj_body(a_ref, b_ref, c_ref, wa_ref, wb_ref, wc_ref, h_ref, o_ref):
    acc = jnp.dot(a_ref[...], wa_ref[...], preferred_element_type=F32)
    acc += jnp.dot(b_ref[...], wb_ref[...], preferred_element_type=F32)
    acc += jnp.dot(c_ref[...], wc_ref[...], preferred_element_type=F32)
    o_ref[...] = h_ref[...] + acc


def _out_proj(oa, ob, oc, w, h2d):
    rows = h2d.shape[0]
    tm = _tall_row_tile(rows)
    tn = 512
    return pl.pallas_call(
        _out_proj_body,
        grid=(rows // tm, D_MODEL // tn),
        in_specs=[pl.BlockSpec((tm, A_WIDTH), lambda i, j: (i, 0)),
                  pl.BlockSpec((tm, B_WIDTH), lambda i, j: (i, 0)),
                  pl.BlockSpec((tm, C_WIDTH), lambda i, j: (i, 0)),
                  pl.BlockSpec((A_WIDTH, tn), lambda i, j: (0, j)),
                  pl.BlockSpec((B_WIDTH, tn), lambda i, j: (A_WIDTH // B_WIDTH, j)),
                  pl.BlockSpec((C_WIDTH, tn), lambda i, j: ((A_WIDTH + B_WIDTH) // C_WIDTH, j)),
                  pl.BlockSpec((tm, tn), lambda i, j: (i, j))],
        out_specs=pl.BlockSpec((tm, tn), lambda i, j: (i, j)),
        out_shape=jax.ShapeDtypeStruct((rows, D_MODEL), F32),
        compiler_params=_cparams(2),
        name="out_proj",
    )(oa, ob, oc, w, w, w, h2d)


def _rope_angles(pos_f, n_freq):
    inv = ROPE_THETA ** (-jnp.arange(n_freq, dtype=F32) / n_freq)
    return pos_f[:, None] * inv[None, :]


def _position_tables(n_tok, bsz):
    rows = n_tok // GRID_W
    z = jnp.zeros((TAIL,), F32)
    row_f = jnp.concatenate([jnp.repeat(jnp.arange(rows, dtype=F32), GRID_W), z])
    col_f = jnp.concatenate([jnp.tile(jnp.arange(GRID_W, dtype=F32), rows), z])
    pos_f = jnp.concatenate([jnp.arange(n_tok, dtype=F32) + N_META,
                             jnp.arange(N_META, dtype=F32), jnp.zeros((TAIL - N_META,), F32)])
    ang_a = jnp.concatenate([_rope_angles(row_f, HEAD_DIM // 4), _rope_angles(col_f, HEAD_DIM // 4)], axis=-1)
    cos_a, sin_a = jnp.cos(ang_a), jnp.sin(ang_a)
    cos_a = jnp.concatenate([cos_a, cos_a], axis=-1)
    sin_a = jnp.concatenate([-sin_a, sin_a], axis=-1)
    ang_b = _rope_angles(pos_f, B_ROPE // 2)
    cos_b, sin_b = jnp.cos(ang_b), jnp.sin(ang_b)
    zb = jnp.zeros_like(cos_b)
    cos_b128 = jnp.concatenate([cos_b, cos_b, zb, zb], axis=-1)
    sin_lo = jnp.concatenate([-sin_b, zb, zb, zb], axis=-1)
    sin_hi = jnp.concatenate([zb, sin_b, zb, zb], axis=-1)
    hi = jnp.floor(pos_f / LANES)
    lo = pos_f - hi * LANES
    one = jnp.ones_like(pos_f)

    def slot_lanes(cols):
        body = jnp.stack(cols, axis=-1)
        return jnp.concatenate([jnp.zeros((lp, C_QK_DIM), F32), body,
                                jnp.zeros((lp, LANES - C_QK_DIM - len(cols)), F32)], axis=-1)

    lp = n_tok + TAIL
    q_terms = slot_lanes([LANES * one, one, -LANES * hi, -lo])
    k_terms = slot_lanes([hi, lo, one, one])
    return tuple(jnp.tile(t, (bsz, 1)) for t in (cos_a, sin_a, cos_b128, sin_lo, sin_hi, q_terms, k_terms))


def _prep_weights(w_in, w_q_b, w_kv_b, w_out, g_qn, g_kn):
    bounds = [0]
    for s in IN_SPLITS:
        bounds.append(bounds[-1] + s)
    col = lambda i: slice(bounds[i], bounds[i + 1])
    w = w_in.astype(BF16)
    aq, ak, av, ag, bq, bkv, bpe, bg, cq, ck, cv, cg = (w[:, col(i)] for i in range(12))
    w_aqk = jnp.concatenate([aq, ak], axis=1)
    w_plain = jnp.concatenate([av, cv], axis=1)
    w_gate = jnp.concatenate([ag, bg, cg], axis=1)
    w_cqk = jnp.concatenate([cq, ck], axis=1)
    w_blow = jnp.concatenate([bq, bkv], axis=1)
    w_bpe = jnp.concatenate([bpe, jnp.zeros((D_MODEL, LANES - B_ROPE), BF16)], axis=1)
    g_aqk = jnp.concatenate([jnp.tile(g_qn, A_HEADS), jnp.tile(g_kn, A_KV_HEADS)]).reshape(1, -1)
    m_aqk = jnp.concatenate([jnp.full((A_WIDTH,), HEAD_DIM ** -0.5 * LOG2E, F32),
                             jnp.ones((A_KV_WIDTH,), F32)]).reshape(1, -1)
    wq = w_q_b.astype(BF16).reshape(B_Q_RANK, B_HEADS, B_NOPE + B_ROPE)
    wq = jnp.concatenate([wq, jnp.zeros((B_Q_RANK, B_HEADS, 2 * LANES - B_NOPE - B_ROPE), BF16)], axis=-1)
    wq = wq.reshape(B_Q_RANK, B_HEADS * 2 * LANES)
    wkv = w_kv_b.astype(BF16).reshape(B_KV_RANK, B_HEADS, B_NOPE + B_VDIM)
    wkv = jnp.concatenate([wkv[:, :, :B_NOPE].reshape(B_KV_RANK, -1),
                           wkv[:, :, B_NOPE:].reshape(B_KV_RANK, -1)], axis=1)
    return dict(aqk=w_aqk, plain=w_plain, gate=w_gate, cqk=w_cqk, blow=w_blow, bpe=w_bpe,
                g_aqk=g_aqk, m_aqk=m_aqk, q_up=wq, kv_up=wkv, out=w_out.astype(BF16))


def _encode(x, meta, layers, g_final, slopes, cq_cols):
    bsz, n_tok, _ = x.shape
    lp = n_tok + TAIL
    rows = bsz * lp
    tail = jnp.concatenate([meta.astype(x.dtype), jnp.zeros((TAIL - N_META, D_MODEL), x.dtype)], axis=0)
    h = jnp.concatenate([x, jnp.broadcast_to(tail[None], (bsz, TAIL, D_MODEL))], axis=1).reshape(rows, D_MODEL)
    cos_a, sin_a, cos_b, sin_lo, sin_hi, q_terms, k_terms = _position_tables(n_tok, bsz)
    three = lambda a: a.reshape(bsz, lp, a.shape[-1])

    for l, p in enumerate(layers):
        lambda_init = 0.8 - 0.6 * math.exp(-0.3 * l)
        u = _rmsnorm_rows(h, p["g_attn"], BF16)
        aqk = _proj(_proj_aqk_body, u, p["aqk"], [p["g_aqk"], p["m_aqk"], cos_a, sin_a],
                    lambda tm, tn: [_col_vec_spec(tm, tn), _col_vec_spec(tm, tn),
                                    _row_table_spec(tm, tn), _row_table_spec(tm, tn)],
                    A_WIDTH + A_KV_WIDTH, BF16, 512, tall=False, name="proj_gqa_qk")
        vt_plain = _proj(_proj_transposed_body, u, p["plain"], [], lambda tm, tn: [],
                         A_KV_WIDTH + C_WIDTH, BF16, 512, transposed=True, name="proj_v")
        gates = _proj(_proj_gate_body, u, p["gate"], [], lambda tm, tn: [],
                      A_WIDTH + B_WIDTH + C_WIDTH, F32, 512, name="proj_gate")
        cqk = _proj(_proj_cqk_body, u, p["cqk"], list(cq_cols) + [q_terms, k_terms],
                    lambda tm, tn: [_col_vec_spec(tm, tn)] * 3 + [_row_table_spec(tm, tn)] * 2,
                    4 * C_QK_WIDTH, BF16, 512, out_tn=1024, name="proj_diff_qk")
        blow = _proj(_proj_plain_body, u, p["blow"], [], lambda tm, tn: [],
                     B_Q_RANK + B_KV_RANK, F32, 512, name="proj_mla_low")
        kpe = _proj(_proj_bpe_body, u, p["bpe"], [cos_b, sin_lo, sin_hi],
                    lambda tm, tn: [_row_table_spec(tm, tn)] * 3, LANES, BF16, LANES, name="proj_mla_pe")
        qb = _bq_up(blow, p["g_q_a"], p["q_up"], cos_b, sin_lo, sin_hi)
        kb, vt_b = _bkv_up(blow, p["g_kv_a"], p["kv_up"], kpe)

        gates3 = three(gates)
        oa = _attn_a(three(aqk), vt_plain, gates3, n_tok)
        ob = _attn_b(three(qb), three(kb), vt_b, gates3, n_tok)
        oc = _attn_c(three(cqk), vt_plain, gates3, p["lamvec"], slopes, p["g_sub"], lambda_init, n_tok)
        h = _out_proj(oa.reshape(rows, -1), ob.reshape(rows, -1), oc.reshape(rows, -1), p["out"], h)

    return _final_norm(h.reshape(bsz, lp, D_MODEL), g_final, n_tok)


def kernel(x_prompt, x_sample, meta, g_attn, w_in, g_qn, g_kn, g_q_a, w_q_b, g_kv_a, w_kv_b,
           lam_q1, lam_k1, lam_q2, lam_k2, g_sub, w_out, g_final):
    layers = []
    for l in range(DEPTH):
        p = _prep_weights(w_in[l], w_q_b[l], w_kv_b[l], w_out[l], g_qn[l], g_kn[l])
        p.update(g_attn=g_attn[l], g_q_a=g_q_a[l], g_kv_a=g_kv_a[l], g_sub=g_sub[l],
                 lamvec=jnp.stack([lam_q1[l], lam_k1[l], lam_q2[l], lam_k2[l]]).astype(F32))
        layers.append(p)
    slope_h = 2.0 ** (-(jnp.arange(C_HEADS, dtype=F32) + 1.0) * 8.0 / C_HEADS)
    slopes = jnp.broadcast_to(slope_h[:, None, None], (C_HEADS, 1, LANES))
    zeros_w, ones_w = jnp.zeros((C_QK_WIDTH,), F32), jnp.ones((C_QK_WIDTH,), F32)
    cq_cols = [jnp.concatenate([C_QK_DIM ** -0.5 * ones_w, ones_w]).reshape(1, -1),
               jnp.concatenate([jnp.repeat(slope_h, 2 * C_QK_DIM), zeros_w]).reshape(1, -1),
               jnp.concatenate([zeros_w, ones_w]).reshape(1, -1)]
    y_prompt = _encode(x_prompt, meta, layers, g_final, slopes, cq_cols)
    y_sample = _encode(x_sample, meta, layers, g_final, slopes, cq_cols)
    return (y_prompt, y_sample)
```

```python
import functools
import math

import jax
import jax.numpy as jnp
from jax import lax
from jax.experimental import pallas as pl
from jax.experimental.pallas import tpu as pltpu

D_MODEL = 4096
DEPTH = 2
N_META = 16
GRID_W = 64
HEAD_DIM = 128
ROPE_THETA = 10000.0
NORM_EPS = 1e-6
A_HEADS = 16
A_KV_HEADS = 4
A_GROUP = A_HEADS // A_KV_HEADS
A_WIDTH = A_HEADS * HEAD_DIM
A_KV_WIDTH = A_KV_HEADS * HEAD_DIM
B_HEADS = 8
B_Q_RANK = 1024
B_KV_RANK = 512
B_NOPE = 128
B_ROPE = 64
B_VDIM = 128
B_WIDTH = B_HEADS * B_VDIM
C_HEADS = 8
C_QK_DIM = 64
C_VDIM = 2 * C_QK_DIM
C_QK_WIDTH = C_HEADS * 2 * C_QK_DIM
C_WIDTH = C_HEADS * C_VDIM
IN_SPLITS = (A_WIDTH, A_KV_WIDTH, A_KV_WIDTH, A_WIDTH,
             B_Q_RANK, B_KV_RANK, B_ROPE, B_WIDTH,
             C_QK_WIDTH, C_QK_WIDTH, C_WIDTH, C_WIDTH)

LANES = 128
TAIL = LANES
KV_CHUNK = 512
Q_SUB = 256
NEG_BIG = -1e30
LOG2E = 1.4426950408889634
A_LOOKAHEAD = 2
B_LOOKAHEAD = 2
C_LOOKAHEAD = 2
ONES_ROWS = 16
BF16 = jnp.bfloat16
F32 = jnp.float32
VMEM_LIMIT = 56 * 1024 * 1024

_NT = (((1,), (1,)), ((), ()))


def _cparams(n_axes):
    return pltpu.CompilerParams(dimension_semantics=("arbitrary",) * n_axes,
                                vmem_limit_bytes=VMEM_LIMIT)


def _row_tile(rows):
    for t in (512, 640, 256, 128):
        if rows % t == 0:
            return t
    raise ValueError(f"no row tile for {rows}")


def _tall_row_tile(rows):
    for t in (1536, 1664):
        if rows % t == 0:
            return t
    return _row_tile(rows)


def _rmsnorm_body(x_ref, g_ref, o_ref):
    x = x_ref[...]
    ms = jnp.mean(x * x, axis=-1, keepdims=True)
    o_ref[...] = ((x * lax.rsqrt(ms + NORM_EPS)) * g_ref[...]).astype(o_ref.dtype)


def _rmsnorm_rows(h2d, g, out_dtype):
    rows, d = h2d.shape
    tm = 256 if rows % 256 == 0 else 128
    return pl.pallas_call(
        _rmsnorm_body,
        grid=(rows // tm,),
        in_specs=[pl.BlockSpec((tm, d), lambda i: (i, 0)),
                  pl.BlockSpec((1, d), lambda i: (0, 0))],
        out_specs=pl.BlockSpec((tm, d), lambda i: (i, 0)),
        out_shape=jax.ShapeDtypeStruct((rows, d), out_dtype),
        compiler_params=_cparams(1),
        name="rmsnorm",
    )(h2d, g.reshape(1, d))


def _final_norm(h3d, g, n_tok):
    bsz, _, d = h3d.shape
    tm = 256
    return pl.pallas_call(
        _rmsnorm_body,
        grid=(bsz, n_tok // tm),
        in_specs=[pl.BlockSpec((None, tm, d), lambda b, i: (b, i, 0)),
                  pl.BlockSpec((1, d), lambda b, i: (0, 0))],
        out_specs=pl.BlockSpec((None, tm, d), lambda b, i: (b, i, 0)),
        out_shape=jax.ShapeDtypeStruct((bsz, n_tok, d), F32),
        compiler_params=_cparams(2),
        name="final_norm",
    )(h3d, g.reshape(1, d))


def _rope_pairs_64(y, cos, sin_signed):
    return y * cos + pltpu.roll(y, 64, 1) * sin_signed


def _rope_pairs_32(x, cos, sin_lo, sin_hi):
    return x * cos + pltpu.roll(x, 96, 1) * sin_lo + pltpu.roll(x, 32, 1) * sin_hi


def _proj_aqk_body(x_ref, w_ref, g_ref, mult_ref, cos_ref, sin_ref, o_ref):
    acc = jnp.dot(x_ref[...], w_ref[...], preferred_element_type=F32)
    cos = cos_ref[...]
    sin = sin_ref[...]
    for h in range(acc.shape[1] // HEAD_DIM):
        sl = slice(h * HEAD_DIM, (h + 1) * HEAD_DIM)
        a = acc[:, sl]
        ms = jnp.mean(a * a, axis=-1, keepdims=True)
        y = (a * lax.rsqrt(ms + NORM_EPS)) * g_ref[:, sl]
        o_ref[:, sl] = (_rope_pairs_64(y, cos, sin) * mult_ref[:, sl]).astype(o_ref.dtype)


def _proj_plain_body(x_ref, w_ref, o_ref):
    o_ref[...] = jnp.dot(x_ref[...], w_ref[...], preferred_element_type=F32).astype(o_ref.dtype)


def _proj_transposed_body(x_ref, w_ref, o_ref):
    o_ref[...] = jnp.dot(x_ref[...], w_ref[...], preferred_element_type=F32).T.astype(o_ref.dtype)


def _proj_gate_body(x_ref, w_ref, o_ref):
    acc = jnp.dot(x_ref[...], w_ref[...], preferred_element_type=F32)
    o_ref[...] = acc * (1.0 / (1.0 + jnp.exp(-acc)))


def _proj_cqk_body(x_ref, w_ref, mult_ref, qcoef_ref, kcoef_ref, qtab_ref, ktab_ref, o_ref):
    acc = jnp.dot(x_ref[...], w_ref[...], preferred_element_type=F32) * mult_ref[...]
    lane = lax.broadcasted_iota(jnp.int32, (acc.shape[0], LANES), 1)
    low = lane < C_QK_DIM
    qtab = qtab_ref[...]
    ktab = ktab_ref[...]
    for t in range(acc.shape[1] // LANES):
        sl = slice(t * LANES, (t + 1) * LANES)
        a = acc[:, sl]
        pos_terms = qtab * qcoef_ref[:, sl] + ktab * kcoef_ref[:, sl]
        o_ref[:, (2 * t) * LANES:(2 * t + 1) * LANES] = jnp.where(low, a, pos_terms).astype(o_ref.dtype)
        o_ref[:, (2 * t + 1) * LANES:(2 * t + 2) * LANES] = (
            jnp.where(low, pltpu.roll(a, 64, 1), pos_terms).astype(o_ref.dtype))


def _proj_bpe_body(x_ref, w_ref, cos_ref, slo_ref, shi_ref, o_ref):
    acc = jnp.dot(x_ref[...], w_ref[...], preferred_element_type=F32)
    o_ref[...] = _rope_pairs_32(acc, cos_ref[...], slo_ref[...], shi_ref[...]).astype(o_ref.dtype)


def _proj(body, u, w, extras, extra_specs, n_out, out_dtype, tn, out_tn=None, transposed=False, tall=True,
          name="proj"):
    rows, k = u.shape
    n = w.shape[1]
    tm = _tall_row_tile(rows) if tall else _row_tile(rows)
    out_tn = tn if out_tn is None else out_tn
    if transposed:
        out_specs = pl.BlockSpec((out_tn, tm), lambda i, j: (j, i))
        out_shape = jax.ShapeDtypeStruct((n_out, rows), out_dtype)
    else:
        out_specs = pl.BlockSpec((tm, out_tn), lambda i, j: (i, j))
        out_shape = jax.ShapeDtypeStruct((rows, n_out), out_dtype)
    return pl.pallas_call(
        body,
        grid=(rows // tm, n // tn),
        in_specs=[pl.BlockSpec((tm, k), lambda i, j: (i, 0)),
                  pl.BlockSpec((k, tn), lambda i, j: (0, j))] + extra_specs(tm, tn),
        out_specs=out_specs,
        out_shape=out_shape,
        compiler_params=_cparams(2),
        name=name,
    )(u, w, *extras)


def _row_table_spec(tm, tn):
    return pl.BlockSpec((tm, LANES), lambda i, j: (i, 0))


def _col_vec_spec(tm, tn):
    return pl.BlockSpec((1, tn), lambda i, j: (0, j))


def _bq_up_body(c_ref, g_ref, w_ref, cos_ref, slo_ref, shi_ref, o_ref):
    c = c_ref[...]
    ms = jnp.mean(c * c, axis=-1, keepdims=True)
    u = ((c * lax.rsqrt(ms + NORM_EPS)) * g_ref[...]).astype(BF16)
    acc = jnp.dot(u, w_ref[...], preferred_element_type=F32) * ((B_NOPE + B_ROPE) ** -0.5 * LOG2E)
    cos, slo, shi = cos_ref[...], slo_ref[...], shi_ref[...]
    for h in range(B_HEADS):
        base = h * 2 * LANES
        o_ref[:, base:base + LANES] = acc[:, base:base + LANES].astype(o_ref.dtype)
        x = acc[:, base + LANES:base + 2 * LANES]
        o_ref[:, base + LANES:base + 2 * LANES] = _rope_pairs_32(x, cos, slo, shi).astype(o_ref.dtype)


def _bkv_up_body(c_ref, g_ref, w_ref, kpe_ref, k_ref, v_ref):
    c = c_ref[...]
    ms = jnp.mean(c * c, axis=-1, keepdims=True)
    u = ((c * lax.rsqrt(ms + NORM_EPS)) * g_ref[...]).astype(BF16)
    acc = jnp.dot(u, w_ref[...], preferred_element_type=F32)
    kpe = kpe_ref[...]
    for h in range(B_HEADS):
        base = h * 2 * LANES
        k_ref[:, base:base + LANES] = acc[:, h * LANES:(h + 1) * LANES].astype(k_ref.dtype)
        k_ref[:, base + LANES:base + 2 * LANES] = kpe
    v_ref[...] = acc[:, B_HEADS * LANES:].T.astype(v_ref.dtype)


def _bq_up(cb, g, w, cos, slo, shi):
    rows = cb.shape[0]
    tm = _row_tile(rows)
    n = w.shape[1]
    tab = pl.BlockSpec((tm, LANES), lambda i: (i, 0))
    return pl.pallas_call(
        _bq_up_body,
        grid=(rows // tm,),
        in_specs=[pl.BlockSpec((tm, B_Q_RANK), lambda i: (i, 0)),
                  pl.BlockSpec((1, B_Q_RANK), lambda i: (0, 0)),
                  pl.BlockSpec((B_Q_RANK, n), lambda i: (0, 0)),
                  tab, tab, tab],
        out_specs=pl.BlockSpec((tm, n), lambda i: (i, 0)),
        out_shape=jax.ShapeDtypeStruct((rows, n), BF16),
        compiler_params=_cparams(1),
        name="mla_q_up",
    )(cb, g.reshape(1, -1), w, cos, slo, shi)


def _bkv_up(cb, g, w, kpe):
    rows = cb.shape[0]
    tm = _row_tile(rows)
    n = w.shape[1]
    return pl.pallas_call(
        _bkv_up_body,
        grid=(rows // tm,),
        in_specs=[pl.BlockSpec((tm, B_KV_RANK), lambda i: (i, B_Q_RANK // B_KV_RANK)),
                  pl.BlockSpec((1, B_KV_RANK), lambda i: (0, 0)),
                  pl.BlockSpec((B_KV_RANK, n), lambda i: (0, 0)),
                  pl.BlockSpec((tm, LANES), lambda i: (i, 0))],
        out_specs=[pl.BlockSpec((tm, B_HEADS * 2 * LANES), lambda i: (i, 0)),
                   pl.BlockSpec((B_HEADS * B_VDIM, tm), lambda i: (0, i))],
        out_shape=[jax.ShapeDtypeStruct((rows, B_HEADS * 2 * LANES), BF16),
                   jax.ShapeDtypeStruct((B_HEADS * B_VDIM, rows), BF16)],
        compiler_params=_cparams(1),
        name="mla_kv_up",
    )(cb, g.reshape(1, -1), w, kpe)


def _online_softmax_sweep(k_ref, vt_ref, m_ref, acc_ref, s_ref, p_ref, a_ref, n_units, n_tok, scores, make_adjust,
                          lookahead, phases=None, first_variant=0, tail_variant=0, tail_fix=False, coef=None):
    look = min(lookahead, n_units)
    n_chunks = n_tok // KV_CHUNK
    if phases is None:
        phases = [(0, n_chunks, 0, 0, False)]

    def probabilities(s, idx):
        m_prev = m_ref[idx]
        m_new = jnp.maximum(m_prev, jnp.max(s, axis=0, keepdims=True))
        m_ref[idx] = m_new
        if coef is None:
            return jnp.exp2(s - m_new).astype(BF16), jnp.exp2(m_prev - m_new)
        return jnp.exp2((s - m_new) * coef).astype(BF16), jnp.exp2((m_prev - m_new) * coef)

    def accumulate(idx, p, alpha, vt_ones):
        acc_ref[idx] = alpha * acc_ref[idx] + jnp.dot(vt_ones, p, preferred_element_type=F32)

    def chunk(s_first, kc, vt, k_off, n_valid, kc_after, behind, variant, variant_after, fix):
        pending = list(s_first)
        vt_ones = _with_ones_rows(vt)
        adjust = make_adjust(k_off, kc.shape[0], n_valid, fix)
        for idx in range(n_units):
            s = pending.pop(0)
            ahead = idx + look
            if ahead < n_units:
                pending.append(scores(kc, ahead, variant))
            elif kc_after is not None:
                s_ref[ahead - n_units] = scores(kc_after, ahead - n_units, variant_after)
            p, alpha = probabilities(adjust(s, idx), idx)
            if behind is not None:
                accumulate(*behind)
            behind = (idx, p, alpha, vt_ones)
        return behind

    def chunk_step(c, carry, variant, variant_after, fix):
        off = pl.multiple_of(c * KV_CHUNK, KV_CHUNK)
        off_before = pl.multiple_of(jnp.maximum(c - 1, 0) * KV_CHUNK, KV_CHUNK)
        off_after = pl.multiple_of(jnp.minimum(c + 1, n_chunks - 1) * KV_CHUNK, KV_CHUNK)
        behind = (n_units - 1, p_ref[...], a_ref[...], _with_ones_rows(vt_ref[:, pl.ds(off_before, KV_CHUNK)]))
        _, p, alpha, _ = chunk([s_ref[j] for j in range(look)], k_ref[pl.ds(off, KV_CHUNK), :],
                               vt_ref[:, pl.ds(off, KV_CHUNK)], N_META + off, None,
                               k_ref[pl.ds(off_after, KV_CHUNK), :], behind, variant, variant_after, fix)
        p_ref[...] = p
        a_ref[...] = alpha
        return carry

    m_ref[...] = jnp.full(m_ref.shape, NEG_BIG, F32)
    acc_ref[...] = jnp.zeros(acc_ref.shape, F32)
    p_ref[...] = jnp.zeros(p_ref.shape, BF16)
    a_ref[...] = jnp.ones(a_ref.shape, F32)
    for j in range(look):
        s_ref[j] = scores(k_ref[0:KV_CHUNK, :], j, first_variant)
    for lo, hi, variant, variant_after, fix in phases:
        lax.fori_loop(lo, hi, functools.partial(chunk_step, variant=variant, variant_after=variant_after, fix=fix), 0)
    accumulate(n_units - 1, p_ref[...], a_ref[...], _with_ones_rows(vt_ref[:, n_tok - KV_CHUNK:n_tok]))
    k_tail = k_ref[n_tok:n_tok + TAIL, :]
    accumulate(*chunk([scores(k_tail, j, tail_variant) for j in range(look)], k_tail, vt_ref[:, n_tok:n_tok + TAIL],
                      0, N_META, None, None, tail_variant, tail_variant, tail_fix))


def _with_ones_rows(vt):
    return jnp.concatenate([vt, jnp.ones((ONES_ROWS, vt.shape[1]), vt.dtype)], axis=0)


def _normalized(acc):
    return acc[:HEAD_DIM] * (1.0 / acc[HEAD_DIM:HEAD_DIM + 1])


def _attn_ab_body(q_ref, k_ref, vt_ref, gate_ref, *rest, group, qsub, tqs, dk, n_tok, pre_transpose, lookahead):
    units = [(g, u) for g in range(group) for u in range(qsub)]
    if pre_transpose:
        o_ref, m_ref, acc_ref, s_ref, p_ref, a_ref, qt_ref = rest[-7:]
        for idx, (g, u) in enumerate(units):
            qt_ref[idx] = q_ref[u * tqs:(u + 1) * tqs, g * dk:(g + 1) * dk].astype(F32).T.astype(BF16)
    else:
        o_ref, m_ref, acc_ref, s_ref, p_ref, a_ref = rest[-6:]

    def scores(kc, idx, variant):
        if pre_transpose:
            return jnp.dot(kc, qt_ref[idx], preferred_element_type=F32)
        g, u = units[idx]
        return lax.dot_general(kc, q_ref[u * tqs:(u + 1) * tqs, g * dk:(g + 1) * dk], _NT,
                               preferred_element_type=F32)

    def make_adjust(k_off, n_keys, n_valid, fix):
        def adjust(s, idx):
            if n_valid is None:
                return s
            row = lax.broadcasted_iota(jnp.int32, s.shape, 0)
            return jnp.where(row < n_valid, s, NEG_BIG)
        return adjust

    _online_softmax_sweep(k_ref, vt_ref, m_ref, acc_ref, s_ref, p_ref, a_ref, len(units), n_tok, scores, make_adjust,
                          lookahead)

    for idx, (g, u) in enumerate(units):
        rows = slice(u * tqs, (u + 1) * tqs)
        cols = slice(g * HEAD_DIM, (g + 1) * HEAD_DIM)
        o = _normalized(acc_ref[idx]).T
        o_ref[rows, cols] = (o * gate_ref[rows, cols]).astype(o_ref.dtype)


def _attn_c_body(q_ref, k_ref, vt_ref, gate_ref, lam_ref, slope_ref, gsub_ref, *rest,
                 qsub, tqs, n_tok, q_is_meta, lambda_init, lookahead):
    o_ref, m_ref, acc_ref, s_ref, p_ref, a_ref, qt_ref = rest[-7:]
    units = [(u, c) for u in range(qsub) for c in range(2)]
    lane = lax.broadcasted_iota(jnp.int32, (1, LANES), 1)
    flip = jnp.where(lane < C_QK_DIM, 1.0, -1.0)
    for idx, (u, c) in enumerate(units):
        q = q_ref[u * tqs:(u + 1) * tqs, c * LANES:(c + 1) * LANES].astype(F32)
        qt_ref[0, idx] = q.T.astype(BF16)
        qt_ref[1, idx] = (q * flip).T.astype(BF16)
    slope2 = 2.0 * slope_ref[:, :1]
    tq = qsub * tqs
    n_chunks = n_tok // KV_CHUNK
    if q_is_meta:
        q_off = 0
        phases = [(0, n_chunks, 1, 1, False)]
        first_variant, tail_fix = 1, True
    else:
        assert tq % KV_CHUNK == 0
        span = tq // KV_CHUNK
        i = pl.program_id(2)
        q_off = N_META + i * tq
        d = i * span
        phases = ([(0, d, 0, 0, False)]
                  + [(d + j, d + j + 1, 0, 0 if j + 1 < span else 1, True) for j in range(span)]
                  + [(d + span, n_chunks, 1, 1, False)])
        first_variant, tail_fix = 0, False

    def scores(kc, idx, variant):
        _, c = units[idx]
        return jnp.dot(kc[:, c * LANES:(c + 1) * LANES], qt_ref[variant, idx], preferred_element_type=F32)

    def make_adjust(k_off, n_keys, n_valid, fix):
        key = lax.broadcasted_iota(jnp.int32, (n_keys, tqs), 0)
        qry = lax.broadcasted_iota(jnp.int32, (n_keys, tqs), 1)
        late = {}

        def adjust(s, idx):
            u, _ = units[idx]
            if fix:
                if u not in late:
                    ahead = ((key - qry) + (k_off - q_off - u * tqs)).astype(F32)
                    late[u] = slope2 * jnp.maximum(ahead, 0.0)
                s = s - late[u]
            if n_valid is not None:
                s = jnp.where(key < n_valid, s, NEG_BIG)
            return s
        return adjust

    _online_softmax_sweep(k_ref, vt_ref, m_ref, acc_ref, s_ref, p_ref, a_ref, len(units), n_tok, scores, make_adjust,
                          lookahead, phases=phases, first_variant=first_variant, tail_variant=0, tail_fix=tail_fix,
                          coef=LOG2E)

    lv = lam_ref[...]
    lam = (jnp.exp(jnp.sum(lv[0:1] * lv[1:2], axis=1, keepdims=True))
           - jnp.exp(jnp.sum(lv[2:3] * lv[3:4], axis=1, keepdims=True)) + lambda_init)
    for u in range(qsub):
        rows = slice(u * tqs, (u + 1) * tqs)
        o = (_normalized(acc_ref[2 * u]) - lam * _normalized(acc_ref[2 * u + 1])).T
        ms = jnp.mean(o * o, axis=-1, keepdims=True)
        o = ((o * lax.rsqrt(ms + NORM_EPS)) * gsub_ref[...]) * (1.0 - lambda_init)
        o_ref[rows, :] = (o * gate_ref[rows, :]).astype(o_ref.dtype)


def _attention(body, q, q_col0, q_w, k, k_col0, k_w, vt, v_row0, gate, gate_col0, extras, extra_specs,
               out_w_total, heads, group, qsub, n_units, n_tok, name, lookahead, qt_shape):
    bsz, lp, _ = q.shape
    out_w = group * HEAD_DIM

    def call(qsub_, tqs, n_qblk, qblk0, prev):
        tq = qsub_ * tqs
        in_specs = [
            pl.BlockSpec((None, tq, q_w), lambda b, h, i: (b, qblk0 + i, q_col0 + h)),
            pl.BlockSpec((None, lp, k_w), lambda b, h, i: (b, 0, k_col0 + h)),
            pl.BlockSpec((HEAD_DIM, lp), lambda b, h, i: (v_row0 + h, b)),
            pl.BlockSpec((None, tq, out_w), lambda b, h, i: (b, qblk0 + i, gate_col0 + h)),
        ] + extra_specs
        args = [q, k, vt, gate] + extras
        aliases = {}
        if prev is not None:
            in_specs.append(pl.BlockSpec(memory_space=pl.ANY))
            args.append(prev)
            aliases = {len(args) - 1: 0}
        units = n_units * qsub_
        qt = qt_shape(units, tqs)
        return pl.pallas_call(
            body(qsub_, tqs, prev is not None),
            grid=(bsz, heads, n_qblk),
            in_specs=in_specs,
            out_specs=pl.BlockSpec((None, tq, out_w), lambda b, h, i: (b, qblk0 + i, h)),
            out_shape=jax.ShapeDtypeStruct((bsz, lp, out_w_total), BF16),
            scratch_shapes=[pltpu.VMEM((units, 1, tqs), F32),
                            pltpu.VMEM((units, HEAD_DIM + ONES_ROWS, tqs), F32),
                            pltpu.VMEM((lookahead, KV_CHUNK, tqs), F32),
                            pltpu.VMEM((KV_CHUNK, tqs), BF16), pltpu.VMEM((1, tqs), F32)]
                           + ([] if qt is None else [pltpu.VMEM(qt, BF16)]),
            input_output_aliases=aliases,
            compiler_params=_cparams(3),
            name=name + ("_meta" if prev is not None else ""),
        )(*args)

    main = call(qsub, Q_SUB, n_tok // (qsub * Q_SUB), 0, None)
    return call(1, TAIL, 1, n_tok // TAIL, main)


def _attn_a(qk, vt_plain, gates, n_tok):
    def body(qsub_, tqs, is_meta):
        return functools.partial(_attn_ab_body, group=A_GROUP, qsub=qsub_, tqs=tqs, dk=HEAD_DIM, n_tok=n_tok,
                                 pre_transpose=False, lookahead=A_LOOKAHEAD)

    return _attention(body, qk, 0, A_GROUP * HEAD_DIM, qk, A_WIDTH // HEAD_DIM, HEAD_DIM, vt_plain, 0,
                      gates, 0, [], [], A_WIDTH, A_KV_HEADS, A_GROUP, 8, A_GROUP, n_tok, "attn_gqa",
                      A_LOOKAHEAD, lambda units, tqs: None)


def _attn_b(qb, kb, vt_b, gates, n_tok):
    dk = 2 * LANES

    def body(qsub_, tqs, is_meta):
        return functools.partial(_attn_ab_body, group=1, qsub=qsub_, tqs=tqs, dk=dk, n_tok=n_tok,
                                 pre_transpose=True, lookahead=B_LOOKAHEAD)

    return _attention(body, qb, 0, dk, kb, 0, dk, vt_b, 0, gates, A_WIDTH // HEAD_DIM, [], [],
                      B_WIDTH, B_HEADS, 1, 16, 1, n_tok, "attn_mla", B_LOOKAHEAD,
                      lambda units, tqs: (units, dk, tqs))


def _attn_c(cqk, vt_plain, gates, lamvec, slopes, g_sub, lambda_init, n_tok):
    dk = 2 * LANES

    def body(qsub_, tqs, is_meta):
        return functools.partial(_attn_c_body, qsub=qsub_, tqs=tqs, n_tok=n_tok, q_is_meta=is_meta,
                                 lambda_init=lambda_init, lookahead=C_LOOKAHEAD)

    extras = [lamvec, slopes, g_sub.reshape(1, C_VDIM)]
    extra_specs = [pl.BlockSpec((4, C_QK_DIM), lambda b, h, i: (0, 0)),
                   pl.BlockSpec((None, 1, LANES), lambda b, h, i: (h, 0, 0)),
                   pl.BlockSpec((1, C_VDIM), lambda b, h, i: (0, 0))]
    return _attention(body, cqk, 0, dk, cqk, C_HEADS, dk, vt_plain, A_KV_HEADS, gates,
                      (A_WIDTH + B_WIDTH) // HEAD_DIM, extras, extra_specs,
                      C_WIDTH, C_HEADS, 1, 8, 2, n_tok, "attn_diff", C_LOOKAHEAD,
                      lambda units, tqs: (2, units, LANES, tqs))


def _out_proj_body(a_ref, b_ref, c_ref, wa_ref, wb_ref, wc_ref, h_ref, o_ref):
    acc = jnp.dot(a_ref[...], wa_ref[...], preferred_element_type=F32)
    acc += jnp.dot(b_ref[...], wb_ref[...], preferred_element_type=F32)
    acc += jnp.dot(c_ref[...], wc_ref[...], preferred_element_type=F32)
    o_ref[...] = h_ref[...] + acc


def _out_proj(oa, ob, oc, w, h2d):
    rows = h2d.shape[0]
    tm = _tall_row_tile(rows)
    tn = 512
    return pl.pallas_call(
        _out_proj_body,
        grid=(rows // tm, D_MODEL // tn),
        in_specs=[pl.BlockSpec((tm, A_WIDTH), lambda i, j: (i, 0)),
                  pl.BlockSpec((tm, B_WIDTH), lambda i, j: (i, 0)),
                  pl.BlockSpec((tm, C_WIDTH), lambda i, j: (i, 0)),
                  pl.BlockSpec((A_WIDTH, tn), lambda i, j: (0, j)),
                  pl.BlockSpec((B_WIDTH, tn), lambda i, j: (A_WIDTH // B_WIDTH, j)),
                  pl.BlockSpec((C_WIDTH, tn), lambda i, j: ((A_WIDTH + B_WIDTH) // C_WIDTH, j)),
                  pl.BlockSpec((tm, tn), lambda i, j: (i, j))],
        out_specs=pl.BlockSpec((tm, tn), lambda i, j: (i, j)),
        out_shape=jax.ShapeDtypeStruct((rows, D_MODEL), F32),
        compiler_params=_cparams(2),
        name="out_proj",
    )(oa, ob, oc, w, w, w, h2d)


def _rope_angles(pos_f, n_freq):
    inv = ROPE_THETA ** (-jnp.arange(n_freq, dtype=F32) / n_freq)
    return pos_f[:, None] * inv[None, :]


def _position_tables(n_tok, bsz):
    rows = n_tok // GRID_W
    z = jnp.zeros((TAIL,), F32)
    row_f = jnp.concatenate([jnp.repeat(jnp.arange(rows, dtype=F32), GRID_W), z])
    col_f = jnp.concatenate([jnp.tile(jnp.arange(GRID_W, dtype=F32), rows), z])
    pos_f = jnp.concatenate([jnp.arange(n_tok, dtype=F32) + N_META,
                             jnp.arange(N_META, dtype=F32), jnp.zeros((TAIL - N_META,), F32)])
    ang_a = jnp.concatenate([_rope_angles(row_f, HEAD_DIM // 4), _rope_angles(col_f, HEAD_DIM // 4)], axis=-1)
    cos_a, sin_a = jnp.cos(ang_a), jnp.sin(ang_a)
    cos_a = jnp.concatenate([cos_a, cos_a], axis=-1)
    sin_a = jnp.concatenate([-sin_a, sin_a], axis=-1)
    ang_b = _rope_angles(pos_f, B_ROPE // 2)
    cos_b, sin_b = jnp.cos(ang_b), jnp.sin(ang_b)
    zb = jnp.zeros_like(cos_b)
    cos_b128 = jnp.concatenate([cos_b, cos_b, zb, zb], axis=-1)
    sin_lo = jnp.concatenate([-sin_b, zb, zb, zb], axis=-1)
    sin_hi = jnp.concatenate([zb, sin_b, zb, zb], axis=-1)
    hi = jnp.floor(pos_f / LANES)
    lo = pos_f - hi * LANES
    one = jnp.ones_like(pos_f)

    def slot_lanes(cols):
        body = jnp.stack(cols, axis=-1)
        return jnp.concatenate([jnp.zeros((lp, C_QK_DIM), F32), body,
                                jnp.zeros((lp, LANES - C_QK_DIM - len(cols)), F32)], axis=-1)

    lp = n_tok + TAIL
    q_terms = slot_lanes([LANES * one, one, -LANES * hi, -lo])
    k_terms = slot_lanes([hi, lo, one, one])
    return tuple(jnp.tile(t, (bsz, 1)) for t in (cos_a, sin_a, cos_b128, sin_lo, sin_hi, q_terms, k_terms))


def _prep_weights(w_in, w_q_b, w_kv_b, w_out, g_qn, g_kn):
    bounds = [0]
    for s in IN_SPLITS:
        bounds.append(bounds[-1] + s)
    col = lambda i: slice(bounds[i], bounds[i + 1])
    w = w_in.astype(BF16)
    aq, ak, av, ag, bq, bkv, bpe, bg, cq, ck, cv, cg = (w[:, col(i)] for i in range(12))
    w_aqk = jnp.concatenate([aq, ak], axis=1)
    w_plain = jnp.concatenate([av, cv], axis=1)
    w_gate = jnp.concatenate([ag, bg, cg], axis=1)
    w_cqk = jnp.concatenate([cq, ck], axis=1)
    w_blow = jnp.concatenate([bq, bkv], axis=1)
    w_bpe = jnp.concatenate([bpe, jnp.zeros((D_MODEL, LANES - B_ROPE), BF16)], axis=1)
    g_aqk = jnp.concatenate([jnp.tile(g_qn, A_HEADS), jnp.tile(g_kn, A_KV_HEADS)]).reshape(1, -1)
    m_aqk = jnp.concatenate([jnp.full((A_WIDTH,), HEAD_DIM ** -0.5 * LOG2E, F32),
                             jnp.ones((A_KV_WIDTH,), F32)]).reshape(1, -1)
    wq = w_q_b.astype(BF16).reshape(B_Q_RANK, B_HEADS, B_NOPE + B_ROPE)
    wq = jnp.concatenate([wq, jnp.zeros((B_Q_RANK, B_HEADS, 2 * LANES - B_NOPE - B_ROPE), BF16)], axis=-1)
    wq = wq.reshape(B_Q_RANK, B_HEADS * 2 * LANES)
    wkv = w_kv_b.astype(BF16).reshape(B_KV_RANK, B_HEADS, B_NOPE + B_VDIM)
    wkv = jnp.concatenate([wkv[:, :, :B_NOPE].reshape(B_KV_RANK, -1),
                           wkv[:, :, B_NOPE:].reshape(B_KV_RANK, -1)], axis=1)
    return dict(aqk=w_aqk, plain=w_plain, gate=w_gate, cqk=w_cqk, blow=w_blow, bpe=w_bpe,
                g_aqk=g_aqk, m_aqk=m_aqk, q_up=wq, kv_up=wkv, out=w_out.astype(BF16))


def _encode(x, meta, layers, g_final, slopes, cq_cols):
    bsz, n_tok, _ = x.shape
    lp = n_tok + TAIL
    rows = bsz * lp
    tail = jnp.concatenate([meta.astype(x.dtype), jnp.zeros((TAIL - N_META, D_MODEL), x.dtype)], axis=0)
    h = jnp.concatenate([x, jnp.broadcast_to(tail[None], (bsz, TAIL, D_MODEL))], axis=1).reshape(rows, D_MODEL)
    cos_a, sin_a, cos_b, sin_lo, sin_hi, q_terms, k_terms = _position_tables(n_tok, bsz)
    three = lambda a: a.reshape(bsz, lp, a.shape[-1])

    for l, p in enumerate(layers):
        lambda_init = 0.8 - 0.6 * math.exp(-0.3 * l)
        u = _rmsnorm_rows(h, p["g_attn"], BF16)
        aqk = _proj(_proj_aqk_body, u, p["aqk"], [p["g_aqk"], p["m_aqk"], cos_a, sin_a],
                    lambda tm, tn: [_col_vec_spec(tm, tn), _col_vec_spec(tm, tn),
                                    _row_table_spec(tm, tn), _row_table_spec(tm, tn)],
                    A_WIDTH + A_KV_WIDTH, BF16, 512, tall=False, name="proj_gqa_qk")
        vt_plain = _proj(_proj_transposed_body, u, p["plain"], [], lambda tm, tn: [],
                         A_KV_WIDTH + C_WIDTH, BF16, 512, transposed=True, name="proj_v")
        gates = _proj(_proj_gate_body, u, p["gate"], [], lambda tm, tn: [],
                      A_WIDTH + B_WIDTH + C_WIDTH, F32, 512, name="proj_gate")
        cqk = _proj(_proj_cqk_body, u, p["cqk"], list(cq_cols) + [q_terms, k_terms],
                    lambda tm, tn: [_col_vec_spec(tm, tn)] * 3 + [_row_table_spec(tm, tn)] * 2,
                    4 * C_QK_WIDTH, BF16, 512, out_tn=1024, name="proj_diff_qk")
        blow = _proj(_proj_plain_body, u, p["blow"], [], lambda tm, tn: [],
                     B_Q_RANK + B_KV_RANK, F32, 512, name="proj_mla_low")
        kpe = _proj(_proj_bpe_body, u, p["bpe"], [cos_b, sin_lo, sin_hi],
                    lambda tm, tn: [_row_table_spec(tm, tn)] * 3, LANES, BF16, LANES, name="proj_mla_pe")
        qb = _bq_up(blow, p["g_q_a"], p["q_up"], cos_b, sin_lo, sin_hi)
        kb, vt_b = _bkv_up(blow, p["g_kv_a"], p["kv_up"], kpe)

        gates3 = three(gates)
        oa = _attn_a(three(aqk), vt_plain, gates3, n_tok)
        ob = _attn_b(three(qb), three(kb), vt_b, gates3, n_tok)
        oc = _attn_c(three(cqk), vt_plain, gates3, p["lamvec"], slopes, p["g_sub"], lambda_init, n_tok)
        h = _out_proj(oa.reshape(rows, -1), ob.reshape(rows, -1), oc.reshape(rows, -1), p["out"], h)

    return _final_norm(h.reshape(bsz, lp, D_MODEL), g_final, n_tok)


def kernel(x_prompt, x_sample, meta, g_attn, w_in, g_qn, g_kn, g_q_a, w_q_b, g_kv_a, w_kv_b,
           lam_q1, lam_k1, lam_q2, lam_k2, g_sub, w_out, g_final):
    layers = []
    for l in range(DEPTH):
        p = _prep_weights(w_in[l], w_q_b[l], w_kv_b[l], w_out[l], g_qn[l], g_kn[l])
        p.update(g_attn=g_attn[l], g_q_a=g_q_a[l], g_kv_a=g_kv_a[l], g_sub=g_sub[l],
                 lamvec=jnp.stack([lam_q1[l], lam_k1[l], lam_q2[l], lam_k2[l]]).astype(F32))
        layers.append(p)
    slope_h = 2.0 ** (-(jnp.arange(C_HEADS, dtype=F32) + 1.0) * 8.0 / C_HEADS)
    slopes = jnp.broadcast_to(slope_h[:, None, None], (C_HEADS, 1, LANES))
    zeros_w, ones_w = jnp.zeros((C_QK_WIDTH,), F32), jnp.ones((C_QK_WIDTH,), F32)
    cq_cols = [jnp.concatenate([C_QK_DIM ** -0.5 * ones_w, ones_w]).reshape(1, -1),
               jnp.concatenate([jnp.repeat(slope_h, 2 * C_QK_DIM), zeros_w]).reshape(1, -1),
               jnp.concatenate([zeros_w, ones_w]).reshape(1, -1)]
    y_prompt = _encode(x_prompt, meta, layers, g_final, slopes, cq_cols)
    y_sample = _encode(x_sample, meta, layers, g_final, slopes, cq_cols)
    return (y_prompt, y_sample)
```

```python
import functools
import math

import jax
import jax.numpy as jnp
from jax import lax
from jax.experimental import pallas as pl
from jax.experimental.pallas import tpu as pltpu

D_MODEL = 4096
DEPTH = 2
N_META = 16
GRID_W = 64
HEAD_DIM = 128
ROPE_THETA = 10000.0
NORM_EPS = 1e-6
A_HEADS = 16
A_KV_HEADS = 4
A_GROUP = A_HEADS // A_KV_HEADS
A_WIDTH = A_HEADS * HEAD_DIM
A_KV_WIDTH = A_KV_HEADS * HEAD_DIM
B_HEADS = 8
B_Q_RANK = 1024
B_KV_RANK = 512
B_NOPE = 128
B_ROPE = 64
B_VDIM = 128
B_WIDTH = B_HEADS * B_VDIM
C_HEADS = 8
C_QK_DIM = 64
C_VDIM = 2 * C_QK_DIM
C_QK_WIDTH = C_HEADS * 2 * C_QK_DIM
C_WIDTH = C_HEADS * C_VDIM
IN_SPLITS = (A_WIDTH, A_KV_WIDTH, A_KV_WIDTH, A_WIDTH,
             B_Q_RANK, B_KV_RANK, B_ROPE, B_WIDTH,
             C_QK_WIDTH, C_QK_WIDTH, C_WIDTH, C_WIDTH)

LANES = 128
TAIL = LANES
KV_CHUNK = 512
Q_SUB = 256
NEG_BIG = -1e30
LOG2E = 1.4426950408889634
A_LOOKAHEAD = 2
B_LOOKAHEAD = 2
C_LOOKAHEAD = 2
ONES_ROWS = 16
BF16 = jnp.bfloat16
F32 = jnp.float32
VMEM_LIMIT = 56 * 1024 * 1024

_NT = (((1,), (1,)), ((), ()))


def _cparams(n_axes):
    return pltpu.CompilerParams(dimension_semantics=("arbitrary",) * n_axes,
                                vmem_limit_bytes=VMEM_LIMIT)


def _row_tile(rows):
    for t in (512, 640, 256, 128):
        if rows % t == 0:
            return t
    raise ValueError(f"no row tile for {rows}")


def _tall_row_tile(rows):
    for t in (1536, 1664):
        if rows % t == 0:
            return t
    return _row_tile(rows)


def _rmsnorm_body(x_ref, g_ref, o_ref):
    x = x_ref[...]
    ms = jnp.mean(x * x, axis=-1, keepdims=True)
    o_ref[...] = ((x * lax.rsqrt(ms + NORM_EPS)) * g_ref[...]).astype(o_ref.dtype)


def _rmsnorm_rows(h2d, g, out_dtype):
    rows, d = h2d.shape
    tm = 256 if rows % 256 == 0 else 128
    return pl.pallas_call(
        _rmsnorm_body,
        grid=(rows // tm,),
        in_specs=[pl.BlockSpec((tm, d), lambda i: (i, 0)),
                  pl.BlockSpec((1, d), lambda i: (0, 0))],
        out_specs=pl.BlockSpec((tm, d), lambda i: (i, 0)),
        out_shape=jax.ShapeDtypeStruct((rows, d), out_dtype),
        compiler_params=_cparams(1),
        name="rmsnorm",
    )(h2d, g.reshape(1, d))


def _final_norm(h3d, g, n_tok):
    bsz, _, d = h3d.shape
    tm = 256
    return pl.pallas_call(
        _rmsnorm_body,
        grid=(bsz, n_tok // tm),
        in_specs=[pl.BlockSpec((None, tm, d), lambda b, i: (b, i, 0)),
                  pl.BlockSpec((1, d), lambda b, i: (0, 0))],
        out_specs=pl.BlockSpec((None, tm, d), lambda b, i: (b, i, 0)),
        out_shape=jax.ShapeDtypeStruct((bsz, n_tok, d), F32),
        compiler_params=_cparams(2),
        name="final_norm",
    )(h3d, g.reshape(1, d))


def _rope_pairs_64(y, cos, sin_signed):
    return y * cos + pltpu.roll(y, 64, 1) * sin_signed


def _rope_pairs_32(x, cos, sin_lo, sin_hi):
    return x * cos + pltpu.roll(x, 96, 1) * sin_lo + pltpu.roll(x, 32, 1) * sin_hi


def _proj_aqk_body(x_ref, w_ref, g_ref, mult_ref, cos_ref, sin_ref, o_ref, acc_ref, *, n_tiles):
    j = pl.program_id(1)

    def matmul():
        acc_ref[j % 2] = jnp.dot(x_ref[...], w_ref[...], preferred_element_type=F32)

    def finish_previous():
        cos = cos_ref[...]
        sin = sin_ref[...]
        slot = (j + 1) % 2
        for h in range(o_ref.shape[1] // HEAD_DIM):
            sl = slice(h * HEAD_DIM, (h + 1) * HEAD_DIM)
            a = acc_ref[slot, :, sl]
            ms = jnp.mean(a * a, axis=-1, keepdims=True)
            y = (a * lax.rsqrt(ms + NORM_EPS)) * g_ref[:, sl]
            o_ref[:, sl] = (_rope_pairs_64(y, cos, sin) * mult_ref[:, sl]).astype(o_ref.dtype)

    @pl.when(j == 0)
    def _():
        matmul()

    @pl.when(jnp.logical_and(j > 0, j < n_tiles))
    def _():
        finish_previous()
        matmul()

    @pl.when(j == n_tiles)
    def _():
        finish_previous()


def _proj_aqk(u, w, g_row, mult_row, cos, sin):
    rows, k = u.shape
    n = w.shape[1]
    tm, tn = _row_tile(rows), 512
    n_tiles = n // tn
    prev = lambda j: jnp.maximum(j - 1, 0)
    return pl.pallas_call(
        functools.partial(_proj_aqk_body, n_tiles=n_tiles),
        grid=(rows // tm, n_tiles + 1),
        in_specs=[pl.BlockSpec((tm, k), lambda i, j: (i, 0)),
                  pl.BlockSpec((k, tn), lambda i, j: (0, jnp.minimum(j, n_tiles - 1))),
                  pl.BlockSpec((1, tn), lambda i, j: (0, prev(j))),
                  pl.BlockSpec((1, tn), lambda i, j: (0, prev(j))),
                  pl.BlockSpec((tm, LANES), lambda i, j: (i, 0)),
                  pl.BlockSpec((tm, LANES), lambda i, j: (i, 0))],
        out_specs=pl.BlockSpec((tm, tn), lambda i, j: (i, prev(j))),
        out_shape=jax.ShapeDtypeStruct((rows, n), BF16),
        scratch_shapes=[pltpu.VMEM((2, tm, tn), F32)],
        compiler_params=_cparams(2),
        name="proj_gqa_qk",
    )(u, w, g_row, mult_row, cos, sin)


def _proj_plain_body(x_ref, w_ref, o_ref):
    o_ref[...] = jnp.dot(x_ref[...], w_ref[...], preferred_element_type=F32).astype(o_ref.dtype)


def _proj_transposed_body(x_ref, w_ref, o_ref):
    o_ref[...] = jnp.dot(x_ref[...], w_ref[...], preferred_element_type=F32).T.astype(o_ref.dtype)


def _proj_gate_body(x_ref, w_ref, o_ref):
    acc = jnp.dot(x_ref[...], w_ref[...], preferred_element_type=F32)
    o_ref[...] = acc * (1.0 / (1.0 + jnp.exp(-acc)))


def _proj_cqk_body(x_ref, w_ref, mult_ref, qcoef_ref, kcoef_ref, qtab_ref, ktab_ref, o_ref):
    acc = jnp.dot(x_ref[...], w_ref[...], preferred_element_type=F32) * mult_ref[...]
    lane = lax.broadcasted_iota(jnp.int32, (acc.shape[0], LANES), 1)
    low = lane < C_QK_DIM
    qtab = qtab_ref[...]
    ktab = ktab_ref[...]
    for t in range(acc.shape[1] // LANES):
        sl = slice(t * LANES, (t + 1) * LANES)
        a = acc[:, sl]
        pos_terms = qtab * qcoef_ref[:, sl] + ktab * kcoef_ref[:, sl]
        o_ref[:, (2 * t) * LANES:(2 * t + 1) * LANES] = jnp.where(low, a, pos_terms).astype(o_ref.dtype)
        o_ref[:, (2 * t + 1) * LANES:(2 * t + 2) * LANES] = (
            jnp.where(low, pltpu.roll(a, 64, 1), pos_terms).astype(o_ref.dtype))


def _proj_bpe_body(x_ref, w_ref, cos_ref, slo_ref, shi_ref, o_ref):
    acc = jnp.dot(x_ref[...], w_ref[...], preferred_element_type=F32)
    o_ref[...] = _rope_pairs_32(acc, cos_ref[...], slo_ref[...], shi_ref[...]).astype(o_ref.dtype)


def _proj(body, u, w, extras, extra_specs, n_out, out_dtype, tn, out_tn=None, transposed=False, tall=True,
          name="proj"):
    rows, k = u.shape
    n = w.shape[1]
    tm = _tall_row_tile(rows) if tall else _row_tile(rows)
    out_tn = tn if out_tn is None else out_tn
    if transposed:
        out_specs = pl.BlockSpec((out_tn, tm), lambda i, j: (j, i))
        out_shape = jax.ShapeDtypeStruct((n_out, rows), out_dtype)
    else:
        out_specs = pl.BlockSpec((tm, out_tn), lambda i, j: (i, j))
        out_shape = jax.ShapeDtypeStruct((rows, n_out), out_dtype)
    return pl.pallas_call(
        body,
        grid=(rows // tm, n // tn),
        in_specs=[pl.BlockSpec((tm, k), lambda i, j: (i, 0)),
                  pl.BlockSpec((k, tn), lambda i, j: (0, j))] + extra_specs(tm, tn),
        out_specs=out_specs,
        out_shape=out_shape,
        compiler_params=_cparams(2),
        name=name,
    )(u, w, *extras)


def _row_table_spec(tm, tn):
    return pl.BlockSpec((tm, LANES), lambda i, j: (i, 0))


def _col_vec_spec(tm, tn):
    return pl.BlockSpec((1, tn), lambda i, j: (0, j))


def _bq_up_body(c_ref, g_ref, w_ref, cos_ref, slo_ref, shi_ref, o_ref):
    c = c_ref[...]
    ms = jnp.mean(c * c, axis=-1, keepdims=True)
    u = ((c * lax.rsqrt(ms + NORM_EPS)) * g_ref[...]).astype(BF16)
    acc = jnp.dot(u, w_ref[...], preferred_element_type=F32) * ((B_NOPE + B_ROPE) ** -0.5 * LOG2E)
    cos, slo, shi = cos_ref[...], slo_ref[...], shi_ref[...]
    for h in range(B_HEADS):
        base = h * 2 * LANES
        o_ref[:, base:base + LANES] = acc[:, base:base + LANES].astype(o_ref.dtype)
        x = acc[:, base + LANES:base + 2 * LANES]
        o_ref[:, base + LANES:base + 2 * LANES] = _rope_pairs_32(x, cos, slo, shi).astype(o_ref.dtype)


def _bkv_up_body(c_ref, g_ref, w_ref, kpe_ref, k_ref, v_ref):
    c = c_ref[...]
    ms = jnp.mean(c * c, axis=-1, keepdims=True)
    u = ((c * lax.rsqrt(ms + NORM_EPS)) * g_ref[...]).astype(BF16)
    acc = jnp.dot(u, w_ref[...], preferred_element_type=F32)
    kpe = kpe_ref[...]
    for h in range(B_HEADS):
        base = h * 2 * LANES
        k_ref[:, base:base + LANES] = acc[:, h * LANES:(h + 1) * LANES].astype(k_ref.dtype)
        k_ref[:, base + LANES:base + 2 * LANES] = kpe
    v_ref[...] = acc[:, B_HEADS * LANES:].T.astype(v_ref.dtype)


def _bq_up(cb, g, w, cos, slo, shi):
    rows = cb.shape[0]
    tm = _row_tile(rows)
    n = w.shape[1]
    tab = pl.BlockSpec((tm, LANES), lambda i: (i, 0))
    return pl.pallas_call(
        _bq_up_body,
        grid=(rows // tm,),
        in_specs=[pl.BlockSpec((tm, B_Q_RANK), lambda i: (i, 0)),
                  pl.BlockSpec((1, B_Q_RANK), lambda i: (0, 0)),
                  pl.BlockSpec((B_Q_RANK, n), lambda i: (0, 0)),
                  tab, tab, tab],
        out_specs=pl.BlockSpec((tm, n), lambda i: (i, 0)),
        out_shape=jax.ShapeDtypeStruct((rows, n), BF16),
        compiler_params=_cparams(1),
        name="mla_q_up",
    )(cb, g.reshape(1, -1), w, cos, slo, shi)


def _bkv_up(cb, g, w, kpe):
    rows = cb.shape[0]
    tm = _row_tile(rows)
    n = w.shape[1]
    return pl.pallas_call(
        _bkv_up_body,
        grid=(rows // tm,),
        in_specs=[pl.BlockSpec((tm, B_KV_RANK), lambda i: (i, B_Q_RANK // B_KV_RANK)),
                  pl.BlockSpec((1, B_KV_RANK), lambda i: (0, 0)),
                  pl.BlockSpec((B_KV_RANK, n), lambda i: (0, 0)),
                  pl.BlockSpec((tm, LANES), lambda i: (i, 0))],
        out_specs=[pl.BlockSpec((tm, B_HEADS * 2 * LANES), lambda i: (i, 0)),
                   pl.BlockSpec((B_HEADS * B_VDIM, tm), lambda i: (0, i))],
        out_shape=[jax.ShapeDtypeStruct((rows, B_HEADS * 2 * LANES), BF16),
                   jax.ShapeDtypeStruct((B_HEADS * B_VDIM, rows), BF16)],
        compiler_params=_cparams(1),
        name="mla_kv_up",
    )(cb, g.reshape(1, -1), w, kpe)


def _online_softmax_sweep(k_ref, vt_ref, m_ref, acc_ref, s_ref, p_ref, a_ref, n_units, n_tok, scores, make_adjust,
                          lookahead, phases=None, first_variant=0, tail_variant=0, tail_fix=False, coef=None):
    look = min(lookahead, n_units)
    n_chunks = n_tok // KV_CHUNK
    if phases is None:
        phases = [(0, n_chunks, 0, 0, False)]

    def probabilities(s, idx):
        m_prev = m_ref[idx]
        m_new = jnp.maximum(m_prev, jnp.max(s, axis=0, keepdims=True))
        m_ref[idx] = m_new
        if coef is None:
            return jnp.exp2(s - m_new).astype(BF16), jnp.exp2(m_prev - m_new)
        return jnp.exp2((s - m_new) * coef).astype(BF16), jnp.exp2((m_prev - m_new) * coef)

    def accumulate(idx, p, alpha, vt_ones):
        acc_ref[idx] = alpha * acc_ref[idx] + jnp.dot(vt_ones, p, preferred_element_type=F32)

    def chunk(s_first, kc, vt, k_off, n_valid, kc_after, behind, variant, variant_after, fix):
        pending = list(s_first)
        vt_ones = _with_ones_rows(vt)
        adjust = make_adjust(k_off, kc.shape[0], n_valid, fix)
        for idx in range(n_units):
            s = pending.pop(0)
            ahead = idx + look
            if ahead < n_units:
                pending.append(scores(kc, ahead, _per_unit(variant, ahead)))
            elif kc_after is not None:
                s_ref[ahead - n_units] = scores(kc_after, ahead - n_units, _per_unit(variant_after, ahead - n_units))
            p, alpha = probabilities(adjust(s, idx), idx)
            if behind is not None:
                accumulate(*behind)
            behind = (idx, p, alpha, vt_ones)
        return behind

    def chunk_step(c, carry, variant, variant_after, fix):
        off = pl.multiple_of(c * KV_CHUNK, KV_CHUNK)
        off_before = pl.multiple_of(jnp.maximum(c - 1, 0) * KV_CHUNK, KV_CHUNK)
        off_after = pl.multiple_of(jnp.minimum(c + 1, n_chunks - 1) * KV_CHUNK, KV_CHUNK)
        behind = (n_units - 1, p_ref[...], a_ref[...], _with_ones_rows(vt_ref[:, pl.ds(off_before, KV_CHUNK)]))
        _, p, alpha, _ = chunk([s_ref[j] for j in range(look)], k_ref[pl.ds(off, KV_CHUNK), :],
                               vt_ref[:, pl.ds(off, KV_CHUNK)], N_META + off, None,
                               k_ref[pl.ds(off_after, KV_CHUNK), :], behind, variant, variant_after, fix)
        p_ref[...] = p
        a_ref[...] = alpha
        return carry

    m_ref[...] = jnp.full(m_ref.shape, NEG_BIG, F32)
    acc_ref[...] = jnp.zeros(acc_ref.shape, F32)
    p_ref[...] = jnp.zeros(p_ref.shape, BF16)
    a_ref[...] = jnp.ones(a_ref.shape, F32)
    for j in range(look):
        s_ref[j] = scores(k_ref[0:KV_CHUNK, :], j, _per_unit(first_variant, j))
    for lo, hi, variant, variant_after, fix in phases:
        lax.fori_loop(lo, hi, functools.partial(chunk_step, variant=variant, variant_after=variant_after, fix=fix), 0)
    accumulate(n_units - 1, p_ref[...], a_ref[...], _with_ones_rows(vt_ref[:, n_tok - KV_CHUNK:n_tok]))
    k_tail = k_ref[n_tok:n_tok + TAIL, :]
    accumulate(*chunk([scores(k_tail, j, _per_unit(tail_variant, j)) for j in range(look)], k_tail,
                      vt_ref[:, n_tok:n_tok + TAIL],
                      0, N_META, None, None, tail_variant, tail_variant, tail_fix))


def _per_unit(choice, idx):
    return choice(idx) if callable(choice) else choice


def _with_ones_rows(vt):
    return jnp.concatenate([vt, jnp.ones((ONES_ROWS, vt.shape[1]), vt.dtype)], axis=0)


def _normalized(acc):
    return acc[:HEAD_DIM] * (1.0 / acc[HEAD_DIM:HEAD_DIM + 1])


def _attn_ab_body(q_ref, k_ref, vt_ref, gate_ref, *rest, group, qsub, tqs, dk, n_tok, pre_transpose, lookahead):
    units = [(g, u) for g in range(group) for u in range(qsub)]
    if pre_transpose:
        o_ref, m_ref, acc_ref, s_ref, p_ref, a_ref, qt_ref = rest[-7:]
        for idx, (g, u) in enumerate(units):
            qt_ref[idx] = q_ref[u * tqs:(u + 1) * tqs, g * dk:(g + 1) * dk].astype(F32).T.astype(BF16)
    else:
        o_ref, m_ref, acc_ref, s_ref, p_ref, a_ref = rest[-6:]

    def scores(kc, idx, variant):
        if pre_transpose:
            return jnp.dot(kc, qt_ref[idx], preferred_element_type=F32)
        g, u = units[idx]
        return lax.dot_general(kc, q_ref[u * tqs:(u + 1) * tqs, g * dk:(g + 1) * dk], _NT,
                               preferred_element_type=F32)

    def make_adjust(k_off, n_keys, n_valid, fix):
        def adjust(s, idx):
            if n_valid is None:
                return s
            row = lax.broadcasted_iota(jnp.int32, s.shape, 0)
            return jnp.where(row < n_valid, s, NEG_BIG)
        return adjust

    _online_softmax_sweep(k_ref, vt_ref, m_ref, acc_ref, s_ref, p_ref, a_ref, len(units), n_tok, scores, make_adjust,
                          lookahead)

    for idx, (g, u) in enumerate(units):
        rows = slice(u * tqs, (u + 1) * tqs)
        cols = slice(g * HEAD_DIM, (g + 1) * HEAD_DIM)
        o = _normalized(acc_ref[idx]).T
        o_ref[rows, cols] = (o * gate_ref[rows, cols]).astype(o_ref.dtype)


def _attn_c_body(q_ref, k_ref, vt_ref, gate_ref, lam_ref, slope_ref, gsub_ref, *rest,
                 qsub, tqs, n_tok, q_is_meta, lambda_init, lookahead):
    o_ref, m_ref, acc_ref, s_ref, p_ref, a_ref, qt_ref = rest[-7:]
    units = [(u, c) for u in range(qsub) for c in range(2)]
    lane = lax.broadcasted_iota(jnp.int32, (1, LANES), 1)
    flip = jnp.where(lane < C_QK_DIM, 1.0, -1.0)
    for idx, (u, c) in enumerate(units):
        q = q_ref[u * tqs:(u + 1) * tqs, c * LANES:(c + 1) * LANES].astype(F32)
        qt_ref[0, idx] = q.T.astype(BF16)
        qt_ref[1, idx] = (q * flip).T.astype(BF16)
    slope2 = 2.0 * slope_ref[:, :1]
    tq = qsub * tqs
    n_chunks = n_tok // KV_CHUNK
    if q_is_meta:
        q_off = 0
        phases = [(0, n_chunks, 1, 1, False)]
        first_variant, tail_fix = 1, True
    else:
        assert tq % KV_CHUNK == 0 and KV_CHUNK % tqs == 0
        span = tq // KV_CHUNK
        per_chunk = KV_CHUNK // tqs
        i = pl.program_id(2)
        q_off = N_META + i * tq
        d = i * span

        def variant_in(j):
            return lambda idx: 1 if units[idx][0] < j * per_chunk else 0

        def fix_in(j):
            return lambda idx: j * per_chunk <= units[idx][0] < (j + 1) * per_chunk

        phases = ([(0, d, 0, 0, False)]
                  + [(d + j, d + j + 1, variant_in(j), variant_in(j + 1) if j + 1 < span else 1, fix_in(j))
                     for j in range(span)]
                  + [(d + span, n_chunks, 1, 1, False)])
        first_variant, tail_fix = 0, False

    def scores(kc, idx, variant):
        _, c = units[idx]
        return jnp.dot(kc[:, c * LANES:(c + 1) * LANES], qt_ref[variant, idx], preferred_element_type=F32)

    def make_adjust(k_off, n_keys, n_valid, fix):
        key = lax.broadcasted_iota(jnp.int32, (n_keys, tqs), 0)
        qry = lax.broadcasted_iota(jnp.int32, (n_keys, tqs), 1)
        late = {}

        def adjust(s, idx):
            u, _ = units[idx]
            if _per_unit(fix, idx):
                if u not in late:
                    ahead = ((key - qry) + (k_off - q_off - u * tqs)).astype(F32)
                    late[u] = slope2 * jnp.maximum(ahead, 0.0)
                s = s - late[u]
            if n_valid is not None:
                s = jnp.where(key < n_valid, s, NEG_BIG)
            return s
        return adjust

    _online_softmax_sweep(k_ref, vt_ref, m_ref, acc_ref, s_ref, p_ref, a_ref, len(units), n_tok, scores, make_adjust,
                          lookahead, phases=phases, first_variant=first_variant, tail_variant=0, tail_fix=tail_fix,
                          coef=LOG2E)

    lv = lam_ref[...]
    lam = (jnp.exp(jnp.sum(lv[0:1] * lv[1:2], axis=1, keepdims=True))
           - jnp.exp(jnp.sum(lv[2:3] * lv[3:4], axis=1, keepdims=True)) + lambda_init)
    for u in range(qsub):
        rows = slice(u * tqs, (u + 1) * tqs)
        o = (_normalized(acc_ref[2 * u]) - lam * _normalized(acc_ref[2 * u + 1])).T
        ms = jnp.mean(o * o, axis=-1, keepdims=True)
        o = ((o * lax.rsqrt(ms + NORM_EPS)) * gsub_ref[...]) * (1.0 - lambda_init)
        o_ref[rows, :] = (o * gate_ref[rows, :]).astype(o_ref.dtype)


def _attention(body, q, q_col0, q_w, k, k_col0, k_w, vt, v_row0, gate, gate_col0, extras, extra_specs,
               out_w_total, heads, group, qsub, n_units, n_tok, name, lookahead, qt_shape):
    bsz, lp, _ = q.shape
    out_w = group * HEAD_DIM

    def call(qsub_, tqs, n_qblk, qblk0, prev):
        tq = qsub_ * tqs
        in_specs = [
            pl.BlockSpec((None, tq, q_w), lambda b, h, i: (b, qblk0 + i, q_col0 + h)),
            pl.BlockSpec((None, lp, k_w), lambda b, h, i: (b, 0, k_col0 + h)),
            pl.BlockSpec((HEAD_DIM, lp), lambda b, h, i: (v_row0 + h, b)),
            pl.BlockSpec((None, tq, out_w), lambda b, h, i: (b, qblk0 + i, gate_col0 + h)),
        ] + extra_specs
        args = [q, k, vt, gate] + extras
        aliases = {}
        if prev is not None:
            in_specs.append(pl.BlockSpec(memory_space=pl.ANY))
            args.append(prev)
            aliases = {len(args) - 1: 0}
        units = n_units * qsub_
        qt = qt_shape(units, tqs)
        return pl.pallas_call(
            body(qsub_, tqs, prev is not None),
            grid=(bsz, heads, n_qblk),
            in_specs=in_specs,
            out_specs=pl.BlockSpec((None, tq, out_w), lambda b, h, i: (b, qblk0 + i, h)),
            out_shape=jax.ShapeDtypeStruct((bsz, lp, out_w_total), BF16),
            scratch_shapes=[pltpu.VMEM((units, 1, tqs), F32),
                            pltpu.VMEM((units, HEAD_DIM + ONES_ROWS, tqs), F32),
                            pltpu.VMEM((lookahead, KV_CHUNK, tqs), F32),
                            pltpu.VMEM((KV_CHUNK, tqs), BF16), pltpu.VMEM((1, tqs), F32)]
                           + ([] if qt is None else [pltpu.VMEM(qt, BF16)]),
            input_output_aliases=aliases,
            compiler_params=_cparams(3),
            name=name + ("_meta" if prev is not None else ""),
        )(*args)

    main = call(qsub, Q_SUB, n_tok // (qsub * Q_SUB), 0, None)
    return call(1, TAIL, 1, n_tok // TAIL, main)


def _attn_a(qk, vt_plain, gates, n_tok):
    def body(qsub_, tqs, is_meta):
        return functools.partial(_attn_ab_body, group=A_GROUP, qsub=qsub_, tqs=tqs, dk=HEAD_DIM, n_tok=n_tok,
                                 pre_transpose=False, lookahead=A_LOOKAHEAD)

    return _attention(body, qk, 0, A_GROUP * HEAD_DIM, qk, A_WIDTH // HEAD_DIM, HEAD_DIM, vt_plain, 0,
                      gates, 0, [], [], A_WIDTH, A_KV_HEADS, A_GROUP, 8, A_GROUP, n_tok, "attn_gqa",
                      A_LOOKAHEAD, lambda units, tqs: None)


def _attn_b(qb, kb, vt_b, gates, n_tok):
    dk = 2 * LANES

    def body(qsub_, tqs, is_meta):
        return functools.partial(_attn_ab_body, group=1, qsub=qsub_, tqs=tqs, dk=dk, n_tok=n_tok,
                                 pre_transpose=True, lookahead=B_LOOKAHEAD)

    return _attention(body, qb, 0, dk, kb, 0, dk, vt_b, 0, gates, A_WIDTH // HEAD_DIM, [], [],
                      B_WIDTH, B_HEADS, 1, 16, 1, n_tok, "attn_mla", B_LOOKAHEAD,
                      lambda units, tqs: (units, dk, tqs))


def _attn_c(cqk, vt_plain, gates, lamvec, slopes, g_sub, lambda_init, n_tok):
    dk = 2 * LANES

    def body(qsub_, tqs, is_meta):
        return functools.partial(_attn_c_body, qsub=qsub_, tqs=tqs, n_tok=n_tok, q_is_meta=is_meta,
                                 lambda_init=lambda_init, lookahead=C_LOOKAHEAD)

    extras = [lamvec, slopes, g_sub.reshape(1, C_VDIM)]
    extra_specs = [pl.BlockSpec((4, C_QK_DIM), lambda b, h, i: (0, 0)),
                   pl.BlockSpec((None, 1, LANES), lambda b, h, i: (h, 0, 0)),
                   pl.BlockSpec((1, C_VDIM), lambda b, h, i: (0, 0))]
    return _attention(body, cqk, 0, dk, cqk, C_HEADS, dk, vt_plain, A_KV_HEADS, gates,
                      (A_WIDTH + B_WIDTH) // HEAD_DIM, extras, extra_specs,
                      C_WIDTH, C_HEADS, 1, 8, 2, n_tok, "attn_diff", C_LOOKAHEAD,
                      lambda units, tqs: (2, units, LANES, tqs))


def _out_proj_body(a_ref, b_ref, c_ref, wa_ref, wb_ref, wc_ref, h_ref, o_ref):
    acc = jnp.dot(a_ref[...], wa_ref[...], preferred_element_type=F32)
    acc += jnp.dot(b_ref[...], wb_ref[...], preferred_element_type=F32)
    acc += jnp.dot(c_ref[...], wc_ref[...], preferred_element_type=F32)
    o_ref[...] = h_ref[...] + acc


def _out_proj(oa, ob, oc, w, h2d):
    rows = h2d.shape[0]
    tm = _tall_row_tile(rows)
    tn = 512
    return pl.pallas_call(
        _out_proj_body,
        grid=(rows // tm, D_MODEL // tn),
        in_specs=[pl.BlockSpec((tm, A_WIDTH), lambda i, j: (i, 0)),
                  pl.BlockSpec((tm, B_WIDTH), lambda i, j: (i, 0)),
                  pl.BlockSpec((tm, C_WIDTH), lambda i, j: (i, 0)),
                  pl.BlockSpec((A_WIDTH, tn), lambda i, j: (0, j)),
                  pl.BlockSpec((B_WIDTH, tn), lambda i, j: (A_WIDTH // B_WIDTH, j)),
                  pl.BlockSpec((C_WIDTH, tn), lambda i, j: ((A_WIDTH + B_WIDTH) // C_WIDTH, j)),
                  pl.BlockSpec((tm, tn), lambda i, j: (i, j))],
        out_specs=pl.BlockSpec((tm, tn), lambda i, j: (i, j)),
        out_shape=jax.ShapeDtypeStruct((rows, D_MODEL), F32),
        compiler_params=_cparams(2),
        name="out_proj",
    )(oa, ob, oc, w, w, w, h2d)


def _rope_angles(pos_f, n_freq):
    inv = ROPE_THETA ** (-jnp.arange(n_freq, dtype=F32) / n_freq)
    return pos_f[:, None] * inv[None, :]


def _position_tables(n_tok, bsz):
    rows = n_tok // GRID_W
    z = jnp.zeros((TAIL,), F32)
    row_f = jnp.concatenate([jnp.repeat(jnp.arange(rows, dtype=F32), GRID_W), z])
    col_f = jnp.concatenate([jnp.tile(jnp.arange(GRID_W, dtype=F32), rows), z])
    pos_f = jnp.concatenate([jnp.arange(n_tok, dtype=F32) + N_META,
                             jnp.arange(N_META, dtype=F32), jnp.zeros((TAIL - N_META,), F32)])
    ang_a = jnp.concatenate([_rope_angles(row_f, HEAD_DIM // 4), _rope_angles(col_f, HEAD_DIM // 4)], axis=-1)
    cos_a, sin_a = jnp.cos(ang_a), jnp.sin(ang_a)
    cos_a = jnp.concatenate([cos_a, cos_a], axis=-1)
    sin_a = jnp.concatenate([-sin_a, sin_a], axis=-1)
    ang_b = _rope_angles(pos_f, B_ROPE // 2)
    cos_b, sin_b = jnp.cos(ang_b), jnp.sin(ang_b)
    zb = jnp.zeros_like(cos_b)
    cos_b128 = jnp.concatenate([cos_b, cos_b, zb, zb], axis=-1)
    sin_lo = jnp.concatenate([-sin_b, zb, zb, zb], axis=-1)
    sin_hi = jnp.concatenate([zb, sin_b, zb, zb], axis=-1)
    hi = jnp.floor(pos_f / LANES)
    lo = pos_f - hi * LANES
    one = jnp.ones_like(pos_f)

    def slot_lanes(cols):
        body = jnp.stack(cols, axis=-1)
        return jnp.concatenate([jnp.zeros((lp, C_QK_DIM), F32), body,
                                jnp.zeros((lp, LANES - C_QK_DIM - len(cols)), F32)], axis=-1)

    lp = n_tok + TAIL
    q_terms = slot_lanes([LANES * one, one, -LANES * hi, -lo])
    k_terms = slot_lanes([hi, lo, one, one])
    return tuple(jnp.tile(t, (bsz, 1)) for t in (cos_a, sin_a, cos_b128, sin_lo, sin_hi, q_terms, k_terms))


def _prep_weights(w_in, w_q_b, w_kv_b, w_out, g_qn, g_kn):
    bounds = [0]
    for s in IN_SPLITS:
        bounds.append(bounds[-1] + s)
    col = lambda i: slice(bounds[i], bounds[i + 1])
    w = w_in.astype(BF16)
    aq, ak, av, ag, bq, bkv, bpe, bg, cq, ck, cv, cg = (w[:, col(i)] for i in range(12))
    w_aqk = jnp.concatenate([aq, ak], axis=1)
    w_plain = jnp.concatenate([av, cv], axis=1)
    w_gate = jnp.concatenate([ag, bg, cg], axis=1)
    w_cqk = jnp.concatenate([cq, ck], axis=1)
    w_blow = jnp.concatenate([bq, bkv], axis=1)
    w_bpe = jnp.concatenate([bpe, jnp.zeros((D_MODEL, LANES - B_ROPE), BF16)], axis=1)
    g_aqk = jnp.concatenate([jnp.tile(g_qn, A_HEADS), jnp.tile(g_kn, A_KV_HEADS)]).reshape(1, -1)
    m_aqk = jnp.concatenate([jnp.full((A_WIDTH,), HEAD_DIM ** -0.5 * LOG2E, F32),
                             jnp.ones((A_KV_WIDTH,), F32)]).reshape(1, -1)
    wq = w_q_b.astype(BF16).reshape(B_Q_RANK, B_HEADS, B_NOPE + B_ROPE)
    wq = jnp.concatenate([wq, jnp.zeros((B_Q_RANK, B_HEADS, 2 * LANES - B_NOPE - B_ROPE), BF16)], axis=-1)
    wq = wq.reshape(B_Q_RANK, B_HEADS * 2 * LANES)
    wkv = w_kv_b.astype(BF16).reshape(B_KV_RANK, B_HEADS, B_NOPE + B_VDIM)
    wkv = jnp.concatenate([wkv[:, :, :B_NOPE].reshape(B_KV_RANK, -1),
                           wkv[:, :, B_NOPE:].reshape(B_KV_RANK, -1)], axis=1)
    return dict(aqk=w_aqk, plain=w_plain, gate=w_gate, cqk=w_cqk, blow=w_blow, bpe=w_bpe,
                g_aqk=g_aqk, m_aqk=m_aqk, q_up=wq, kv_up=wkv, out=w_out.astype(BF16))


def _encode(x, meta, layers, g_final, slopes, cq_cols):
    bsz, n_tok, _ = x.shape
    lp = n_tok + TAIL
    rows = bsz * lp
    tail = jnp.concatenate([meta.astype(x.dtype), jnp.zeros((TAIL - N_META, D_MODEL), x.dtype)], axis=0)
    h = jnp.concatenate([x, jnp.broadcast_to(tail[None], (bsz, TAIL, D_MODEL))], axis=1).reshape(rows, D_MODEL)
    cos_a, sin_a, cos_b, sin_lo, sin_hi, q_terms, k_terms = _position_tables(n_tok, bsz)
    three = lambda a: a.reshape(bsz, lp, a.shape[-1])

    for l, p in enumerate(layers):
        lambda_init = 0.8 - 0.6 * math.exp(-0.3 * l)
        u = _rmsnorm_rows(h, p["g_attn"], BF16)
        aqk = _proj_aqk(u, p["aqk"], p["g_aqk"], p["m_aqk"], cos_a, sin_a)
        vt_plain = _proj(_proj_transposed_body, u, p["plain"], [], lambda tm, tn: [],
                         A_KV_WIDTH + C_WIDTH, BF16, 512, transposed=True, name="proj_v")
        gates = _proj(_proj_gate_body, u, p["gate"], [], lambda tm, tn: [],
                      A_WIDTH + B_WIDTH + C_WIDTH, F32, 512, name="proj_gate")
        cqk = _proj(_proj_cqk_body, u, p["cqk"], list(cq_cols) + [q_terms, k_terms],
                    lambda tm, tn: [_col_vec_spec(tm, tn)] * 3 + [_row_table_spec(tm, tn)] * 2,
                    4 * C_QK_WIDTH, BF16, 512, out_tn=1024, name="proj_diff_qk")
        blow = _proj(_proj_plain_body, u, p["blow"], [], lambda tm, tn: [],
                     B_Q_RANK + B_KV_RANK, F32, 512, name="proj_mla_low")
        kpe = _proj(_proj_bpe_body, u, p["bpe"], [cos_b, sin_lo, sin_hi],
                    lambda tm, tn: [_row_table_spec(tm, tn)] * 3, LANES, BF16, LANES, name="proj_mla_pe")
        qb = _bq_up(blow, p["g_q_a"], p["q_up"], cos_b, sin_lo, sin_hi)
        kb, vt_b = _bkv_up(blow, p["g_kv_a"], p["kv_up"], kpe)

        gates3 = three(gates)
        oa = _attn_a(three(aqk), vt_plain, gates3, n_tok)
        ob = _attn_b(three(qb), three(kb), vt_b, gates3, n_tok)
        oc = _attn_c(three(cqk), vt_plain, gates3, p["lamvec"], slopes, p["g_sub"], lambda_init, n_tok)
        h = _out_proj(oa.reshape(rows, -1), ob.reshape(rows, -1), oc.reshape(rows, -1), p["out"], h)

    return _final_norm(h.reshape(bsz, lp, D_MODEL), g_final, n_tok)


def kernel(x_prompt, x_sample, meta, g_attn, w_in, g_qn, g_kn, g_q_a, w_q_b, g_kv_a, w_kv_b,
           lam_q1, lam_k1, lam_q2, lam_k2, g_sub, w_out, g_final):
    layers = []
    for l in range(DEPTH):
        p = _prep_weights(w_in[l], w_q_b[l], w_kv_b[l], w_out[l], g_qn[l], g_kn[l])
        p.update(g_attn=g_attn[l], g_q_a=g_q_a[l], g_kv_a=g_kv_a[l], g_sub=g_sub[l],
                 lamvec=jnp.stack([lam_q1[l], lam_k1[l], lam_q2[l], lam_k2[l]]).astype(F32))
        layers.append(p)
    slope_h = 2.0 ** (-(jnp.arange(C_HEADS, dtype=F32) + 1.0) * 8.0 / C_HEADS)
    slopes = jnp.broadcast_to(slope_h[:, None, None], (C_HEADS, 1, LANES))
    zeros_w, ones_w = jnp.zeros((C_QK_WIDTH,), F32), jnp.ones((C_QK_WIDTH,), F32)
    cq_cols = [jnp.concatenate([C_QK_DIM ** -0.5 * ones_w, ones_w]).reshape(1, -1),
               jnp.concatenate([jnp.repeat(slope_h, 2 * C_QK_DIM), zeros_w]).reshape(1, -1),
               jnp.concatenate([zeros_w, ones_w]).reshape(1, -1)]
    y_prompt = _encode(x_prompt, meta, layers, g_final, slopes, cq_cols)
    y_sample = _encode(x_sample, meta, layers, g_final, slopes, cq_cols)
    return (y_prompt, y_sample)
```

```python
import functools
import math

import jax
import jax.numpy as jnp
from jax import lax
from jax.experimental import pallas as pl
from jax.experimental.pallas import tpu as pltpu

D_MODEL = 4096
DEPTH = 2
N_META = 16
GRID_W = 64
HEAD_DIM = 128
ROPE_THETA = 10000.0
NORM_EPS = 1e-6
A_HEADS = 16
A_KV_HEADS = 4
A_GROUP = A_HEADS // A_KV_HEADS
A_WIDTH = A_HEADS * HEAD_DIM
A_KV_WIDTH = A_KV_HEADS * HEAD_DIM
B_HEADS = 8
B_Q_RANK = 1024
B_KV_RANK = 512
B_NOPE = 128
B_ROPE = 64
B_VDIM = 128
B_WIDTH = B_HEADS * B_VDIM
C_HEADS = 8
C_QK_DIM = 64
C_VDIM = 2 * C_QK_DIM
C_QK_WIDTH = C_HEADS * 2 * C_QK_DIM
C_WIDTH = C_HEADS * C_VDIM
IN_SPLITS = (A_WIDTH, A_KV_WIDTH, A_KV_WIDTH, A_WIDTH,
             B_Q_RANK, B_KV_RANK, B_ROPE, B_WIDTH,
             C_QK_WIDTH, C_QK_WIDTH, C_WIDTH, C_WIDTH)

LANES = 128
TAIL = LANES
KV_CHUNK = 512
Q_SUB = 256
NEG_BIG = -1e30
LOG2E = 1.4426950408889634
A_LOOKAHEAD = 2
B_LOOKAHEAD = 2
C_LOOKAHEAD = 2
ONES_ROWS = 16
BF16 = jnp.bfloat16
F32 = jnp.float32
VMEM_LIMIT = 56 * 1024 * 1024

_NT = (((1,), (1,)), ((), ()))


def _cparams(n_axes):
    return pltpu.CompilerParams(dimension_semantics=("arbitrary",) * n_axes,
                                vmem_limit_bytes=VMEM_LIMIT)


def _row_tile(rows):
    for t in (512, 640, 256, 128):
        if rows % t == 0:
            return t
    raise ValueError(f"no row tile for {rows}")


def _tall_row_tile(rows):
    for t in (1536, 1664):
        if rows % t == 0:
            return t
    return _row_tile(rows)


def _rmsnorm_body(x_ref, g_ref, o_ref):
    x = x_ref[...]
    ms = jnp.mean(x * x, axis=-1, keepdims=True)
    o_ref[...] = ((x * lax.rsqrt(ms + NORM_EPS)) * g_ref[...]).astype(o_ref.dtype)


def _rmsnorm_rows(h2d, g, out_dtype):
    rows, d = h2d.shape
    tm = 256 if rows % 256 == 0 else 128
    return pl.pallas_call(
        _rmsnorm_body,
        grid=(rows // tm,),
        in_specs=[pl.BlockSpec((tm, d), lambda i: (i, 0)),
                  pl.BlockSpec((1, d), lambda i: (0, 0))],
        out_specs=pl.BlockSpec((tm, d), lambda i: (i, 0)),
        out_shape=jax.ShapeDtypeStruct((rows, d), out_dtype),
        compiler_params=_cparams(1),
        name="rmsnorm",
    )(h2d, g.reshape(1, d))


def _final_norm(h3d, g, n_tok):
    bsz, _, d = h3d.shape
    tm = 256
    return pl.pallas_call(
        _rmsnorm_body,
        grid=(bsz, n_tok // tm),
        in_specs=[pl.BlockSpec((None, tm, d), lambda b, i: (b, i, 0)),
                  pl.BlockSpec((1, d), lambda b, i: (0, 0))],
        out_specs=pl.BlockSpec((None, tm, d), lambda b, i: (b, i, 0)),
        out_shape=jax.ShapeDtypeStruct((bsz, n_tok, d), F32),
        compiler_params=_cparams(2),
        name="final_norm",
    )(h3d, g.reshape(1, d))


def _rope_pairs_64(y, cos, sin_signed):
    return y * cos + pltpu.roll(y, 64, 1) * sin_signed


def _rope_pairs_32(x, cos, sin_lo, sin_hi):
    return x * cos + pltpu.roll(x, 96, 1) * sin_lo + pltpu.roll(x, 32, 1) * sin_hi


def _staggered_proj_body(x_ref, w_ref, *rest, n_tiles, epilogue):
    *extra, o_ref, acc_ref = rest
    j = pl.program_id(1)

    def matmul():
        acc_ref[j % 2] = jnp.dot(x_ref[...], w_ref[...], preferred_element_type=F32)

    def finish_previous():
        epilogue(acc_ref.at[(j + 1) % 2], *extra, o_ref)

    @pl.when(j == 0)
    def _():
        matmul()

    @pl.when(jnp.logical_and(j > 0, j < n_tiles))
    def _():
        finish_previous()
        matmul()

    @pl.when(j == n_tiles)
    def _():
        finish_previous()


def _staggered_proj(epilogue, u, w, col_vecs, row_tables, n_out, out_dtype, tn, out_tn, tall, name):
    rows, k = u.shape
    n = w.shape[1]
    tm = _tall_row_tile(rows) if tall else _row_tile(rows)
    n_tiles = n // tn
    prev = lambda j: jnp.maximum(j - 1, 0)
    return pl.pallas_call(
        functools.partial(_staggered_proj_body, n_tiles=n_tiles, epilogue=epilogue),
        grid=(rows // tm, n_tiles + 1),
        in_specs=([pl.BlockSpec((tm, k), lambda i, j: (i, 0)),
                   pl.BlockSpec((k, tn), lambda i, j: (0, jnp.minimum(j, n_tiles - 1)))]
                  + [pl.BlockSpec((1, tn), lambda i, j: (0, prev(j)))] * len(col_vecs)
                  + [pl.BlockSpec((tm, LANES), lambda i, j: (i, 0))] * len(row_tables)),
        out_specs=pl.BlockSpec((tm, out_tn), lambda i, j: (i, prev(j))),
        out_shape=jax.ShapeDtypeStruct((rows, n_out), out_dtype),
        scratch_shapes=[pltpu.VMEM((2, tm, tn), F32)],
        compiler_params=_cparams(2),
        name=name,
    )(u, w, *col_vecs, *row_tables)


def _direct_proj_body(x_ref, w_ref, *rest, epilogue):
    *extra, o_ref = rest
    epilogue(jnp.dot(x_ref[...], w_ref[...], preferred_element_type=F32), *extra, o_ref)


def _aqk_epilogue(acc_ref, g_ref, mult_ref, cos_ref, sin_ref, o_ref):
    cos = cos_ref[...]
    sin = sin_ref[...]
    for h in range(o_ref.shape[1] // HEAD_DIM):
        sl = slice(h * HEAD_DIM, (h + 1) * HEAD_DIM)
        a = acc_ref[:, sl]
        ms = jnp.mean(a * a, axis=-1, keepdims=True)
        y = (a * lax.rsqrt(ms + NORM_EPS)) * g_ref[:, sl]
        o_ref[:, sl] = (_rope_pairs_64(y, cos, sin) * mult_ref[:, sl]).astype(o_ref.dtype)


def _gate_epilogue(acc_ref, o_ref):
    a = acc_ref[...]
    o_ref[...] = a * (1.0 / (1.0 + jnp.exp(-a)))


def _proj_plain_body(x_ref, w_ref, o_ref):
    o_ref[...] = jnp.dot(x_ref[...], w_ref[...], preferred_element_type=F32).astype(o_ref.dtype)


def _proj_transposed_body(x_ref, w_ref, o_ref):
    o_ref[...] = jnp.dot(x_ref[...], w_ref[...], preferred_element_type=F32).T.astype(o_ref.dtype)


def _cqk_epilogue(acc_ref, mult_ref, qcoef_ref, kcoef_ref, qtab_ref, ktab_ref, o_ref):
    acc = acc_ref[...] * mult_ref[...]
    lane = lax.broadcasted_iota(jnp.int32, (acc.shape[0], LANES), 1)
    low = lane < C_QK_DIM
    qtab = qtab_ref[...]
    ktab = ktab_ref[...]
    for t in range(acc.shape[1] // LANES):
        sl = slice(t * LANES, (t + 1) * LANES)
        a = acc[:, sl]
        pos_terms = qtab * qcoef_ref[:, sl] + ktab * kcoef_ref[:, sl]
        o_ref[:, (2 * t) * LANES:(2 * t + 1) * LANES] = jnp.where(low, a, pos_terms).astype(o_ref.dtype)
        o_ref[:, (2 * t + 1) * LANES:(2 * t + 2) * LANES] = (
            jnp.where(low, pltpu.roll(a, 64, 1), pos_terms).astype(o_ref.dtype))


def _proj_bpe_body(x_ref, w_ref, cos_ref, slo_ref, shi_ref, o_ref):
    acc = jnp.dot(x_ref[...], w_ref[...], preferred_element_type=F32)
    o_ref[...] = _rope_pairs_32(acc, cos_ref[...], slo_ref[...], shi_ref[...]).astype(o_ref.dtype)


def _proj(body, u, w, extras, extra_specs, n_out, out_dtype, tn, out_tn=None, transposed=False, tall=True,
          name="proj"):
    rows, k = u.shape
    n = w.shape[1]
    tm = _tall_row_tile(rows) if tall else _row_tile(rows)
    out_tn = tn if out_tn is None else out_tn
    if transposed:
        out_specs = pl.BlockSpec((out_tn, tm), lambda i, j: (j, i))
        out_shape = jax.ShapeDtypeStruct((n_out, rows), out_dtype)
    else:
        out_specs = pl.BlockSpec((tm, out_tn), lambda i, j: (i, j))
        out_shape = jax.ShapeDtypeStruct((rows, n_out), out_dtype)
    return pl.pallas_call(
        body,
        grid=(rows // tm, n // tn),
        in_specs=[pl.BlockSpec((tm, k), lambda i, j: (i, 0)),
                  pl.BlockSpec((k, tn), lambda i, j: (0, j))] + extra_specs(tm, tn),
        out_specs=out_specs,
        out_shape=out_shape,
        compiler_params=_cparams(2),
        name=name,
    )(u, w, *extras)


def _row_table_spec(tm, tn):
    return pl.BlockSpec((tm, LANES), lambda i, j: (i, 0))


def _col_vec_spec(tm, tn):
    return pl.BlockSpec((1, tn), lambda i, j: (0, j))


def _bq_up_body(c_ref, g_ref, w_ref, cos_ref, slo_ref, shi_ref, o_ref):
    c = c_ref[...]
    ms = jnp.mean(c * c, axis=-1, keepdims=True)
    u = ((c * lax.rsqrt(ms + NORM_EPS)) * g_ref[...]).astype(BF16)
    acc = jnp.dot(u, w_ref[...], preferred_element_type=F32) * ((B_NOPE + B_ROPE) ** -0.5 * LOG2E)
    cos, slo, shi = cos_ref[...], slo_ref[...], shi_ref[...]
    for h in range(B_HEADS):
        base = h * 2 * LANES
        o_ref[:, base:base + LANES] = acc[:, base:base + LANES].astype(o_ref.dtype)
        x = acc[:, base + LANES:base + 2 * LANES]
        o_ref[:, base + LANES:base + 2 * LANES] = _rope_pairs_32(x, cos, slo, shi).astype(o_ref.dtype)


def _bkv_up_body(c_ref, g_ref, w_ref, kpe_ref, k_ref, v_ref):
    c = c_ref[...]
    ms = jnp.mean(c * c, axis=-1, keepdims=True)
    u = ((c * lax.rsqrt(ms + NORM_EPS)) * g_ref[...]).astype(BF16)
    acc = jnp.dot(u, w_ref[...], preferred_element_type=F32)
    kpe = kpe_ref[...]
    for h in range(B_HEADS):
        base = h * 2 * LANES
        k_ref[:, base:base + LANES] = acc[:, h * LANES:(h + 1) * LANES].astype(k_ref.dtype)
        k_ref[:, base + LANES:base + 2 * LANES] = kpe
    v_ref[...] = acc[:, B_HEADS * LANES:].T.astype(v_ref.dtype)


def _bq_up(cb, g, w, cos, slo, shi):
    rows = cb.shape[0]
    tm = _row_tile(rows)
    n = w.shape[1]
    tab = pl.BlockSpec((tm, LANES), lambda i: (i, 0))
    return pl.pallas_call(
        _bq_up_body,
        grid=(rows // tm,),
        in_specs=[pl.BlockSpec((tm, B_Q_RANK), lambda i: (i, 0)),
                  pl.BlockSpec((1, B_Q_RANK), lambda i: (0, 0)),
                  pl.BlockSpec((B_Q_RANK, n), lambda i: (0, 0)),
                  tab, tab, tab],
        out_specs=pl.BlockSpec((tm, n), lambda i: (i, 0)),
        out_shape=jax.ShapeDtypeStruct((rows, n), BF16),
        compiler_params=_cparams(1),
        name="mla_q_up",
    )(cb, g.reshape(1, -1), w, cos, slo, shi)


def _bkv_up(cb, g, w, kpe):
    rows = cb.shape[0]
    tm = _row_tile(rows)
    n = w.shape[1]
    return pl.pallas_call(
        _bkv_up_body,
        grid=(rows // tm,),
        in_specs=[pl.BlockSpec((tm, B_KV_RANK), lambda i: (i, B_Q_RANK // B_KV_RANK)),
                  pl.BlockSpec((1, B_KV_RANK), lambda i: (0, 0)),
                  pl.BlockSpec((B_KV_RANK, n), lambda i: (0, 0)),
                  pl.BlockSpec((tm, LANES), lambda i: (i, 0))],
        out_specs=[pl.BlockSpec((tm, B_HEADS * 2 * LANES), lambda i: (i, 0)),
                   pl.BlockSpec((B_HEADS * B_VDIM, tm), lambda i: (0, i))],
        out_shape=[jax.ShapeDtypeStruct((rows, B_HEADS * 2 * LANES), BF16),
                   jax.ShapeDtypeStruct((B_HEADS * B_VDIM, rows), BF16)],
        compiler_params=_cparams(1),
        name="mla_kv_up",
    )(cb, g.reshape(1, -1), w, kpe)


def _online_softmax_sweep(k_ref, vt_ref, m_ref, acc_ref, s_ref, p_ref, a_ref, n_units, n_tok, scores, make_adjust,
                          lookahead, phases=None, first_variant=0, tail_variant=0, tail_fix=False,
                          value_head=lambda idx: 0):
    look = min(lookahead, n_units)
    n_chunks = n_tok // KV_CHUNK
    if phases is None:
        phases = [(0, n_chunks, 0, 0, False)]

    def probabilities(s, idx):
        m_prev = m_ref[idx]
        m_new = jnp.maximum(m_prev, jnp.max(s, axis=0, keepdims=True))
        m_ref[idx] = m_new
        return jnp.exp2(s - m_new).astype(BF16), jnp.exp2(m_prev - m_new)

    def accumulate(idx, p, alpha, vt_ones):
        acc_ref[idx] = alpha * acc_ref[idx] + jnp.dot(vt_ones, p, preferred_element_type=F32)

    def values_of(head, cols):
        return _with_ones_rows(vt_ref[head * HEAD_DIM:(head + 1) * HEAD_DIM, cols])

    def chunk(s_first, kc, cols, k_off, n_valid, kc_after, behind, variant, variant_after, fix):
        pending = list(s_first)
        vt_ones = {}
        adjust = make_adjust(k_off, kc.shape[0], n_valid, fix)
        for idx in range(n_units):
            s = pending.pop(0)
            ahead = idx + look
            if ahead < n_units:
                pending.append(scores(kc, ahead, _per_unit(variant, ahead)))
            elif kc_after is not None:
                s_ref[ahead - n_units] = scores(kc_after, ahead - n_units, _per_unit(variant_after, ahead - n_units))
            p, alpha = probabilities(adjust(s, idx), idx)
            if behind is not None:
                accumulate(*behind)
            head = value_head(idx)
            if head not in vt_ones:
                vt_ones[head] = values_of(head, cols)
            behind = (idx, p, alpha, vt_ones[head])
        return behind

    def chunk_step(c, carry, variant, variant_after, fix):
        off = pl.multiple_of(c * KV_CHUNK, KV_CHUNK)
        off_before = pl.multiple_of(jnp.maximum(c - 1, 0) * KV_CHUNK, KV_CHUNK)
        off_after = pl.multiple_of(jnp.minimum(c + 1, n_chunks - 1) * KV_CHUNK, KV_CHUNK)
        behind = (n_units - 1, p_ref[...], a_ref[...],
                  values_of(value_head(n_units - 1), pl.ds(off_before, KV_CHUNK)))
        _, p, alpha, _ = chunk([s_ref[j] for j in range(look)], k_ref[pl.ds(off, KV_CHUNK), :],
                               pl.ds(off, KV_CHUNK), N_META + off, None,
                               k_ref[pl.ds(off_after, KV_CHUNK), :], behind, variant, variant_after, fix)
        p_ref[...] = p
        a_ref[...] = alpha
        return carry

    m_ref[...] = jnp.full(m_ref.shape, NEG_BIG, F32)
    acc_ref[...] = jnp.zeros(acc_ref.shape, F32)
    p_ref[...] = jnp.zeros(p_ref.shape, BF16)
    a_ref[...] = jnp.ones(a_ref.shape, F32)
    for j in range(look):
        s_ref[j] = scores(k_ref[0:KV_CHUNK, :], j, _per_unit(first_variant, j))
    for lo, hi, variant, variant_after, fix in phases:
        lax.fori_loop(lo, hi, functools.partial(chunk_step, variant=variant, variant_after=variant_after, fix=fix), 0)
    accumulate(n_units - 1, p_ref[...], a_ref[...],
               values_of(value_head(n_units - 1), slice(n_tok - KV_CHUNK, n_tok)))
    k_tail = k_ref[n_tok:n_tok + TAIL, :]
    accumulate(*chunk([scores(k_tail, j, _per_unit(tail_variant, j)) for j in range(look)], k_tail,
                      slice(n_tok, n_tok + TAIL),
                      0, N_META, None, None, tail_variant, tail_variant, tail_fix))


def _per_unit(choice, idx):
    return choice(idx) if callable(choice) else choice


def _with_ones_rows(vt):
    return jnp.concatenate([vt, jnp.ones((ONES_ROWS, vt.shape[1]), vt.dtype)], axis=0)


def _normalized(acc):
    return acc[:HEAD_DIM] * (1.0 / acc[HEAD_DIM:HEAD_DIM + 1])


def _attn_ab_body(q_ref, k_ref, vt_ref, gate_ref, *rest, group, qsub, tqs, dk, n_tok, pre_transpose, lookahead):
    units = [(g, u) for g in range(group) for u in range(qsub)]
    if pre_transpose:
        o_ref, m_ref, acc_ref, s_ref, p_ref, a_ref, qt_ref = rest[-7:]
        for idx, (g, u) in enumerate(units):
            qt_ref[idx] = q_ref[u * tqs:(u + 1) * tqs, g * dk:(g + 1) * dk].astype(F32).T.astype(BF16)
    else:
        o_ref, m_ref, acc_ref, s_ref, p_ref, a_ref = rest[-6:]

    def scores(kc, idx, variant):
        if pre_transpose:
            return jnp.dot(kc, qt_ref[idx], preferred_element_type=F32)
        g, u = units[idx]
        return lax.dot_general(kc, q_ref[u * tqs:(u + 1) * tqs, g * dk:(g + 1) * dk], _NT,
                               preferred_element_type=F32)

    def make_adjust(k_off, n_keys, n_valid, fix):
        def adjust(s, idx):
            if n_valid is None:
                return s
            row = lax.broadcasted_iota(jnp.int32, s.shape, 0)
            return jnp.where(row < n_valid, s, NEG_BIG)
        return adjust

    _online_softmax_sweep(k_ref, vt_ref, m_ref, acc_ref, s_ref, p_ref, a_ref, len(units), n_tok, scores, make_adjust,
                          lookahead)

    for idx, (g, u) in enumerate(units):
        rows = slice(u * tqs, (u + 1) * tqs)
        cols = slice(g * HEAD_DIM, (g + 1) * HEAD_DIM)
        o = _normalized(acc_ref[idx]).T
        o_ref[rows, cols] = (o * gate_ref[rows, cols]).astype(o_ref.dtype)


def _attn_c_body(q_ref, k_ref, vt_ref, gate_ref, lam_ref, slope_ref, gsub_ref, *rest,
                 heads, qsub, tqs, n_tok, q_is_meta, lambda_init, lookahead):
    o_ref, m_ref, acc_ref, s_ref, p_ref, a_ref, qt_ref = rest[-7:]
    slot_w = 2 * LANES
    units = [(hd, u, c) for hd in range(heads) for u in range(qsub) for c in range(2)]
    lane = lax.broadcasted_iota(jnp.int32, (1, LANES), 1)
    flip = jnp.where(lane < C_QK_DIM, 1.0, -1.0)
    for idx, (hd, u, c) in enumerate(units):
        col = hd * slot_w + c * LANES
        q = q_ref[u * tqs:(u + 1) * tqs, col:col + LANES].astype(F32)
        qt_ref[0, idx] = q.T.astype(BF16)
        qt_ref[1, idx] = (q * flip).T.astype(BF16)
    slope2 = [(2.0 * LOG2E) * slope_ref[hd, :, :1] for hd in range(heads)]
    tq = qsub * tqs
    n_chunks = n_tok // KV_CHUNK
    if q_is_meta:
        q_off = 0
        phases = [(0, n_chunks, 1, 1, False)]
        first_variant, tail_fix = 1, True
    else:
        assert tq % KV_CHUNK == 0 and KV_CHUNK % tqs == 0
        span = tq // KV_CHUNK
        per_chunk = KV_CHUNK // tqs
        i = pl.program_id(2)
        q_off = N_META + i * tq
        d = i * span

        def variant_in(j):
            return lambda idx: 1 if units[idx][1] < j * per_chunk else 0

        def fix_in(j):
            return lambda idx: j * per_chunk <= units[idx][1] < (j + 1) * per_chunk

        phases = ([(0, d, 0, 0, False)]
                  + [(d + j, d + j + 1, variant_in(j), variant_in(j + 1) if j + 1 < span else 1, fix_in(j))
                     for j in range(span)]
                  + [(d + span, n_chunks, 1, 1, False)])
        first_variant, tail_fix = 0, False

    def scores(kc, idx, variant):
        hd, _, c = units[idx]
        col = hd * slot_w + c * LANES
        return jnp.dot(kc[:, col:col + LANES], qt_ref[variant, idx], preferred_element_type=F32)

    def make_adjust(k_off, n_keys, n_valid, fix):
        key = lax.broadcasted_iota(jnp.int32, (n_keys, tqs), 0)
        qry = lax.broadcasted_iota(jnp.int32, (n_keys, tqs), 1)
        late = {}

        def adjust(s, idx):
            hd, u, _ = units[idx]
            if _per_unit(fix, idx):
                if (hd, u) not in late:
                    ahead = ((key - qry) + (k_off - q_off - u * tqs)).astype(F32)
                    late[hd, u] = slope2[hd] * jnp.maximum(ahead, 0.0)
                s = s - late[hd, u]
            if n_valid is not None:
                s = jnp.where(key < n_valid, s, NEG_BIG)
            return s
        return adjust

    _online_softmax_sweep(k_ref, vt_ref, m_ref, acc_ref, s_ref, p_ref, a_ref, len(units), n_tok, scores, make_adjust,
                          lookahead, phases=phases, first_variant=first_variant, tail_variant=0, tail_fix=tail_fix,
                          value_head=lambda idx: units[idx][0])

    lv = lam_ref[...]
    lam = (jnp.exp(jnp.sum(lv[0:1] * lv[1:2], axis=1, keepdims=True))
           - jnp.exp(jnp.sum(lv[2:3] * lv[3:4], axis=1, keepdims=True)) + lambda_init)
    for hd in range(heads):
        cols = slice(hd * C_VDIM, (hd + 1) * C_VDIM)
        for u in range(qsub):
            rows = slice(u * tqs, (u + 1) * tqs)
            first = 2 * (hd * qsub + u)
            o = (_normalized(acc_ref[first]) - lam * _normalized(acc_ref[first + 1])).T
            ms = jnp.mean(o * o, axis=-1, keepdims=True)
            o = ((o * lax.rsqrt(ms + NORM_EPS)) * gsub_ref[...]) * (1.0 - lambda_init)
            o_ref[rows, cols] = (o * gate_ref[rows, cols]).astype(o_ref.dtype)


def _attention(body, q, q_col0, q_w, k, k_col0, k_w, vt, v_row0, gate, gate_col0, extras, extra_specs,
               out_w_total, heads, group, qsub, n_units, n_tok, name, lookahead, qt_shape, v_heads=1):
    bsz, lp, _ = q.shape
    out_w = group * HEAD_DIM

    def call(qsub_, tqs, n_qblk, qblk0, prev):
        tq = qsub_ * tqs
        in_specs = [
            pl.BlockSpec((None, tq, q_w), lambda b, h, i: (b, qblk0 + i, q_col0 + h)),
            pl.BlockSpec((None, lp, k_w), lambda b, h, i: (b, 0, k_col0 + h)),
            pl.BlockSpec((v_heads * HEAD_DIM, lp), lambda b, h, i: (v_row0 + h, b)),
            pl.BlockSpec((None, tq, out_w), lambda b, h, i: (b, qblk0 + i, gate_col0 + h)),
        ] + extra_specs
        args = [q, k, vt, gate] + extras
        aliases = {}
        if prev is not None:
            in_specs.append(pl.BlockSpec(memory_space=pl.ANY))
            args.append(prev)
            aliases = {len(args) - 1: 0}
        units = n_units * qsub_
        qt = qt_shape(units, tqs)
        return pl.pallas_call(
            body(qsub_, tqs, prev is not None),
            grid=(bsz, heads, n_qblk),
            in_specs=in_specs,
            out_specs=pl.BlockSpec((None, tq, out_w), lambda b, h, i: (b, qblk0 + i, h)),
            out_shape=jax.ShapeDtypeStruct((bsz, lp, out_w_total), BF16),
            scratch_shapes=[pltpu.VMEM((units, 1, tqs), F32),
                            pltpu.VMEM((units, HEAD_DIM + ONES_ROWS, tqs), F32),
                            pltpu.VMEM((lookahead, KV_CHUNK, tqs), F32),
                            pltpu.VMEM((KV_CHUNK, tqs), BF16), pltpu.VMEM((1, tqs), F32)]
                           + ([] if qt is None else [pltpu.VMEM(qt, BF16)]),
            input_output_aliases=aliases,
            compiler_params=_cparams(3),
            name=name + ("_meta" if prev is not None else ""),
        )(*args)

    main = call(qsub, Q_SUB, n_tok // (qsub * Q_SUB), 0, None)
    return call(1, TAIL, 1, n_tok // TAIL, main)


def _attn_a(qk, vt_plain, gates, n_tok):
    def body(qsub_, tqs, is_meta):
        return functools.partial(_attn_ab_body, group=A_GROUP, qsub=qsub_, tqs=tqs, dk=HEAD_DIM, n_tok=n_tok,
                                 pre_transpose=False, lookahead=A_LOOKAHEAD)

    return _attention(body, qk, 0, A_GROUP * HEAD_DIM, qk, A_WIDTH // HEAD_DIM, HEAD_DIM, vt_plain, 0,
                      gates, 0, [], [], A_WIDTH, A_KV_HEADS, A_GROUP, 8, A_GROUP, n_tok, "attn_gqa",
                      A_LOOKAHEAD, lambda units, tqs: None)


def _attn_b(qb, kb, vt_b, gates, n_tok):
    dk = 2 * LANES

    def body(qsub_, tqs, is_meta):
        return functools.partial(_attn_ab_body, group=1, qsub=qsub_, tqs=tqs, dk=dk, n_tok=n_tok,
                                 pre_transpose=True, lookahead=B_LOOKAHEAD)

    return _attention(body, qb, 0, dk, kb, 0, dk, vt_b, 0, gates, A_WIDTH // HEAD_DIM, [], [],
                      B_WIDTH, B_HEADS, 1, 16, 1, n_tok, "attn_mla", B_LOOKAHEAD,
                      lambda units, tqs: (units, dk, tqs))


def _attn_c(cqk, vt_plain, gates, lamvec, slopes, g_sub, lambda_init, n_tok):
    pair = 2
    dk = pair * 2 * LANES

    def body(qsub_, tqs, is_meta):
        return functools.partial(_attn_c_body, heads=pair, qsub=qsub_, tqs=tqs, n_tok=n_tok, q_is_meta=is_meta,
                                 lambda_init=lambda_init, lookahead=C_LOOKAHEAD)

    extras = [lamvec, slopes, g_sub.reshape(1, C_VDIM)]
    extra_specs = [pl.BlockSpec((4, C_QK_DIM), lambda b, h, i: (0, 0)),
                   pl.BlockSpec((pair, 1, LANES), lambda b, h, i: (h, 0, 0)),
                   pl.BlockSpec((1, C_VDIM), lambda b, h, i: (0, 0))]
    return _attention(body, cqk, 0, dk, cqk, C_HEADS // pair, dk, vt_plain, A_KV_HEADS // pair, gates,
                      (A_WIDTH + B_WIDTH) // (pair * HEAD_DIM), extras, extra_specs,
                      C_WIDTH, C_HEADS // pair, pair, 8, 2 * pair, n_tok, "attn_diff", C_LOOKAHEAD,
                      lambda units, tqs: (2, units, LANES, tqs), v_heads=pair)


def _out_proj_body(a_ref, b_ref, c_ref, wa_ref, wb_ref, wc_ref, h_ref, o_ref):
    acc = jnp.dot(a_ref[...], wa_ref[...], preferred_element_type=F32)
    acc += jnp.dot(b_ref[...], wb_ref[...], preferred_element_type=F32)
    acc += jnp.dot(c_ref[...], wc_ref[...], preferred_element_type=F32)
    o_ref[...] = h_ref[...] + acc


def _out_proj(oa, ob, oc, w, h2d):
    rows = h2d.shape[0]
    tm = _tall_row_tile(rows)
    tn = 512
    return pl.pallas_call(
        _out_proj_body,
        grid=(rows // tm, D_MODEL // tn),
        in_specs=[pl.BlockSpec((tm, A_WIDTH), lambda i, j: (i, 0)),
                  pl.BlockSpec((tm, B_WIDTH), lambda i, j: (i, 0)),
                  pl.BlockSpec((tm, C_WIDTH), lambda i, j: (i, 0)),
                  pl.BlockSpec((A_WIDTH, tn), lambda i, j: (0, j)),
                  pl.BlockSpec((B_WIDTH, tn), lambda i, j: (A_WIDTH // B_WIDTH, j)),
                  pl.BlockSpec((C_WIDTH, tn), lambda i, j: ((A_WIDTH + B_WIDTH) // C_WIDTH, j)),
                  pl.BlockSpec((tm, tn), lambda i, j: (i, j))],
        out_specs=pl.BlockSpec((tm, tn), lambda i, j: (i, j)),
        out_shape=jax.ShapeDtypeStruct((rows, D_MODEL), F32),
        compiler_params=_cparams(2),
        name="out_proj",
    )(oa, ob, oc, w, w, w, h2d)


def _rope_angles(pos_f, n_freq):
    inv = ROPE_THETA ** (-jnp.arange(n_freq, dtype=F32) / n_freq)
    return pos_f[:, None] * inv[None, :]


def _position_tables(n_tok, bsz):
    rows = n_tok // GRID_W
    z = jnp.zeros((TAIL,), F32)
    row_f = jnp.concatenate([jnp.repeat(jnp.arange(rows, dtype=F32), GRID_W), z])
    col_f = jnp.concatenate([jnp.tile(jnp.arange(GRID_W, dtype=F32), rows), z])
    pos_f = jnp.concatenate([jnp.arange(n_tok, dtype=F32) + N_META,
                             jnp.arange(N_META, dtype=F32), jnp.zeros((TAIL - N_META,), F32)])
    ang_a = jnp.concatenate([_rope_angles(row_f, HEAD_DIM // 4), _rope_angles(col_f, HEAD_DIM // 4)], axis=-1)
    cos_a, sin_a = jnp.cos(ang_a), jnp.sin(ang_a)
    cos_a = jnp.concatenate([cos_a, cos_a], axis=-1)
    sin_a = jnp.concatenate([-sin_a, sin_a], axis=-1)
    ang_b = _rope_angles(pos_f, B_ROPE // 2)
    cos_b, sin_b = jnp.cos(ang_b), jnp.sin(ang_b)
    zb = jnp.zeros_like(cos_b)
    cos_b128 = jnp.concatenate([cos_b, cos_b, zb, zb], axis=-1)
    sin_lo = jnp.concatenate([-sin_b, zb, zb, zb], axis=-1)
    sin_hi = jnp.concatenate([zb, sin_b, zb, zb], axis=-1)
    hi = jnp.floor(pos_f / LANES)
    lo = pos_f - hi * LANES
    one = jnp.ones_like(pos_f)

    def slot_lanes(cols):
        body = jnp.stack(cols, axis=-1)
        return jnp.concatenate([jnp.zeros((lp, C_QK_DIM), F32), body,
                                jnp.zeros((lp, LANES - C_QK_DIM - len(cols)), F32)], axis=-1)

    lp = n_tok + TAIL
    pieces, rest = [], jnp.float32(LOG2E)
    for _ in range(3):
        piece = rest.astype(BF16).astype(F32)
        pieces.append(piece)
        rest = rest - piece
    q_terms = slot_lanes([col for c in pieces for col in (LANES * c * one, c * one, -LANES * hi, -lo)])
    k_terms = slot_lanes([col for c in pieces for col in (hi, lo, c * one, c * one)])
    return tuple(jnp.tile(t, (bsz, 1)) for t in (cos_a, sin_a, cos_b128, sin_lo, sin_hi, q_terms, k_terms))


def _prep_weights(w_in, w_q_b, w_kv_b, w_out, g_qn, g_kn):
    bounds = [0]
    for s in IN_SPLITS:
        bounds.append(bounds[-1] + s)
    col = lambda i: slice(bounds[i], bounds[i + 1])
    w = w_in.astype(BF16)
    aq, ak, av, ag, bq, bkv, bpe, bg, cq, ck, cv, cg = (w[:, col(i)] for i in range(12))
    w_aqk = jnp.concatenate([aq, ak], axis=1)
    w_plain = jnp.concatenate([av, cv], axis=1)
    w_gate = jnp.concatenate([ag, bg, cg], axis=1)
    w_cqk = jnp.concatenate([cq, ck], axis=1)
    w_blow = jnp.concatenate([bq, bkv], axis=1)
    w_bpe = jnp.concatenate([bpe, jnp.zeros((D_MODEL, LANES - B_ROPE), BF16)], axis=1)
    g_aqk = jnp.concatenate([jnp.tile(g_qn, A_HEADS), jnp.tile(g_kn, A_KV_HEADS)]).reshape(1, -1)
    m_aqk = jnp.concatenate([jnp.full((A_WIDTH,), HEAD_DIM ** -0.5 * LOG2E, F32),
                             jnp.ones((A_KV_WIDTH,), F32)]).reshape(1, -1)
    wq = w_q_b.astype(BF16).reshape(B_Q_RANK, B_HEADS, B_NOPE + B_ROPE)
    wq = jnp.concatenate([wq, jnp.zeros((B_Q_RANK, B_HEADS, 2 * LANES - B_NOPE - B_ROPE), BF16)], axis=-1)
    wq = wq.reshape(B_Q_RANK, B_HEADS * 2 * LANES)
    wkv = w_kv_b.astype(BF16).reshape(B_KV_RANK, B_HEADS, B_NOPE + B_VDIM)
    wkv = jnp.concatenate([wkv[:, :, :B_NOPE].reshape(B_KV_RANK, -1),
                           wkv[:, :, B_NOPE:].reshape(B_KV_RANK, -1)], axis=1)
    return dict(aqk=w_aqk, plain=w_plain, gate=w_gate, cqk=w_cqk, blow=w_blow, bpe=w_bpe,
                g_aqk=g_aqk, m_aqk=m_aqk, q_up=wq, kv_up=wkv, out=w_out.astype(BF16))


def _encode(x, meta, layers, g_final, slopes, cq_cols):
    bsz, n_tok, _ = x.shape
    lp = n_tok + TAIL
    rows = bsz * lp
    tail = jnp.concatenate([meta.astype(x.dtype), jnp.zeros((TAIL - N_META, D_MODEL), x.dtype)], axis=0)
    h = jnp.concatenate([x, jnp.broadcast_to(tail[None], (bsz, TAIL, D_MODEL))], axis=1).reshape(rows, D_MODEL)
    cos_a, sin_a, cos_b, sin_lo, sin_hi, q_terms, k_terms = _position_tables(n_tok, bsz)
    three = lambda a: a.reshape(bsz, lp, a.shape[-1])

    for l, p in enumerate(layers):
        lambda_init = 0.8 - 0.6 * math.exp(-0.3 * l)
        u = _rmsnorm_rows(h, p["g_attn"], BF16)
        aqk = _staggered_proj(_aqk_epilogue, u, p["aqk"], [p["g_aqk"], p["m_aqk"]], [cos_a, sin_a],
                              A_WIDTH + A_KV_WIDTH, BF16, 512, 512, False, "proj_gqa_qk")
        vt_plain = _proj(_proj_transposed_body, u, p["plain"], [], lambda tm, tn: [],
                         A_KV_WIDTH + C_WIDTH, BF16, 512, transposed=True, name="proj_v")
        gates = _proj(functools.partial(_direct_proj_body, epilogue=_gate_epilogue), u, p["gate"], [],
                      lambda tm, tn: [], A_WIDTH + B_WIDTH + C_WIDTH, F32, 512, name="proj_gate")
        cqk = _proj(functools.partial(_direct_proj_body, epilogue=_cqk_epilogue), u, p["cqk"],
                    list(cq_cols) + [q_terms, k_terms],
                    lambda tm, tn: [_col_vec_spec(tm, tn)] * 3 + [_row_table_spec(tm, tn)] * 2,
                    4 * C_QK_WIDTH, BF16, 512, out_tn=1024, name="proj_diff_qk")
        blow = _proj(_proj_plain_body, u, p["blow"], [], lambda tm, tn: [],
                     B_Q_RANK + B_KV_RANK, F32, 512, name="proj_mla_low")
        kpe = _proj(_proj_bpe_body, u, p["bpe"], [cos_b, sin_lo, sin_hi],
                    lambda tm, tn: [_row_table_spec(tm, tn)] * 3, LANES, BF16, LANES, name="proj_mla_pe")
        qb = _bq_up(blow, p["g_q_a"], p["q_up"], cos_b, sin_lo, sin_hi)
        kb, vt_b = _bkv_up(blow, p["g_kv_a"], p["kv_up"], kpe)

        gates3 = three(gates)
        oa = _attn_a(three(aqk), vt_plain, gates3, n_tok)
        ob = _attn_b(three(qb), three(kb), vt_b, gates3, n_tok)
        oc = _attn_c(three(cqk), vt_plain, gates3, p["lamvec"], slopes, p["g_sub"], lambda_init, n_tok)
        h = _out_proj(oa.reshape(rows, -1), ob.reshape(rows, -1), oc.reshape(rows, -1), p["out"], h)

    return _final_norm(h.reshape(bsz, lp, D_MODEL), g_final, n_tok)


def kernel(x_prompt, x_sample, meta, g_attn, w_in, g_qn, g_kn, g_q_a, w_q_b, g_kv_a, w_kv_b,
           lam_q1, lam_k1, lam_q2, lam_k2, g_sub, w_out, g_final):
    layers = []
    for l in range(DEPTH):
        p = _prep_weights(w_in[l], w_q_b[l], w_kv_b[l], w_out[l], g_qn[l], g_kn[l])
        p.update(g_attn=g_attn[l], g_q_a=g_q_a[l], g_kv_a=g_kv_a[l], g_sub=g_sub[l],
                 lamvec=jnp.stack([lam_q1[l], lam_k1[l], lam_q2[l], lam_k2[l]]).astype(F32))
        layers.append(p)
    slope_h = 2.0 ** (-(jnp.arange(C_HEADS, dtype=F32) + 1.0) * 8.0 / C_HEADS)
    slopes = jnp.broadcast_to(slope_h[:, None, None], (C_HEADS, 1, LANES))
    zeros_w, ones_w = jnp.zeros((C_QK_WIDTH,), F32), jnp.ones((C_QK_WIDTH,), F32)
    cq_cols = [jnp.concatenate([C_QK_DIM ** -0.5 * LOG2E * ones_w, ones_w]).reshape(1, -1),
               jnp.concatenate([jnp.repeat(slope_h, 2 * C_QK_DIM), zeros_w]).reshape(1, -1),
               jnp.concatenate([zeros_w, ones_w]).reshape(1, -1)]
    y_prompt = _encode(x_prompt, meta, layers, g_final, slopes, cq_cols)
    y_sample = _encode(x_sample, meta, layers, g_final, slopes, cq_cols)
    return (y_prompt, y_sample)
```

```python
import functools
import math

import jax
import jax.numpy as jnp
from jax import lax
from jax.experimental import pallas as pl
from jax.experimental.pallas import tpu as pltpu

D_MODEL = 4096
DEPTH = 2
N_META = 16
GRID_W = 64
HEAD_DIM = 128
ROPE_THETA = 10000.0
NORM_EPS = 1e-6
A_HEADS = 16
A_KV_HEADS = 4
A_GROUP = A_HEADS // A_KV_HEADS
A_WIDTH = A_HEADS * HEAD_DIM
A_KV_WIDTH = A_KV_HEADS * HEAD_DIM
B_HEADS = 8
B_Q_RANK = 1024
B_KV_RANK = 512
B_NOPE = 128
B_ROPE = 64
B_VDIM = 128
B_WIDTH = B_HEADS * B_VDIM
C_HEADS = 8
C_QK_DIM = 64
C_VDIM = 2 * C_QK_DIM
C_QK_WIDTH = C_HEADS * 2 * C_QK_DIM
C_WIDTH = C_HEADS * C_VDIM
IN_SPLITS = (A_WIDTH, A_KV_WIDTH, A_KV_WIDTH, A_WIDTH,
             B_Q_RANK, B_KV_RANK, B_ROPE, B_WIDTH,
             C_QK_WIDTH, C_QK_WIDTH, C_WIDTH, C_WIDTH)

LANES = 128
TAIL = LANES
KV_CHUNK = 512
Q_SUB = 256
NEG_BIG = -1e30
LOG2E = 1.4426950408889634
A_LOOKAHEAD = 2
B_LOOKAHEAD = 2
C_LOOKAHEAD = 2
ONES_ROWS = 16
BF16 = jnp.bfloat16
F32 = jnp.float32
VMEM_LIMIT = 56 * 1024 * 1024

_NT = (((1,), (1,)), ((), ()))


def _cparams(n_axes):
    return pltpu.CompilerParams(dimension_semantics=("arbitrary",) * n_axes,
                                vmem_limit_bytes=VMEM_LIMIT)


def _row_tile(rows):
    for t in (512, 640, 256, 128):
        if rows % t == 0:
            return t
    raise ValueError(f"no row tile for {rows}")


def _tall_row_tile(rows):
    for t in (1536, 1664):
        if rows % t == 0:
            return t
    return _row_tile(rows)


def _rmsnorm_body(x_ref, g_ref, o_ref):
    x = x_ref[...]
    ms = jnp.mean(x * x, axis=-1, keepdims=True)
    o_ref[...] = ((x * lax.rsqrt(ms + NORM_EPS)) * g_ref[...]).astype(o_ref.dtype)


def _rmsnorm_rows(h2d, g, out_dtype):
    rows, d = h2d.shape
    tm = 256 if rows % 256 == 0 else 128
    return pl.pallas_call(
        _rmsnorm_body,
        grid=(rows // tm,),
        in_specs=[pl.BlockSpec((tm, d), lambda i: (i, 0)),
                  pl.BlockSpec((1, d), lambda i: (0, 0))],
        out_specs=pl.BlockSpec((tm, d), lambda i: (i, 0)),
        out_shape=jax.ShapeDtypeStruct((rows, d), out_dtype),
        compiler_params=_cparams(1),
        name="rmsnorm",
    )(h2d, g.reshape(1, d))


def _final_norm(h3d, g, n_tok):
    bsz, _, d = h3d.shape
    tm = 256
    return pl.pallas_call(
        _rmsnorm_body,
        grid=(bsz, n_tok // tm),
        in_specs=[pl.BlockSpec((None, tm, d), lambda b, i: (b, i, 0)),
                  pl.BlockSpec((1, d), lambda b, i: (0, 0))],
        out_specs=pl.BlockSpec((None, tm, d), lambda b, i: (b, i, 0)),
        out_shape=jax.ShapeDtypeStruct((bsz, n_tok, d), F32),
        compiler_params=_cparams(2),
        name="final_norm",
    )(h3d, g.reshape(1, d))


def _rope_pairs_64(y, cos, sin_signed):
    return y * cos + pltpu.roll(y, 64, 1) * sin_signed


def _rope_pairs_32(x, cos, sin_lo, sin_hi):
    return x * cos + pltpu.roll(x, 96, 1) * sin_lo + pltpu.roll(x, 32, 1) * sin_hi


def _staggered_proj_body(x_ref, w_ref, *rest, n_tiles, epilogue):
    *extra, o_ref, acc_ref = rest
    j = pl.program_id(1)

    def matmul():
        acc_ref[j % 2] = jnp.dot(x_ref[...], w_ref[...], preferred_element_type=F32)

    def finish_previous():
        epilogue(acc_ref.at[(j + 1) % 2], *extra, o_ref)

    @pl.when(j == 0)
    def _():
        matmul()

    @pl.when(jnp.logical_and(j > 0, j < n_tiles))
    def _():
        finish_previous()
        matmul()

    @pl.when(j == n_tiles)
    def _():
        finish_previous()


def _staggered_proj(epilogue, u, w, col_vecs, row_tables, n_out, out_dtype, tn, out_tn, tall, name):
    rows, k = u.shape
    n = w.shape[1]
    tm = _tall_row_tile(rows) if tall else _row_tile(rows)
    n_tiles = n // tn
    prev = lambda j: jnp.maximum(j - 1, 0)
    return pl.pallas_call(
        functools.partial(_staggered_proj_body, n_tiles=n_tiles, epilogue=epilogue),
        grid=(rows // tm, n_tiles + 1),
        in_specs=([pl.BlockSpec((tm, k), lambda i, j: (i, 0)),
                   pl.BlockSpec((k, tn), lambda i, j: (0, jnp.minimum(j, n_tiles - 1)))]
                  + [pl.BlockSpec((1, tn), lambda i, j: (0, prev(j)))] * len(col_vecs)
                  + [pl.BlockSpec((tm, LANES), lambda i, j: (i, 0))] * len(row_tables)),
        out_specs=pl.BlockSpec((tm, out_tn), lambda i, j: (i, prev(j))),
        out_shape=jax.ShapeDtypeStruct((rows, n_out), out_dtype),
        scratch_shapes=[pltpu.VMEM((2, tm, tn), F32)],
        compiler_params=_cparams(2),
        name=name,
    )(u, w, *col_vecs, *row_tables)


def _direct_proj_body(x_ref, w_ref, *rest, epilogue):
    *extra, o_ref = rest
    epilogue(jnp.dot(x_ref[...], w_ref[...], preferred_element_type=F32), *extra, o_ref)


def _aqk_epilogue(acc_ref, g_ref, mult_ref, cos_ref, sin_ref, o_ref):
    cos = cos_ref[...]
    sin = sin_ref[...]
    for h in range(o_ref.shape[1] // HEAD_DIM):
        sl = slice(h * HEAD_DIM, (h + 1) * HEAD_DIM)
        a = acc_ref[:, sl]
        ms = jnp.mean(a * a, axis=-1, keepdims=True)
        y = (a * lax.rsqrt(ms + NORM_EPS)) * g_ref[:, sl]
        o_ref[:, sl] = (_rope_pairs_64(y, cos, sin) * mult_ref[:, sl]).astype(o_ref.dtype)


def _gate_epilogue(acc_ref, o_ref):
    a = acc_ref[...]
    o_ref[...] = a * (1.0 / (1.0 + jnp.exp(-a)))


def _proj_plain_body(x_ref, w_ref, o_ref):
    o_ref[...] = jnp.dot(x_ref[...], w_ref[...], preferred_element_type=F32).astype(o_ref.dtype)


def _proj_transposed_body(x_ref, w_ref, o_ref):
    o_ref[...] = jnp.dot(x_ref[...], w_ref[...], preferred_element_type=F32).T.astype(o_ref.dtype)


def _cqk_epilogue(acc_ref, mult_ref, qcoef_ref, kcoef_ref, qtab_ref, ktab_ref, o_ref):
    acc = acc_ref[...] * mult_ref[...]
    lane = lax.broadcasted_iota(jnp.int32, (acc.shape[0], LANES), 1)
    low = lane < C_QK_DIM
    qtab = qtab_ref[...]
    ktab = ktab_ref[...]
    for t in range(acc.shape[1] // LANES):
        sl = slice(t * LANES, (t + 1) * LANES)
        a = acc[:, sl]
        pos_terms = qtab * qcoef_ref[:, sl] + ktab * kcoef_ref[:, sl]
        o_ref[:, (2 * t) * LANES:(2 * t + 1) * LANES] = jnp.where(low, a, pos_terms).astype(o_ref.dtype)
        o_ref[:, (2 * t + 1) * LANES:(2 * t + 2) * LANES] = (
            jnp.where(low, pltpu.roll(a, 64, 1), pos_terms).astype(o_ref.dtype))


def _proj_bpe_body(x_ref, w_ref, cos_ref, slo_ref, shi_ref, o_ref):
    acc = jnp.dot(x_ref[...], w_ref[...], preferred_element_type=F32)
    o_ref[...] = _rope_pairs_32(acc, cos_ref[...], slo_ref[...], shi_ref[...]).astype(o_ref.dtype)


def _proj(body, u, w, extras, extra_specs, n_out, out_dtype, tn, out_tn=None, transposed=False, tall=True,
          name="proj"):
    rows, k = u.shape
    n = w.shape[1]
    tm = _tall_row_tile(rows) if tall else _row_tile(rows)
    out_tn = tn if out_tn is None else out_tn
    if transposed:
        out_specs = pl.BlockSpec((out_tn, tm), lambda i, j: (j, i))
        out_shape = jax.ShapeDtypeStruct((n_out, rows), out_dtype)
    else:
        out_specs = pl.BlockSpec((tm, out_tn), lambda i, j: (i, j))
        out_shape = jax.ShapeDtypeStruct((rows, n_out), out_dtype)
    return pl.pallas_call(
        body,
        grid=(rows // tm, n // tn),
        in_specs=[pl.BlockSpec((tm, k), lambda i, j: (i, 0)),
                  pl.BlockSpec((k, tn), lambda i, j: (0, j))] + extra_specs(tm, tn),
        out_specs=out_specs,
        out_shape=out_shape,
        compiler_params=_cparams(2),
        name=name,
    )(u, w, *extras)


def _row_table_spec(tm, tn):
    return pl.BlockSpec((tm, LANES), lambda i, j: (i, 0))


def _col_vec_spec(tm, tn):
    return pl.BlockSpec((1, tn), lambda i, j: (0, j))


def _bq_up_body(c_ref, g_ref, w_ref, cos_ref, slo_ref, shi_ref, o_ref):
    c = c_ref[...]
    ms = jnp.mean(c * c, axis=-1, keepdims=True)
    u = ((c * lax.rsqrt(ms + NORM_EPS)) * g_ref[...]).astype(BF16)
    acc = jnp.dot(u, w_ref[...], preferred_element_type=F32) * ((B_NOPE + B_ROPE) ** -0.5 * LOG2E)
    cos, slo, shi = cos_ref[...], slo_ref[...], shi_ref[...]
    for h in range(B_HEADS):
        base = h * 2 * LANES
        o_ref[:, base:base + LANES] = acc[:, base:base + LANES].astype(o_ref.dtype)
        x = acc[:, base + LANES:base + 2 * LANES]
        o_ref[:, base + LANES:base + 2 * LANES] = _rope_pairs_32(x, cos, slo, shi).astype(o_ref.dtype)


def _bkv_up_body(c_ref, g_ref, w_ref, kpe_ref, k_ref, v_ref):
    c = c_ref[...]
    ms = jnp.mean(c * c, axis=-1, keepdims=True)
    u = ((c * lax.rsqrt(ms + NORM_EPS)) * g_ref[...]).astype(BF16)
    acc = jnp.dot(u, w_ref[...], preferred_element_type=F32)
    kpe = kpe_ref[...]
    for h in range(B_HEADS):
        base = h * 2 * LANES
        k_ref[:, base:base + LANES] = acc[:, h * LANES:(h + 1) * LANES].astype(k_ref.dtype)
        k_ref[:, base + LANES:base + 2 * LANES] = kpe
    v_ref[...] = acc[:, B_HEADS * LANES:].T.astype(v_ref.dtype)


def _bq_up(cb, g, w, cos, slo, shi):
    rows = cb.shape[0]
    tm = _row_tile(rows)
    n = w.shape[1]
    tab = pl.BlockSpec((tm, LANES), lambda i: (i, 0))
    return pl.pallas_call(
        _bq_up_body,
        grid=(rows // tm,),
        in_specs=[pl.BlockSpec((tm, B_Q_RANK), lambda i: (i, 0)),
                  pl.BlockSpec((1, B_Q_RANK), lambda i: (0, 0)),
                  pl.BlockSpec((B_Q_RANK, n), lambda i: (0, 0)),
                  tab, tab, tab],
        out_specs=pl.BlockSpec((tm, n), lambda i: (i, 0)),
        out_shape=jax.ShapeDtypeStruct((rows, n), BF16),
        compiler_params=_cparams(1),
        name="mla_q_up",
    )(cb, g.reshape(1, -1), w, cos, slo, shi)


def _bkv_up(cb, g, w, kpe):
    rows = cb.shape[0]
    tm = _row_tile(rows)
    n = w.shape[1]
    return pl.pallas_call(
        _bkv_up_body,
        grid=(rows // tm,),
        in_specs=[pl.BlockSpec((tm, B_KV_RANK), lambda i: (i, B_Q_RANK // B_KV_RANK)),
                  pl.BlockSpec((1, B_KV_RANK), lambda i: (0, 0)),
                  pl.BlockSpec((B_KV_RANK, n), lambda i: (0, 0)),
                  pl.BlockSpec((tm, LANES), lambda i: (i, 0))],
        out_specs=[pl.BlockSpec((tm, B_HEADS * 2 * LANES), lambda i: (i, 0)),
                   pl.BlockSpec((B_HEADS * B_VDIM, tm), lambda i: (0, i))],
        out_shape=[jax.ShapeDtypeStruct((rows, B_HEADS * 2 * LANES), BF16),
                   jax.ShapeDtypeStruct((B_HEADS * B_VDIM, rows), BF16)],
        compiler_params=_cparams(1),
        name="mla_kv_up",
    )(cb, g.reshape(1, -1), w, kpe)


def _online_softmax_sweep(k_ref, vt_ref, m_ref, acc_ref, s_ref, p_ref, a_ref, n_units, n_tok, scores, make_adjust,
                          lookahead, phases=None, first_variant=0, tail_variant=0, tail_fix=False,
                          value_head=lambda idx: 0):
    look = min(lookahead, n_units)
    n_chunks = n_tok // KV_CHUNK
    if phases is None:
        phases = [(0, n_chunks, 0, 0, False)]

    def probabilities(s, idx):
        m_prev = m_ref[idx]
        m_new = jnp.maximum(m_prev, jnp.max(s, axis=0, keepdims=True))
        m_ref[idx] = m_new
        return jnp.exp2(s - m_new).astype(BF16), jnp.exp2(m_prev - m_new)

    def accumulate(idx, p, alpha, vt_ones):
        acc_ref[idx] = alpha * acc_ref[idx] + jnp.dot(vt_ones, p, preferred_element_type=F32)

    def values_of(head, cols):
        return _with_ones_rows(vt_ref[head * HEAD_DIM:(head + 1) * HEAD_DIM, cols])

    def chunk(s_first, kc, cols, k_off, n_valid, kc_after, behind, variant, variant_after, fix):
        pending = list(s_first)
        vt_ones = {}
        adjust = make_adjust(k_off, kc.shape[0], n_valid, fix)
        for idx in range(n_units):
            s = pending.pop(0)
            ahead = idx + look
            if ahead < n_units:
                pending.append(scores(kc, ahead, _per_unit(variant, ahead)))
            elif kc_after is not None:
                s_ref[ahead - n_units] = scores(kc_after, ahead - n_units, _per_unit(variant_after, ahead - n_units))
            p, alpha = probabilities(adjust(s, idx), idx)
            if behind is not None:
                accumulate(*behind)
            head = value_head(idx)
            if head not in vt_ones:
                vt_ones[head] = values_of(head, cols)
            behind = (idx, p, alpha, vt_ones[head])
        return behind

    def chunk_step(c, carry, variant, variant_after, fix):
        off = pl.multiple_of(c * KV_CHUNK, KV_CHUNK)
        off_before = pl.multiple_of(jnp.maximum(c - 1, 0) * KV_CHUNK, KV_CHUNK)
        off_after = pl.multiple_of(jnp.minimum(c + 1, n_chunks - 1) * KV_CHUNK, KV_CHUNK)
        behind = (n_units - 1, p_ref[...], a_ref[...],
                  values_of(value_head(n_units - 1), pl.ds(off_before, KV_CHUNK)))
        _, p, alpha, _ = chunk([s_ref[j] for j in range(look)], k_ref[pl.ds(off, KV_CHUNK), :],
                               pl.ds(off, KV_CHUNK), N_META + off, None,
                               k_ref[pl.ds(off_after, KV_CHUNK), :], behind, variant, variant_after, fix)
        p_ref[...] = p
        a_ref[...] = alpha
        return carry

    m_ref[...] = jnp.full(m_ref.shape, NEG_BIG, F32)
    acc_ref[...] = jnp.zeros(acc_ref.shape, F32)
    p_ref[...] = jnp.zeros(p_ref.shape, BF16)
    a_ref[...] = jnp.ones(a_ref.shape, F32)
    for j in range(look):
        s_ref[j] = scores(k_ref[0:KV_CHUNK, :], j, _per_unit(first_variant, j))
    for lo, hi, variant, variant_after, fix in phases:
        lax.fori_loop(lo, hi, functools.partial(chunk_step, variant=variant, variant_after=variant_after, fix=fix), 0)
    accumulate(n_units - 1, p_ref[...], a_ref[...],
               values_of(value_head(n_units - 1), slice(n_tok - KV_CHUNK, n_tok)))
    k_tail = k_ref[n_tok:n_tok + TAIL, :]
    accumulate(*chunk([scores(k_tail, j, _per_unit(tail_variant, j)) for j in range(look)], k_tail,
                      slice(n_tok, n_tok + TAIL),
                      0, N_META, None, None, tail_variant, tail_variant, tail_fix))


def _per_unit(choice, idx):
    return choice(idx) if callable(choice) else choice


def _with_ones_rows(vt):
    return jnp.concatenate([vt, jnp.ones((ONES_ROWS, vt.shape[1]), vt.dtype)], axis=0)


def _normalized(acc):
    return acc[:HEAD_DIM] * (1.0 / acc[HEAD_DIM:HEAD_DIM + 1])


def _attn_ab_body(q_ref, k_ref, vt_ref, gate_ref, *rest, group, qsub, tqs, dk, n_tok, pre_transpose, lookahead):
    units = [(g, u) for g in range(group) for u in range(qsub)]
    if pre_transpose:
        o_ref, m_ref, acc_ref, s_ref, p_ref, a_ref, qt_ref = rest[-7:]
        for idx, (g, u) in enumerate(units):
            qt_ref[idx] = q_ref[u * tqs:(u + 1) * tqs, g * dk:(g + 1) * dk].astype(F32).T.astype(BF16)
    else:
        o_ref, m_ref, acc_ref, s_ref, p_ref, a_ref = rest[-6:]

    def scores(kc, idx, variant):
        if pre_transpose:
            return jnp.dot(kc, qt_ref[idx], preferred_element_type=F32)
        g, u = units[idx]
        return lax.dot_general(kc, q_ref[u * tqs:(u + 1) * tqs, g * dk:(g + 1) * dk], _NT,
                               preferred_element_type=F32)

    def make_adjust(k_off, n_keys, n_valid, fix):
        def adjust(s, idx):
            if n_valid is None:
                return s
            row = lax.broadcasted_iota(jnp.int32, s.shape, 0)
            return jnp.where(row < n_valid, s, NEG_BIG)
        return adjust

    _online_softmax_sweep(k_ref, vt_ref, m_ref, acc_ref, s_ref, p_ref, a_ref, len(units), n_tok, scores, make_adjust,
                          lookahead)

    for idx, (g, u) in enumerate(units):
        rows = slice(u * tqs, (u + 1) * tqs)
        cols = slice(g * HEAD_DIM, (g + 1) * HEAD_DIM)
        o = _normalized(acc_ref[idx]).T
        o_ref[rows, cols] = (o * gate_ref[rows, cols]).astype(o_ref.dtype)


def _attn_c_body(q_ref, k_ref, vt_ref, gate_ref, lam_ref, slope_ref, gsub_ref, *rest,
                 heads, qsub, tqs, n_tok, q_is_meta, lambda_init, lookahead):
    o_ref, m_ref, acc_ref, s_ref, p_ref, a_ref, qt_ref = rest[-7:]
    slot_w = 2 * LANES
    units = [(hd, u, c) for hd in range(heads) for u in range(qsub) for c in range(2)]
    lane = lax.broadcasted_iota(jnp.int32, (1, LANES), 1)
    flip = jnp.where(lane < C_QK_DIM, 1.0, -1.0)
    for idx, (hd, u, c) in enumerate(units):
        col = hd * slot_w + c * LANES
        q = q_ref[u * tqs:(u + 1) * tqs, col:col + LANES].astype(F32)
        qt_ref[0, idx] = q.T.astype(BF16)
        qt_ref[1, idx] = (q * flip).T.astype(BF16)
    slope2 = [(2.0 * LOG2E) * slope_ref[hd, :, :1] for hd in range(heads)]
    tq = qsub * tqs
    n_chunks = n_tok // KV_CHUNK
    if q_is_meta:
        q_off = 0
        phases = [(0, n_chunks, 1, 1, False)]
        first_variant, tail_fix = 1, True
    else:
        assert tq % KV_CHUNK == 0 and KV_CHUNK % tqs == 0
        span = tq // KV_CHUNK
        per_chunk = KV_CHUNK // tqs
        i = pl.program_id(2)
        q_off = N_META + i * tq
        d = i * span

        def variant_in(j):
            return lambda idx: 1 if units[idx][1] < j * per_chunk else 0

        def fix_in(j):
            return lambda idx: j * per_chunk <= units[idx][1] < (j + 1) * per_chunk

        phases = ([(0, d, 0, 0, False)]
                  + [(d + j, d + j + 1, variant_in(j), variant_in(j + 1) if j + 1 < span else 1, fix_in(j))
                     for j in range(span)]
                  + [(d + span, n_chunks, 1, 1, False)])
        first_variant, tail_fix = 0, False

    def scores(kc, idx, variant):
        hd, _, c = units[idx]
        col = hd * slot_w + c * LANES
        return jnp.dot(kc[:, col:col + LANES], qt_ref[variant, idx], preferred_element_type=F32)

    def make_adjust(k_off, n_keys, n_valid, fix):
        key = lax.broadcasted_iota(jnp.int32, (n_keys, tqs), 0)
        qry = lax.broadcasted_iota(jnp.int32, (n_keys, tqs), 1)
        late = {}

        def adjust(s, idx):
            hd, u, _ = units[idx]
            if _per_unit(fix, idx):
                if (hd, u) not in late:
                    ahead = ((key - qry) + (k_off - q_off - u * tqs)).astype(F32)
                    late[hd, u] = slope2[hd] * jnp.maximum(ahead, 0.0)
                s = s - late[hd, u]
            if n_valid is not None:
                s = jnp.where(key < n_valid, s, NEG_BIG)
            return s
        return adjust

    _online_softmax_sweep(k_ref, vt_ref, m_ref, acc_ref, s_ref, p_ref, a_ref, len(units), n_tok, scores, make_adjust,
                          lookahead, phases=phases, first_variant=first_variant, tail_variant=0, tail_fix=tail_fix,
                          value_head=lambda idx: units[idx][0])

    lv = lam_ref[...]
    lam = (jnp.exp(jnp.sum(lv[0:1] * lv[1:2], axis=1, keepdims=True))
           - jnp.exp(jnp.sum(lv[2:3] * lv[3:4], axis=1, keepdims=True)) + lambda_init)
    for hd in range(heads):
        cols = slice(hd * C_VDIM, (hd + 1) * C_VDIM)
        for u in range(qsub):
            rows = slice(u * tqs, (u + 1) * tqs)
            first = 2 * (hd * qsub + u)
            o = (_normalized(acc_ref[first]) - lam * _normalized(acc_ref[first + 1])).T
            ms = jnp.mean(o * o, axis=-1, keepdims=True)
            o = ((o * lax.rsqrt(ms + NORM_EPS)) * gsub_ref[...]) * (1.0 - lambda_init)
            o_ref[rows, cols] = (o * gate_ref[rows, cols]).astype(o_ref.dtype)


def _attention(body, q, q_col0, q_w, k, k_col0, k_w, vt, v_row0, gate, gate_col0, extras, extra_specs,
               out_w_total, heads, group, qsub, n_units, n_tok, name, lookahead, qt_shape, v_heads=1):
    bsz, lp, _ = q.shape
    out_w = group * HEAD_DIM

    def call(qsub_, tqs, n_qblk, qblk0, prev):
        tq = qsub_ * tqs
        in_specs = [
            pl.BlockSpec((None, tq, q_w), lambda b, h, i: (b, qblk0 + i, q_col0 + h)),
            pl.BlockSpec((None, lp, k_w), lambda b, h, i: (b, 0, k_col0 + h)),
            pl.BlockSpec((v_heads * HEAD_DIM, lp), lambda b, h, i: (v_row0 + h, b)),
            pl.BlockSpec((None, tq, out_w), lambda b, h, i: (b, qblk0 + i, gate_col0 + h)),
        ] + extra_specs
        args = [q, k, vt, gate] + extras
        aliases = {}
        if prev is not None:
            in_specs.append(pl.BlockSpec(memory_space=pl.ANY))
            args.append(prev)
            aliases = {len(args) - 1: 0}
        units = n_units * qsub_
        qt = qt_shape(units, tqs)
        return pl.pallas_call(
            body(qsub_, tqs, prev is not None),
            grid=(bsz, heads, n_qblk),
            in_specs=in_specs,
            out_specs=pl.BlockSpec((None, tq, out_w), lambda b, h, i: (b, qblk0 + i, h)),
            out_shape=jax.ShapeDtypeStruct((bsz, lp, out_w_total), BF16),
            scratch_shapes=[pltpu.VMEM((units, 1, tqs), F32),
                            pltpu.VMEM((units, HEAD_DIM + ONES_ROWS, tqs), F32),
                            pltpu.VMEM((lookahead, KV_CHUNK, tqs), F32),
                            pltpu.VMEM((KV_CHUNK, tqs), BF16), pltpu.VMEM((1, tqs), F32)]
                           + ([] if qt is None else [pltpu.VMEM(qt, BF16)]),
            input_output_aliases=aliases,
            compiler_params=_cparams(3),
            name=name + ("_meta" if prev is not None else ""),
        )(*args)

    main = call(qsub, Q_SUB, n_tok // (qsub * Q_SUB), 0, None)
    return call(1, TAIL, 1, n_tok // TAIL, main)


def _attn_a(qk, vt_plain, gates, n_tok):
    def body(qsub_, tqs, is_meta):
        return functools.partial(_attn_ab_body, group=A_GROUP, qsub=qsub_, tqs=tqs, dk=HEAD_DIM, n_tok=n_tok,
                                 pre_transpose=False, lookahead=A_LOOKAHEAD)

    return _attention(body, qk, 0, A_GROUP * HEAD_DIM, qk, A_WIDTH // HEAD_DIM, HEAD_DIM, vt_plain, 0,
                      gates, 0, [], [], A_WIDTH, A_KV_HEADS, A_GROUP, 8, A_GROUP, n_tok, "attn_gqa",
                      A_LOOKAHEAD, lambda units, tqs: None)


def _attn_b(qb, kb, vt_b, gates, n_tok):
    dk = 2 * LANES

    def body(qsub_, tqs, is_meta):
        return functools.partial(_attn_ab_body, group=1, qsub=qsub_, tqs=tqs, dk=dk, n_tok=n_tok,
                                 pre_transpose=True, lookahead=B_LOOKAHEAD)

    return _attention(body, qb, 0, dk, kb, 0, dk, vt_b, 0, gates, A_WIDTH // HEAD_DIM, [], [],
                      B_WIDTH, B_HEADS, 1, 16, 1, n_tok, "attn_mla", B_LOOKAHEAD,
                      lambda units, tqs: (units, dk, tqs))


def _attn_c(cqk, vt_plain, gates, lamvec, slopes, g_sub, lambda_init, n_tok):
    pair = 2
    dk = pair * 2 * LANES

    def body(qsub_, tqs, is_meta):
        return functools.partial(_attn_c_body, heads=pair, qsub=qsub_, tqs=tqs, n_tok=n_tok, q_is_meta=is_meta,
                                 lambda_init=lambda_init, lookahead=C_LOOKAHEAD)

    extras = [lamvec, slopes, g_sub.reshape(1, C_VDIM)]
    extra_specs = [pl.BlockSpec((4, C_QK_DIM), lambda b, h, i: (0, 0)),
                   pl.BlockSpec((pair, 1, LANES), lambda b, h, i: (h, 0, 0)),
                   pl.BlockSpec((1, C_VDIM), lambda b, h, i: (0, 0))]
    return _attention(body, cqk, 0, dk, cqk, C_HEADS // pair, dk, vt_plain, A_KV_HEADS // pair, gates,
                      (A_WIDTH + B_WIDTH) // (pair * HEAD_DIM), extras, extra_specs,
                      C_WIDTH, C_HEADS // pair, pair, 8, 2 * pair, n_tok, "attn_diff", C_LOOKAHEAD,
                      lambda units, tqs: (2, units, LANES, tqs), v_heads=pair)


def _out_proj_body(a_ref, b_ref, c_ref, wa_ref, wb_ref, wc_ref, h_ref, o_ref):
    acc = jnp.dot(a_ref[...], wa_ref[...], preferred_element_type=F32)
    acc += jnp.dot(b_ref[...], wb_ref[...], preferred_element_type=F32)
    acc += jnp.dot(c_ref[...], wc_ref[...], preferred_element_type=F32)
    o_ref[...] = h_ref[...] + acc


def _out_proj(oa, ob, oc, w, h2d):
    rows = h2d.shape[0]
    tm = _tall_row_tile(rows)
    tn = 512
    return pl.pallas_call(
        _out_proj_body,
        grid=(rows // tm, D_MODEL // tn),
        in_specs=[pl.BlockSpec((tm, A_WIDTH), lambda i, j: (i, 0)),
                  pl.BlockSpec((tm, B_WIDTH), lambda i, j: (i, 0)),
                  pl.BlockSpec((tm, C_WIDTH), lambda i, j: (i, 0)),
                  pl.BlockSpec((A_WIDTH, tn), lambda i, j: (0, j)),
                  pl.BlockSpec((B_WIDTH, tn), lambda i, j: (A_WIDTH // B_WIDTH, j)),
                  pl.BlockSpec((C_WIDTH, tn), lambda i, j: ((A_WIDTH + B_WIDTH) // C_WIDTH, j)),
                  pl.BlockSpec((tm, tn), lambda i, j: (i, j))],
        out_specs=pl.BlockSpec((tm, tn), lambda i, j: (i, j)),
        out_shape=jax.ShapeDtypeStruct((rows, D_MODEL), F32),
        compiler_params=_cparams(2),
        name="out_proj",
    )(oa, ob, oc, w, w, w, h2d)


def _rope_angles(pos_f, n_freq):
    inv = ROPE_THETA ** (-jnp.arange(n_freq, dtype=F32) / n_freq)
    return pos_f[:, None] * inv[None, :]


def _position_tables(n_tok, bsz):
    rows = n_tok // GRID_W
    z = jnp.zeros((TAIL,), F32)
    row_f = jnp.concatenate([jnp.repeat(jnp.arange(rows, dtype=F32), GRID_W), z])
    col_f = jnp.concatenate([jnp.tile(jnp.arange(GRID_W, dtype=F32), rows), z])
    pos_f = jnp.concatenate([jnp.arange(n_tok, dtype=F32) + N_META,
                             jnp.arange(N_META, dtype=F32), jnp.zeros((TAIL - N_META,), F32)])
    ang_a = jnp.concatenate([_rope_angles(row_f, HEAD_DIM // 4), _rope_angles(col_f, HEAD_DIM // 4)], axis=-1)
    cos_a, sin_a = jnp.cos(ang_a), jnp.sin(ang_a)
    cos_a = jnp.concatenate([cos_a, cos_a], axis=-1)
    sin_a = jnp.concatenate([-sin_a, sin_a], axis=-1)
    ang_b = _rope_angles(pos_f, B_ROPE // 2)
    cos_b, sin_b = jnp.cos(ang_b), jnp.sin(ang_b)
    zb = jnp.zeros_like(cos_b)
    cos_b128 = jnp.concatenate([cos_b, cos_b, zb, zb], axis=-1)
    sin_lo = jnp.concatenate([-sin_b, zb, zb, zb], axis=-1)
    sin_hi = jnp.concatenate([zb, sin_b, zb, zb], axis=-1)
    hi = jnp.floor(pos_f / LANES)
    lo = pos_f - hi * LANES
    one = jnp.ones_like(pos_f)

    def slot_lanes(cols):
        body = jnp.stack(cols, axis=-1)
        return jnp.concatenate([jnp.zeros((lp, C_QK_DIM), F32), body,
                                jnp.zeros((lp, LANES - C_QK_DIM - len(cols)), F32)], axis=-1)

    lp = n_tok + TAIL
    pieces, rest = [], jnp.float32(LOG2E)
    for _ in range(3):
        piece = rest.astype(BF16).astype(F32)
        pieces.append(piece)
        rest = rest - piece
    q_terms = slot_lanes([col for c in pieces for col in (LANES * c * one, c * one, -LANES * hi, -lo)])
    k_terms = slot_lanes([col for c in pieces for col in (hi, lo, c * one, c * one)])
    return tuple(jnp.tile(t, (bsz, 1)) for t in (cos_a, sin_a, cos_b128, sin_lo, sin_hi, q_terms, k_terms))


def _prep_weights(w_in, w_q_b, w_kv_b, w_out, g_qn, g_kn):
    bounds = [0]
    for s in IN_SPLITS:
        bounds.append(bounds[-1] + s)
    col = lambda i: slice(bounds[i], bounds[i + 1])
    w = w_in.astype(BF16)
    aq, ak, av, ag, bq, bkv, bpe, bg, cq, ck, cv, cg = (w[:, col(i)] for i in range(12))
    w_aqk = jnp.concatenate([aq, ak], axis=1)
    w_plain = jnp.concatenate([av, cv], axis=1)
    w_gate = jnp.concatenate([ag, bg, cg], axis=1)
    w_cqk = jnp.concatenate([cq, ck], axis=1)
    w_blow = jnp.concatenate([bq, bkv], axis=1)
    w_bpe = jnp.concatenate([bpe, jnp.zeros((D_MODEL, LANES - B_ROPE), BF16)], axis=1)
    g_aqk = jnp.concatenate([jnp.tile(g_qn, A_HEADS), jnp.tile(g_kn, A_KV_HEADS)]).reshape(1, -1)
    m_aqk = jnp.concatenate([jnp.full((A_WIDTH,), HEAD_DIM ** -0.5 * LOG2E, F32),
                             jnp.ones((A_KV_WIDTH,), F32)]).reshape(1, -1)
    wq = w_q_b.astype(BF16).reshape(B_Q_RANK, B_HEADS, B_NOPE + B_ROPE)
    wq = jnp.concatenate([wq, jnp.zeros((B_Q_RANK, B_HEADS, 2 * LANES - B_NOPE - B_ROPE), BF16)], axis=-1)
    wq = wq.reshape(B_Q_RANK, B_HEADS * 2 * LANES)
    wkv = w_kv_b.astype(BF16).reshape(B_KV_RANK, B_HEADS, B_NOPE + B_VDIM)
    wkv = jnp.concatenate([wkv[:, :, :B_NOPE].reshape(B_KV_RANK, -1),
                           wkv[:, :, B_NOPE:].reshape(B_KV_RANK, -1)], axis=1)
    return dict(aqk=w_aqk, plain=w_plain, gate=w_gate, cqk=w_cqk, blow=w_blow, bpe=w_bpe,
                g_aqk=g_aqk, m_aqk=m_aqk, q_up=wq, kv_up=wkv, out=w_out.astype(BF16))


def _encode(x, meta, layers, g_final, slopes, cq_cols):
    bsz, n_tok, _ = x.shape
    lp = n_tok + TAIL
    rows = bsz * lp
    tail = jnp.concatenate([meta.astype(x.dtype), jnp.zeros((TAIL - N_META, D_MODEL), x.dtype)], axis=0)
    h = jnp.concatenate([x, jnp.broadcast_to(tail[None], (bsz, TAIL, D_MODEL))], axis=1).reshape(rows, D_MODEL)
    cos_a, sin_a, cos_b, sin_lo, sin_hi, q_terms, k_terms = _position_tables(n_tok, bsz)
    three = lambda a: a.reshape(bsz, lp, a.shape[-1])

    for l, p in enumerate(layers):
        lambda_init = 0.8 - 0.6 * math.exp(-0.3 * l)
        u = _rmsnorm_rows(h, p["g_attn"], BF16)
        aqk = _staggered_proj(_aqk_epilogue, u, p["aqk"], [p["g_aqk"], p["m_aqk"]], [cos_a, sin_a],
                              A_WIDTH + A_KV_WIDTH, BF16, 512, 512, True, "proj_gqa_qk")
        vt_plain = _proj(_proj_transposed_body, u, p["plain"], [], lambda tm, tn: [],
                         A_KV_WIDTH + C_WIDTH, BF16, 512, transposed=True, name="proj_v")
        gates = _proj(functools.partial(_direct_proj_body, epilogue=_gate_epilogue), u, p["gate"], [],
                      lambda tm, tn: [], A_WIDTH + B_WIDTH + C_WIDTH, F32, 512, name="proj_gate")
        cqk = _proj(functools.partial(_direct_proj_body, epilogue=_cqk_epilogue), u, p["cqk"],
                    list(cq_cols) + [q_terms, k_terms],
                    lambda tm, tn: [_col_vec_spec(tm, tn)] * 3 + [_row_table_spec(tm, tn)] * 2,
                    4 * C_QK_WIDTH, BF16, 512, out_tn=1024, name="proj_diff_qk")
        blow = _proj(_proj_plain_body, u, p["blow"], [], lambda tm, tn: [],
                     B_Q_RANK + B_KV_RANK, F32, 512, name="proj_mla_low")
        kpe = _proj(_proj_bpe_body, u, p["bpe"], [cos_b, sin_lo, sin_hi],
                    lambda tm, tn: [_row_table_spec(tm, tn)] * 3, LANES, BF16, LANES, name="proj_mla_pe")
        qb = _bq_up(blow, p["g_q_a"], p["q_up"], cos_b, sin_lo, sin_hi)
        kb, vt_b = _bkv_up(blow, p["g_kv_a"], p["kv_up"], kpe)

        gates3 = three(gates)
        oa = _attn_a(three(aqk), vt_plain, gates3, n_tok)
        ob = _attn_b(three(qb), three(kb), vt_b, gates3, n_tok)
        oc = _attn_c(three(cqk), vt_plain, gates3, p["lamvec"], slopes, p["g_sub"], lambda_init, n_tok)
        h = _out_proj(oa.reshape(rows, -1), ob.reshape(rows, -1), oc.reshape(rows, -1), p["out"], h)

    return _final_norm(h.reshape(bsz, lp, D_MODEL), g_final, n_tok)


def kernel(x_prompt, x_sample, meta, g_attn, w_in, g_qn, g_kn, g_q_a, w_q_b, g_kv_a, w_kv_b,
           lam_q1, lam_k1, lam_q2, lam_k2, g_sub, w_out, g_final):
    layers = []
    for l in range(DEPTH):
        p = _prep_weights(w_in[l], w_q_b[l], w_kv_b[l], w_out[l], g_qn[l], g_kn[l])
        p.update(g_attn=g_attn[l], g_q_a=g_q_a[l], g_kv_a=g_kv_a[l], g_sub=g_sub[l],
                 lamvec=jnp.stack([lam_q1[l], lam_k1[l], lam_q2[l], lam_k2[l]]).astype(F32))
        layers.append(p)
    slope_h = 2.0 ** (-(jnp.arange(C_HEADS, dtype=F32) + 1.0) * 8.0 / C_HEADS)
    slopes = jnp.broadcast_to(slope_h[:, None, None], (C_HEADS, 1, LANES))
    zeros_w, ones_w = jnp.zeros((C_QK_WIDTH,), F32), jnp.ones((C_QK_WIDTH,), F32)
    cq_cols = [jnp.concatenate([C_QK_DIM ** -0.5 * LOG2E * ones_w, ones_w]).reshape(1, -1),
               jnp.concatenate([jnp.repeat(slope_h, 2 * C_QK_DIM), zeros_w]).reshape(1, -1),
               jnp.concatenate([zeros_w, ones_w]).reshape(1, -1)]
    y_prompt = _encode(x_prompt, meta, layers, g_final, slopes, cq_cols)
    y_sample = _encode(x_sample, meta, layers, g_final, slopes, cq_cols)
    return (y_prompt, y_sample)
```

```python
import functools
import math

import jax
import jax.numpy as jnp
from jax import lax
from jax.experimental import pallas as pl
from jax.experimental.pallas import tpu as pltpu

D_MODEL = 4096
DEPTH = 2
N_META = 16
GRID_W = 64
HEAD_DIM = 128
ROPE_THETA = 10000.0
NORM_EPS = 1e-6
A_HEADS = 16
A_KV_HEADS = 4
A_GROUP = A_HEADS // A_KV_HEADS
A_WIDTH = A_HEADS * HEAD_DIM
A_KV_WIDTH = A_KV_HEADS * HEAD_DIM
B_HEADS = 8
B_Q_RANK = 1024
B_KV_RANK = 512
B_NOPE = 128
B_ROPE = 64
B_VDIM = 128
B_WIDTH = B_HEADS * B_VDIM
C_HEADS = 8
C_QK_DIM = 64
C_VDIM = 2 * C_QK_DIM
C_QK_WIDTH = C_HEADS * 2 * C_QK_DIM
C_WIDTH = C_HEADS * C_VDIM
IN_SPLITS = (A_WIDTH, A_KV_WIDTH, A_KV_WIDTH, A_WIDTH,
             B_Q_RANK, B_KV_RANK, B_ROPE, B_WIDTH,
             C_QK_WIDTH, C_QK_WIDTH, C_WIDTH, C_WIDTH)

LANES = 128
TAIL = LANES
KV_CHUNK = 512
Q_SUB = 256
NEG_BIG = -1e30
LOG2E = 1.4426950408889634
A_LOOKAHEAD = 2
B_LOOKAHEAD = 2
C_LOOKAHEAD = 2
ONES_ROWS = 16
BF16 = jnp.bfloat16
F32 = jnp.float32
VMEM_LIMIT = 56 * 1024 * 1024

_NT = (((1,), (1,)), ((), ()))


def _cparams(n_axes):
    return pltpu.CompilerParams(dimension_semantics=("arbitrary",) * n_axes,
                                vmem_limit_bytes=VMEM_LIMIT)


def _row_tile(rows):
    for t in (512, 640, 256, 128):
        if rows % t == 0:
            return t
    raise ValueError(f"no row tile for {rows}")


def _tall_row_tile(rows):
    for t in (1536, 1664):
        if rows % t == 0:
            return t
    return _row_tile(rows)


def _rmsnorm_body(x_ref, g_ref, o_ref):
    x = x_ref[...]
    ms = jnp.mean(x * x, axis=-1, keepdims=True)
    o_ref[...] = ((x * lax.rsqrt(ms + NORM_EPS)) * g_ref[...]).astype(o_ref.dtype)


def _rmsnorm_rows(h2d, g, out_dtype):
    rows, d = h2d.shape
    tm = 256 if rows % 256 == 0 else 128
    return pl.pallas_call(
        _rmsnorm_body,
        grid=(rows // tm,),
        in_specs=[pl.BlockSpec((tm, d), lambda i: (i, 0)),
                  pl.BlockSpec((1, d), lambda i: (0, 0))],
        out_specs=pl.BlockSpec((tm, d), lambda i: (i, 0)),
        out_shape=jax.ShapeDtypeStruct((rows, d), out_dtype),
        compiler_params=_cparams(1),
        name="rmsnorm",
    )(h2d, g.reshape(1, d))


def _embed_norm_body(x_ref, tail_ref, g_ref, u_ref, h_ref, *, n_real):
    def emit(x):
        h_ref[...] = x
        ms = jnp.mean(x * x, axis=-1, keepdims=True)
        u_ref[...] = ((x * lax.rsqrt(ms + NORM_EPS)) * g_ref[...]).astype(u_ref.dtype)

    @pl.when(pl.program_id(1) < n_real)
    def _():
        emit(x_ref[...])

    @pl.when(pl.program_id(1) == n_real)
    def _():
        emit(tail_ref[...])


def _embed_norm(x, tail, g):
    bsz, n_tok, d = x.shape
    lp = n_tok + TAIL
    n_real = n_tok // TAIL
    block = pl.BlockSpec((None, TAIL, d), lambda b, i: (b, i, 0))
    u, h = pl.pallas_call(
        functools.partial(_embed_norm_body, n_real=n_real),
        grid=(bsz, n_real + 1),
        in_specs=[pl.BlockSpec((None, TAIL, d), lambda b, i: (b, jnp.minimum(i, n_real - 1), 0)),
                  pl.BlockSpec((TAIL, d), lambda b, i: (0, 0)),
                  pl.BlockSpec((1, d), lambda b, i: (0, 0))],
        out_specs=[block, block],
        out_shape=[jax.ShapeDtypeStruct((bsz, lp, d), BF16), jax.ShapeDtypeStruct((bsz, lp, d), x.dtype)],
        compiler_params=_cparams(2),
        name="embed_norm",
    )(x, tail, g.reshape(1, d))
    return u.reshape(bsz * lp, d), h.reshape(bsz * lp, d)


def _final_norm(h3d, g, n_tok):
    bsz, _, d = h3d.shape
    tm = 256
    return pl.pallas_call(
        _rmsnorm_body,
        grid=(bsz, n_tok // tm),
        in_specs=[pl.BlockSpec((None, tm, d), lambda b, i: (b, i, 0)),
                  pl.BlockSpec((1, d), lambda b, i: (0, 0))],
        out_specs=pl.BlockSpec((None, tm, d), lambda b, i: (b, i, 0)),
        out_shape=jax.ShapeDtypeStruct((bsz, n_tok, d), F32),
        compiler_params=_cparams(2),
        name="final_norm",
    )(h3d, g.reshape(1, d))


def _rope_pairs_64(y, cos, sin_signed):
    return y * cos + pltpu.roll(y, 64, 1) * sin_signed


def _rope_pairs_32(x, cos, sin_lo, sin_hi):
    return x * cos + pltpu.roll(x, 96, 1) * sin_lo + pltpu.roll(x, 32, 1) * sin_hi


def _staggered_proj_body(x_ref, w_ref, *rest, n_tiles, epilogue):
    *extra, o_ref, acc_ref = rest
    j = pl.program_id(1)

    def matmul():
        acc_ref[j % 2] = jnp.dot(x_ref[...], w_ref[...], preferred_element_type=F32)

    def finish_previous():
        epilogue(acc_ref.at[(j + 1) % 2], *extra, o_ref)

    @pl.when(j == 0)
    def _():
        matmul()

    @pl.when(jnp.logical_and(j > 0, j < n_tiles))
    def _():
        finish_previous()
        matmul()

    @pl.when(j == n_tiles)
    def _():
        finish_previous()


def _staggered_proj(epilogue, u, w, col_vecs, row_tables, n_out, out_dtype, tn, out_tn, tall, name):
    rows, k = u.shape
    n = w.shape[1]
    tm = _tall_row_tile(rows) if tall else _row_tile(rows)
    n_tiles = n // tn
    prev = lambda j: jnp.maximum(j - 1, 0)
    return pl.pallas_call(
        functools.partial(_staggered_proj_body, n_tiles=n_tiles, epilogue=epilogue),
        grid=(rows // tm, n_tiles + 1),
        in_specs=([pl.BlockSpec((tm, k), lambda i, j: (i, 0)),
                   pl.BlockSpec((k, tn), lambda i, j: (0, jnp.minimum(j, n_tiles - 1)))]
                  + [pl.BlockSpec((1, tn), lambda i, j: (0, prev(j)))] * len(col_vecs)
                  + [pl.BlockSpec((tm, LANES), lambda i, j: (i, 0))] * len(row_tables)),
        out_specs=pl.BlockSpec((tm, out_tn), lambda i, j: (i, prev(j))),
        out_shape=jax.ShapeDtypeStruct((rows, n_out), out_dtype),
        scratch_shapes=[pltpu.VMEM((2, tm, tn), F32)],
        compiler_params=_cparams(2),
        name=name,
    )(u, w, *col_vecs, *row_tables)


def _direct_proj_body(x_ref, w_ref, *rest, epilogue):
    *extra, o_ref = rest
    epilogue(jnp.dot(x_ref[...], w_ref[...], preferred_element_type=F32), *extra, o_ref)


def _aqk_epilogue(acc_ref, g_ref, mult_ref, cos_ref, sin_ref, o_ref):
    cos = cos_ref[...]
    sin = sin_ref[...]
    for h in range(o_ref.shape[1] // HEAD_DIM):
        sl = slice(h * HEAD_DIM, (h + 1) * HEAD_DIM)
        a = acc_ref[:, sl]
        ms = jnp.mean(a * a, axis=-1, keepdims=True)
        y = (a * lax.rsqrt(ms + NORM_EPS)) * g_ref[:, sl]
        o_ref[:, sl] = (_rope_pairs_64(y, cos, sin) * mult_ref[:, sl]).astype(o_ref.dtype)


def _gate_epilogue(acc_ref, o_ref):
    a = acc_ref[...]
    o_ref[...] = a * (1.0 / (1.0 + jnp.exp(-a)))


def _proj_plain_body(x_ref, w_ref, o_ref):
    o_ref[...] = jnp.dot(x_ref[...], w_ref[...], preferred_element_type=F32).astype(o_ref.dtype)


def _proj_transposed_body(x_ref, w_ref, o_ref):
    o_ref[...] = jnp.dot(x_ref[...], w_ref[...], preferred_element_type=F32).T.astype(o_ref.dtype)


def _cqk_epilogue(acc_ref, mult_ref, qcoef_ref, kcoef_ref, qtab_ref, ktab_ref, o_ref):
    acc = acc_ref[...] * mult_ref[...]
    lane = lax.broadcasted_iota(jnp.int32, (acc.shape[0], LANES), 1)
    low = lane < C_QK_DIM
    qtab = qtab_ref[...]
    ktab = ktab_ref[...]
    for t in range(acc.shape[1] // LANES):
        sl = slice(t * LANES, (t + 1) * LANES)
        a = acc[:, sl]
        pos_terms = qtab * qcoef_ref[:, sl] + ktab * kcoef_ref[:, sl]
        o_ref[:, (2 * t) * LANES:(2 * t + 1) * LANES] = jnp.where(low, a, pos_terms).astype(o_ref.dtype)
        o_ref[:, (2 * t + 1) * LANES:(2 * t + 2) * LANES] = (
            jnp.where(low, pltpu.roll(a, 64, 1), pos_terms).astype(o_ref.dtype))


def _proj_bpe_body(x_ref, w_ref, cos_ref, slo_ref, shi_ref, o_ref):
    acc = jnp.dot(x_ref[...], w_ref[...], preferred_element_type=F32)
    o_ref[...] = _rope_pairs_32(acc, cos_ref[...], slo_ref[...], shi_ref[...]).astype(o_ref.dtype)


def _proj(body, u, w, extras, extra_specs, n_out, out_dtype, tn, out_tn=None, transposed=False, tall=True,
          name="proj"):
    rows, k = u.shape
    n = w.shape[1]
    tm = _tall_row_tile(rows) if tall else _row_tile(rows)
    out_tn = tn if out_tn is None else out_tn
    if transposed:
        out_specs = pl.BlockSpec((out_tn, tm), lambda i, j: (j, i))
        out_shape = jax.ShapeDtypeStruct((n_out, rows), out_dtype)
    else:
        out_specs = pl.BlockSpec((tm, out_tn), lambda i, j: (i, j))
        out_shape = jax.ShapeDtypeStruct((rows, n_out), out_dtype)
    return pl.pallas_call(
        body,
        grid=(rows // tm, n // tn),
        in_specs=[pl.BlockSpec((tm, k), lambda i, j: (i, 0)),
                  pl.BlockSpec((k, tn), lambda i, j: (0, j))] + extra_specs(tm, tn),
        out_specs=out_specs,
        out_shape=out_shape,
        compiler_params=_cparams(2),
        name=name,
    )(u, w, *extras)


def _row_table_spec(tm, tn):
    return pl.BlockSpec((tm, LANES), lambda i, j: (i, 0))


def _col_vec_spec(tm, tn):
    return pl.BlockSpec((1, tn), lambda i, j: (0, j))


def _bq_up_body(c_ref, g_ref, w_ref, cos_ref, slo_ref, shi_ref, o_ref):
    c = c_ref[...]
    ms = jnp.mean(c * c, axis=-1, keepdims=True)
    u = ((c * lax.rsqrt(ms + NORM_EPS)) * g_ref[...]).astype(BF16)
    acc = jnp.dot(u, w_ref[...], preferred_element_type=F32) * ((B_NOPE + B_ROPE) ** -0.5 * LOG2E)
    cos, slo, shi = cos_ref[...], slo_ref[...], shi_ref[...]
    for h in range(B_HEADS):
        base = h * 2 * LANES
        o_ref[:, base:base + LANES] = acc[:, base:base + LANES].astype(o_ref.dtype)
        x = acc[:, base + LANES:base + 2 * LANES]
        o_ref[:, base + LANES:base + 2 * LANES] = _rope_pairs_32(x, cos, slo, shi).astype(o_ref.dtype)


def _bkv_up_body(c_ref, g_ref, w_ref, kpe_ref, k_ref, v_ref):
    c = c_ref[...]
    ms = jnp.mean(c * c, axis=-1, keepdims=True)
    u = ((c * lax.rsqrt(ms + NORM_EPS)) * g_ref[...]).astype(BF16)
    acc = jnp.dot(u, w_ref[...], preferred_element_type=F32)
    kpe = kpe_ref[...]
    for h in range(B_HEADS):
        base = h * 2 * LANES
        k_ref[:, base:base + LANES] = acc[:, h * LANES:(h + 1) * LANES].astype(k_ref.dtype)
        k_ref[:, base + LANES:base + 2 * LANES] = kpe
    v_ref[...] = acc[:, B_HEADS * LANES:].T.astype(v_ref.dtype)


def _bq_up(cb, g, w, cos, slo, shi):
    rows = cb.shape[0]
    tm = _row_tile(rows)
    n = w.shape[1]
    tab = pl.BlockSpec((tm, LANES), lambda i: (i, 0))
    return pl.pallas_call(
        _bq_up_body,
        grid=(rows // tm,),
        in_specs=[pl.BlockSpec((tm, B_Q_RANK), lambda i: (i, 0)),
                  pl.BlockSpec((1, B_Q_RANK), lambda i: (0, 0)),
                  pl.BlockSpec((B_Q_RANK, n), lambda i: (0, 0)),
                  tab, tab, tab],
        out_specs=pl.BlockSpec((tm, n), lambda i: (i, 0)),
        out_shape=jax.ShapeDtypeStruct((rows, n), BF16),
        compiler_params=_cparams(1),
        name="mla_q_up",
    )(cb, g.reshape(1, -1), w, cos, slo, shi)


def _bkv_up(cb, g, w, kpe):
    rows = cb.shape[0]
    tm = _row_tile(rows)
    n = w.shape[1]
    return pl.pallas_call(
        _bkv_up_body,
        grid=(rows // tm,),
        in_specs=[pl.BlockSpec((tm, B_KV_RANK), lambda i: (i, B_Q_RANK // B_KV_RANK)),
                  pl.BlockSpec((1, B_KV_RANK), lambda i: (0, 0)),
                  pl.BlockSpec((B_KV_RANK, n), lambda i: (0, 0)),
                  pl.BlockSpec((tm, LANES), lambda i: (i, 0))],
        out_specs=[pl.BlockSpec((tm, B_HEADS * 2 * LANES), lambda i: (i, 0)),
                   pl.BlockSpec((B_HEADS * B_VDIM, tm), lambda i: (0, i))],
        out_shape=[jax.ShapeDtypeStruct((rows, B_HEADS * 2 * LANES), BF16),
                   jax.ShapeDtypeStruct((B_HEADS * B_VDIM, rows), BF16)],
        compiler_params=_cparams(1),
        name="mla_kv_up",
    )(cb, g.reshape(1, -1), w, kpe)


def _online_softmax_sweep(k_ref, vt_ref, m_ref, acc_ref, s_ref, p_ref, a_ref, n_units, n_tok, scores, make_adjust,
                          lookahead, phases=None, first_variant=0, tail_variant=0, tail_fix=False,
                          value_head=lambda idx: 0):
    look = min(lookahead, n_units)
    n_chunks = n_tok // KV_CHUNK
    if phases is None:
        phases = [(0, n_chunks, 0, 0, False)]

    def probabilities(s, idx):
        m_prev = m_ref[idx]
        m_new = jnp.maximum(m_prev, jnp.max(s, axis=0, keepdims=True))
        m_ref[idx] = m_new
        return jnp.exp2(s - m_new).astype(BF16), jnp.exp2(m_prev - m_new)

    def accumulate(idx, p, alpha, vt_ones):
        acc_ref[idx] = alpha * acc_ref[idx] + jnp.dot(vt_ones, p, preferred_element_type=F32)

    def values_of(head, cols):
        return _with_ones_rows(vt_ref[head * HEAD_DIM:(head + 1) * HEAD_DIM, cols])

    def chunk(s_first, kc, cols, k_off, n_valid, kc_after, behind, variant, variant_after, fix):
        pending = list(s_first)
        vt_ones = {}
        adjust = make_adjust(k_off, kc.shape[0], n_valid, fix)
        for idx in range(n_units):
            s = pending.pop(0)
            ahead = idx + look
            if ahead < n_units:
                pending.append(scores(kc, ahead, _per_unit(variant, ahead)))
            elif kc_after is not None:
                s_ref[ahead - n_units] = scores(kc_after, ahead - n_units, _per_unit(variant_after, ahead - n_units))
            p, alpha = probabilities(adjust(s, idx), idx)
            if behind is not None:
                accumulate(*behind)
            head = value_head(idx)
            if head not in vt_ones:
                vt_ones[head] = values_of(head, cols)
            behind = (idx, p, alpha, vt_ones[head])
        return behind

    def chunk_step(c, carry, variant, variant_after, fix):
        off = pl.multiple_of(c * KV_CHUNK, KV_CHUNK)
        off_before = pl.multiple_of(jnp.maximum(c - 1, 0) * KV_CHUNK, KV_CHUNK)
        off_after = pl.multiple_of(jnp.minimum(c + 1, n_chunks - 1) * KV_CHUNK, KV_CHUNK)
        behind = (n_units - 1, p_ref[...], a_ref[...],
                  values_of(value_head(n_units - 1), pl.ds(off_before, KV_CHUNK)))
        _, p, alpha, _ = chunk([s_ref[j] for j in range(look)], k_ref[pl.ds(off, KV_CHUNK), :],
                               pl.ds(off, KV_CHUNK), N_META + off, None,
                               k_ref[pl.ds(off_after, KV_CHUNK), :], behind, variant, variant_after, fix)
        p_ref[...] = p
        a_ref[...] = alpha
        return carry

    m_ref[...] = jnp.full(m_ref.shape, NEG_BIG, F32)
    acc_ref[...] = jnp.zeros(acc_ref.shape, F32)
    p_ref[...] = jnp.zeros(p_ref.shape, BF16)
    a_ref[...] = jnp.ones(a_ref.shape, F32)
    for j in range(look):
        s_ref[j] = scores(k_ref[0:KV_CHUNK, :], j, _per_unit(first_variant, j))
    for lo, hi, variant, variant_after, fix in phases:
        lax.fori_loop(lo, hi, functools.partial(chunk_step, variant=variant, variant_after=variant_after, fix=fix), 0)
    accumulate(n_units - 1, p_ref[...], a_ref[...],
               values_of(value_head(n_units - 1), slice(n_tok - KV_CHUNK, n_tok)))
    k_tail = k_ref[n_tok:n_tok + TAIL, :]
    accumulate(*chunk([scores(k_tail, j, _per_unit(tail_variant, j)) for j in range(look)], k_tail,
                      slice(n_tok, n_tok + TAIL),
                      0, N_META, None, None, tail_variant, tail_variant, tail_fix))


def _per_unit(choice, idx):
    return choice(idx) if callable(choice) else choice


def _with_ones_rows(vt):
    return jnp.concatenate([vt, jnp.ones((ONES_ROWS, vt.shape[1]), vt.dtype)], axis=0)


def _normalized(acc):
    return acc[:HEAD_DIM] * (1.0 / acc[HEAD_DIM:HEAD_DIM + 1])


def _attn_ab_body(q_ref, k_ref, vt_ref, gate_ref, *rest, group, qsub, tqs, dk, n_tok, pre_transpose, lookahead):
    units = [(g, u) for g in range(group) for u in range(qsub)]
    if pre_transpose:
        o_ref, m_ref, acc_ref, s_ref, p_ref, a_ref, qt_ref = rest[-7:]
        for idx, (g, u) in enumerate(units):
            qt_ref[idx] = q_ref[u * tqs:(u + 1) * tqs, g * dk:(g + 1) * dk].astype(F32).T.astype(BF16)
    else:
        o_ref, m_ref, acc_ref, s_ref, p_ref, a_ref = rest[-6:]

    def scores(kc, idx, variant):
        if pre_transpose:
            return jnp.dot(kc, qt_ref[idx], preferred_element_type=F32)
        g, u = units[idx]
        return lax.dot_general(kc, q_ref[u * tqs:(u + 1) * tqs, g * dk:(g + 1) * dk], _NT,
                               preferred_element_type=F32)

    def make_adjust(k_off, n_keys, n_valid, fix):
        def adjust(s, idx):
            if n_valid is None:
                return s
            row = lax.broadcasted_iota(jnp.int32, s.shape, 0)
            return jnp.where(row < n_valid, s, NEG_BIG)
        return adjust

    _online_softmax_sweep(k_ref, vt_ref, m_ref, acc_ref, s_ref, p_ref, a_ref, len(units), n_tok, scores, make_adjust,
                          lookahead)

    for idx, (g, u) in enumerate(units):
        rows = slice(u * tqs, (u + 1) * tqs)
        cols = slice(g * HEAD_DIM, (g + 1) * HEAD_DIM)
        o = _normalized(acc_ref[idx]).T
        o_ref[rows, cols] = (o * gate_ref[rows, cols]).astype(o_ref.dtype)


def _attn_c_body(q_ref, k_ref, vt_ref, gate_ref, lam_ref, slope_ref, gsub_ref, *rest,
                 heads, qsub, tqs, n_tok, q_is_meta, lambda_init, lookahead):
    o_ref, m_ref, acc_ref, s_ref, p_ref, a_ref, qt_ref = rest[-7:]
    slot_w = 2 * LANES
    units = [(hd, u, c) for hd in range(heads) for u in range(qsub) for c in range(2)]
    lane = lax.broadcasted_iota(jnp.int32, (1, LANES), 1)
    flip = jnp.where(lane < C_QK_DIM, 1.0, -1.0)
    for idx, (hd, u, c) in enumerate(units):
        col = hd * slot_w + c * LANES
        q = q_ref[u * tqs:(u + 1) * tqs, col:col + LANES].astype(F32)
        qt_ref[0, idx] = q.T.astype(BF16)
        qt_ref[1, idx] = (q * flip).T.astype(BF16)
    slope2 = [(2.0 * LOG2E) * slope_ref[hd, :, :1] for hd in range(heads)]
    tq = qsub * tqs
    n_chunks = n_tok // KV_CHUNK
    if q_is_meta:
        q_off = 0
        phases = [(0, n_chunks, 1, 1, False)]
        first_variant, tail_fix = 1, True
    else:
        assert tq % KV_CHUNK == 0 and KV_CHUNK % tqs == 0
        span = tq // KV_CHUNK
        per_chunk = KV_CHUNK // tqs
        i = pl.program_id(2)
        q_off = N_META + i * tq
        d = i * span

        def variant_in(j):
            return lambda idx: 1 if units[idx][1] < j * per_chunk else 0

        def fix_in(j):
            return lambda idx: j * per_chunk <= units[idx][1] < (j + 1) * per_chunk

        phases = ([(0, d, 0, 0, False)]
                  + [(d + j, d + j + 1, variant_in(j), variant_in(j + 1) if j + 1 < span else 1, fix_in(j))
                     for j in range(span)]
                  + [(d + span, n_chunks, 1, 1, False)])
        first_variant, tail_fix = 0, False

    def scores(kc, idx, variant):
        hd, _, c = units[idx]
        col = hd * slot_w + c * LANES
        return jnp.dot(kc[:, col:col + LANES], qt_ref[variant, idx], preferred_element_type=F32)

    def make_adjust(k_off, n_keys, n_valid, fix):
        key = lax.broadcasted_iota(jnp.int32, (n_keys, tqs), 0)
        qry = lax.broadcasted_iota(jnp.int32, (n_keys, tqs), 1)
        late = {}

        def adjust(s, idx):
            hd, u, _ = units[idx]
            if _per_unit(fix, idx):
                if (hd, u) not in late:
                    ahead = ((key - qry) + (k_off - q_off - u * tqs)).astype(F32)
                    late[hd, u] = slope2[hd] * jnp.maximum(ahead, 0.0)
                s = s - late[hd, u]
            if n_valid is not None:
                s = jnp.where(key < n_valid, s, NEG_BIG)
            return s
        return adjust

    _online_softmax_sweep(k_ref, vt_ref, m_ref, acc_ref, s_ref, p_ref, a_ref, len(units), n_tok, scores, make_adjust,
                          lookahead, phases=phases, first_variant=first_variant, tail_variant=0, tail_fix=tail_fix,
                          value_head=lambda idx: units[idx][0])

    lv = lam_ref[...]
    lam = (jnp.exp(jnp.sum(lv[0:1] * lv[1:2], axis=1, keepdims=True))
           - jnp.exp(jnp.sum(lv[2:3] * lv[3:4], axis=1, keepdims=True)) + lambda_init)
    for hd in range(heads):
        cols = slice(hd * C_VDIM, (hd + 1) * C_VDIM)
        for u in range(qsub):
            rows = slice(u * tqs, (u + 1) * tqs)
            first = 2 * (hd * qsub + u)
            o = (_normalized(acc_ref[first]) - lam * _normalized(acc_ref[first + 1])).T
            ms = jnp.mean(o * o, axis=-1, keepdims=True)
            o = ((o * lax.rsqrt(ms + NORM_EPS)) * gsub_ref[...]) * (1.0 - lambda_init)
            o_ref[rows, cols] = (o * gate_ref[rows, cols]).astype(o_ref.dtype)


def _attention(body, q, q_col0, q_w, k, k_col0, k_w, vt, v_row0, gate, gate_col0, extras, extra_specs,
               out_w_total, heads, group, qsub, n_units, n_tok, name, lookahead, qt_shape, v_heads=1):
    bsz, lp, _ = q.shape
    out_w = group * HEAD_DIM

    def call(qsub_, tqs, n_qblk, qblk0, prev):
        tq = qsub_ * tqs
        in_specs = [
            pl.BlockSpec((None, tq, q_w), lambda b, h, i: (b, qblk0 + i, q_col0 + h)),
            pl.BlockSpec((None, lp, k_w), lambda b, h, i: (b, 0, k_col0 + h)),
            pl.BlockSpec((v_heads * HEAD_DIM, lp), lambda b, h, i: (v_row0 + h, b)),
            pl.BlockSpec((None, tq, out_w), lambda b, h, i: (b, qblk0 + i, gate_col0 + h)),
        ] + extra_specs
        args = [q, k, vt, gate] + extras
        aliases = {}
        if prev is not None:
            in_specs.append(pl.BlockSpec(memory_space=pl.ANY))
            args.append(prev)
            aliases = {len(args) - 1: 0}
        units = n_units * qsub_
        qt = qt_shape(units, tqs)
        return pl.pallas_call(
            body(qsub_, tqs, prev is not None),
            grid=(bsz, heads, n_qblk),
            in_specs=in_specs,
            out_specs=pl.BlockSpec((None, tq, out_w), lambda b, h, i: (b, qblk0 + i, h)),
            out_shape=jax.ShapeDtypeStruct((bsz, lp, out_w_total), BF16),
            scratch_shapes=[pltpu.VMEM((units, 1, tqs), F32),
                            pltpu.VMEM((units, HEAD_DIM + ONES_ROWS, tqs), F32),
                            pltpu.VMEM((lookahead, KV_CHUNK, tqs), F32),
                            pltpu.VMEM((KV_CHUNK, tqs), BF16), pltpu.VMEM((1, tqs), F32)]
                           + ([] if qt is None else [pltpu.VMEM(qt, BF16)]),
            input_output_aliases=aliases,
            compiler_params=_cparams(3),
            name=name + ("_meta" if prev is not None else ""),
        )(*args)

    main = call(qsub, Q_SUB, n_tok // (qsub * Q_SUB), 0, None)
    return call(1, TAIL, 1, n_tok // TAIL, main)


def _attn_a(qk, vt_plain, gates, n_tok):
    def body(qsub_, tqs, is_meta):
        return functools.partial(_attn_ab_body, group=A_GROUP, qsub=qsub_, tqs=tqs, dk=HEAD_DIM, n_tok=n_tok,
                                 pre_transpose=False, lookahead=A_LOOKAHEAD)

    return _attention(body, qk, 0, A_GROUP * HEAD_DIM, qk, A_WIDTH // HEAD_DIM, HEAD_DIM, vt_plain, 0,
                      gates, 0, [], [], A_WIDTH, A_KV_HEADS, A_GROUP, 8, A_GROUP, n_tok, "attn_gqa",
                      A_LOOKAHEAD, lambda units, tqs: None)


def _attn_b(qb, kb, vt_b, gates, n_tok):
    dk = 2 * LANES

    def body(qsub_, tqs, is_meta):
        return functools.partial(_attn_ab_body, group=1, qsub=qsub_, tqs=tqs, dk=dk, n_tok=n_tok,
                                 pre_transpose=True, lookahead=B_LOOKAHEAD)

    return _attention(body, qb, 0, dk, kb, 0, dk, vt_b, 0, gates, A_WIDTH // HEAD_DIM, [], [],
                      B_WIDTH, B_HEADS, 1, 16, 1, n_tok, "attn_mla", B_LOOKAHEAD,
                      lambda units, tqs: (units, dk, tqs))


def _attn_c(cqk, vt_plain, gates, lamvec, slopes, g_sub, lambda_init, n_tok):
    pair = 2
    dk = pair * 2 * LANES

    def body(qsub_, tqs, is_meta):
        return functools.partial(_attn_c_body, heads=pair, qsub=qsub_, tqs=tqs, n_tok=n_tok, q_is_meta=is_meta,
                                 lambda_init=lambda_init, lookahead=C_LOOKAHEAD)

    extras = [lamvec, slopes, g_sub.reshape(1, C_VDIM)]
    extra_specs = [pl.BlockSpec((4, C_QK_DIM), lambda b, h, i: (0, 0)),
                   pl.BlockSpec((pair, 1, LANES), lambda b, h, i: (h, 0, 0)),
                   pl.BlockSpec((1, C_VDIM), lambda b, h, i: (0, 0))]
    return _attention(body, cqk, 0, dk, cqk, C_HEADS // pair, dk, vt_plain, A_KV_HEADS // pair, gates,
                      (A_WIDTH + B_WIDTH) // (pair * HEAD_DIM), extras, extra_specs,
                      C_WIDTH, C_HEADS // pair, pair, 8, 2 * pair, n_tok, "attn_diff", C_LOOKAHEAD,
                      lambda units, tqs: (2, units, LANES, tqs), v_heads=pair)


def _out_proj_body(a_ref, b_ref, c_ref, wa_ref, wb_ref, wc_ref, h_ref, o_ref):
    acc = jnp.dot(a_ref[...], wa_ref[...], preferred_element_type=F32)
    acc += jnp.dot(b_ref[...], wb_ref[...], preferred_element_type=F32)
    acc += jnp.dot(c_ref[...], wc_ref[...], preferred_element_type=F32)
    o_ref[...] = h_ref[...] + acc


def _out_proj(oa, ob, oc, w, h2d):
    rows = h2d.shape[0]
    tm = _tall_row_tile(rows)
    tn = 512
    return pl.pallas_call(
        _out_proj_body,
        grid=(rows // tm, D_MODEL // tn),
        in_specs=[pl.BlockSpec((tm, A_WIDTH), lambda i, j: (i, 0)),
                  pl.BlockSpec((tm, B_WIDTH), lambda i, j: (i, 0)),
                  pl.BlockSpec((tm, C_WIDTH), lambda i, j: (i, 0)),
                  pl.BlockSpec((A_WIDTH, tn), lambda i, j: (0, j)),
                  pl.BlockSpec((B_WIDTH, tn), lambda i, j: (A_WIDTH // B_WIDTH, j)),
                  pl.BlockSpec((C_WIDTH, tn), lambda i, j: ((A_WIDTH + B_WIDTH) // C_WIDTH, j)),
                  pl.BlockSpec((tm, tn), lambda i, j: (i, j))],
        out_specs=pl.BlockSpec((tm, tn), lambda i, j: (i, j)),
        out_shape=jax.ShapeDtypeStruct((rows, D_MODEL), F32),
        compiler_params=_cparams(2),
        name="out_proj",
    )(oa, ob, oc, w, w, w, h2d)


def _rope_angles(pos_f, n_freq):
    inv = ROPE_THETA ** (-jnp.arange(n_freq, dtype=F32) / n_freq)
    return pos_f[:, None] * inv[None, :]


def _position_tables(n_tok, bsz):
    rows = n_tok // GRID_W
    z = jnp.zeros((TAIL,), F32)
    row_f = jnp.concatenate([jnp.repeat(jnp.arange(rows, dtype=F32), GRID_W), z])
    col_f = jnp.concatenate([jnp.tile(jnp.arange(GRID_W, dtype=F32), rows), z])
    pos_f = jnp.concatenate([jnp.arange(n_tok, dtype=F32) + N_META,
                             jnp.arange(N_META, dtype=F32), jnp.zeros((TAIL - N_META,), F32)])
    ang_a = jnp.concatenate([_rope_angles(row_f, HEAD_DIM // 4), _rope_angles(col_f, HEAD_DIM // 4)], axis=-1)
    cos_a, sin_a = jnp.cos(ang_a), jnp.sin(ang_a)
    cos_a = jnp.concatenate([cos_a, cos_a], axis=-1)
    sin_a = jnp.concatenate([-sin_a, sin_a], axis=-1)
    ang_b = _rope_angles(pos_f, B_ROPE // 2)
    cos_b, sin_b = jnp.cos(ang_b), jnp.sin(ang_b)
    zb = jnp.zeros_like(cos_b)
    cos_b128 = jnp.concatenate([cos_b, cos_b, zb, zb], axis=-1)
    sin_lo = jnp.concatenate([-sin_b, zb, zb, zb], axis=-1)
    sin_hi = jnp.concatenate([zb, sin_b, zb, zb], axis=-1)
    hi = jnp.floor(pos_f / LANES)
    lo = pos_f - hi * LANES
    one = jnp.ones_like(pos_f)

    def slot_lanes(cols):
        body = jnp.stack(cols, axis=-1)
        return jnp.concatenate([jnp.zeros((lp, C_QK_DIM), F32), body,
                                jnp.zeros((lp, LANES - C_QK_DIM - len(cols)), F32)], axis=-1)

    lp = n_tok + TAIL
    pieces, rest = [], jnp.float32(LOG2E)
    for _ in range(3):
        piece = rest.astype(BF16).astype(F32)
        pieces.append(piece)
        rest = rest - piece
    q_terms = slot_lanes([col for c in pieces for col in (LANES * c * one, c * one, -LANES * hi, -lo)])
    k_terms = slot_lanes([col for c in pieces for col in (hi, lo, c * one, c * one)])
    return tuple(jnp.tile(t, (bsz, 1)) for t in (cos_a, sin_a, cos_b128, sin_lo, sin_hi, q_terms, k_terms))


def _prep_weights(w_in, w_q_b, w_kv_b, w_out, g_qn, g_kn):
    bounds = [0]
    for s in IN_SPLITS:
        bounds.append(bounds[-1] + s)
    col = lambda i: slice(bounds[i], bounds[i + 1])
    aq, ak, av, ag, bq, bkv, bpe, bg, cq, ck, cv, cg = (w_in[:, col(i)] for i in range(12))
    w_aqk = jnp.concatenate([aq, ak], axis=1).astype(BF16)
    w_plain = jnp.concatenate([av, cv], axis=1).astype(BF16)
    w_gate = jnp.concatenate([ag, bg, cg], axis=1).astype(BF16)
    w_cqk = jnp.concatenate([cq, ck], axis=1).astype(BF16)
    w_blow = jnp.concatenate([bq, bkv], axis=1).astype(BF16)
    w_bpe = jnp.concatenate([bpe, jnp.zeros((D_MODEL, LANES - B_ROPE), F32)], axis=1).astype(BF16)
    g_aqk = jnp.concatenate([jnp.tile(g_qn, A_HEADS), jnp.tile(g_kn, A_KV_HEADS)]).reshape(1, -1)
    m_aqk = jnp.concatenate([jnp.full((A_WIDTH,), HEAD_DIM ** -0.5 * LOG2E, F32),
                             jnp.ones((A_KV_WIDTH,), F32)]).reshape(1, -1)
    wq = w_q_b.astype(BF16).reshape(B_Q_RANK, B_HEADS, B_NOPE + B_ROPE)
    wq = jnp.concatenate([wq, jnp.zeros((B_Q_RANK, B_HEADS, 2 * LANES - B_NOPE - B_ROPE), BF16)], axis=-1)
    wq = wq.reshape(B_Q_RANK, B_HEADS * 2 * LANES)
    wkv = w_kv_b.astype(BF16).reshape(B_KV_RANK, B_HEADS, B_NOPE + B_VDIM)
    wkv = jnp.concatenate([wkv[:, :, :B_NOPE].reshape(B_KV_RANK, -1),
                           wkv[:, :, B_NOPE:].reshape(B_KV_RANK, -1)], axis=1)
    return dict(aqk=w_aqk, plain=w_plain, gate=w_gate, cqk=w_cqk, blow=w_blow, bpe=w_bpe,
                g_aqk=g_aqk, m_aqk=m_aqk, q_up=wq, kv_up=wkv, out=w_out.astype(BF16))


def _encode(x, meta, layers, g_final, slopes, cq_cols):
    bsz, n_tok, _ = x.shape
    lp = n_tok + TAIL
    rows = bsz * lp
    tail = jnp.concatenate([meta.astype(x.dtype), jnp.zeros((TAIL - N_META, D_MODEL), x.dtype)], axis=0)
    u, h = _embed_norm(x, tail, layers[0]["g_attn"])
    cos_a, sin_a, cos_b, sin_lo, sin_hi, q_terms, k_terms = _position_tables(n_tok, bsz)
    three = lambda a: a.reshape(bsz, lp, a.shape[-1])

    for l, p in enumerate(layers):
        lambda_init = 0.8 - 0.6 * math.exp(-0.3 * l)
        if l > 0:
            u = _rmsnorm_rows(h, p["g_attn"], BF16)
        aqk = _staggered_proj(_aqk_epilogue, u, p["aqk"], [p["g_aqk"], p["m_aqk"]], [cos_a, sin_a],
                              A_WIDTH + A_KV_WIDTH, BF16, 512, 512, True, "proj_gqa_qk")
        vt_plain = _proj(_proj_transposed_body, u, p["plain"], [], lambda tm, tn: [],
                         A_KV_WIDTH + C_WIDTH, BF16, 512, transposed=True, name="proj_v")
        gates = _proj(functools.partial(_direct_proj_body, epilogue=_gate_epilogue), u, p["gate"], [],
                      lambda tm, tn: [], A_WIDTH + B_WIDTH + C_WIDTH, F32, 512, name="proj_gate")
        cqk = _proj(functools.partial(_direct_proj_body, epilogue=_cqk_epilogue), u, p["cqk"],
                    list(cq_cols) + [q_terms, k_terms],
                    lambda tm, tn: [_col_vec_spec(tm, tn)] * 3 + [_row_table_spec(tm, tn)] * 2,
                    4 * C_QK_WIDTH, BF16, 512, out_tn=1024, name="proj_diff_qk")
        blow = _proj(_proj_plain_body, u, p["blow"], [], lambda tm, tn: [],
                     B_Q_RANK + B_KV_RANK, F32, 512, name="proj_mla_low")
        kpe = _proj(_proj_bpe_body, u, p["bpe"], [cos_b, sin_lo, sin_hi],
                    lambda tm, tn: [_row_table_spec(tm, tn)] * 3, LANES, BF16, LANES, name="proj_mla_pe")
        qb = _bq_up(blow, p["g_q_a"], p["q_up"], cos_b, sin_lo, sin_hi)
        kb, vt_b = _bkv_up(blow, p["g_kv_a"], p["kv_up"], kpe)

        gates3 = three(gates)
        oa = _attn_a(three(aqk), vt_plain, gates3, n_tok)
        ob = _attn_b(three(qb), three(kb), vt_b, gates3, n_tok)
        oc = _attn_c(three(cqk), vt_plain, gates3, p["lamvec"], slopes, p["g_sub"], lambda_init, n_tok)
        h = _out_proj(oa.reshape(rows, -1), ob.reshape(rows, -1), oc.reshape(rows, -1), p["out"], h)

    return _final_norm(h.reshape(bsz, lp, D_MODEL), g_final, n_tok)


def kernel(x_prompt, x_sample, meta, g_attn, w_in, g_qn, g_kn, g_q_a, w_q_b, g_kv_a, w_kv_b,
           lam_q1, lam_k1, lam_q2, lam_k2, g_sub, w_out, g_final):
    layers = []
    for l in range(DEPTH):
        p = _prep_weights(w_in[l], w_q_b[l], w_kv_b[l], w_out[l], g_qn[l], g_kn[l])
        p.update(g_attn=g_attn[l], g_q_a=g_q_a[l], g_kv_a=g_kv_a[l], g_sub=g_sub[l],
                 lamvec=jnp.stack([lam_q1[l], lam_k1[l], lam_q2[l], lam_k2[l]]).astype(F32))
        layers.append(p)
    slope_h = 2.0 ** (-(jnp.arange(C_HEADS, dtype=F32) + 1.0) * 8.0 / C_HEADS)
    slopes = jnp.broadcast_to(slope_h[:, None, None], (C_HEADS, 1, LANES))
    zeros_w, ones_w = jnp.zeros((C_QK_WIDTH,), F32), jnp.ones((C_QK_WIDTH,), F32)
    cq_cols = [jnp.concatenate([C_QK_DIM ** -0.5 * LOG2E * ones_w, ones_w]).reshape(1, -1),
               jnp.concatenate([jnp.repeat(slope_h, 2 * C_QK_DIM), zeros_w]).reshape(1, -1),
               jnp.concatenate([zeros_w, ones_w]).reshape(1, -1)]
    y_prompt = _encode(x_prompt, meta, layers, g_final, slopes, cq_cols)
    y_sample = _encode(x_sample, meta, layers, g_final, slopes, cq_cols)
    return (y_prompt, y_sample)
```

```python
import functools
import math

import jax
import jax.numpy as jnp
from jax import lax
from jax.experimental import pallas as pl
from jax.experimental.pallas import tpu as pltpu

D_MODEL = 4096
DEPTH = 2
N_META = 16
GRID_W = 64
HEAD_DIM = 128
ROPE_THETA = 10000.0
NORM_EPS = 1e-6
A_HEADS = 16
A_KV_HEADS = 4
A_GROUP = A_HEADS // A_KV_HEADS
A_WIDTH = A_HEADS * HEAD_DIM
A_KV_WIDTH = A_KV_HEADS * HEAD_DIM
B_HEADS = 8
B_Q_RANK = 1024
B_KV_RANK = 512
B_NOPE = 128
B_ROPE = 64
B_VDIM = 128
B_WIDTH = B_HEADS * B_VDIM
C_HEADS = 8
C_QK_DIM = 64
C_VDIM = 2 * C_QK_DIM
C_QK_WIDTH = C_HEADS * 2 * C_QK_DIM
C_WIDTH = C_HEADS * C_VDIM
IN_SPLITS = (A_WIDTH, A_KV_WIDTH, A_KV_WIDTH, A_WIDTH,
             B_Q_RANK, B_KV_RANK, B_ROPE, B_WIDTH,
             C_QK_WIDTH, C_QK_WIDTH, C_WIDTH, C_WIDTH)

LANES = 128
TAIL = LANES
KV_CHUNK = 512
Q_SUB = 256
NEG_BIG = -1e30
LOG2E = 1.4426950408889634
A_LOOKAHEAD = 2
B_LOOKAHEAD = 2
C_LOOKAHEAD = 2
ONES_ROWS = 16
BF16 = jnp.bfloat16
F32 = jnp.float32
VMEM_LIMIT = 56 * 1024 * 1024

_NT = (((1,), (1,)), ((), ()))


def _cparams(n_axes):
    return pltpu.CompilerParams(dimension_semantics=("arbitrary",) * n_axes,
                                vmem_limit_bytes=VMEM_LIMIT)


def _row_tile(rows):
    for t in (512, 640, 256, 128):
        if rows % t == 0:
            return t
    raise ValueError(f"no row tile for {rows}")


def _tall_row_tile(rows):
    for t in (1536, 1664):
        if rows % t == 0:
            return t
    return _row_tile(rows)


def _rmsnorm_body(x_ref, g_ref, o_ref):
    x = x_ref[...]
    ms = jnp.mean(x * x, axis=-1, keepdims=True)
    o_ref[...] = ((x * lax.rsqrt(ms + NORM_EPS)) * g_ref[...]).astype(o_ref.dtype)


def _rmsnorm_rows(h2d, g, out_dtype):
    rows, d = h2d.shape
    tm = 256 if rows % 256 == 0 else 128
    return pl.pallas_call(
        _rmsnorm_body,
        grid=(rows // tm,),
        in_specs=[pl.BlockSpec((tm, d), lambda i: (i, 0)),
                  pl.BlockSpec((1, d), lambda i: (0, 0))],
        out_specs=pl.BlockSpec((tm, d), lambda i: (i, 0)),
        out_shape=jax.ShapeDtypeStruct((rows, d), out_dtype),
        compiler_params=_cparams(1),
        name="rmsnorm",
    )(h2d, g.reshape(1, d))


def _embed_norm_body(x_ref, tail_ref, g_ref, u_ref, h_ref, *, n_real):
    def emit(x):
        h_ref[...] = x
        ms = jnp.mean(x * x, axis=-1, keepdims=True)
        u_ref[...] = ((x * lax.rsqrt(ms + NORM_EPS)) * g_ref[...]).astype(u_ref.dtype)

    @pl.when(pl.program_id(1) < n_real)
    def _():
        emit(x_ref[...])

    @pl.when(pl.program_id(1) == n_real)
    def _():
        emit(tail_ref[...])


def _embed_norm(x, tail, g):
    bsz, n_tok, d = x.shape
    lp = n_tok + TAIL
    n_real = n_tok // TAIL
    block = pl.BlockSpec((None, TAIL, d), lambda b, i: (b, i, 0))
    u, h = pl.pallas_call(
        functools.partial(_embed_norm_body, n_real=n_real),
        grid=(bsz, n_real + 1),
        in_specs=[pl.BlockSpec((None, TAIL, d), lambda b, i: (b, jnp.minimum(i, n_real - 1), 0)),
                  pl.BlockSpec((TAIL, d), lambda b, i: (0, 0)),
                  pl.BlockSpec((1, d), lambda b, i: (0, 0))],
        out_specs=[block, block],
        out_shape=[jax.ShapeDtypeStruct((bsz, lp, d), BF16), jax.ShapeDtypeStruct((bsz, lp, d), x.dtype)],
        compiler_params=_cparams(2),
        name="embed_norm",
    )(x, tail, g.reshape(1, d))
    return u.reshape(bsz * lp, d), h.reshape(bsz * lp, d)


def _final_norm(h3d, g, n_tok):
    bsz, _, d = h3d.shape
    tm = 256
    return pl.pallas_call(
        _rmsnorm_body,
        grid=(bsz, n_tok // tm),
        in_specs=[pl.BlockSpec((None, tm, d), lambda b, i: (b, i, 0)),
                  pl.BlockSpec((1, d), lambda b, i: (0, 0))],
        out_specs=pl.BlockSpec((None, tm, d), lambda b, i: (b, i, 0)),
        out_shape=jax.ShapeDtypeStruct((bsz, n_tok, d), F32),
        compiler_params=_cparams(2),
        name="final_norm",
    )(h3d, g.reshape(1, d))


def _rope_pairs_64(y, cos, sin_signed):
    return y * cos + pltpu.roll(y, 64, 1) * sin_signed


def _rope_pairs_32(x, cos, sin_lo, sin_hi):
    return x * cos + pltpu.roll(x, 96, 1) * sin_lo + pltpu.roll(x, 32, 1) * sin_hi


def _staggered_proj_body(x_ref, w_ref, *rest, n_tiles, epilogue):
    *extra, o_ref, acc_ref = rest
    j = pl.program_id(1)

    def matmul():
        acc_ref[j % 2] = jnp.dot(x_ref[...], w_ref[...], preferred_element_type=F32)

    def finish_previous():
        epilogue(acc_ref.at[(j + 1) % 2], *extra, o_ref)

    @pl.when(j == 0)
    def _():
        matmul()

    @pl.when(jnp.logical_and(j > 0, j < n_tiles))
    def _():
        finish_previous()
        matmul()

    @pl.when(j == n_tiles)
    def _():
        finish_previous()


def _staggered_proj(epilogue, u, w, col_vecs, row_tables, n_out, out_dtype, tn, out_tn, tall, name):
    rows, k = u.shape
    n = w.shape[1]
    tm = _tall_row_tile(rows) if tall else _row_tile(rows)
    n_tiles = n // tn
    prev = lambda j: jnp.maximum(j - 1, 0)
    return pl.pallas_call(
        functools.partial(_staggered_proj_body, n_tiles=n_tiles, epilogue=epilogue),
        grid=(rows // tm, n_tiles + 1),
        in_specs=([pl.BlockSpec((tm, k), lambda i, j: (i, 0)),
                   pl.BlockSpec((k, tn), lambda i, j: (0, jnp.minimum(j, n_tiles - 1)))]
                  + [pl.BlockSpec((1, tn), lambda i, j: (0, prev(j)))] * len(col_vecs)
                  + [pl.BlockSpec((tm, LANES), lambda i, j: (i, 0))] * len(row_tables)),
        out_specs=pl.BlockSpec((tm, out_tn), lambda i, j: (i, prev(j))),
        out_shape=jax.ShapeDtypeStruct((rows, n_out), out_dtype),
        scratch_shapes=[pltpu.VMEM((2, tm, tn), F32)],
        compiler_params=_cparams(2),
        name=name,
    )(u, w, *col_vecs, *row_tables)


def _direct_proj_body(x_ref, w_ref, *rest, epilogue):
    *extra, o_ref = rest
    epilogue(jnp.dot(x_ref[...], w_ref[...], preferred_element_type=F32), *extra, o_ref)


def _aqk_epilogue(acc_ref, g_ref, mult_ref, cos_ref, sin_ref, o_ref):
    cos = cos_ref[...]
    sin = sin_ref[...]
    for h in range(o_ref.shape[1] // HEAD_DIM):
        sl = slice(h * HEAD_DIM, (h + 1) * HEAD_DIM)
        a = acc_ref[:, sl]
        ms = jnp.mean(a * a, axis=-1, keepdims=True)
        y = (a * lax.rsqrt(ms + NORM_EPS)) * g_ref[:, sl]
        o_ref[:, sl] = (_rope_pairs_64(y, cos, sin) * mult_ref[:, sl]).astype(o_ref.dtype)


def _gate_epilogue(acc_ref, o_ref):
    a = acc_ref[...]
    o_ref[...] = a * (1.0 / (1.0 + jnp.exp(-a)))


def _proj_plain_body(x_ref, w_ref, o_ref):
    o_ref[...] = jnp.dot(x_ref[...], w_ref[...], preferred_element_type=F32).astype(o_ref.dtype)


def _proj_transposed_body(x_ref, w_ref, o_ref):
    o_ref[...] = jnp.dot(x_ref[...], w_ref[...], preferred_element_type=F32).T.astype(o_ref.dtype)


def _cqk_epilogue(acc_ref, mult_ref, qcoef_ref, kcoef_ref, qtab_ref, ktab_ref, o_ref):
    acc = acc_ref[...] * mult_ref[...]
    lane = lax.broadcasted_iota(jnp.int32, (acc.shape[0], LANES), 1)
    low = lane < C_QK_DIM
    qtab = qtab_ref[...]
    ktab = ktab_ref[...]
    for t in range(acc.shape[1] // LANES):
        sl = slice(t * LANES, (t + 1) * LANES)
        a = acc[:, sl]
        pos_terms = qtab * qcoef_ref[:, sl] + ktab * kcoef_ref[:, sl]
        o_ref[:, (2 * t) * LANES:(2 * t + 1) * LANES] = jnp.where(low, a, pos_terms).astype(o_ref.dtype)
        o_ref[:, (2 * t + 1) * LANES:(2 * t + 2) * LANES] = (
            jnp.where(low, pltpu.roll(a, 64, 1), pos_terms).astype(o_ref.dtype))


def _proj_bpe_body(x_ref, w_ref, cos_ref, slo_ref, shi_ref, o_ref):
    acc = jnp.dot(x_ref[...], w_ref[...], preferred_element_type=F32)
    o_ref[...] = _rope_pairs_32(acc, cos_ref[...], slo_ref[...], shi_ref[...]).astype(o_ref.dtype)


def _proj(body, u, w, extras, extra_specs, n_out, out_dtype, tn, out_tn=None, transposed=False, tall=True,
          name="proj"):
    rows, k = u.shape
    n = w.shape[1]
    tm = _tall_row_tile(rows) if tall else _row_tile(rows)
    out_tn = tn if out_tn is None else out_tn
    if transposed:
        out_specs = pl.BlockSpec((out_tn, tm), lambda i, j: (j, i))
        out_shape = jax.ShapeDtypeStruct((n_out, rows), out_dtype)
    else:
        out_specs = pl.BlockSpec((tm, out_tn), lambda i, j: (i, j))
        out_shape = jax.ShapeDtypeStruct((rows, n_out), out_dtype)
    return pl.pallas_call(
        body,
        grid=(rows // tm, n // tn),
        in_specs=[pl.BlockSpec((tm, k), lambda i, j: (i, 0)),
                  pl.BlockSpec((k, tn), lambda i, j: (0, j))] + extra_specs(tm, tn),
        out_specs=out_specs,
        out_shape=out_shape,
        compiler_params=_cparams(2),
        name=name,
    )(u, w, *extras)


def _row_table_spec(tm, tn):
    return pl.BlockSpec((tm, LANES), lambda i, j: (i, 0))


def _col_vec_spec(tm, tn):
    return pl.BlockSpec((1, tn), lambda i, j: (0, j))


def _bq_up_body(c_ref, g_ref, w_ref, cos_ref, slo_ref, shi_ref, o_ref):
    c = c_ref[...]
    ms = jnp.mean(c * c, axis=-1, keepdims=True)
    u = ((c * lax.rsqrt(ms + NORM_EPS)) * g_ref[...]).astype(BF16)
    acc = jnp.dot(u, w_ref[...], preferred_element_type=F32) * ((B_NOPE + B_ROPE) ** -0.5 * LOG2E)
    cos, slo, shi = cos_ref[...], slo_ref[...], shi_ref[...]
    for h in range(B_HEADS):
        base = h * 2 * LANES
        o_ref[:, base:base + LANES] = acc[:, base:base + LANES].astype(o_ref.dtype)
        x = acc[:, base + LANES:base + 2 * LANES]
        o_ref[:, base + LANES:base + 2 * LANES] = _rope_pairs_32(x, cos, slo, shi).astype(o_ref.dtype)


def _bkv_up_body(c_ref, g_ref, w_ref, kpe_ref, k_ref, v_ref):
    c = c_ref[...]
    ms = jnp.mean(c * c, axis=-1, keepdims=True)
    u = ((c * lax.rsqrt(ms + NORM_EPS)) * g_ref[...]).astype(BF16)
    acc = jnp.dot(u, w_ref[...], preferred_element_type=F32)
    kpe = kpe_ref[...]
    for h in range(B_HEADS):
        base = h * 2 * LANES
        k_ref[:, base:base + LANES] = acc[:, h * LANES:(h + 1) * LANES].astype(k_ref.dtype)
        k_ref[:, base + LANES:base + 2 * LANES] = kpe
    v_ref[...] = acc[:, B_HEADS * LANES:].T.astype(v_ref.dtype)


def _bq_up(cb, g, w, cos, slo, shi):
    rows = cb.shape[0]
    tm = _row_tile(rows)
    n = w.shape[1]
    tab = pl.BlockSpec((tm, LANES), lambda i: (i, 0))
    return pl.pallas_call(
        _bq_up_body,
        grid=(rows // tm,),
        in_specs=[pl.BlockSpec((tm, B_Q_RANK), lambda i: (i, 0)),
                  pl.BlockSpec((1, B_Q_RANK), lambda i: (0, 0)),
                  pl.BlockSpec((B_Q_RANK, n), lambda i: (0, 0)),
                  tab, tab, tab],
        out_specs=pl.BlockSpec((tm, n), lambda i: (i, 0)),
        out_shape=jax.ShapeDtypeStruct((rows, n), BF16),
        compiler_params=_cparams(1),
        name="mla_q_up",
    )(cb, g.reshape(1, -1), w, cos, slo, shi)


def _bkv_up(cb, g, w, kpe):
    rows = cb.shape[0]
    tm = _row_tile(rows)
    n = w.shape[1]
    return pl.pallas_call(
        _bkv_up_body,
        grid=(rows // tm,),
        in_specs=[pl.BlockSpec((tm, B_KV_RANK), lambda i: (i, B_Q_RANK // B_KV_RANK)),
                  pl.BlockSpec((1, B_KV_RANK), lambda i: (0, 0)),
                  pl.BlockSpec((B_KV_RANK, n), lambda i: (0, 0)),
                  pl.BlockSpec((tm, LANES), lambda i: (i, 0))],
        out_specs=[pl.BlockSpec((tm, B_HEADS * 2 * LANES), lambda i: (i, 0)),
                   pl.BlockSpec((B_HEADS * B_VDIM, tm), lambda i: (0, i))],
        out_shape=[jax.ShapeDtypeStruct((rows, B_HEADS * 2 * LANES), BF16),
                   jax.ShapeDtypeStruct((B_HEADS * B_VDIM, rows), BF16)],
        compiler_params=_cparams(1),
        name="mla_kv_up",
    )(cb, g.reshape(1, -1), w, kpe)


def _online_softmax_sweep(k_ref, vt_ref, m_ref, acc_ref, s_ref, p_ref, a_ref, n_units, n_tok, scores, make_adjust,
                          lookahead, phases=None, first_variant=0, tail_variant=0, tail_fix=False,
                          value_head=lambda idx: 0):
    look = min(lookahead, n_units)
    n_chunks = n_tok // KV_CHUNK
    if phases is None:
        phases = [(0, n_chunks, 0, 0, False)]

    def probabilities(s, idx):
        m_prev = m_ref[idx]
        m_new = jnp.maximum(m_prev, jnp.max(s, axis=0, keepdims=True))
        m_ref[idx] = m_new
        return jnp.exp2(s - m_new).astype(BF16), jnp.exp2(m_prev - m_new)

    def accumulate(idx, p, alpha, vt_ones):
        acc_ref[idx] = alpha * acc_ref[idx] + jnp.dot(vt_ones, p, preferred_element_type=F32)

    def values_of(head, cols):
        return _with_ones_rows(vt_ref[head * HEAD_DIM:(head + 1) * HEAD_DIM, cols])

    def chunk(s_first, kc, cols, k_off, n_valid, kc_after, behind, variant, variant_after, fix):
        pending = list(s_first)
        vt_ones = {}
        adjust = make_adjust(k_off, kc.shape[0], n_valid, fix)
        for idx in range(n_units):
            s = pending.pop(0)
            ahead = idx + look
            if ahead < n_units:
                pending.append(scores(kc, ahead, _per_unit(variant, ahead)))
            elif kc_after is not None:
                s_ref[ahead - n_units] = scores(kc_after, ahead - n_units, _per_unit(variant_after, ahead - n_units))
            p, alpha = probabilities(adjust(s, idx), idx)
            if behind is not None:
                accumulate(*behind)
            head = value_head(idx)
            if head not in vt_ones:
                vt_ones[head] = values_of(head, cols)
            behind = (idx, p, alpha, vt_ones[head])
        return behind

    def chunk_step(c, carry, variant, variant_after, fix):
        off = pl.multiple_of(c * KV_CHUNK, KV_CHUNK)
        off_before = pl.multiple_of(jnp.maximum(c - 1, 0) * KV_CHUNK, KV_CHUNK)
        off_after = pl.multiple_of(jnp.minimum(c + 1, n_chunks - 1) * KV_CHUNK, KV_CHUNK)
        behind = (n_units - 1, p_ref[...], a_ref[...],
                  values_of(value_head(n_units - 1), pl.ds(off_before, KV_CHUNK)))
        _, p, alpha, _ = chunk([s_ref[j] for j in range(look)], k_ref[pl.ds(off, KV_CHUNK), :],
                               pl.ds(off, KV_CHUNK), N_META + off, None,
                               k_ref[pl.ds(off_after, KV_CHUNK), :], behind, variant, variant_after, fix)
        p_ref[...] = p
        a_ref[...] = alpha
        return carry

    m_ref[...] = jnp.full(m_ref.shape, NEG_BIG, F32)
    acc_ref[...] = jnp.zeros(acc_ref.shape, F32)
    p_ref[...] = jnp.zeros(p_ref.shape, BF16)
    a_ref[...] = jnp.ones(a_ref.shape, F32)
    for j in range(look):
        s_ref[j] = scores(k_ref[0:KV_CHUNK, :], j, _per_unit(first_variant, j))
    for lo, hi, variant, variant_after, fix in phases:
        lax.fori_loop(lo, hi, functools.partial(chunk_step, variant=variant, variant_after=variant_after, fix=fix), 0)
    accumulate(n_units - 1, p_ref[...], a_ref[...],
               values_of(value_head(n_units - 1), slice(n_tok - KV_CHUNK, n_tok)))
    k_tail = k_ref[n_tok:n_tok + TAIL, :]
    accumulate(*chunk([scores(k_tail, j, _per_unit(tail_variant, j)) for j in range(look)], k_tail,
                      slice(n_tok, n_tok + TAIL),
                      0, N_META, None, None, tail_variant, tail_variant, tail_fix))


def _per_unit(choice, idx):
    return choice(idx) if callable(choice) else choice


def _with_ones_rows(vt):
    return jnp.concatenate([vt, jnp.ones((ONES_ROWS, vt.shape[1]), vt.dtype)], axis=0)


def _normalized(acc):
    return acc[:HEAD_DIM] * (1.0 / acc[HEAD_DIM:HEAD_DIM + 1])


def _attn_ab_body(q_ref, k_ref, vt_ref, gate_ref, *rest, group, qsub, tqs, dk, n_tok, pre_transpose, lookahead):
    units = [(g, u) for g in range(group) for u in range(qsub)]
    if pre_transpose:
        o_ref, m_ref, acc_ref, s_ref, p_ref, a_ref, qt_ref = rest[-7:]
        for idx, (g, u) in enumerate(units):
            qt_ref[idx] = q_ref[u * tqs:(u + 1) * tqs, g * dk:(g + 1) * dk].astype(F32).T.astype(BF16)
    else:
        o_ref, m_ref, acc_ref, s_ref, p_ref, a_ref = rest[-6:]

    def scores(kc, idx, variant):
        if pre_transpose:
            return jnp.dot(kc, qt_ref[idx], preferred_element_type=F32)
        g, u = units[idx]
        return lax.dot_general(kc, q_ref[u * tqs:(u + 1) * tqs, g * dk:(g + 1) * dk], _NT,
                               preferred_element_type=F32)

    def make_adjust(k_off, n_keys, n_valid, fix):
        def adjust(s, idx):
            if n_valid is None:
                return s
            row = lax.broadcasted_iota(jnp.int32, s.shape, 0)
            return jnp.where(row < n_valid, s, NEG_BIG)
        return adjust

    _online_softmax_sweep(k_ref, vt_ref, m_ref, acc_ref, s_ref, p_ref, a_ref, len(units), n_tok, scores, make_adjust,
                          lookahead)

    for idx, (g, u) in enumerate(units):
        rows = slice(u * tqs, (u + 1) * tqs)
        cols = slice(g * HEAD_DIM, (g + 1) * HEAD_DIM)
        o = _normalized(acc_ref[idx]).T
        o_ref[rows, cols] = (o * gate_ref[rows, cols]).astype(o_ref.dtype)


def _attn_c_body(q_ref, k_ref, vt_ref, gate_ref, lam_ref, slope_ref, gsub_ref, *rest,
                 heads, qsub, tqs, n_tok, q_is_meta, lambda_init, lookahead):
    o_ref, m_ref, acc_ref, s_ref, p_ref, a_ref, q2_ref = rest[-7:]
    slot_w = 2 * LANES
    units = [(hd, u, c) for hd in range(heads) for u in range(qsub) for c in range(2)]
    lane = lax.broadcasted_iota(jnp.int32, (1, LANES), 1)
    flip = jnp.where(lane < C_QK_DIM, 1.0, -1.0)
    for idx, (hd, u, c) in enumerate(units):
        col = hd * slot_w + c * LANES
        q = q_ref[u * tqs:(u + 1) * tqs, col:col + LANES]
        q2_ref[0, idx] = q
        q2_ref[1, idx] = (q.astype(F32) * flip).astype(BF16)
    slope2 = [(2.0 * LOG2E) * slope_ref[hd, :, :1] for hd in range(heads)]
    tq = qsub * tqs
    n_chunks = n_tok // KV_CHUNK
    if q_is_meta:
        q_off = 0
        phases = [(0, n_chunks, 1, 1, False)]
        first_variant, tail_fix = 1, True
    else:
        assert tq % KV_CHUNK == 0 and KV_CHUNK % tqs == 0
        span = tq // KV_CHUNK
        per_chunk = KV_CHUNK // tqs
        i = pl.program_id(2)
        q_off = N_META + i * tq
        d = i * span

        def variant_in(j):
            return lambda idx: 1 if units[idx][1] < j * per_chunk else 0

        def fix_in(j):
            return lambda idx: j * per_chunk <= units[idx][1] < (j + 1) * per_chunk

        phases = ([(0, d, 0, 0, False)]
                  + [(d + j, d + j + 1, variant_in(j), variant_in(j + 1) if j + 1 < span else 1, fix_in(j))
                     for j in range(span)]
                  + [(d + span, n_chunks, 1, 1, False)])
        first_variant, tail_fix = 0, False

    def scores(kc, idx, variant):
        hd, _, c = units[idx]
        col = hd * slot_w + c * LANES
        return lax.dot_general(kc[:, col:col + LANES], q2_ref[variant, idx], _NT, preferred_element_type=F32)

    def make_adjust(k_off, n_keys, n_valid, fix):
        key = lax.broadcasted_iota(jnp.int32, (n_keys, tqs), 0)
        qry = lax.broadcasted_iota(jnp.int32, (n_keys, tqs), 1)
        late = {}

        def adjust(s, idx):
            hd, u, _ = units[idx]
            if _per_unit(fix, idx):
                if (hd, u) not in late:
                    ahead = ((key - qry) + (k_off - q_off - u * tqs)).astype(F32)
                    late[hd, u] = slope2[hd] * jnp.maximum(ahead, 0.0)
                s = s - late[hd, u]
            if n_valid is not None:
                s = jnp.where(key < n_valid, s, NEG_BIG)
            return s
        return adjust

    _online_softmax_sweep(k_ref, vt_ref, m_ref, acc_ref, s_ref, p_ref, a_ref, len(units), n_tok, scores, make_adjust,
                          lookahead, phases=phases, first_variant=first_variant, tail_variant=0, tail_fix=tail_fix,
                          value_head=lambda idx: units[idx][0])

    lv = lam_ref[...]
    lam = (jnp.exp(jnp.sum(lv[0:1] * lv[1:2], axis=1, keepdims=True))
           - jnp.exp(jnp.sum(lv[2:3] * lv[3:4], axis=1, keepdims=True)) + lambda_init)
    for hd in range(heads):
        cols = slice(hd * C_VDIM, (hd + 1) * C_VDIM)
        for u in range(qsub):
            rows = slice(u * tqs, (u + 1) * tqs)
            first = 2 * (hd * qsub + u)
            o = (_normalized(acc_ref[first]) - lam * _normalized(acc_ref[first + 1])).T
            ms = jnp.mean(o * o, axis=-1, keepdims=True)
            o = ((o * lax.rsqrt(ms + NORM_EPS)) * gsub_ref[...]) * (1.0 - lambda_init)
            o_ref[rows, cols] = (o * gate_ref[rows, cols]).astype(o_ref.dtype)


def _attention(body, q, q_col0, q_w, k, k_col0, k_w, vt, v_row0, gate, gate_col0, extras, extra_specs,
               out_w_total, heads, group, qsub, n_units, n_tok, name, lookahead, qt_shape, v_heads=1):
    bsz, lp, _ = q.shape
    out_w = group * HEAD_DIM

    def call(qsub_, tqs, n_qblk, qblk0, prev):
        tq = qsub_ * tqs
        in_specs = [
            pl.BlockSpec((None, tq, q_w), lambda b, h, i: (b, qblk0 + i, q_col0 + h)),
            pl.BlockSpec((None, lp, k_w), lambda b, h, i: (b, 0, k_col0 + h)),
            pl.BlockSpec((v_heads * HEAD_DIM, lp), lambda b, h, i: (v_row0 + h, b)),
            pl.BlockSpec((None, tq, out_w), lambda b, h, i: (b, qblk0 + i, gate_col0 + h)),
        ] + extra_specs
        args = [q, k, vt, gate] + extras
        aliases = {}
        if prev is not None:
            in_specs.append(pl.BlockSpec(memory_space=pl.ANY))
            args.append(prev)
            aliases = {len(args) - 1: 0}
        units = n_units * qsub_
        qt = qt_shape(units, tqs)
        return pl.pallas_call(
            body(qsub_, tqs, prev is not None),
            grid=(bsz, heads, n_qblk),
            in_specs=in_specs,
            out_specs=pl.BlockSpec((None, tq, out_w), lambda b, h, i: (b, qblk0 + i, h)),
            out_shape=jax.ShapeDtypeStruct((bsz, lp, out_w_total), BF16),
            scratch_shapes=[pltpu.VMEM((units, 1, tqs), F32),
                            pltpu.VMEM((units, HEAD_DIM + ONES_ROWS, tqs), F32),
                            pltpu.VMEM((lookahead, KV_CHUNK, tqs), F32),
                            pltpu.VMEM((KV_CHUNK, tqs), BF16), pltpu.VMEM((1, tqs), F32)]
                           + ([] if qt is None else [pltpu.VMEM(qt, BF16)]),
            input_output_aliases=aliases,
            compiler_params=_cparams(3),
            name=name + ("_meta" if prev is not None else ""),
        )(*args)

    main = call(qsub, Q_SUB, n_tok // (qsub * Q_SUB), 0, None)
    return call(1, TAIL, 1, n_tok // TAIL, main)


def _attn_a(qk, vt_plain, gates, n_tok):
    def body(qsub_, tqs, is_meta):
        return functools.partial(_attn_ab_body, group=A_GROUP, qsub=qsub_, tqs=tqs, dk=HEAD_DIM, n_tok=n_tok,
                                 pre_transpose=False, lookahead=A_LOOKAHEAD)

    return _attention(body, qk, 0, A_GROUP * HEAD_DIM, qk, A_WIDTH // HEAD_DIM, HEAD_DIM, vt_plain, 0,
                      gates, 0, [], [], A_WIDTH, A_KV_HEADS, A_GROUP, 8, A_GROUP, n_tok, "attn_gqa",
                      A_LOOKAHEAD, lambda units, tqs: None)


def _attn_b(qb, kb, vt_b, gates, n_tok):
    dk = 2 * LANES

    def body(qsub_, tqs, is_meta):
        return functools.partial(_attn_ab_body, group=1, qsub=qsub_, tqs=tqs, dk=dk, n_tok=n_tok,
                                 pre_transpose=True, lookahead=B_LOOKAHEAD)

    return _attention(body, qb, 0, dk, kb, 0, dk, vt_b, 0, gates, A_WIDTH // HEAD_DIM, [], [],
                      B_WIDTH, B_HEADS, 1, 16, 1, n_tok, "attn_mla", B_LOOKAHEAD,
                      lambda units, tqs: (units, dk, tqs))


def _attn_c(cqk, vt_plain, gates, lamvec, slopes, g_sub, lambda_init, n_tok):
    pair = 2
    dk = pair * 2 * LANES

    def body(qsub_, tqs, is_meta):
        return functools.partial(_attn_c_body, heads=pair, qsub=qsub_, tqs=tqs, n_tok=n_tok, q_is_meta=is_meta,
                                 lambda_init=lambda_init, lookahead=C_LOOKAHEAD)

    extras = [lamvec, slopes, g_sub.reshape(1, C_VDIM)]
    extra_specs = [pl.BlockSpec((4, C_QK_DIM), lambda b, h, i: (0, 0)),
                   pl.BlockSpec((pair, 1, LANES), lambda b, h, i: (h, 0, 0)),
                   pl.BlockSpec((1, C_VDIM), lambda b, h, i: (0, 0))]
    return _attention(body, cqk, 0, dk, cqk, C_HEADS // pair, dk, vt_plain, A_KV_HEADS // pair, gates,
                      (A_WIDTH + B_WIDTH) // (pair * HEAD_DIM), extras, extra_specs,
                      C_WIDTH, C_HEADS // pair, pair, 8, 2 * pair, n_tok, "attn_diff", C_LOOKAHEAD,
                      lambda units, tqs: (2, units, tqs, LANES), v_heads=pair)


def _out_proj_body(a_ref, b_ref, c_ref, wa_ref, wb_ref, wc_ref, h_ref, o_ref):
    acc = jnp.dot(a_ref[...], wa_ref[...], preferred_element_type=F32)
    acc += jnp.dot(b_ref[...], wb_ref[...], preferred_element_type=F32)
    acc += jnp.dot(c_ref[...], wc_ref[...], preferred_element_type=F32)
    o_ref[...] = h_ref[...] + acc


def _out_proj(oa, ob, oc, w, h2d):
    rows = h2d.shape[0]
    tm = _tall_row_tile(rows)
    tn = 512
    return pl.pallas_call(
        _out_proj_body,
        grid=(rows // tm, D_MODEL // tn),
        in_specs=[pl.BlockSpec((tm, A_WIDTH), lambda i, j: (i, 0)),
                  pl.BlockSpec((tm, B_WIDTH), lambda i, j: (i, 0)),
                  pl.BlockSpec((tm, C_WIDTH), lambda i, j: (i, 0)),
                  pl.BlockSpec((A_WIDTH, tn), lambda i, j: (0, j)),
                  pl.BlockSpec((B_WIDTH, tn), lambda i, j: (A_WIDTH // B_WIDTH, j)),
                  pl.BlockSpec((C_WIDTH, tn), lambda i, j: ((A_WIDTH + B_WIDTH) // C_WIDTH, j)),
                  pl.BlockSpec((tm, tn), lambda i, j: (i, j))],
        out_specs=pl.BlockSpec((tm, tn), lambda i, j: (i, j)),
        out_shape=jax.ShapeDtypeStruct((rows, D_MODEL), F32),
        compiler_params=_cparams(2),
        name="out_proj",
    )(oa, ob, oc, w, w, w, h2d)


def _rope_angles(pos_f, n_freq):
    inv = ROPE_THETA ** (-jnp.arange(n_freq, dtype=F32) / n_freq)
    return pos_f[:, None] * inv[None, :]


def _position_tables(n_tok, bsz):
    rows = n_tok // GRID_W
    z = jnp.zeros((TAIL,), F32)
    row_f = jnp.concatenate([jnp.repeat(jnp.arange(rows, dtype=F32), GRID_W), z])
    col_f = jnp.concatenate([jnp.tile(jnp.arange(GRID_W, dtype=F32), rows), z])
    pos_f = jnp.concatenate([jnp.arange(n_tok, dtype=F32) + N_META,
                             jnp.arange(N_META, dtype=F32), jnp.zeros((TAIL - N_META,), F32)])
    ang_a = jnp.concatenate([_rope_angles(row_f, HEAD_DIM // 4), _rope_angles(col_f, HEAD_DIM // 4)], axis=-1)
    cos_a, sin_a = jnp.cos(ang_a), jnp.sin(ang_a)
    cos_a = jnp.concatenate([cos_a, cos_a], axis=-1)
    sin_a = jnp.concatenate([-sin_a, sin_a], axis=-1)
    ang_b = _rope_angles(pos_f, B_ROPE // 2)
    cos_b, sin_b = jnp.cos(ang_b), jnp.sin(ang_b)
    zb = jnp.zeros_like(cos_b)
    cos_b128 = jnp.concatenate([cos_b, cos_b, zb, zb], axis=-1)
    sin_lo = jnp.concatenate([-sin_b, zb, zb, zb], axis=-1)
    sin_hi = jnp.concatenate([zb, sin_b, zb, zb], axis=-1)
    hi = jnp.floor(pos_f / LANES)
    lo = pos_f - hi * LANES
    one = jnp.ones_like(pos_f)

    def slot_lanes(cols):
        body = jnp.stack(cols, axis=-1)
        return jnp.concatenate([jnp.zeros((lp, C_QK_DIM), F32), body,
                                jnp.zeros((lp, LANES - C_QK_DIM - len(cols)), F32)], axis=-1)

    lp = n_tok + TAIL
    pieces, rest = [], jnp.float32(LOG2E)
    for _ in range(3):
        piece = rest.astype(BF16).astype(F32)
        pieces.append(piece)
        rest = rest - piece
    q_terms = slot_lanes([col for c in pieces for col in (LANES * c * one, c * one, -LANES * hi, -lo)])
    k_terms = slot_lanes([col for c in pieces for col in (hi, lo, c * one, c * one)])
    return tuple(jnp.tile(t, (bsz, 1)) for t in (cos_a, sin_a, cos_b128, sin_lo, sin_hi, q_terms, k_terms))


def _prep_weights(w_in, w_q_b, w_kv_b, w_out, g_qn, g_kn):
    bounds = [0]
    for s in IN_SPLITS:
        bounds.append(bounds[-1] + s)
    col = lambda i: slice(bounds[i], bounds[i + 1])
    aq, ak, av, ag, bq, bkv, bpe, bg, cq, ck, cv, cg = (w_in[:, col(i)] for i in range(12))
    w_aqk = jnp.concatenate([aq, ak], axis=1).astype(BF16)
    w_plain = jnp.concatenate([av, cv], axis=1).astype(BF16)
    w_gate = jnp.concatenate([ag, bg, cg], axis=1).astype(BF16)
    w_cqk = jnp.concatenate([cq, ck], axis=1).astype(BF16)
    w_blow = jnp.concatenate([bq, bkv], axis=1).astype(BF16)
    w_bpe = jnp.concatenate([bpe, jnp.zeros((D_MODEL, LANES - B_ROPE), F32)], axis=1).astype(BF16)
    g_aqk = jnp.concatenate([jnp.tile(g_qn, A_HEADS), jnp.tile(g_kn, A_KV_HEADS)]).reshape(1, -1)
    m_aqk = jnp.concatenate([jnp.full((A_WIDTH,), HEAD_DIM ** -0.5 * LOG2E, F32),
                             jnp.ones((A_KV_WIDTH,), F32)]).reshape(1, -1)
    wq = w_q_b.astype(BF16).reshape(B_Q_RANK, B_HEADS, B_NOPE + B_ROPE)
    wq = jnp.concatenate([wq, jnp.zeros((B_Q_RANK, B_HEADS, 2 * LANES - B_NOPE - B_ROPE), BF16)], axis=-1)
    wq = wq.reshape(B_Q_RANK, B_HEADS * 2 * LANES)
    wkv = w_kv_b.astype(BF16).reshape(B_KV_RANK, B_HEADS, B_NOPE + B_VDIM)
    wkv = jnp.concatenate([wkv[:, :, :B_NOPE].reshape(B_KV_RANK, -1),
                           wkv[:, :, B_NOPE:].reshape(B_KV_RANK, -1)], axis=1)
    return dict(aqk=w_aqk, plain=w_plain, gate=w_gate, cqk=w_cqk, blow=w_blow, bpe=w_bpe,
                g_aqk=g_aqk, m_aqk=m_aqk, q_up=wq, kv_up=wkv, out=w_out.astype(BF16))


def _encode(x, meta, layers, g_final, slopes, cq_cols):
    bsz, n_tok, _ = x.shape
    lp = n_tok + TAIL
    rows = bsz * lp
    tail = jnp.concatenate([meta.astype(x.dtype), jnp.zeros((TAIL - N_META, D_MODEL), x.dtype)], axis=0)
    u, h = _embed_norm(x, tail, layers[0]["g_attn"])
    cos_a, sin_a, cos_b, sin_lo, sin_hi, q_terms, k_terms = _position_tables(n_tok, bsz)
    three = lambda a: a.reshape(bsz, lp, a.shape[-1])

    for l, p in enumerate(layers):
        lambda_init = 0.8 - 0.6 * math.exp(-0.3 * l)
        if l > 0:
            u = _rmsnorm_rows(h, p["g_attn"], BF16)
        aqk = _staggered_proj(_aqk_epilogue, u, p["aqk"], [p["g_aqk"], p["m_aqk"]], [cos_a, sin_a],
                              A_WIDTH + A_KV_WIDTH, BF16, 512, 512, True, "proj_gqa_qk")
        vt_plain = _proj(_proj_transposed_body, u, p["plain"], [], lambda tm, tn: [],
                         A_KV_WIDTH + C_WIDTH, BF16, 512, transposed=True, name="proj_v")
        gates = _proj(functools.partial(_direct_proj_body, epilogue=_gate_epilogue), u, p["gate"], [],
                      lambda tm, tn: [], A_WIDTH + B_WIDTH + C_WIDTH, F32, 512, name="proj_gate")
        cqk = _proj(functools.partial(_direct_proj_body, epilogue=_cqk_epilogue), u, p["cqk"],
                    list(cq_cols) + [q_terms, k_terms],
                    lambda tm, tn: [_col_vec_spec(tm, tn)] * 3 + [_row_table_spec(tm, tn)] * 2,
                    4 * C_QK_WIDTH, BF16, 512, out_tn=1024, name="proj_diff_qk")
        blow = _proj(_proj_plain_body, u, p["blow"], [], lambda tm, tn: [],
                     B_Q_RANK + B_KV_RANK, F32, 512, name="proj_mla_low")
        kpe = _proj(_proj_bpe_body, u, p["bpe"], [cos_b, sin_lo, sin_hi],
                    lambda tm, tn: [_row_table_spec(tm, tn)] * 3, LANES, BF16, LANES, name="proj_mla_pe")
        qb = _bq_up(blow, p["g_q_a"], p["q_up"], cos_b, sin_lo, sin_hi)
        kb, vt_b = _bkv_up(blow, p["g_kv_a"], p["kv_up"], kpe)

        gates3 = three(gates)
        oa = _attn_a(three(aqk), vt_plain, gates3, n_tok)
        ob = _attn_b(three(qb), three(kb), vt_b, gates3, n_tok)
        oc = _attn_c(three(cqk), vt_plain, gates3, p["lamvec"], slopes, p["g_sub"], lambda_init, n_tok)
        h = _out_proj(oa.reshape(rows, -1), ob.reshape(rows, -1), oc.reshape(rows, -1), p["out"], h)

    return _final_norm(h.reshape(bsz, lp, D_MODEL), g_final, n_tok)


def kernel(x_prompt, x_sample, meta, g_attn, w_in, g_qn, g_kn, g_q_a, w_q_b, g_kv_a, w_kv_b,
           lam_q1, lam_k1, lam_q2, lam_k2, g_sub, w_out, g_final):
    layers = []
    for l in range(DEPTH):
        p = _prep_weights(w_in[l], w_q_b[l], w_kv_b[l], w_out[l], g_qn[l], g_kn[l])
        p.update(g_attn=g_attn[l], g_q_a=g_q_a[l], g_kv_a=g_kv_a[l], g_sub=g_sub[l],
                 lamvec=jnp.stack([lam_q1[l], lam_k1[l], lam_q2[l], lam_k2[l]]).astype(F32))
        layers.append(p)
    slope_h = 2.0 ** (-(jnp.arange(C_HEADS, dtype=F32) + 1.0) * 8.0 / C_HEADS)
    slopes = jnp.broadcast_to(slope_h[:, None, None], (C_HEADS, 1, LANES))
    zeros_w, ones_w = jnp.zeros((C_QK_WIDTH,), F32), jnp.ones((C_QK_WIDTH,), F32)
    cq_cols = [jnp.concatenate([C_QK_DIM ** -0.5 * LOG2E * ones_w, ones_w]).reshape(1, -1),
               jnp.concatenate([jnp.repeat(slope_h, 2 * C_QK_DIM), zeros_w]).reshape(1, -1),
               jnp.concatenate([zeros_w, ones_w]).reshape(1, -1)]
    y_prompt = _encode(x_prompt, meta, layers, g_final, slopes, cq_cols)
    y_sample = _encode(x_sample, meta, layers, g_final, slopes, cq_cols)
    return (y_prompt, y_sample)
```

```python
import functools
import math

import jax
import jax.numpy as jnp
from jax import lax
from jax.experimental import pallas as pl
from jax.experimental.pallas import tpu as pltpu

D_MODEL = 4096
DEPTH = 2
N_META = 16
GRID_W = 64
HEAD_DIM = 128
ROPE_THETA = 10000.0
NORM_EPS = 1e-6
A_HEADS = 16
A_KV_HEADS = 4
A_GROUP = A_HEADS // A_KV_HEADS
A_WIDTH = A_HEADS * HEAD_DIM
A_KV_WIDTH = A_KV_HEADS * HEAD_DIM
B_HEADS = 8
B_Q_RANK = 1024
B_KV_RANK = 512
B_NOPE = 128
B_ROPE = 64
B_VDIM = 128
B_WIDTH = B_HEADS * B_VDIM
C_HEADS = 8
C_QK_DIM = 64
C_VDIM = 2 * C_QK_DIM
C_QK_WIDTH = C_HEADS * 2 * C_QK_DIM
C_WIDTH = C_HEADS * C_VDIM
IN_SPLITS = (A_WIDTH, A_KV_WIDTH, A_KV_WIDTH, A_WIDTH,
             B_Q_RANK, B_KV_RANK, B_ROPE, B_WIDTH,
             C_QK_WIDTH, C_QK_WIDTH, C_WIDTH, C_WIDTH)

LANES = 128
TAIL = LANES
KV_CHUNK = 512
Q_SUB = 256
NEG_BIG = -1e30
LOG2E = 1.4426950408889634
A_LOOKAHEAD = 2
B_LOOKAHEAD = 2
C_LOOKAHEAD = 2
ONES_ROWS = 16
BF16 = jnp.bfloat16
F32 = jnp.float32
VMEM_LIMIT = 56 * 1024 * 1024

_NT = (((1,), (1,)), ((), ()))


def _cparams(n_axes):
    return pltpu.CompilerParams(dimension_semantics=("arbitrary",) * n_axes,
                                vmem_limit_bytes=VMEM_LIMIT)


def _row_tile(rows):
    for t in (512, 640, 256, 128):
        if rows % t == 0:
            return t
    raise ValueError(f"no row tile for {rows}")


def _tall_row_tile(rows):
    for t in (1536, 1664):
        if rows % t == 0:
            return t
    return _row_tile(rows)


def _rmsnorm_body(x_ref, g_ref, o_ref):
    x = x_ref[...]
    ms = jnp.mean(x * x, axis=-1, keepdims=True)
    o_ref[...] = ((x * lax.rsqrt(ms + NORM_EPS)) * g_ref[...]).astype(o_ref.dtype)


def _rmsnorm_rows(h2d, g, out_dtype):
    rows, d = h2d.shape
    tm = 256 if rows % 256 == 0 else 128
    return pl.pallas_call(
        _rmsnorm_body,
        grid=(rows // tm,),
        in_specs=[pl.BlockSpec((tm, d), lambda i: (i, 0)),
                  pl.BlockSpec((1, d), lambda i: (0, 0))],
        out_specs=pl.BlockSpec((tm, d), lambda i: (i, 0)),
        out_shape=jax.ShapeDtypeStruct((rows, d), out_dtype),
        compiler_params=_cparams(1),
        name="rmsnorm",
    )(h2d, g.reshape(1, d))


def _embed_norm_body(x_ref, tail_ref, g_ref, u_ref, h_ref, *, n_real):
    def emit(x):
        h_ref[...] = x
        ms = jnp.mean(x * x, axis=-1, keepdims=True)
        u_ref[...] = ((x * lax.rsqrt(ms + NORM_EPS)) * g_ref[...]).astype(u_ref.dtype)

    @pl.when(pl.program_id(1) < n_real)
    def _():
        emit(x_ref[...])

    @pl.when(pl.program_id(1) == n_real)
    def _():
        emit(tail_ref[...])


def _embed_norm(x, tail, g):
    bsz, n_tok, d = x.shape
    lp = n_tok + TAIL
    n_real = n_tok // TAIL
    block = pl.BlockSpec((None, TAIL, d), lambda b, i: (b, i, 0))
    u, h = pl.pallas_call(
        functools.partial(_embed_norm_body, n_real=n_real),
        grid=(bsz, n_real + 1),
        in_specs=[pl.BlockSpec((None, TAIL, d), lambda b, i: (b, jnp.minimum(i, n_real - 1), 0)),
                  pl.BlockSpec((TAIL, d), lambda b, i: (0, 0)),
                  pl.BlockSpec((1, d), lambda b, i: (0, 0))],
        out_specs=[block, block],
        out_shape=[jax.ShapeDtypeStruct((bsz, lp, d), BF16), jax.ShapeDtypeStruct((bsz, lp, d), x.dtype)],
        compiler_params=_cparams(2),
        name="embed_norm",
    )(x, tail, g.reshape(1, d))
    return u.reshape(bsz * lp, d), h.reshape(bsz * lp, d)


def _final_norm(h3d, g, n_tok):
    bsz, _, d = h3d.shape
    tm = 256
    return pl.pallas_call(
        _rmsnorm_body,
        grid=(bsz, n_tok // tm),
        in_specs=[pl.BlockSpec((None, tm, d), lambda b, i: (b, i, 0)),
                  pl.BlockSpec((1, d), lambda b, i: (0, 0))],
        out_specs=pl.BlockSpec((None, tm, d), lambda b, i: (b, i, 0)),
        out_shape=jax.ShapeDtypeStruct((bsz, n_tok, d), F32),
        compiler_params=_cparams(2),
        name="final_norm",
    )(h3d, g.reshape(1, d))


def _rope_pairs_64(y, cos, sin_signed):
    return y * cos + pltpu.roll(y, 64, 1) * sin_signed


def _rope_pairs_32(x, cos, sin_lo, sin_hi):
    return x * cos + pltpu.roll(x, 96, 1) * sin_lo + pltpu.roll(x, 32, 1) * sin_hi


def _staggered_proj_body(x_ref, w_ref, *rest, n_tiles, epilogue):
    *extra, o_ref, acc_ref = rest
    j = pl.program_id(1)

    def matmul():
        acc_ref[j % 2] = jnp.dot(x_ref[...], w_ref[...], preferred_element_type=F32)

    def finish_previous():
        epilogue(acc_ref.at[(j + 1) % 2], *extra, o_ref)

    @pl.when(j == 0)
    def _():
        matmul()

    @pl.when(jnp.logical_and(j > 0, j < n_tiles))
    def _():
        finish_previous()
        matmul()

    @pl.when(j == n_tiles)
    def _():
        finish_previous()


def _staggered_proj(epilogue, u, w, col_vecs, row_tables, n_out, out_dtype, tn, out_tn, tall, name):
    rows, k = u.shape
    n = w.shape[1]
    tm = _tall_row_tile(rows) if tall else _row_tile(rows)
    n_tiles = n // tn
    prev = lambda j: jnp.maximum(j - 1, 0)
    return pl.pallas_call(
        functools.partial(_staggered_proj_body, n_tiles=n_tiles, epilogue=epilogue),
        grid=(rows // tm, n_tiles + 1),
        in_specs=([pl.BlockSpec((tm, k), lambda i, j: (i, 0)),
                   pl.BlockSpec((k, tn), lambda i, j: (0, jnp.minimum(j, n_tiles - 1)))]
                  + [pl.BlockSpec((1, tn), lambda i, j: (0, prev(j)))] * len(col_vecs)
                  + [pl.BlockSpec((tm, LANES), lambda i, j: (i, 0))] * len(row_tables)),
        out_specs=pl.BlockSpec((tm, out_tn), lambda i, j: (i, prev(j))),
        out_shape=jax.ShapeDtypeStruct((rows, n_out), out_dtype),
        scratch_shapes=[pltpu.VMEM((2, tm, tn), F32)],
        compiler_params=_cparams(2),
        name=name,
    )(u, w, *col_vecs, *row_tables)


def _direct_proj_body(x_ref, w_ref, *rest, epilogue):
    *extra, o_ref = rest
    epilogue(jnp.dot(x_ref[...], w_ref[...], preferred_element_type=F32), *extra, o_ref)


def _aqk_epilogue(acc_ref, g_ref, mult_ref, cos_ref, sin_ref, o_ref):
    cos = cos_ref[...]
    sin = sin_ref[...]
    for h in range(o_ref.shape[1] // HEAD_DIM):
        sl = slice(h * HEAD_DIM, (h + 1) * HEAD_DIM)
        a = acc_ref[:, sl]
        ms = jnp.mean(a * a, axis=-1, keepdims=True)
        y = (a * lax.rsqrt(ms + NORM_EPS)) * g_ref[:, sl]
        o_ref[:, sl] = (_rope_pairs_64(y, cos, sin) * mult_ref[:, sl]).astype(o_ref.dtype)


def _gate_epilogue(acc_ref, o_ref):
    a = acc_ref[...]
    o_ref[...] = a * (1.0 / (1.0 + jnp.exp(-a)))


def _proj_plain_body(x_ref, w_ref, o_ref):
    o_ref[...] = jnp.dot(x_ref[...], w_ref[...], preferred_element_type=F32).astype(o_ref.dtype)


def _proj_transposed_body(x_ref, w_ref, o_ref):
    o_ref[...] = jnp.dot(x_ref[...], w_ref[...], preferred_element_type=F32).T.astype(o_ref.dtype)


def _cqk_epilogue(acc_ref, mult_ref, qcoef_ref, kcoef_ref, qtab_ref, ktab_ref, o_ref):
    acc = acc_ref[...] * mult_ref[...]
    lane = lax.broadcasted_iota(jnp.int32, (acc.shape[0], LANES), 1)
    low = lane < C_QK_DIM
    qtab = qtab_ref[...]
    ktab = ktab_ref[...]
    for t in range(acc.shape[1] // LANES):
        sl = slice(t * LANES, (t + 1) * LANES)
        a = acc[:, sl]
        pos_terms = qtab * qcoef_ref[:, sl] + ktab * kcoef_ref[:, sl]
        o_ref[:, (2 * t) * LANES:(2 * t + 1) * LANES] = jnp.where(low, a, pos_terms).astype(o_ref.dtype)
        o_ref[:, (2 * t + 1) * LANES:(2 * t + 2) * LANES] = (
            jnp.where(low, pltpu.roll(a, 64, 1), pos_terms).astype(o_ref.dtype))


def _proj_bpe_body(x_ref, w_ref, cos_ref, slo_ref, shi_ref, o_ref):
    acc = jnp.dot(x_ref[...], w_ref[...], preferred_element_type=F32)
    o_ref[...] = _rope_pairs_32(acc, cos_ref[...], slo_ref[...], shi_ref[...]).astype(o_ref.dtype)


def _proj(body, u, w, extras, extra_specs, n_out, out_dtype, tn, out_tn=None, transposed=False, tall=True,
          name="proj"):
    rows, k = u.shape
    n = w.shape[1]
    tm = _tall_row_tile(rows) if tall else _row_tile(rows)
    out_tn = tn if out_tn is None else out_tn
    if transposed:
        out_specs = pl.BlockSpec((out_tn, tm), lambda i, j: (j, i))
        out_shape = jax.ShapeDtypeStruct((n_out, rows), out_dtype)
    else:
        out_specs = pl.BlockSpec((tm, out_tn), lambda i, j: (i, j))
        out_shape = jax.ShapeDtypeStruct((rows, n_out), out_dtype)
    return pl.pallas_call(
        body,
        grid=(rows // tm, n // tn),
        in_specs=[pl.BlockSpec((tm, k), lambda i, j: (i, 0)),
                  pl.BlockSpec((k, tn), lambda i, j: (0, j))] + extra_specs(tm, tn),
        out_specs=out_specs,
        out_shape=out_shape,
        compiler_params=_cparams(2),
        name=name,
    )(u, w, *extras)


def _row_table_spec(tm, tn):
    return pl.BlockSpec((tm, LANES), lambda i, j: (i, 0))


def _col_vec_spec(tm, tn):
    return pl.BlockSpec((1, tn), lambda i, j: (0, j))


def _bq_up_body(c_ref, g_ref, w_ref, cos_ref, slo_ref, shi_ref, o_ref):
    c = c_ref[...]
    ms = jnp.mean(c * c, axis=-1, keepdims=True)
    u = ((c * lax.rsqrt(ms + NORM_EPS)) * g_ref[...]).astype(BF16)
    acc = jnp.dot(u, w_ref[...], preferred_element_type=F32) * ((B_NOPE + B_ROPE) ** -0.5 * LOG2E)
    cos, slo, shi = cos_ref[...], slo_ref[...], shi_ref[...]
    for h in range(B_HEADS):
        base = h * 2 * LANES
        o_ref[:, base:base + LANES] = acc[:, base:base + LANES].astype(o_ref.dtype)
        x = acc[:, base + LANES:base + 2 * LANES]
        o_ref[:, base + LANES:base + 2 * LANES] = _rope_pairs_32(x, cos, slo, shi).astype(o_ref.dtype)


def _bkv_up_body(c_ref, g_ref, w_ref, kpe_ref, k_ref, v_ref):
    c = c_ref[...]
    ms = jnp.mean(c * c, axis=-1, keepdims=True)
    u = ((c * lax.rsqrt(ms + NORM_EPS)) * g_ref[...]).astype(BF16)
    acc = jnp.dot(u, w_ref[...], preferred_element_type=F32)
    kpe = kpe_ref[...]
    for h in range(B_HEADS):
        base = h * 2 * LANES
        k_ref[:, base:base + LANES] = acc[:, h * LANES:(h + 1) * LANES].astype(k_ref.dtype)
        k_ref[:, base + LANES:base + 2 * LANES] = kpe
    v_ref[...] = acc[:, B_HEADS * LANES:].T.astype(v_ref.dtype)


def _bq_up(cb, g, w, cos, slo, shi):
    rows = cb.shape[0]
    tm = _row_tile(rows)
    n = w.shape[1]
    tab = pl.BlockSpec((tm, LANES), lambda i: (i, 0))
    return pl.pallas_call(
        _bq_up_body,
        grid=(rows // tm,),
        in_specs=[pl.BlockSpec((tm, B_Q_RANK), lambda i: (i, 0)),
                  pl.BlockSpec((1, B_Q_RANK), lambda i: (0, 0)),
                  pl.BlockSpec((B_Q_RANK, n), lambda i: (0, 0)),
                  tab, tab, tab],
        out_specs=pl.BlockSpec((tm, n), lambda i: (i, 0)),
        out_shape=jax.ShapeDtypeStruct((rows, n), BF16),
        compiler_params=_cparams(1),
        name="mla_q_up",
    )(cb, g.reshape(1, -1), w, cos, slo, shi)


def _bkv_up(cb, g, w, kpe):
    rows = cb.shape[0]
    tm = _row_tile(rows)
    n = w.shape[1]
    return pl.pallas_call(
        _bkv_up_body,
        grid=(rows // tm,),
        in_specs=[pl.BlockSpec((tm, B_KV_RANK), lambda i: (i, B_Q_RANK // B_KV_RANK)),
                  pl.BlockSpec((1, B_KV_RANK), lambda i: (0, 0)),
                  pl.BlockSpec((B_KV_RANK, n), lambda i: (0, 0)),
                  pl.BlockSpec((tm, LANES), lambda i: (i, 0))],
        out_specs=[pl.BlockSpec((tm, B_HEADS * 2 * LANES), lambda i: (i, 0)),
                   pl.BlockSpec((B_HEADS * B_VDIM, tm), lambda i: (0, i))],
        out_shape=[jax.ShapeDtypeStruct((rows, B_HEADS * 2 * LANES), BF16),
                   jax.ShapeDtypeStruct((B_HEADS * B_VDIM, rows), BF16)],
        compiler_params=_cparams(1),
        name="mla_kv_up",
    )(cb, g.reshape(1, -1), w, kpe)


def _online_softmax_sweep(k_ref, vt_ref, m_ref, acc_ref, s_ref, p_ref, a_ref, n_units, n_tok, scores, make_adjust,
                          lookahead, phases=None, first_variant=0, tail_variant=0, tail_fix=False,
                          value_head=lambda idx: 0):
    look = min(lookahead, n_units)
    n_chunks = n_tok // KV_CHUNK
    if phases is None:
        phases = [(0, n_chunks, 0, 0, False)]

    def probabilities(s, idx):
        m_prev = m_ref[idx]
        m_new = jnp.maximum(m_prev, jnp.max(s, axis=0, keepdims=True))
        m_ref[idx] = m_new
        return jnp.exp2(s - m_new).astype(BF16), jnp.exp2(m_prev - m_new)

    def accumulate(idx, p, alpha, vt_ones):
        acc_ref[idx] = alpha * acc_ref[idx] + jnp.dot(vt_ones, p, preferred_element_type=F32)

    def values_of(head, cols):
        return _with_ones_rows(vt_ref[head * HEAD_DIM:(head + 1) * HEAD_DIM, cols])

    def chunk(s_first, kc, cols, k_off, kc_after, behind, variant, variant_after, fix):
        pending = list(s_first)
        vt_ones = {}
        adjust = (lambda s, idx: s) if make_adjust is None else make_adjust(k_off, kc.shape[0], fix)
        for idx in range(n_units):
            s = pending.pop(0)
            ahead = idx + look
            if ahead < n_units:
                pending.append(scores(kc, ahead, _per_unit(variant, ahead)))
            elif kc_after is not None:
                s_ref[ahead - n_units] = scores(kc_after, ahead - n_units, _per_unit(variant_after, ahead - n_units))
            p, alpha = probabilities(adjust(s, idx), idx)
            if behind is not None:
                accumulate(*behind)
            head = value_head(idx)
            if head not in vt_ones:
                vt_ones[head] = values_of(head, cols)
            missing = vt_ones[head].shape[1] - p.shape[0]
            if missing:
                p = jnp.concatenate([p, jnp.zeros((missing, p.shape[1]), p.dtype)], axis=0)
            behind = (idx, p, alpha, vt_ones[head])
        return behind

    def chunk_step(c, carry, variant, variant_after, fix):
        off = pl.multiple_of(c * KV_CHUNK, KV_CHUNK)
        off_before = pl.multiple_of(jnp.maximum(c - 1, 0) * KV_CHUNK, KV_CHUNK)
        off_after = pl.multiple_of(jnp.minimum(c + 1, n_chunks - 1) * KV_CHUNK, KV_CHUNK)
        behind = (n_units - 1, p_ref[...], a_ref[...],
                  values_of(value_head(n_units - 1), pl.ds(off_before, KV_CHUNK)))
        _, p, alpha, _ = chunk([s_ref[j] for j in range(look)], k_ref[pl.ds(off, KV_CHUNK), :],
                               pl.ds(off, KV_CHUNK), N_META + off,
                               k_ref[pl.ds(off_after, KV_CHUNK), :], behind, variant, variant_after, fix)
        p_ref[...] = p
        a_ref[...] = alpha
        return carry

    m_ref[...] = jnp.full(m_ref.shape, NEG_BIG, F32)
    acc_ref[...] = jnp.zeros(acc_ref.shape, F32)
    p_ref[...] = jnp.zeros(p_ref.shape, BF16)
    a_ref[...] = jnp.ones(a_ref.shape, F32)
    for j in range(look):
        s_ref[j] = scores(k_ref[0:KV_CHUNK, :], j, _per_unit(first_variant, j))
    for lo, hi, variant, variant_after, fix in phases:
        lax.fori_loop(lo, hi, functools.partial(chunk_step, variant=variant, variant_after=variant_after, fix=fix), 0)
    accumulate(n_units - 1, p_ref[...], a_ref[...],
               values_of(value_head(n_units - 1), slice(n_tok - KV_CHUNK, n_tok)))
    k_tail = k_ref[n_tok:n_tok + N_META, :]
    accumulate(*chunk([scores(k_tail, j, _per_unit(tail_variant, j)) for j in range(look)], k_tail,
                      slice(n_tok, n_tok + TAIL),
                      0, None, None, tail_variant, tail_variant, tail_fix))


def _per_unit(choice, idx):
    return choice(idx) if callable(choice) else choice


def _with_ones_rows(vt):
    return jnp.concatenate([vt, jnp.ones((ONES_ROWS, vt.shape[1]), vt.dtype)], axis=0)


def _normalized(acc):
    return acc[:HEAD_DIM] * (1.0 / acc[HEAD_DIM:HEAD_DIM + 1])


def _attn_ab_body(q_ref, k_ref, vt_ref, gate_ref, *rest, group, qsub, tqs, dk, n_tok, pre_transpose, lookahead):
    units = [(g, u) for g in range(group) for u in range(qsub)]
    if pre_transpose:
        o_ref, m_ref, acc_ref, s_ref, p_ref, a_ref, qt_ref = rest[-7:]
        for idx, (g, u) in enumerate(units):
            qt_ref[idx] = q_ref[u * tqs:(u + 1) * tqs, g * dk:(g + 1) * dk].astype(F32).T.astype(BF16)
    else:
        o_ref, m_ref, acc_ref, s_ref, p_ref, a_ref = rest[-6:]

    def scores(kc, idx, variant):
        if pre_transpose:
            return jnp.dot(kc, qt_ref[idx], preferred_element_type=F32)
        g, u = units[idx]
        return lax.dot_general(kc, q_ref[u * tqs:(u + 1) * tqs, g * dk:(g + 1) * dk], _NT,
                               preferred_element_type=F32)

    _online_softmax_sweep(k_ref, vt_ref, m_ref, acc_ref, s_ref, p_ref, a_ref, len(units), n_tok, scores, None,
                          lookahead)

    for idx, (g, u) in enumerate(units):
        rows = slice(u * tqs, (u + 1) * tqs)
        cols = slice(g * HEAD_DIM, (g + 1) * HEAD_DIM)
        o = _normalized(acc_ref[idx]).T
        o_ref[rows, cols] = (o * gate_ref[rows, cols]).astype(o_ref.dtype)


def _attn_c_body(q_ref, k_ref, vt_ref, gate_ref, lam_ref, slope_ref, gsub_ref, *rest,
                 heads, qsub, tqs, n_tok, q_is_meta, lambda_init, lookahead):
    o_ref, m_ref, acc_ref, s_ref, p_ref, a_ref, q2_ref = rest[-7:]
    slot_w = 2 * LANES
    units = [(hd, u, c) for hd in range(heads) for u in range(qsub) for c in range(2)]
    lane = lax.broadcasted_iota(jnp.int32, (1, LANES), 1)
    flip = jnp.where(lane < C_QK_DIM, 1.0, -1.0)
    for idx, (hd, u, c) in enumerate(units):
        col = hd * slot_w + c * LANES
        q = q_ref[u * tqs:(u + 1) * tqs, col:col + LANES]
        q2_ref[0, idx] = q
        q2_ref[1, idx] = (q.astype(F32) * flip).astype(BF16)
    slope2 = [(2.0 * LOG2E) * slope_ref[hd, :, :1] for hd in range(heads)]
    tq = qsub * tqs
    n_chunks = n_tok // KV_CHUNK
    if q_is_meta:
        q_off = 0
        phases = [(0, n_chunks, 1, 1, False)]
        first_variant, tail_fix = 1, True
    else:
        assert tq % KV_CHUNK == 0 and KV_CHUNK % tqs == 0
        span = tq // KV_CHUNK
        per_chunk = KV_CHUNK // tqs
        i = pl.program_id(2)
        q_off = N_META + i * tq
        d = i * span

        def variant_in(j):
            return lambda idx: 1 if units[idx][1] < j * per_chunk else 0

        def fix_in(j):
            return lambda idx: j * per_chunk <= units[idx][1] < (j + 1) * per_chunk

        phases = ([(0, d, 0, 0, False)]
                  + [(d + j, d + j + 1, variant_in(j), variant_in(j + 1) if j + 1 < span else 1, fix_in(j))
                     for j in range(span)]
                  + [(d + span, n_chunks, 1, 1, False)])
        first_variant, tail_fix = 0, False

    def scores(kc, idx, variant):
        hd, _, c = units[idx]
        col = hd * slot_w + c * LANES
        return lax.dot_general(kc[:, col:col + LANES], q2_ref[variant, idx], _NT, preferred_element_type=F32)

    def make_adjust(k_off, n_keys, fix):
        key = lax.broadcasted_iota(jnp.int32, (n_keys, tqs), 0)
        qry = lax.broadcasted_iota(jnp.int32, (n_keys, tqs), 1)
        ahead, late = {}, {}

        def adjust(s, idx):
            hd, u, _ = units[idx]
            if _per_unit(fix, idx):
                if u not in ahead:
                    ahead[u] = jnp.maximum(((key - qry) + (k_off - q_off - u * tqs)).astype(F32), 0.0)
                if (hd, u) not in late:
                    late[hd, u] = slope2[hd] * ahead[u]
                s = s - late[hd, u]
            return s
        return adjust

    _online_softmax_sweep(k_ref, vt_ref, m_ref, acc_ref, s_ref, p_ref, a_ref, len(units), n_tok, scores, make_adjust,
                          lookahead, phases=phases, first_variant=first_variant, tail_variant=0, tail_fix=tail_fix,
                          value_head=lambda idx: units[idx][0])

    lv = lam_ref[...]
    lam = (jnp.exp(jnp.sum(lv[0:1] * lv[1:2], axis=1, keepdims=True))
           - jnp.exp(jnp.sum(lv[2:3] * lv[3:4], axis=1, keepdims=True)) + lambda_init)
    for hd in range(heads):
        cols = slice(hd * C_VDIM, (hd + 1) * C_VDIM)
        for u in range(qsub):
            rows = slice(u * tqs, (u + 1) * tqs)
            first = 2 * (hd * qsub + u)
            o = (_normalized(acc_ref[first]) - lam * _normalized(acc_ref[first + 1])).T
            ms = jnp.mean(o * o, axis=-1, keepdims=True)
            o = ((o * lax.rsqrt(ms + NORM_EPS)) * gsub_ref[...]) * (1.0 - lambda_init)
            o_ref[rows, cols] = (o * gate_ref[rows, cols]).astype(o_ref.dtype)


def _attention(body, q, q_col0, q_w, k, k_col0, k_w, vt, v_row0, gate, gate_col0, extras, extra_specs,
               out_w_total, heads, group, qsub, n_units, n_tok, name, lookahead, qt_shape, v_heads=1):
    bsz, lp, _ = q.shape
    out_w = group * HEAD_DIM

    def call(qsub_, tqs, n_qblk, qblk0, prev):
        tq = qsub_ * tqs
        in_specs = [
            pl.BlockSpec((None, tq, q_w), lambda b, h, i: (b, qblk0 + i, q_col0 + h)),
            pl.BlockSpec((None, lp, k_w), lambda b, h, i: (b, 0, k_col0 + h)),
            pl.BlockSpec((v_heads * HEAD_DIM, lp), lambda b, h, i: (v_row0 + h, b)),
            pl.BlockSpec((None, tq, out_w), lambda b, h, i: (b, qblk0 + i, gate_col0 + h)),
        ] + extra_specs
        args = [q, k, vt, gate] + extras
        aliases = {}
        if prev is not None:
            in_specs.append(pl.BlockSpec(memory_space=pl.ANY))
            args.append(prev)
            aliases = {len(args) - 1: 0}
        units = n_units * qsub_
        qt = qt_shape(units, tqs)
        return pl.pallas_call(
            body(qsub_, tqs, prev is not None),
            grid=(bsz, heads, n_qblk),
            in_specs=in_specs,
            out_specs=pl.BlockSpec((None, tq, out_w), lambda b, h, i: (b, qblk0 + i, h)),
            out_shape=jax.ShapeDtypeStruct((bsz, lp, out_w_total), BF16),
            scratch_shapes=[pltpu.VMEM((units, 1, tqs), F32),
                            pltpu.VMEM((units, HEAD_DIM + ONES_ROWS, tqs), F32),
                            pltpu.VMEM((lookahead, KV_CHUNK, tqs), F32),
                            pltpu.VMEM((KV_CHUNK, tqs), BF16), pltpu.VMEM((1, tqs), F32)]
                           + ([] if qt is None else [pltpu.VMEM(qt, BF16)]),
            input_output_aliases=aliases,
            compiler_params=_cparams(3),
            name=name + ("_meta" if prev is not None else ""),
        )(*args)

    main = call(qsub, Q_SUB, n_tok // (qsub * Q_SUB), 0, None)
    return call(1, TAIL, 1, n_tok // TAIL, main)


def _attn_a(qk, vt_plain, gates, n_tok):
    def body(qsub_, tqs, is_meta):
        return functools.partial(_attn_ab_body, group=A_GROUP, qsub=qsub_, tqs=tqs, dk=HEAD_DIM, n_tok=n_tok,
                                 pre_transpose=False, lookahead=A_LOOKAHEAD)

    return _attention(body, qk, 0, A_GROUP * HEAD_DIM, qk, A_WIDTH // HEAD_DIM, HEAD_DIM, vt_plain, 0,
                      gates, 0, [], [], A_WIDTH, A_KV_HEADS, A_GROUP, 8, A_GROUP, n_tok, "attn_gqa",
                      A_LOOKAHEAD, lambda units, tqs: None)


def _attn_b(qb, kb, vt_b, gates, n_tok):
    dk = 2 * LANES

    def body(qsub_, tqs, is_meta):
        return functools.partial(_attn_ab_body, group=1, qsub=qsub_, tqs=tqs, dk=dk, n_tok=n_tok,
                                 pre_transpose=True, lookahead=B_LOOKAHEAD)

    return _attention(body, qb, 0, dk, kb, 0, dk, vt_b, 0, gates, A_WIDTH // HEAD_DIM, [], [],
                      B_WIDTH, B_HEADS, 1, 16, 1, n_tok, "attn_mla", B_LOOKAHEAD,
                      lambda units, tqs: (units, dk, tqs))


def _attn_c(cqk, vt_plain, gates, lamvec, slopes, g_sub, lambda_init, n_tok):
    pair = 2
    dk = pair * 2 * LANES

    def body(qsub_, tqs, is_meta):
        return functools.partial(_attn_c_body, heads=pair, qsub=qsub_, tqs=tqs, n_tok=n_tok, q_is_meta=is_meta,
                                 lambda_init=lambda_init, lookahead=C_LOOKAHEAD)

    extras = [lamvec, slopes, g_sub.reshape(1, C_VDIM)]
    extra_specs = [pl.BlockSpec((4, C_QK_DIM), lambda b, h, i: (0, 0)),
                   pl.BlockSpec((pair, 1, LANES), lambda b, h, i: (h, 0, 0)),
                   pl.BlockSpec((1, C_VDIM), lambda b, h, i: (0, 0))]
    return _attention(body, cqk, 0, dk, cqk, C_HEADS // pair, dk, vt_plain, A_KV_HEADS // pair, gates,
                      (A_WIDTH + B_WIDTH) // (pair * HEAD_DIM), extras, extra_specs,
                      C_WIDTH, C_HEADS // pair, pair, 8, 2 * pair, n_tok, "attn_diff", C_LOOKAHEAD,
                      lambda units, tqs: (2, units, tqs, LANES), v_heads=pair)


def _out_proj_body(a_ref, b_ref, c_ref, wa_ref, wb_ref, wc_ref, h_ref, o_ref):
    acc = jnp.dot(a_ref[...], wa_ref[...], preferred_element_type=F32)
    acc += jnp.dot(b_ref[...], wb_ref[...], preferred_element_type=F32)
    acc += jnp.dot(c_ref[...], wc_ref[...], preferred_element_type=F32)
    o_ref[...] = h_ref[...] + acc


def _out_proj(oa, ob, oc, w, h2d):
    rows = h2d.shape[0]
    tm = _tall_row_tile(rows)
    tn = 512
    return pl.pallas_call(
        _out_proj_body,
        grid=(rows // tm, D_MODEL // tn),
        in_specs=[pl.BlockSpec((tm, A_WIDTH), lambda i, j: (i, 0)),
                  pl.BlockSpec((tm, B_WIDTH), lambda i, j: (i, 0)),
                  pl.BlockSpec((tm, C_WIDTH), lambda i, j: (i, 0)),
                  pl.BlockSpec((A_WIDTH, tn), lambda i, j: (0, j)),
                  pl.BlockSpec((B_WIDTH, tn), lambda i, j: (A_WIDTH // B_WIDTH, j)),
                  pl.BlockSpec((C_WIDTH, tn), lambda i, j: ((A_WIDTH + B_WIDTH) // C_WIDTH, j)),
                  pl.BlockSpec((tm, tn), lambda i, j: (i, j))],
        out_specs=pl.BlockSpec((tm, tn), lambda i, j: (i, j)),
        out_shape=jax.ShapeDtypeStruct((rows, D_MODEL), F32),
        compiler_params=_cparams(2),
        name="out_proj",
    )(oa, ob, oc, w, w, w, h2d)


def _rope_angles(pos_f, n_freq):
    inv = ROPE_THETA ** (-jnp.arange(n_freq, dtype=F32) / n_freq)
    return pos_f[:, None] * inv[None, :]


def _position_tables(n_tok, bsz):
    rows = n_tok // GRID_W
    z = jnp.zeros((TAIL,), F32)
    row_f = jnp.concatenate([jnp.repeat(jnp.arange(rows, dtype=F32), GRID_W), z])
    col_f = jnp.concatenate([jnp.tile(jnp.arange(GRID_W, dtype=F32), rows), z])
    pos_f = jnp.concatenate([jnp.arange(n_tok, dtype=F32) + N_META,
                             jnp.arange(N_META, dtype=F32), jnp.zeros((TAIL - N_META,), F32)])
    ang_a = jnp.concatenate([_rope_angles(row_f, HEAD_DIM // 4), _rope_angles(col_f, HEAD_DIM // 4)], axis=-1)
    cos_a, sin_a = jnp.cos(ang_a), jnp.sin(ang_a)
    cos_a = jnp.concatenate([cos_a, cos_a], axis=-1)
    sin_a = jnp.concatenate([-sin_a, sin_a], axis=-1)
    ang_b = _rope_angles(pos_f, B_ROPE // 2)
    cos_b, sin_b = jnp.cos(ang_b), jnp.sin(ang_b)
    zb = jnp.zeros_like(cos_b)
    cos_b128 = jnp.concatenate([cos_b, cos_b, zb, zb], axis=-1)
    sin_lo = jnp.concatenate([-sin_b, zb, zb, zb], axis=-1)
    sin_hi = jnp.concatenate([zb, sin_b, zb, zb], axis=-1)
    hi = jnp.floor(pos_f / LANES)
    lo = pos_f - hi * LANES
    one = jnp.ones_like(pos_f)

    def slot_lanes(cols):
        body = jnp.stack(cols, axis=-1)
        return jnp.concatenate([jnp.zeros((lp, C_QK_DIM), F32), body,
                                jnp.zeros((lp, LANES - C_QK_DIM - len(cols)), F32)], axis=-1)

    lp = n_tok + TAIL
    pieces, rest = [], jnp.float32(LOG2E)
    for _ in range(3):
        piece = rest.astype(BF16).astype(F32)
        pieces.append(piece)
        rest = rest - piece
    q_terms = slot_lanes([col for c in pieces for col in (LANES * c * one, c * one, -LANES * hi, -lo)])
    k_terms = slot_lanes([col for c in pieces for col in (hi, lo, c * one, c * one)])
    return tuple(jnp.tile(t, (bsz, 1)) for t in (cos_a, sin_a, cos_b128, sin_lo, sin_hi, q_terms, k_terms))


def _prep_weights(w_in, w_q_b, w_kv_b, w_out, g_qn, g_kn):
    bounds = [0]
    for s in IN_SPLITS:
        bounds.append(bounds[-1] + s)
    col = lambda i: slice(bounds[i], bounds[i + 1])
    aq, ak, av, ag, bq, bkv, bpe, bg, cq, ck, cv, cg = (w_in[:, col(i)] for i in range(12))
    w_aqk = jnp.concatenate([aq, ak], axis=1).astype(BF16)
    w_plain = jnp.concatenate([av, cv], axis=1).astype(BF16)
    w_gate = jnp.concatenate([ag, bg, cg], axis=1).astype(BF16)
    w_cqk = jnp.concatenate([cq, ck], axis=1).astype(BF16)
    w_blow = jnp.concatenate([bq, bkv], axis=1).astype(BF16)
    w_bpe = jnp.concatenate([bpe, jnp.zeros((D_MODEL, LANES - B_ROPE), F32)], axis=1).astype(BF16)
    g_aqk = jnp.concatenate([jnp.tile(g_qn, A_HEADS), jnp.tile(g_kn, A_KV_HEADS)]).reshape(1, -1)
    m_aqk = jnp.concatenate([jnp.full((A_WIDTH,), HEAD_DIM ** -0.5 * LOG2E, F32),
                             jnp.ones((A_KV_WIDTH,), F32)]).reshape(1, -1)
    wq = w_q_b.astype(BF16).reshape(B_Q_RANK, B_HEADS, B_NOPE + B_ROPE)
    wq = jnp.concatenate([wq, jnp.zeros((B_Q_RANK, B_HEADS, 2 * LANES - B_NOPE - B_ROPE), BF16)], axis=-1)
    wq = wq.reshape(B_Q_RANK, B_HEADS * 2 * LANES)
    wkv = w_kv_b.astype(BF16).reshape(B_KV_RANK, B_HEADS, B_NOPE + B_VDIM)
    wkv = jnp.concatenate([wkv[:, :, :B_NOPE].reshape(B_KV_RANK, -1),
                           wkv[:, :, B_NOPE:].reshape(B_KV_RANK, -1)], axis=1)
    return dict(aqk=w_aqk, plain=w_plain, gate=w_gate, cqk=w_cqk, blow=w_blow, bpe=w_bpe,
                g_aqk=g_aqk, m_aqk=m_aqk, q_up=wq, kv_up=wkv, out=w_out.astype(BF16))


def _encode(x, meta, layers, g_final, slopes, cq_cols):
    bsz, n_tok, _ = x.shape
    lp = n_tok + TAIL
    rows = bsz * lp
    tail = jnp.concatenate([meta.astype(x.dtype), jnp.zeros((TAIL - N_META, D_MODEL), x.dtype)], axis=0)
    u, h = _embed_norm(x, tail, layers[0]["g_attn"])
    cos_a, sin_a, cos_b, sin_lo, sin_hi, q_terms, k_terms = _position_tables(n_tok, bsz)
    three = lambda a: a.reshape(bsz, lp, a.shape[-1])

    for l, p in enumerate(layers):
        lambda_init = 0.8 - 0.6 * math.exp(-0.3 * l)
        if l > 0:
            u = _rmsnorm_rows(h, p["g_attn"], BF16)
        aqk = _staggered_proj(_aqk_epilogue, u, p["aqk"], [p["g_aqk"], p["m_aqk"]], [cos_a, sin_a],
                              A_WIDTH + A_KV_WIDTH, BF16, 512, 512, True, "proj_gqa_qk")
        vt_plain = _proj(_proj_transposed_body, u, p["plain"], [], lambda tm, tn: [],
                         A_KV_WIDTH + C_WIDTH, BF16, 512, transposed=True, name="proj_v")
        gates = _proj(functools.partial(_direct_proj_body, epilogue=_gate_epilogue), u, p["gate"], [],
                      lambda tm, tn: [], A_WIDTH + B_WIDTH + C_WIDTH, F32, 512, name="proj_gate")
        cqk = _proj(functools.partial(_direct_proj_body, epilogue=_cqk_epilogue), u, p["cqk"],
                    list(cq_cols) + [q_terms, k_terms],
                    lambda tm, tn: [_col_vec_spec(tm, tn)] * 3 + [_row_table_spec(tm, tn)] * 2,
                    4 * C_QK_WIDTH, BF16, 512, out_tn=1024, name="proj_diff_qk")
        blow = _proj(_proj_plain_body, u, p["blow"], [], lambda tm, tn: [],
                     B_Q_RANK + B_KV_RANK, F32, 512, name="proj_mla_low")
        kpe = _proj(_proj_bpe_body, u, p["bpe"], [cos_b, sin_lo, sin_hi],
                    lambda tm, tn: [_row_table_spec(tm, tn)] * 3, LANES, BF16, LANES, name="proj_mla_pe")
        qb = _bq_up(blow, p["g_q_a"], p["q_up"], cos_b, sin_lo, sin_hi)
        kb, vt_b = _bkv_up(blow, p["g_kv_a"], p["kv_up"], kpe)

        gates3 = three(gates)
        oa = _attn_a(three(aqk), vt_plain, gates3, n_tok)
        ob = _attn_b(three(qb), three(kb), vt_b, gates3, n_tok)
        oc = _attn_c(three(cqk), vt_plain, gates3, p["lamvec"], slopes, p["g_sub"], lambda_init, n_tok)
        h = _out_proj(oa.reshape(rows, -1), ob.reshape(rows, -1), oc.reshape(rows, -1), p["out"], h)

    return _final_norm(h.reshape(bsz, lp, D_MODEL), g_final, n_tok)


def kernel(x_prompt, x_sample, meta, g_attn, w_in, g_qn, g_kn, g_q_a, w_q_b, g_kv_a, w_kv_b,
           lam_q1, lam_k1, lam_q2, lam_k2, g_sub, w_out, g_final):
    layers = []
    for l in range(DEPTH):
        p = _prep_weights(w_in[l], w_q_b[l], w_kv_b[l], w_out[l], g_qn[l], g_kn[l])
        p.update(g_attn=g_attn[l], g_q_a=g_q_a[l], g_kv_a=g_kv_a[l], g_sub=g_sub[l],
                 lamvec=jnp.stack([lam_q1[l], lam_k1[l], lam_q2[l], lam_k2[l]]).astype(F32))
        layers.append(p)
    slope_h = 2.0 ** (-(jnp.arange(C_HEADS, dtype=F32) + 1.0) * 8.0 / C_HEADS)
    slopes = jnp.broadcast_to(slope_h[:, None, None], (C_HEADS, 1, LANES))
    zeros_w, ones_w = jnp.zeros((C_QK_WIDTH,), F32), jnp.ones((C_QK_WIDTH,), F32)
    cq_cols = [jnp.concatenate([C_QK_DIM ** -0.5 * LOG2E * ones_w, ones_w]).reshape(1, -1),
               jnp.concatenate([jnp.repeat(slope_h, 2 * C_QK_DIM), zeros_w]).reshape(1, -1),
               jnp.concatenate([zeros_w, ones_w]).reshape(1, -1)]
    y_prompt = _encode(x_prompt, meta, layers, g_final, slopes, cq_cols)
    y_sample = _encode(x_sample, meta, layers, g_final, slopes, cq_cols)
    return (y_prompt, y_sample)
```

```python
import functools
import math

import jax
import jax.numpy as jnp
from jax import lax
from jax.experimental import pallas as pl
from jax.experimental.pallas import tpu as pltpu

D_MODEL = 4096
DEPTH = 2
N_META = 16
GRID_W = 64
HEAD_DIM = 128
ROPE_THETA = 10000.0
NORM_EPS = 1e-6
A_HEADS = 16
A_KV_HEADS = 4
A_GROUP = A_HEADS // A_KV_HEADS
A_WIDTH = A_HEADS * HEAD_DIM
A_KV_WIDTH = A_KV_HEADS * HEAD_DIM
B_HEADS = 8
B_Q_RANK = 1024
B_KV_RANK = 512
B_NOPE = 128
B_ROPE = 64
B_VDIM = 128
B_WIDTH = B_HEADS * B_VDIM
C_HEADS = 8
C_QK_DIM = 64
C_VDIM = 2 * C_QK_DIM
C_QK_WIDTH = C_HEADS * 2 * C_QK_DIM
C_WIDTH = C_HEADS * C_VDIM
IN_SPLITS = (A_WIDTH, A_KV_WIDTH, A_KV_WIDTH, A_WIDTH,
             B_Q_RANK, B_KV_RANK, B_ROPE, B_WIDTH,
             C_QK_WIDTH, C_QK_WIDTH, C_WIDTH, C_WIDTH)

LANES = 128
TAIL = LANES
KV_CHUNK = 512
Q_SUB = 256
NEG_BIG = -1e30
LOG2E = 1.4426950408889634
A_LOOKAHEAD = 2
B_LOOKAHEAD = 2
C_LOOKAHEAD = 2
ONES_ROWS = 16
BF16 = jnp.bfloat16
F32 = jnp.float32
VMEM_LIMIT = 56 * 1024 * 1024

_NT = (((1,), (1,)), ((), ()))


def _cparams(n_axes):
    return pltpu.CompilerParams(dimension_semantics=("arbitrary",) * n_axes,
                                vmem_limit_bytes=VMEM_LIMIT)


def _row_tile(rows):
    for t in (512, 640, 256, 128):
        if rows % t == 0:
            return t
    raise ValueError(f"no row tile for {rows}")


def _tall_row_tile(rows):
    for t in (1536, 1664):
        if rows % t == 0:
            return t
    return _row_tile(rows)


def _rmsnorm_body(x_ref, g_ref, o_ref):
    x = x_ref[...]
    ms = jnp.mean(x * x, axis=-1, keepdims=True)
    o_ref[...] = ((x * lax.rsqrt(ms + NORM_EPS)) * g_ref[...]).astype(o_ref.dtype)


def _rmsnorm_rows(h2d, g, out_dtype):
    rows, d = h2d.shape
    tm = 256 if rows % 256 == 0 else 128
    return pl.pallas_call(
        _rmsnorm_body,
        grid=(rows // tm,),
        in_specs=[pl.BlockSpec((tm, d), lambda i: (i, 0)),
                  pl.BlockSpec((1, d), lambda i: (0, 0))],
        out_specs=pl.BlockSpec((tm, d), lambda i: (i, 0)),
        out_shape=jax.ShapeDtypeStruct((rows, d), out_dtype),
        compiler_params=_cparams(1),
        name="rmsnorm",
    )(h2d, g.reshape(1, d))


def _embed_norm_body(x_ref, tail_ref, g_ref, u_ref, h_ref, *, n_real):
    def emit(x):
        h_ref[...] = x
        ms = jnp.mean(x * x, axis=-1, keepdims=True)
        u_ref[...] = ((x * lax.rsqrt(ms + NORM_EPS)) * g_ref[...]).astype(u_ref.dtype)

    @pl.when(pl.program_id(1) < n_real)
    def _():
        emit(x_ref[...])

    @pl.when(pl.program_id(1) == n_real)
    def _():
        emit(tail_ref[...])


def _embed_norm(x, tail, g):
    bsz, n_tok, d = x.shape
    lp = n_tok + TAIL
    n_real = n_tok // TAIL
    block = pl.BlockSpec((None, TAIL, d), lambda b, i: (b, i, 0))
    u, h = pl.pallas_call(
        functools.partial(_embed_norm_body, n_real=n_real),
        grid=(bsz, n_real + 1),
        in_specs=[pl.BlockSpec((None, TAIL, d), lambda b, i: (b, jnp.minimum(i, n_real - 1), 0)),
                  pl.BlockSpec((TAIL, d), lambda b, i: (0, 0)),
                  pl.BlockSpec((1, d), lambda b, i: (0, 0))],
        out_specs=[block, block],
        out_shape=[jax.ShapeDtypeStruct((bsz, lp, d), BF16), jax.ShapeDtypeStruct((bsz, lp, d), x.dtype)],
        compiler_params=_cparams(2),
        name="embed_norm",
    )(x, tail, g.reshape(1, d))
    return u.reshape(bsz * lp, d), h.reshape(bsz * lp, d)


def _final_norm(h3d, g, n_tok):
    bsz, _, d = h3d.shape
    tm = 256
    return pl.pallas_call(
        _rmsnorm_body,
        grid=(bsz, n_tok // tm),
        in_specs=[pl.BlockSpec((None, tm, d), lambda b, i: (b, i, 0)),
                  pl.BlockSpec((1, d), lambda b, i: (0, 0))],
        out_specs=pl.BlockSpec((None, tm, d), lambda b, i: (b, i, 0)),
        out_shape=jax.ShapeDtypeStruct((bsz, n_tok, d), F32),
        compiler_params=_cparams(2),
        name="final_norm",
    )(h3d, g.reshape(1, d))


def _rope_pairs_64(y, cos, sin_signed):
    return y * cos + pltpu.roll(y, 64, 1) * sin_signed


def _rope_pairs_32(x, cos, sin_lo, sin_hi):
    return x * cos + pltpu.roll(x, 96, 1) * sin_lo + pltpu.roll(x, 32, 1) * sin_hi


def _staggered_proj_body(x_ref, w_ref, *rest, n_tiles, epilogue):
    *extra, o_ref, acc_ref = rest
    j = pl.program_id(1)

    def matmul():
        acc_ref[j % 2] = jnp.dot(x_ref[...], w_ref[...], preferred_element_type=F32)

    def finish_previous():
        epilogue(acc_ref.at[(j + 1) % 2], *extra, o_ref)

    @pl.when(j == 0)
    def _():
        matmul()

    @pl.when(jnp.logical_and(j > 0, j < n_tiles))
    def _():
        finish_previous()
        matmul()

    @pl.when(j == n_tiles)
    def _():
        finish_previous()


def _staggered_proj(epilogue, u, w, col_vecs, row_tables, n_out, out_dtype, tn, out_tn, tall, name):
    rows, k = u.shape
    n = w.shape[1]
    tm = _tall_row_tile(rows) if tall else _row_tile(rows)
    n_tiles = n // tn
    prev = lambda j: jnp.maximum(j - 1, 0)
    return pl.pallas_call(
        functools.partial(_staggered_proj_body, n_tiles=n_tiles, epilogue=epilogue),
        grid=(rows // tm, n_tiles + 1),
        in_specs=([pl.BlockSpec((tm, k), lambda i, j: (i, 0)),
                   pl.BlockSpec((k, tn), lambda i, j: (0, jnp.minimum(j, n_tiles - 1)))]
                  + [pl.BlockSpec((1, tn), lambda i, j: (0, prev(j)))] * len(col_vecs)
                  + [pl.BlockSpec((tm, LANES), lambda i, j: (i, 0))] * len(row_tables)),
        out_specs=pl.BlockSpec((tm, out_tn), lambda i, j: (i, prev(j))),
        out_shape=jax.ShapeDtypeStruct((rows, n_out), out_dtype),
        scratch_shapes=[pltpu.VMEM((2, tm, tn), F32)],
        compiler_params=_cparams(2),
        name=name,
    )(u, w, *col_vecs, *row_tables)


def _direct_proj_body(x_ref, w_ref, *rest, epilogue):
    *extra, o_ref = rest
    epilogue(jnp.dot(x_ref[...], w_ref[...], preferred_element_type=F32), *extra, o_ref)


def _aqk_epilogue(acc_ref, g_ref, mult_ref, cos_ref, sin_ref, o_ref):
    cos = cos_ref[...]
    sin = sin_ref[...]
    for h in range(o_ref.shape[1] // HEAD_DIM):
        sl = slice(h * HEAD_DIM, (h + 1) * HEAD_DIM)
        a = acc_ref[:, sl]
        ms = jnp.mean(a * a, axis=-1, keepdims=True)
        y = (a * lax.rsqrt(ms + NORM_EPS)) * g_ref[:, sl]
        o_ref[:, sl] = (_rope_pairs_64(y, cos, sin) * mult_ref[:, sl]).astype(o_ref.dtype)


def _gate_epilogue(acc_ref, o_ref):
    a = acc_ref[...]
    o_ref[...] = a * (1.0 / (1.0 + jnp.exp(-a)))


def _proj_plain_body(x_ref, w_ref, o_ref):
    o_ref[...] = jnp.dot(x_ref[...], w_ref[...], preferred_element_type=F32).astype(o_ref.dtype)


def _proj_transposed_body(x_ref, w_ref, o_ref):
    o_ref[...] = jnp.dot(x_ref[...], w_ref[...], preferred_element_type=F32).T.astype(o_ref.dtype)


def _cqk_epilogue(acc_ref, mult_ref, qcoef_ref, kcoef_ref, qtab_ref, ktab_ref, o_ref):
    acc = acc_ref[...] * mult_ref[...]
    lane = lax.broadcasted_iota(jnp.int32, (acc.shape[0], LANES), 1)
    low = lane < C_QK_DIM
    qtab = qtab_ref[...]
    ktab = ktab_ref[...]
    for t in range(acc.shape[1] // LANES):
        sl = slice(t * LANES, (t + 1) * LANES)
        a = acc[:, sl]
        pos_terms = qtab * qcoef_ref[:, sl] + ktab * kcoef_ref[:, sl]
        o_ref[:, (2 * t) * LANES:(2 * t + 1) * LANES] = jnp.where(low, a, pos_terms).astype(o_ref.dtype)
        o_ref[:, (2 * t + 1) * LANES:(2 * t + 2) * LANES] = (
            jnp.where(low, pltpu.roll(a, 64, 1), pos_terms).astype(o_ref.dtype))


def _proj_bpe_body(x_ref, w_ref, cos_ref, slo_ref, shi_ref, o_ref):
    acc = jnp.dot(x_ref[...], w_ref[...], preferred_element_type=F32)
    o_ref[...] = _rope_pairs_32(acc, cos_ref[...], slo_ref[...], shi_ref[...]).astype(o_ref.dtype)


def _proj(body, u, w, extras, extra_specs, n_out, out_dtype, tn, out_tn=None, transposed=False, tall=True,
          name="proj"):
    rows, k = u.shape
    n = w.shape[1]
    tm = _tall_row_tile(rows) if tall else _row_tile(rows)
    out_tn = tn if out_tn is None else out_tn
    if transposed:
        out_specs = pl.BlockSpec((out_tn, tm), lambda i, j: (j, i))
        out_shape = jax.ShapeDtypeStruct((n_out, rows), out_dtype)
    else:
        out_specs = pl.BlockSpec((tm, out_tn), lambda i, j: (i, j))
        out_shape = jax.ShapeDtypeStruct((rows, n_out), out_dtype)
    return pl.pallas_call(
        body,
        grid=(rows // tm, n // tn),
        in_specs=[pl.BlockSpec((tm, k), lambda i, j: (i, 0)),
                  pl.BlockSpec((k, tn), lambda i, j: (0, j))] + extra_specs(tm, tn),
        out_specs=out_specs,
        out_shape=out_shape,
        compiler_params=_cparams(2),
        name=name,
    )(u, w, *extras)


def _row_table_spec(tm, tn):
    return pl.BlockSpec((tm, LANES), lambda i, j: (i, 0))


def _col_vec_spec(tm, tn):
    return pl.BlockSpec((1, tn), lambda i, j: (0, j))


def _bq_up_body(c_ref, g_ref, w_ref, cos_ref, slo_ref, shi_ref, o_ref):
    c = c_ref[...]
    ms = jnp.mean(c * c, axis=-1, keepdims=True)
    u = ((c * lax.rsqrt(ms + NORM_EPS)) * g_ref[...]).astype(BF16)
    acc = jnp.dot(u, w_ref[...], preferred_element_type=F32) * ((B_NOPE + B_ROPE) ** -0.5 * LOG2E)
    cos, slo, shi = cos_ref[...], slo_ref[...], shi_ref[...]
    for h in range(B_HEADS):
        base = h * 2 * LANES
        o_ref[:, base:base + LANES] = acc[:, base:base + LANES].astype(o_ref.dtype)
        x = acc[:, base + LANES:base + 2 * LANES]
        o_ref[:, base + LANES:base + 2 * LANES] = _rope_pairs_32(x, cos, slo, shi).astype(o_ref.dtype)


def _bkv_up_body(c_ref, g_ref, w_ref, kpe_ref, k_ref, v_ref):
    c = c_ref[...]
    ms = jnp.mean(c * c, axis=-1, keepdims=True)
    u = ((c * lax.rsqrt(ms + NORM_EPS)) * g_ref[...]).astype(BF16)
    acc = jnp.dot(u, w_ref[...], preferred_element_type=F32)
    kpe = kpe_ref[...]
    for h in range(B_HEADS):
        base = h * 2 * LANES
        k_ref[:, base:base + LANES] = acc[:, h * LANES:(h + 1) * LANES].astype(k_ref.dtype)
        k_ref[:, base + LANES:base + 2 * LANES] = kpe
    v_ref[...] = acc[:, B_HEADS * LANES:].T.astype(v_ref.dtype)


def _bq_up(cb, g, w, cos, slo, shi):
    rows = cb.shape[0]
    tm = _row_tile(rows)
    n = w.shape[1]
    tab = pl.BlockSpec((tm, LANES), lambda i: (i, 0))
    return pl.pallas_call(
        _bq_up_body,
        grid=(rows // tm,),
        in_specs=[pl.BlockSpec((tm, B_Q_RANK), lambda i: (i, 0)),
                  pl.BlockSpec((1, B_Q_RANK), lambda i: (0, 0)),
                  pl.BlockSpec((B_Q_RANK, n), lambda i: (0, 0)),
                  tab, tab, tab],
        out_specs=pl.BlockSpec((tm, n), lambda i: (i, 0)),
        out_shape=jax.ShapeDtypeStruct((rows, n), BF16),
        compiler_params=_cparams(1),
        name="mla_q_up",
    )(cb, g.reshape(1, -1), w, cos, slo, shi)


def _bkv_up(cb, g, w, kpe):
    rows = cb.shape[0]
    tm = _row_tile(rows)
    n = w.shape[1]
    return pl.pallas_call(
        _bkv_up_body,
        grid=(rows // tm,),
        in_specs=[pl.BlockSpec((tm, B_KV_RANK), lambda i: (i, B_Q_RANK // B_KV_RANK)),
                  pl.BlockSpec((1, B_KV_RANK), lambda i: (0, 0)),
                  pl.BlockSpec((B_KV_RANK, n), lambda i: (0, 0)),
                  pl.BlockSpec((tm, LANES), lambda i: (i, 0))],
        out_specs=[pl.BlockSpec((tm, B_HEADS * 2 * LANES), lambda i: (i, 0)),
                   pl.BlockSpec((B_HEADS * B_VDIM, tm), lambda i: (0, i))],
        out_shape=[jax.ShapeDtypeStruct((rows, B_HEADS * 2 * LANES), BF16),
                   jax.ShapeDtypeStruct((B_HEADS * B_VDIM, rows), BF16)],
        compiler_params=_cparams(1),
        name="mla_kv_up",
    )(cb, g.reshape(1, -1), w, kpe)


def _online_softmax_sweep(k_ref, vt_ref, m_ref, acc_ref, s_ref, p_ref, a_ref, n_units, n_tok, scores, make_adjust,
                          lookahead, phases=None, first_variant=0, tail_variant=0, tail_fix=False,
                          value_head=lambda idx: 0):
    look = min(lookahead, n_units)
    n_chunks = n_tok // KV_CHUNK
    if phases is None:
        phases = [(0, n_chunks, 0, 0, False)]

    def probabilities(s, idx):
        m_prev = m_ref[idx]
        m_new = jnp.maximum(m_prev, jnp.max(s, axis=0, keepdims=True))
        m_ref[idx] = m_new
        return jnp.exp2(s - m_new).astype(BF16), jnp.exp2(m_prev - m_new)

    def accumulate(idx, p, alpha, vt_ones):
        acc_ref[idx] = alpha * acc_ref[idx] + jnp.dot(vt_ones, p, preferred_element_type=F32)

    def values_of(head, cols):
        return _with_ones_rows(vt_ref[head * HEAD_DIM:(head + 1) * HEAD_DIM, cols])

    def chunk(s_first, kc, cols, k_off, kc_after, behind, variant, variant_after, fix):
        pending = list(s_first)
        vt_ones = {}
        adjust = (lambda s, idx: s) if make_adjust is None else make_adjust(k_off, kc.shape[0], fix)
        for idx in range(n_units):
            s = pending.pop(0)
            ahead = idx + look
            if ahead < n_units:
                pending.append(scores(kc, ahead, _per_unit(variant, ahead)))
            elif kc_after is not None:
                s_ref[ahead - n_units] = scores(kc_after, ahead - n_units, _per_unit(variant_after, ahead - n_units))
            p, alpha = probabilities(adjust(s, idx), idx)
            if behind is not None:
                accumulate(*behind)
            head = value_head(idx)
            if head not in vt_ones:
                vt_ones[head] = values_of(head, cols)
            missing = vt_ones[head].shape[1] - p.shape[0]
            if missing:
                p = jnp.concatenate([p, jnp.zeros((missing, p.shape[1]), p.dtype)], axis=0)
            behind = (idx, p, alpha, vt_ones[head])
        return behind

    def chunk_step(c, carry, variant, variant_after, fix):
        off = pl.multiple_of(c * KV_CHUNK, KV_CHUNK)
        off_before = pl.multiple_of(jnp.maximum(c - 1, 0) * KV_CHUNK, KV_CHUNK)
        off_after = pl.multiple_of(jnp.minimum(c + 1, n_chunks - 1) * KV_CHUNK, KV_CHUNK)
        behind = (n_units - 1, p_ref[...], a_ref[...],
                  values_of(value_head(n_units - 1), pl.ds(off_before, KV_CHUNK)))
        _, p, alpha, _ = chunk([s_ref[j] for j in range(look)], k_ref[pl.ds(off, KV_CHUNK), :],
                               pl.ds(off, KV_CHUNK), N_META + off,
                               k_ref[pl.ds(off_after, KV_CHUNK), :], behind, variant, variant_after, fix)
        p_ref[...] = p
        a_ref[...] = alpha
        return carry

    m_ref[...] = jnp.full(m_ref.shape, NEG_BIG, F32)
    acc_ref[...] = jnp.zeros(acc_ref.shape, F32)
    p_ref[...] = jnp.zeros(p_ref.shape, BF16)
    a_ref[...] = jnp.ones(a_ref.shape, F32)
    for j in range(look):
        s_ref[j] = scores(k_ref[0:KV_CHUNK, :], j, _per_unit(first_variant, j))
    for lo, hi, variant, variant_after, fix in phases:
        lax.fori_loop(lo, hi, functools.partial(chunk_step, variant=variant, variant_after=variant_after, fix=fix), 0)
    accumulate(n_units - 1, p_ref[...], a_ref[...],
               values_of(value_head(n_units - 1), slice(n_tok - KV_CHUNK, n_tok)))
    k_tail = k_ref[n_tok:n_tok + N_META, :]
    accumulate(*chunk([scores(k_tail, j, _per_unit(tail_variant, j)) for j in range(look)], k_tail,
                      slice(n_tok, n_tok + TAIL),
                      0, None, None, tail_variant, tail_variant, tail_fix))


def _per_unit(choice, idx):
    return choice(idx) if callable(choice) else choice


def _with_ones_rows(vt):
    return jnp.concatenate([vt, jnp.ones((ONES_ROWS, vt.shape[1]), vt.dtype)], axis=0)


def _normalized(acc):
    return acc[:HEAD_DIM] * (1.0 / acc[HEAD_DIM:HEAD_DIM + 1])


def _attn_ab_body(q_ref, k_ref, vt_ref, gate_ref, *rest, group, qsub, tqs, dk, n_tok, pre_transpose, lookahead):
    units = [(g, u) for g in range(group) for u in range(qsub)]
    if pre_transpose:
        o_ref, m_ref, acc_ref, s_ref, p_ref, a_ref, qt_ref = rest[-7:]
        for idx, (g, u) in enumerate(units):
            qt_ref[idx] = q_ref[u * tqs:(u + 1) * tqs, g * dk:(g + 1) * dk].astype(F32).T.astype(BF16)
    else:
        o_ref, m_ref, acc_ref, s_ref, p_ref, a_ref = rest[-6:]

    def scores(kc, idx, variant):
        if pre_transpose:
            return jnp.dot(kc, qt_ref[idx], preferred_element_type=F32)
        g, u = units[idx]
        return lax.dot_general(kc, q_ref[u * tqs:(u + 1) * tqs, g * dk:(g + 1) * dk], _NT,
                               preferred_element_type=F32)

    _online_softmax_sweep(k_ref, vt_ref, m_ref, acc_ref, s_ref, p_ref, a_ref, len(units), n_tok, scores, None,
                          lookahead)

    for idx, (g, u) in enumerate(units):
        rows = slice(u * tqs, (u + 1) * tqs)
        cols = slice(g * HEAD_DIM, (g + 1) * HEAD_DIM)
        o = _normalized(acc_ref[idx]).T
        o_ref[rows, cols] = (o * gate_ref[rows, cols]).astype(o_ref.dtype)


def _attn_c_body(q_ref, k_ref, vt_ref, gate_ref, lam_ref, slope_ref, gsub_ref, *rest,
                 heads, qsub, tqs, n_tok, q_is_meta, lambda_init, lookahead):
    o_ref, m_ref, acc_ref, s_ref, p_ref, a_ref, q2_ref = rest[-7:]
    slot_w = 2 * LANES
    units = [(hd, u, c) for hd in range(heads) for u in range(qsub) for c in range(2)]
    lane = lax.broadcasted_iota(jnp.int32, (1, LANES), 1)
    flip = jnp.where(lane < C_QK_DIM, 1.0, -1.0)
    for idx, (hd, u, c) in enumerate(units):
        col = hd * slot_w + c * LANES
        q = q_ref[u * tqs:(u + 1) * tqs, col:col + LANES]
        q2_ref[0, idx] = q
        q2_ref[1, idx] = (q.astype(F32) * flip).astype(BF16)
    slope2 = [(2.0 * LOG2E) * slope_ref[hd, :, :1] for hd in range(heads)]
    tq = qsub * tqs
    n_chunks = n_tok // KV_CHUNK
    if q_is_meta:
        q_off = 0
        phases = [(0, n_chunks, 1, 1, False)]
        first_variant, tail_fix = 1, True
    else:
        assert tq % KV_CHUNK == 0 and KV_CHUNK % tqs == 0
        span = tq // KV_CHUNK
        per_chunk = KV_CHUNK // tqs
        i = pl.program_id(2)
        q_off = N_META + i * tq
        d = i * span

        def variant_in(j):
            return lambda idx: 1 if units[idx][1] < j * per_chunk else 0

        def fix_in(j):
            return lambda idx: j * per_chunk <= units[idx][1] < (j + 1) * per_chunk

        phases = ([(0, d, 0, 0, False)]
                  + [(d + j, d + j + 1, variant_in(j), variant_in(j + 1) if j + 1 < span else 1, fix_in(j))
                     for j in range(span)]
                  + [(d + span, n_chunks, 1, 1, False)])
        first_variant, tail_fix = 0, False

    def scores(kc, idx, variant):
        hd, _, c = units[idx]
        col = hd * slot_w + c * LANES
        return lax.dot_general(kc[:, col:col + LANES], q2_ref[variant, idx], _NT, preferred_element_type=F32)

    def make_adjust(k_off, n_keys, fix):
        key = lax.broadcasted_iota(jnp.int32, (n_keys, tqs), 0)
        qry = lax.broadcasted_iota(jnp.int32, (n_keys, tqs), 1)
        ahead, late = {}, {}

        def adjust(s, idx):
            hd, u, _ = units[idx]
            if _per_unit(fix, idx):
                if u not in ahead:
                    ahead[u] = jnp.maximum(((key - qry) + (k_off - q_off - u * tqs)).astype(F32), 0.0)
                if (hd, u) not in late:
                    late[hd, u] = slope2[hd] * ahead[u]
                s = s - late[hd, u]
            return s
        return adjust

    _online_softmax_sweep(k_ref, vt_ref, m_ref, acc_ref, s_ref, p_ref, a_ref, len(units), n_tok, scores, make_adjust,
                          lookahead, phases=phases, first_variant=first_variant, tail_variant=0, tail_fix=tail_fix,
                          value_head=lambda idx: units[idx][0])

    lv = lam_ref[...]
    lam = (jnp.exp(jnp.sum(lv[0:1] * lv[1:2], axis=1, keepdims=True))
           - jnp.exp(jnp.sum(lv[2:3] * lv[3:4], axis=1, keepdims=True)) + lambda_init)
    for hd in range(heads):
        cols = slice(hd * C_VDIM, (hd + 1) * C_VDIM)
        for u in range(qsub):
            rows = slice(u * tqs, (u + 1) * tqs)
            first = 2 * (hd * qsub + u)
            o = (_normalized(acc_ref[first]) - lam * _normalized(acc_ref[first + 1])).T
            ms = jnp.mean(o * o, axis=-1, keepdims=True)
            o = ((o * lax.rsqrt(ms + NORM_EPS)) * gsub_ref[...]) * (1.0 - lambda_init)
            o_ref[rows, cols] = (o * gate_ref[rows, cols]).astype(o_ref.dtype)


def _attention(body, q, q_col0, q_w, k, k_col0, k_w, vt, v_row0, gate, gate_col0, extras, extra_specs,
               out_w_total, heads, group, qsub, n_units, n_tok, name, lookahead, qt_shape, dest, v_heads=1):
    bsz, lp, _ = q.shape
    out_w = group * HEAD_DIM
    if dest is None:
        dest = jnp.zeros((bsz, lp, out_w_total), BF16)

    def call(qsub_, tqs, n_qblk, qblk0, is_meta, dest):
        tq = qsub_ * tqs
        in_specs = [
            pl.BlockSpec((None, tq, q_w), lambda b, h, i: (b, qblk0 + i, q_col0 + h)),
            pl.BlockSpec((None, lp, k_w), lambda b, h, i: (b, 0, k_col0 + h)),
            pl.BlockSpec((v_heads * HEAD_DIM, lp), lambda b, h, i: (v_row0 + h, b)),
            pl.BlockSpec((None, tq, out_w), lambda b, h, i: (b, qblk0 + i, gate_col0 + h)),
        ] + extra_specs
        in_specs.append(pl.BlockSpec(memory_space=pl.ANY))
        args = [q, k, vt, gate] + extras + [dest]
        units = n_units * qsub_
        qt = qt_shape(units, tqs)
        return pl.pallas_call(
            body(qsub_, tqs, is_meta),
            grid=(bsz, heads, n_qblk),
            in_specs=in_specs,
            out_specs=pl.BlockSpec((None, tq, out_w), lambda b, h, i: (b, qblk0 + i, h)),
            out_shape=jax.ShapeDtypeStruct((bsz, lp, out_w_total), BF16),
            scratch_shapes=[pltpu.VMEM((units, 1, tqs), F32),
                            pltpu.VMEM((units, HEAD_DIM + ONES_ROWS, tqs), F32),
                            pltpu.VMEM((lookahead, KV_CHUNK, tqs), F32),
                            pltpu.VMEM((KV_CHUNK, tqs), BF16), pltpu.VMEM((1, tqs), F32)]
                           + ([] if qt is None else [pltpu.VMEM(qt, BF16)]),
            input_output_aliases={len(args) - 1: 0},
            compiler_params=_cparams(3),
            name=name + ("_meta" if is_meta else ""),
        )(*args)

    main = call(qsub, Q_SUB, n_tok // (qsub * Q_SUB), 0, False, dest)
    return call(1, TAIL, 1, n_tok // TAIL, True, main)


def _attn_a(qk, vt_plain, gates, n_tok, dest):
    def body(qsub_, tqs, is_meta):
        return functools.partial(_attn_ab_body, group=A_GROUP, qsub=qsub_, tqs=tqs, dk=HEAD_DIM, n_tok=n_tok,
                                 pre_transpose=False, lookahead=A_LOOKAHEAD)

    return _attention(body, qk, 0, A_GROUP * HEAD_DIM, qk, A_WIDTH // HEAD_DIM, HEAD_DIM, vt_plain, 0,
                      gates, 0, [], [], A_WIDTH, A_KV_HEADS, A_GROUP, 8, A_GROUP, n_tok, "attn_gqa",
                      A_LOOKAHEAD, lambda units, tqs: None, dest)


def _attn_b(qb, kb, vt_b, gates, n_tok, dest):
    dk = 2 * LANES

    def body(qsub_, tqs, is_meta):
        return functools.partial(_attn_ab_body, group=1, qsub=qsub_, tqs=tqs, dk=dk, n_tok=n_tok,
                                 pre_transpose=True, lookahead=B_LOOKAHEAD)

    return _attention(body, qb, 0, dk, kb, 0, dk, vt_b, 0, gates, A_WIDTH // HEAD_DIM, [], [],
                      B_WIDTH, B_HEADS, 1, 16, 1, n_tok, "attn_mla", B_LOOKAHEAD,
                      lambda units, tqs: (units, dk, tqs), dest)


def _attn_c(cqk, vt_plain, gates, lamvec, slopes, g_sub, lambda_init, n_tok, dest):
    pair = 2
    dk = pair * 2 * LANES

    def body(qsub_, tqs, is_meta):
        return functools.partial(_attn_c_body, heads=pair, qsub=qsub_, tqs=tqs, n_tok=n_tok, q_is_meta=is_meta,
                                 lambda_init=lambda_init, lookahead=C_LOOKAHEAD)

    extras = [lamvec, slopes, g_sub.reshape(1, C_VDIM)]
    extra_specs = [pl.BlockSpec((4, C_QK_DIM), lambda b, h, i: (0, 0)),
                   pl.BlockSpec((pair, 1, LANES), lambda b, h, i: (h, 0, 0)),
                   pl.BlockSpec((1, C_VDIM), lambda b, h, i: (0, 0))]
    return _attention(body, cqk, 0, dk, cqk, C_HEADS // pair, dk, vt_plain, A_KV_HEADS // pair, gates,
                      (A_WIDTH + B_WIDTH) // (pair * HEAD_DIM), extras, extra_specs,
                      C_WIDTH, C_HEADS // pair, pair, 8, 2 * pair, n_tok, "attn_diff", C_LOOKAHEAD,
                      lambda units, tqs: (2, units, tqs, LANES), dest, v_heads=pair)


def _out_proj_body(a_ref, b_ref, c_ref, wa_ref, wb_ref, wc_ref, h_ref, o_ref):
    acc = jnp.dot(a_ref[...], wa_ref[...], preferred_element_type=F32)
    acc += jnp.dot(b_ref[...], wb_ref[...], preferred_element_type=F32)
    acc += jnp.dot(c_ref[...], wc_ref[...], preferred_element_type=F32)
    o_ref[...] = h_ref[...] + acc


def _out_proj(oa, ob, oc, w, h2d):
    rows = h2d.shape[0]
    tm = _tall_row_tile(rows)
    tn = 512
    return pl.pallas_call(
        _out_proj_body,
        grid=(rows // tm, D_MODEL // tn),
        in_specs=[pl.BlockSpec((tm, A_WIDTH), lambda i, j: (i, 0)),
                  pl.BlockSpec((tm, B_WIDTH), lambda i, j: (i, 0)),
                  pl.BlockSpec((tm, C_WIDTH), lambda i, j: (i, 0)),
                  pl.BlockSpec((A_WIDTH, tn), lambda i, j: (0, j)),
                  pl.BlockSpec((B_WIDTH, tn), lambda i, j: (A_WIDTH // B_WIDTH, j)),
                  pl.BlockSpec((C_WIDTH, tn), lambda i, j: ((A_WIDTH + B_WIDTH) // C_WIDTH, j)),
                  pl.BlockSpec((tm, tn), lambda i, j: (i, j))],
        out_specs=pl.BlockSpec((tm, tn), lambda i, j: (i, j)),
        out_shape=jax.ShapeDtypeStruct((rows, D_MODEL), F32),
        compiler_params=_cparams(2),
        name="out_proj",
    )(oa, ob, oc, w, w, w, h2d)


def _rope_angles(pos_f, n_freq):
    inv = ROPE_THETA ** (-jnp.arange(n_freq, dtype=F32) / n_freq)
    return pos_f[:, None] * inv[None, :]


def _position_tables(n_tok, bsz):
    rows = n_tok // GRID_W
    z = jnp.zeros((TAIL,), F32)
    row_f = jnp.concatenate([jnp.repeat(jnp.arange(rows, dtype=F32), GRID_W), z])
    col_f = jnp.concatenate([jnp.tile(jnp.arange(GRID_W, dtype=F32), rows), z])
    pos_f = jnp.concatenate([jnp.arange(n_tok, dtype=F32) + N_META,
                             jnp.arange(N_META, dtype=F32), jnp.zeros((TAIL - N_META,), F32)])
    ang_a = jnp.concatenate([_rope_angles(row_f, HEAD_DIM // 4), _rope_angles(col_f, HEAD_DIM // 4)], axis=-1)
    cos_a, sin_a = jnp.cos(ang_a), jnp.sin(ang_a)
    cos_a = jnp.concatenate([cos_a, cos_a], axis=-1)
    sin_a = jnp.concatenate([-sin_a, sin_a], axis=-1)
    ang_b = _rope_angles(pos_f, B_ROPE // 2)
    cos_b, sin_b = jnp.cos(ang_b), jnp.sin(ang_b)
    zb = jnp.zeros_like(cos_b)
    cos_b128 = jnp.concatenate([cos_b, cos_b, zb, zb], axis=-1)
    sin_lo = jnp.concatenate([-sin_b, zb, zb, zb], axis=-1)
    sin_hi = jnp.concatenate([zb, sin_b, zb, zb], axis=-1)
    hi = jnp.floor(pos_f / LANES)
    lo = pos_f - hi * LANES
    one = jnp.ones_like(pos_f)

    def slot_lanes(cols):
        body = jnp.stack(cols, axis=-1)
        return jnp.concatenate([jnp.zeros((lp, C_QK_DIM), F32), body,
                                jnp.zeros((lp, LANES - C_QK_DIM - len(cols)), F32)], axis=-1)

    lp = n_tok + TAIL
    pieces, rest = [], jnp.float32(LOG2E)
    for _ in range(3):
        piece = rest.astype(BF16).astype(F32)
        pieces.append(piece)
        rest = rest - piece
    q_terms = slot_lanes([col for c in pieces for col in (LANES * c * one, c * one, -LANES * hi, -lo)])
    k_terms = slot_lanes([col for c in pieces for col in (hi, lo, c * one, c * one)])
    return tuple(jnp.tile(t, (bsz, 1)) for t in (cos_a, sin_a, cos_b128, sin_lo, sin_hi, q_terms, k_terms))


def _prep_weights(w_in, w_q_b, w_kv_b, w_out, g_qn, g_kn):
    bounds = [0]
    for s in IN_SPLITS:
        bounds.append(bounds[-1] + s)
    col = lambda i: slice(bounds[i], bounds[i + 1])
    aq, ak, av, ag, bq, bkv, bpe, bg, cq, ck, cv, cg = (w_in[:, col(i)] for i in range(12))
    w_aqk = jnp.concatenate([aq, ak], axis=1).astype(BF16)
    w_plain = jnp.concatenate([av, cv], axis=1).astype(BF16)
    w_gate = jnp.concatenate([ag, bg, cg], axis=1).astype(BF16)
    w_cqk = jnp.concatenate([cq, ck], axis=1).astype(BF16)
    w_blow = jnp.concatenate([bq, bkv], axis=1).astype(BF16)
    w_bpe = jnp.concatenate([bpe, jnp.zeros((D_MODEL, LANES - B_ROPE), F32)], axis=1).astype(BF16)
    g_aqk = jnp.concatenate([jnp.tile(g_qn, A_HEADS), jnp.tile(g_kn, A_KV_HEADS)]).reshape(1, -1)
    m_aqk = jnp.concatenate([jnp.full((A_WIDTH,), HEAD_DIM ** -0.5 * LOG2E, F32),
                             jnp.ones((A_KV_WIDTH,), F32)]).reshape(1, -1)
    wq = w_q_b.astype(BF16).reshape(B_Q_RANK, B_HEADS, B_NOPE + B_ROPE)
    wq = jnp.concatenate([wq, jnp.zeros((B_Q_RANK, B_HEADS, 2 * LANES - B_NOPE - B_ROPE), BF16)], axis=-1)
    wq = wq.reshape(B_Q_RANK, B_HEADS * 2 * LANES)
    wkv = w_kv_b.astype(BF16).reshape(B_KV_RANK, B_HEADS, B_NOPE + B_VDIM)
    wkv = jnp.concatenate([wkv[:, :, :B_NOPE].reshape(B_KV_RANK, -1),
                           wkv[:, :, B_NOPE:].reshape(B_KV_RANK, -1)], axis=1)
    return dict(aqk=w_aqk, plain=w_plain, gate=w_gate, cqk=w_cqk, blow=w_blow, bpe=w_bpe,
                g_aqk=g_aqk, m_aqk=m_aqk, q_up=wq, kv_up=wkv, out=w_out.astype(BF16))


def _encode(x, meta, layers, g_final, slopes, cq_cols):
    bsz, n_tok, _ = x.shape
    lp = n_tok + TAIL
    rows = bsz * lp
    tail = jnp.concatenate([meta.astype(x.dtype), jnp.zeros((TAIL - N_META, D_MODEL), x.dtype)], axis=0)
    u, h = _embed_norm(x, tail, layers[0]["g_attn"])
    cos_a, sin_a, cos_b, sin_lo, sin_hi, q_terms, k_terms = _position_tables(n_tok, bsz)
    three = lambda a: a.reshape(bsz, lp, a.shape[-1])
    oa = ob = oc = None

    for l, p in enumerate(layers):
        lambda_init = 0.8 - 0.6 * math.exp(-0.3 * l)
        if l > 0:
            u = _rmsnorm_rows(h, p["g_attn"], BF16)
        aqk = _staggered_proj(_aqk_epilogue, u, p["aqk"], [p["g_aqk"], p["m_aqk"]], [cos_a, sin_a],
                              A_WIDTH + A_KV_WIDTH, BF16, 512, 512, True, "proj_gqa_qk")
        vt_plain = _proj(_proj_transposed_body, u, p["plain"], [], lambda tm, tn: [],
                         A_KV_WIDTH + C_WIDTH, BF16, 512, transposed=True, name="proj_v")
        gates = _proj(functools.partial(_direct_proj_body, epilogue=_gate_epilogue), u, p["gate"], [],
                      lambda tm, tn: [], A_WIDTH + B_WIDTH + C_WIDTH, F32, 512, name="proj_gate")
        cqk = _proj(functools.partial(_direct_proj_body, epilogue=_cqk_epilogue), u, p["cqk"],
                    list(cq_cols) + [q_terms, k_terms],
                    lambda tm, tn: [_col_vec_spec(tm, tn)] * 3 + [_row_table_spec(tm, tn)] * 2,
                    4 * C_QK_WIDTH, BF16, 512, out_tn=1024, name="proj_diff_qk")
        blow = _proj(_proj_plain_body, u, p["blow"], [], lambda tm, tn: [],
                     B_Q_RANK + B_KV_RANK, F32, 512, name="proj_mla_low")
        kpe = _proj(_proj_bpe_body, u, p["bpe"], [cos_b, sin_lo, sin_hi],
                    lambda tm, tn: [_row_table_spec(tm, tn)] * 3, LANES, BF16, LANES, name="proj_mla_pe")
        qb = _bq_up(blow, p["g_q_a"], p["q_up"], cos_b, sin_lo, sin_hi)
        kb, vt_b = _bkv_up(blow, p["g_kv_a"], p["kv_up"], kpe)

        gates3 = three(gates)
        oa = _attn_a(three(aqk), vt_plain, gates3, n_tok, oa)
        ob = _attn_b(three(qb), three(kb), vt_b, gates3, n_tok, ob)
        oc = _attn_c(three(cqk), vt_plain, gates3, p["lamvec"], slopes, p["g_sub"], lambda_init, n_tok, oc)
        h = _out_proj(oa.reshape(rows, -1), ob.reshape(rows, -1), oc.reshape(rows, -1), p["out"], h)

    return _final_norm(h.reshape(bsz, lp, D_MODEL), g_final, n_tok)


def kernel(x_prompt, x_sample, meta, g_attn, w_in, g_qn, g_kn, g_q_a, w_q_b, g_kv_a, w_kv_b,
           lam_q1, lam_k1, lam_q2, lam_k2, g_sub, w_out, g_final):
    layers = []
    for l in range(DEPTH):
        p = _prep_weights(w_in[l], w_q_b[l], w_kv_b[l], w_out[l], g_qn[l], g_kn[l])
        p.update(g_attn=g_attn[l], g_q_a=g_q_a[l], g_kv_a=g_kv_a[l], g_sub=g_sub[l],
                 lamvec=jnp.stack([lam_q1[l], lam_k1[l], lam_q2[l], lam_k2[l]]).astype(F32))
        layers.append(p)
    slope_h = 2.0 ** (-(jnp.arange(C_HEADS, dtype=F32) + 1.0) * 8.0 / C_HEADS)
    slopes = jnp.broadcast_to(slope_h[:, None, None], (C_HEADS, 1, LANES))
    zeros_w, ones_w = jnp.zeros((C_QK_WIDTH,), F32), jnp.ones((C_QK_WIDTH,), F32)
    cq_cols = [jnp.concatenate([C_QK_DIM ** -0.5 * LOG2E * ones_w, ones_w]).reshape(1, -1),
               jnp.concatenate([jnp.repeat(slope_h, 2 * C_QK_DIM), zeros_w]).reshape(1, -1),
               jnp.concatenate([zeros_w, ones_w]).reshape(1, -1)]
    y_prompt = _encode(x_prompt, meta, layers, g_final, slopes, cq_cols)
    y_sample = _encode(x_sample, meta, layers, g_final, slopes, cq_cols)
    return (y_prompt, y_sample)
```

```python
import functools
import math

import jax
import jax.numpy as jnp
from jax import lax
from jax.experimental import pallas as pl
from jax.experimental.pallas import tpu as pltpu

D_MODEL = 4096
DEPTH = 2
N_META = 16
GRID_W = 64
HEAD_DIM = 128
ROPE_THETA = 10000.0
NORM_EPS = 1e-6
A_HEADS = 16
A_KV_HEADS = 4
A_GROUP = A_HEADS // A_KV_HEADS
A_WIDTH = A_HEADS * HEAD_DIM
A_KV_WIDTH = A_KV_HEADS * HEAD_DIM
B_HEADS = 8
B_Q_RANK = 1024
B_KV_RANK = 512
B_NOPE = 128
B_ROPE = 64
B_VDIM = 128
B_WIDTH = B_HEADS * B_VDIM
C_HEADS = 8
C_QK_DIM = 64
C_VDIM = 2 * C_QK_DIM
C_QK_WIDTH = C_HEADS * 2 * C_QK_DIM
C_WIDTH = C_HEADS * C_VDIM
IN_SPLITS = (A_WIDTH, A_KV_WIDTH, A_KV_WIDTH, A_WIDTH,
             B_Q_RANK, B_KV_RANK, B_ROPE, B_WIDTH,
             C_QK_WIDTH, C_QK_WIDTH, C_WIDTH, C_WIDTH)

LANES = 128
TAIL = LANES
KV_CHUNK = 512
Q_SUB = 256
NEG_BIG = -1e30
LOG2E = 1.4426950408889634
A_LOOKAHEAD = 2
B_LOOKAHEAD = 2
C_LOOKAHEAD = 2
ONES_ROWS = 16
BF16 = jnp.bfloat16
F32 = jnp.float32
VMEM_LIMIT = 56 * 1024 * 1024

_NT = (((1,), (1,)), ((), ()))


def _cparams(n_axes):
    return pltpu.CompilerParams(dimension_semantics=("arbitrary",) * n_axes,
                                vmem_limit_bytes=VMEM_LIMIT)


def _row_tile(rows):
    for t in (512, 640, 256, 128):
        if rows % t == 0:
            return t
    raise ValueError(f"no row tile for {rows}")


def _tall_row_tile(rows):
    for t in (1536, 1664):
        if rows % t == 0:
            return t
    return _row_tile(rows)


def _rmsnorm_body(x_ref, g_ref, o_ref):
    x = x_ref[...]
    ms = jnp.mean(x * x, axis=-1, keepdims=True)
    o_ref[...] = ((x * lax.rsqrt(ms + NORM_EPS)) * g_ref[...]).astype(o_ref.dtype)


def _rmsnorm_rows(h2d, g, out_dtype):
    rows, d = h2d.shape
    tm = 256 if rows % 256 == 0 else 128
    return pl.pallas_call(
        _rmsnorm_body,
        grid=(rows // tm,),
        in_specs=[pl.BlockSpec((tm, d), lambda i: (i, 0)),
                  pl.BlockSpec((1, d), lambda i: (0, 0))],
        out_specs=pl.BlockSpec((tm, d), lambda i: (i, 0)),
        out_shape=jax.ShapeDtypeStruct((rows, d), out_dtype),
        compiler_params=_cparams(1),
        name="rmsnorm",
    )(h2d, g.reshape(1, d))


def _embed_norm_body(x_ref, tail_ref, g_ref, u_ref, h_ref, *, n_real):
    def emit(x):
        h_ref[...] = x
        ms = jnp.mean(x * x, axis=-1, keepdims=True)
        u_ref[...] = ((x * lax.rsqrt(ms + NORM_EPS)) * g_ref[...]).astype(u_ref.dtype)

    @pl.when(pl.program_id(1) < n_real)
    def _():
        emit(x_ref[...])

    @pl.when(pl.program_id(1) == n_real)
    def _():
        emit(tail_ref[...])


def _embed_norm(x, tail, g):
    bsz, n_tok, d = x.shape
    lp = n_tok + TAIL
    n_real = n_tok // TAIL
    block = pl.BlockSpec((None, TAIL, d), lambda b, i: (b, i, 0))
    u, h = pl.pallas_call(
        functools.partial(_embed_norm_body, n_real=n_real),
        grid=(bsz, n_real + 1),
        in_specs=[pl.BlockSpec((None, TAIL, d), lambda b, i: (b, jnp.minimum(i, n_real - 1), 0)),
                  pl.BlockSpec((TAIL, d), lambda b, i: (0, 0)),
                  pl.BlockSpec((1, d), lambda b, i: (0, 0))],
        out_specs=[block, block],
        out_shape=[jax.ShapeDtypeStruct((bsz, lp, d), BF16), jax.ShapeDtypeStruct((bsz, lp, d), x.dtype)],
        compiler_params=_cparams(2),
        name="embed_norm",
    )(x, tail, g.reshape(1, d))
    return u.reshape(bsz * lp, d), h.reshape(bsz * lp, d)


def _final_norm(h3d, g, n_tok):
    bsz, _, d = h3d.shape
    tm = 256
    return pl.pallas_call(
        _rmsnorm_body,
        grid=(bsz, n_tok // tm),
        in_specs=[pl.BlockSpec((None, tm, d), lambda b, i: (b, i, 0)),
                  pl.BlockSpec((1, d), lambda b, i: (0, 0))],
        out_specs=pl.BlockSpec((None, tm, d), lambda b, i: (b, i, 0)),
        out_shape=jax.ShapeDtypeStruct((bsz, n_tok, d), F32),
        compiler_params=_cparams(2),
        name="final_norm",
    )(h3d, g.reshape(1, d))


def _rope_pairs_64(y, cos, sin_signed):
    return y * cos + pltpu.roll(y, 64, 1) * sin_signed


def _rope_pairs_32(x, cos, sin_lo, sin_hi):
    return x * cos + pltpu.roll(x, 96, 1) * sin_lo + pltpu.roll(x, 32, 1) * sin_hi


def _staggered_proj_body(x_ref, w_ref, *rest, n_tiles, epilogue):
    *extra, o_ref, acc_ref = rest
    j = pl.program_id(1)

    def matmul():
        acc_ref[j % 2] = jnp.dot(x_ref[...], w_ref[...], preferred_element_type=F32)

    def finish_previous():
        epilogue(acc_ref.at[(j + 1) % 2], *extra, o_ref)

    @pl.when(j == 0)
    def _():
        matmul()

    @pl.when(jnp.logical_and(j > 0, j < n_tiles))
    def _():
        finish_previous()
        matmul()

    @pl.when(j == n_tiles)
    def _():
        finish_previous()


def _staggered_proj(epilogue, u, w, col_vecs, row_tables, n_out, out_dtype, tn, out_tn, tall, name):
    rows, k = u.shape
    n = w.shape[1]
    tm = _tall_row_tile(rows) if tall else _row_tile(rows)
    n_tiles = n // tn
    prev = lambda j: jnp.maximum(j - 1, 0)
    return pl.pallas_call(
        functools.partial(_staggered_proj_body, n_tiles=n_tiles, epilogue=epilogue),
        grid=(rows // tm, n_tiles + 1),
        in_specs=([pl.BlockSpec((tm, k), lambda i, j: (i, 0)),
                   pl.BlockSpec((k, tn), lambda i, j: (0, jnp.minimum(j, n_tiles - 1)))]
                  + [pl.BlockSpec((1, tn), lambda i, j: (0, prev(j)))] * len(col_vecs)
                  + [pl.BlockSpec((tm, LANES), lambda i, j: (i, 0))] * len(row_tables)),
        out_specs=pl.BlockSpec((tm, out_tn), lambda i, j: (i, prev(j))),
        out_shape=jax.ShapeDtypeStruct((rows, n_out), out_dtype),
        scratch_shapes=[pltpu.VMEM((2, tm, tn), F32)],
        compiler_params=_cparams(2),
        name=name,
    )(u, w, *col_vecs, *row_tables)


def _direct_proj_body(x_ref, w_ref, *rest, epilogue):
    *extra, o_ref = rest
    epilogue(jnp.dot(x_ref[...], w_ref[...], preferred_element_type=F32), *extra, o_ref)


def _aqk_epilogue(acc_ref, g_ref, mult_ref, cos_ref, sin_ref, o_ref):
    cos = cos_ref[...]
    sin = sin_ref[...]
    for h in range(o_ref.shape[1] // HEAD_DIM):
        sl = slice(h * HEAD_DIM, (h + 1) * HEAD_DIM)
        a = acc_ref[:, sl]
        ms = jnp.mean(a * a, axis=-1, keepdims=True)
        y = (a * lax.rsqrt(ms + NORM_EPS)) * g_ref[:, sl]
        o_ref[:, sl] = (_rope_pairs_64(y, cos, sin) * mult_ref[:, sl]).astype(o_ref.dtype)


def _gate_epilogue(acc_ref, o_ref):
    a = acc_ref[...]
    o_ref[...] = a * (1.0 / (1.0 + jnp.exp(-a)))


def _proj_plain_body(x_ref, w_ref, o_ref):
    o_ref[...] = jnp.dot(x_ref[...], w_ref[...], preferred_element_type=F32).astype(o_ref.dtype)


def _proj_transposed_body(x_ref, w_ref, o_ref):
    o_ref[...] = jnp.dot(x_ref[...], w_ref[...], preferred_element_type=F32).T.astype(o_ref.dtype)


def _cqk_epilogue(acc_ref, mult_ref, qcoef_ref, kcoef_ref, qtab_ref, ktab_ref, o_ref):
    acc = acc_ref[...] * mult_ref[...]
    lane = lax.broadcasted_iota(jnp.int32, (acc.shape[0], LANES), 1)
    low = lane < C_QK_DIM
    qtab = qtab_ref[...]
    ktab = ktab_ref[...]
    for t in range(acc.shape[1] // LANES):
        sl = slice(t * LANES, (t + 1) * LANES)
        a = acc[:, sl]
        pos_terms = qtab * qcoef_ref[:, sl] + ktab * kcoef_ref[:, sl]
        o_ref[:, (2 * t) * LANES:(2 * t + 1) * LANES] = jnp.where(low, a, pos_terms).astype(o_ref.dtype)
        o_ref[:, (2 * t + 1) * LANES:(2 * t + 2) * LANES] = (
            jnp.where(low, pltpu.roll(a, 64, 1), pos_terms).astype(o_ref.dtype))


def _proj_bpe_body(x_ref, w_ref, cos_ref, slo_ref, shi_ref, o_ref):
    acc = jnp.dot(x_ref[...], w_ref[...], preferred_element_type=F32)
    o_ref[...] = _rope_pairs_32(acc, cos_ref[...], slo_ref[...], shi_ref[...]).astype(o_ref.dtype)


def _proj(body, u, w, extras, extra_specs, n_out, out_dtype, tn, out_tn=None, transposed=False, tall=True,
          name="proj"):
    rows, k = u.shape
    n = w.shape[1]
    tm = _tall_row_tile(rows) if tall else _row_tile(rows)
    out_tn = tn if out_tn is None else out_tn
    if transposed:
        out_specs = pl.BlockSpec((out_tn, tm), lambda i, j: (j, i))
        out_shape = jax.ShapeDtypeStruct((n_out, rows), out_dtype)
    else:
        out_specs = pl.BlockSpec((tm, out_tn), lambda i, j: (i, j))
        out_shape = jax.ShapeDtypeStruct((rows, n_out), out_dtype)
    return pl.pallas_call(
        body,
        grid=(rows // tm, n // tn),
        in_specs=[pl.BlockSpec((tm, k), lambda i, j: (i, 0)),
                  pl.BlockSpec((k, tn), lambda i, j: (0, j))] + extra_specs(tm, tn),
        out_specs=out_specs,
        out_shape=out_shape,
        compiler_params=_cparams(2),
        name=name,
    )(u, w, *extras)


def _row_table_spec(tm, tn):
    return pl.BlockSpec((tm, LANES), lambda i, j: (i, 0))


def _col_vec_spec(tm, tn):
    return pl.BlockSpec((1, tn), lambda i, j: (0, j))


def _bq_up_body(c_ref, g_ref, w_ref, cos_ref, slo_ref, shi_ref, o_ref):
    c = c_ref[...]
    ms = jnp.mean(c * c, axis=-1, keepdims=True)
    u = ((c * lax.rsqrt(ms + NORM_EPS)) * g_ref[...]).astype(BF16)
    acc = jnp.dot(u, w_ref[...], preferred_element_type=F32) * ((B_NOPE + B_ROPE) ** -0.5 * LOG2E)
    cos, slo, shi = cos_ref[...], slo_ref[...], shi_ref[...]
    for h in range(B_HEADS):
        base = h * 2 * LANES
        o_ref[:, base:base + LANES] = acc[:, base:base + LANES].astype(o_ref.dtype)
        x = acc[:, base + LANES:base + 2 * LANES]
        o_ref[:, base + LANES:base + 2 * LANES] = _rope_pairs_32(x, cos, slo, shi).astype(o_ref.dtype)


def _bkv_up_body(c_ref, g_ref, w_ref, kpe_ref, k_ref, v_ref):
    c = c_ref[...]
    ms = jnp.mean(c * c, axis=-1, keepdims=True)
    u = ((c * lax.rsqrt(ms + NORM_EPS)) * g_ref[...]).astype(BF16)
    acc = jnp.dot(u, w_ref[...], preferred_element_type=F32)
    kpe = kpe_ref[...]
    for h in range(B_HEADS):
        base = h * 2 * LANES
        k_ref[:, base:base + LANES] = acc[:, h * LANES:(h + 1) * LANES].astype(k_ref.dtype)
        k_ref[:, base + LANES:base + 2 * LANES] = kpe
    v_ref[...] = acc[:, B_HEADS * LANES:].T.astype(v_ref.dtype)


def _bq_up(cb, g, w, cos, slo, shi):
    rows = cb.shape[0]
    tm = _row_tile(rows)
    n = w.shape[1]
    tab = pl.BlockSpec((tm, LANES), lambda i: (i, 0))
    return pl.pallas_call(
        _bq_up_body,
        grid=(rows // tm,),
        in_specs=[pl.BlockSpec((tm, B_Q_RANK), lambda i: (i, 0)),
                  pl.BlockSpec((1, B_Q_RANK), lambda i: (0, 0)),
                  pl.BlockSpec((B_Q_RANK, n), lambda i: (0, 0)),
                  tab, tab, tab],
        out_specs=pl.BlockSpec((tm, n), lambda i: (i, 0)),
        out_shape=jax.ShapeDtypeStruct((rows, n), BF16),
        compiler_params=_cparams(1),
        name="mla_q_up",
    )(cb, g.reshape(1, -1), w, cos, slo, shi)


def _bkv_up(cb, g, w, kpe):
    rows = cb.shape[0]
    tm = _row_tile(rows)
    n = w.shape[1]
    return pl.pallas_call(
        _bkv_up_body,
        grid=(rows // tm,),
        in_specs=[pl.BlockSpec((tm, B_KV_RANK), lambda i: (i, B_Q_RANK // B_KV_RANK)),
                  pl.BlockSpec((1, B_KV_RANK), lambda i: (0, 0)),
                  pl.BlockSpec((B_KV_RANK, n), lambda i: (0, 0)),
                  pl.BlockSpec((tm, LANES), lambda i: (i, 0))],
        out_specs=[pl.BlockSpec((tm, B_HEADS * 2 * LANES), lambda i: (i, 0)),
                   pl.BlockSpec((B_HEADS * B_VDIM, tm), lambda i: (0, i))],
        out_shape=[jax.ShapeDtypeStruct((rows, B_HEADS * 2 * LANES), BF16),
                   jax.ShapeDtypeStruct((B_HEADS * B_VDIM, rows), BF16)],
        compiler_params=_cparams(1),
        name="mla_kv_up",
    )(cb, g.reshape(1, -1), w, kpe)


def _online_softmax_sweep(k_ref, vt_ref, m_ref, acc_ref, s_ref, p_ref, a_ref, n_units, n_tok, scores, make_adjust,
                          lookahead, phases=None, first_variant=0, tail_variant=0, tail_fix=False,
                          value_head=lambda idx: 0):
    look = min(lookahead, n_units)
    n_chunks = n_tok // KV_CHUNK
    if phases is None:
        phases = [(0, n_chunks, 0, 0, False)]

    def probabilities(s, idx):
        m_prev = m_ref[idx]
        m_new = jnp.maximum(m_prev, jnp.max(s, axis=0, keepdims=True))
        m_ref[idx] = m_new
        return jnp.exp2(s - m_new).astype(BF16), jnp.exp2(m_prev - m_new)

    def accumulate(idx, p, alpha, vt_ones):
        acc_ref[idx] = alpha * acc_ref[idx] + jnp.dot(vt_ones, p, preferred_element_type=F32)

    def values_of(head, cols):
        return _with_ones_rows(vt_ref[head * HEAD_DIM:(head + 1) * HEAD_DIM, cols])

    def chunk(s_first, kc, cols, k_off, kc_after, behind, variant, variant_after, fix):
        pending = list(s_first)
        vt_ones = {}
        adjust = (lambda s, idx: s) if make_adjust is None else make_adjust(k_off, kc.shape[0], fix)
        for idx in range(n_units):
            s = pending.pop(0)
            ahead = idx + look
            if ahead < n_units:
                pending.append(scores(kc, ahead, _per_unit(variant, ahead)))
            elif kc_after is not None:
                s_ref[ahead - n_units] = scores(kc_after, ahead - n_units, _per_unit(variant_after, ahead - n_units))
            p, alpha = probabilities(adjust(s, idx), idx)
            if behind is not None:
                accumulate(*behind)
            head = value_head(idx)
            if head not in vt_ones:
                vt_ones[head] = values_of(head, cols)
            missing = vt_ones[head].shape[1] - p.shape[0]
            if missing:
                p = jnp.concatenate([p, jnp.zeros((missing, p.shape[1]), p.dtype)], axis=0)
            behind = (idx, p, alpha, vt_ones[head])
        return behind

    def chunk_step(c, carry, variant, variant_after, fix):
        off = pl.multiple_of(c * KV_CHUNK, KV_CHUNK)
        off_before = pl.multiple_of(jnp.maximum(c - 1, 0) * KV_CHUNK, KV_CHUNK)
        off_after = pl.multiple_of(jnp.minimum(c + 1, n_chunks - 1) * KV_CHUNK, KV_CHUNK)
        behind = (n_units - 1, p_ref[...], a_ref[...],
                  values_of(value_head(n_units - 1), pl.ds(off_before, KV_CHUNK)))
        _, p, alpha, _ = chunk([s_ref[j] for j in range(look)], k_ref[pl.ds(off, KV_CHUNK), :],
                               pl.ds(off, KV_CHUNK), N_META + off,
                               k_ref[pl.ds(off_after, KV_CHUNK), :], behind, variant, variant_after, fix)
        p_ref[...] = p
        a_ref[...] = alpha
        return carry

    m_ref[...] = jnp.full(m_ref.shape, NEG_BIG, F32)
    acc_ref[...] = jnp.zeros(acc_ref.shape, F32)
    p_ref[...] = jnp.zeros(p_ref.shape, BF16)
    a_ref[...] = jnp.ones(a_ref.shape, F32)
    for j in range(look):
        s_ref[j] = scores(k_ref[0:KV_CHUNK, :], j, _per_unit(first_variant, j))
    for lo, hi, variant, variant_after, fix in phases:
        lax.fori_loop(lo, hi, functools.partial(chunk_step, variant=variant, variant_after=variant_after, fix=fix), 0)
    accumulate(n_units - 1, p_ref[...], a_ref[...],
               values_of(value_head(n_units - 1), slice(n_tok - KV_CHUNK, n_tok)))
    k_tail = k_ref[n_tok:n_tok + N_META, :]
    accumulate(*chunk([scores(k_tail, j, _per_unit(tail_variant, j)) for j in range(look)], k_tail,
                      slice(n_tok, n_tok + TAIL),
                      0, None, None, tail_variant, tail_variant, tail_fix))


def _per_unit(choice, idx):
    return choice(idx) if callable(choice) else choice


def _with_ones_rows(vt):
    return jnp.concatenate([vt, jnp.ones((ONES_ROWS, vt.shape[1]), vt.dtype)], axis=0)


def _normalized(acc):
    return acc[:HEAD_DIM] * (1.0 / acc[HEAD_DIM:HEAD_DIM + 1])


def _attn_ab_body(q_ref, k_ref, vt_ref, gate_ref, *rest, group, qsub, tqs, dk, n_tok, pre_transpose, lookahead):
    units = [(g, u) for g in range(group) for u in range(qsub)]
    if pre_transpose:
        o_ref, m_ref, acc_ref, s_ref, p_ref, a_ref, qt_ref = rest[-7:]
        for idx, (g, u) in enumerate(units):
            qt_ref[idx] = q_ref[u * tqs:(u + 1) * tqs, g * dk:(g + 1) * dk].astype(F32).T.astype(BF16)
    else:
        o_ref, m_ref, acc_ref, s_ref, p_ref, a_ref = rest[-6:]

    def scores(kc, idx, variant):
        if pre_transpose:
            return jnp.dot(kc, qt_ref[idx], preferred_element_type=F32)
        g, u = units[idx]
        return lax.dot_general(kc, q_ref[u * tqs:(u + 1) * tqs, g * dk:(g + 1) * dk], _NT,
                               preferred_element_type=F32)

    _online_softmax_sweep(k_ref, vt_ref, m_ref, acc_ref, s_ref, p_ref, a_ref, len(units), n_tok, scores, None,
                          lookahead)

    for idx, (g, u) in enumerate(units):
        rows = slice(u * tqs, (u + 1) * tqs)
        cols = slice(g * HEAD_DIM, (g + 1) * HEAD_DIM)
        o = _normalized(acc_ref[idx]).T
        o_ref[rows, cols] = (o * gate_ref[rows, cols]).astype(o_ref.dtype)


def _attn_c_body(q_ref, k_ref, vt_ref, gate_ref, lam_ref, slope_ref, gsub_ref, *rest,
                 heads, qsub, tqs, n_tok, q_is_meta, lambda_init, lookahead):
    o_ref, m_ref, acc_ref, s_ref, p_ref, a_ref, q2_ref = rest[-7:]
    slot_w = 2 * LANES
    units = [(hd, u, c) for hd in range(heads) for u in range(qsub) for c in range(2)]
    lane = lax.broadcasted_iota(jnp.int32, (1, LANES), 1)
    flip = jnp.where(lane < C_QK_DIM, 1.0, -1.0)
    for idx, (hd, u, c) in enumerate(units):
        col = hd * slot_w + c * LANES
        q = q_ref[u * tqs:(u + 1) * tqs, col:col + LANES]
        q2_ref[0, idx] = q
        q2_ref[1, idx] = (q.astype(F32) * flip).astype(BF16)
    slope2 = [(2.0 * LOG2E) * slope_ref[hd, :, :1] for hd in range(heads)]
    tq = qsub * tqs
    n_chunks = n_tok // KV_CHUNK
    if q_is_meta:
        q_off = 0
        phases = [(0, n_chunks, 1, 1, False)]
        first_variant, tail_fix = 1, True
    else:
        assert tq % KV_CHUNK == 0 and KV_CHUNK % tqs == 0
        span = tq // KV_CHUNK
        per_chunk = KV_CHUNK // tqs
        i = pl.program_id(2)
        q_off = N_META + i * tq
        d = i * span

        def variant_in(j):
            return lambda idx: 1 if units[idx][1] < j * per_chunk else 0

        def fix_in(j):
            return lambda idx: j * per_chunk <= units[idx][1] < (j + 1) * per_chunk

        phases = ([(0, d, 0, 0, False)]
                  + [(d + j, d + j + 1, variant_in(j), variant_in(j + 1) if j + 1 < span else 1, fix_in(j))
                     for j in range(span)]
                  + [(d + span, n_chunks, 1, 1, False)])
        first_variant, tail_fix = 0, False

    def scores(kc, idx, variant):
        hd, _, c = units[idx]
        col = hd * slot_w + c * LANES
        return lax.dot_general(kc[:, col:col + LANES], q2_ref[variant, idx], _NT, preferred_element_type=F32)

    def make_adjust(k_off, n_keys, fix):
        key = lax.broadcasted_iota(jnp.int32, (n_keys, tqs), 0)
        qry = lax.broadcasted_iota(jnp.int32, (n_keys, tqs), 1)
        ahead, late = {}, {}

        def adjust(s, idx):
            hd, u, _ = units[idx]
            if _per_unit(fix, idx):
                if u not in ahead:
                    ahead[u] = jnp.maximum(((key - qry) + (k_off - q_off - u * tqs)).astype(F32), 0.0)
                if (hd, u) not in late:
                    late[hd, u] = slope2[hd] * ahead[u]
                s = s - late[hd, u]
            return s
        return adjust

    _online_softmax_sweep(k_ref, vt_ref, m_ref, acc_ref, s_ref, p_ref, a_ref, len(units), n_tok, scores, make_adjust,
                          lookahead, phases=phases, first_variant=first_variant, tail_variant=0, tail_fix=tail_fix,
                          value_head=lambda idx: units[idx][0])

    lv = lam_ref[...]
    lam = (jnp.exp(jnp.sum(lv[0:1] * lv[1:2], axis=1, keepdims=True))
           - jnp.exp(jnp.sum(lv[2:3] * lv[3:4], axis=1, keepdims=True)) + lambda_init)
    for hd in range(heads):
        cols = slice(hd * C_VDIM, (hd + 1) * C_VDIM)
        for u in range(qsub):
            rows = slice(u * tqs, (u + 1) * tqs)
            first = 2 * (hd * qsub + u)
            o = (_normalized(acc_ref[first]) - lam * _normalized(acc_ref[first + 1])).T
            ms = jnp.mean(o * o, axis=-1, keepdims=True)
            o = ((o * lax.rsqrt(ms + NORM_EPS)) * gsub_ref[...]) * (1.0 - lambda_init)
            o_ref[rows, cols] = (o * gate_ref[rows, cols]).astype(o_ref.dtype)


def _attention(body, q, q_col0, q_w, k, k_col0, k_w, vt, v_row0, gate, gate_col0, extras, extra_specs,
               out_w_total, heads, group, qsub, n_units, n_tok, name, lookahead, query_scratch, dest, v_heads=1):
    bsz, lp, _ = q.shape
    out_w = group * HEAD_DIM
    if dest is None:
        dest = jnp.zeros((bsz, lp, out_w_total), BF16)

    def call(qsub_, tqs, n_qblk, qblk0, is_meta, dest):
        tq = qsub_ * tqs
        in_specs = [
            pl.BlockSpec((None, tq, q_w), lambda b, h, i: (b, qblk0 + i, q_col0 + h)),
            pl.BlockSpec((None, lp, k_w), lambda b, h, i: (b, 0, k_col0 + h)),
            pl.BlockSpec((v_heads * HEAD_DIM, lp), lambda b, h, i: (v_row0 + h, b)),
            pl.BlockSpec((None, tq, out_w), lambda b, h, i: (b, qblk0 + i, gate_col0 + h)),
        ] + extra_specs
        in_specs.append(pl.BlockSpec(memory_space=pl.ANY))
        args = [q, k, vt, gate] + extras + [dest]
        units = n_units * qsub_
        qt = query_scratch(units, tqs)
        return pl.pallas_call(
            body(qsub_, tqs, is_meta),
            grid=(bsz, heads, n_qblk),
            in_specs=in_specs,
            out_specs=pl.BlockSpec((None, tq, out_w), lambda b, h, i: (b, qblk0 + i, h)),
            out_shape=jax.ShapeDtypeStruct((bsz, lp, out_w_total), BF16),
            scratch_shapes=[pltpu.VMEM((units, 1, tqs), F32),
                            pltpu.VMEM((units, HEAD_DIM + ONES_ROWS, tqs), F32),
                            pltpu.VMEM((lookahead, KV_CHUNK, tqs), F32),
                            pltpu.VMEM((KV_CHUNK, tqs), BF16), pltpu.VMEM((1, tqs), F32)]
                           + ([] if qt is None else [pltpu.VMEM(qt, BF16)]),
            input_output_aliases={len(args) - 1: 0},
            compiler_params=_cparams(3),
            name=name + ("_meta" if is_meta else ""),
        )(*args)

    main = call(qsub, Q_SUB, n_tok // (qsub * Q_SUB), 0, False, dest)
    return call(1, TAIL, 1, n_tok // TAIL, True, main)


def _attn_a(qk, vt_plain, gates, n_tok, dest):
    def body(qsub_, tqs, is_meta):
        return functools.partial(_attn_ab_body, group=A_GROUP, qsub=qsub_, tqs=tqs, dk=HEAD_DIM, n_tok=n_tok,
                                 pre_transpose=False, lookahead=A_LOOKAHEAD)

    return _attention(body, qk, 0, A_GROUP * HEAD_DIM, qk, A_WIDTH // HEAD_DIM, HEAD_DIM, vt_plain, 0,
                      gates, 0, [], [], A_WIDTH, A_KV_HEADS, A_GROUP, 8, A_GROUP, n_tok, "attn_gqa",
                      A_LOOKAHEAD, lambda units, tqs: None, dest)


def _attn_b(qb, kb, vt_b, gates, n_tok, dest):
    dk = 2 * LANES

    def body(qsub_, tqs, is_meta):
        return functools.partial(_attn_ab_body, group=1, qsub=qsub_, tqs=tqs, dk=dk, n_tok=n_tok,
                                 pre_transpose=True, lookahead=B_LOOKAHEAD)

    return _attention(body, qb, 0, dk, kb, 0, dk, vt_b, 0, gates, A_WIDTH // HEAD_DIM, [], [],
                      B_WIDTH, B_HEADS, 1, 16, 1, n_tok, "attn_mla", B_LOOKAHEAD,
                      lambda units, tqs: (units, dk, tqs), dest)


def _attn_c(cqk, vt_plain, gates, lamvec, slopes, g_sub, lambda_init, n_tok, dest):
    pair = 2
    dk = pair * 2 * LANES

    def body(qsub_, tqs, is_meta):
        return functools.partial(_attn_c_body, heads=pair, qsub=qsub_, tqs=tqs, n_tok=n_tok, q_is_meta=is_meta,
                                 lambda_init=lambda_init, lookahead=C_LOOKAHEAD)

    extras = [lamvec, slopes, g_sub.reshape(1, C_VDIM)]
    extra_specs = [pl.BlockSpec((4, C_QK_DIM), lambda b, h, i: (0, 0)),
                   pl.BlockSpec((pair, 1, LANES), lambda b, h, i: (h, 0, 0)),
                   pl.BlockSpec((1, C_VDIM), lambda b, h, i: (0, 0))]
    return _attention(body, cqk, 0, dk, cqk, C_HEADS // pair, dk, vt_plain, A_KV_HEADS // pair, gates,
                      (A_WIDTH + B_WIDTH) // (pair * HEAD_DIM), extras, extra_specs,
                      C_WIDTH, C_HEADS // pair, pair, 8, 2 * pair, n_tok, "attn_diff", C_LOOKAHEAD,
                      lambda units, tqs: (2, units, tqs, LANES), dest, v_heads=pair)


def _out_proj_body(a_ref, b_ref, c_ref, wa_ref, wb_ref, wc_ref, h_ref, o_ref):
    acc = jnp.dot(a_ref[...], wa_ref[...], preferred_element_type=F32)
    acc += jnp.dot(b_ref[...], wb_ref[...], preferred_element_type=F32)
    acc += jnp.dot(c_ref[...], wc_ref[...], preferred_element_type=F32)
    o_ref[...] = h_ref[...] + acc


def _out_proj(oa, ob, oc, w, h2d):
    rows = h2d.shape[0]
    tm = _tall_row_tile(rows)
    tn = 512
    return pl.pallas_call(
        _out_proj_body,
        grid=(rows // tm, D_MODEL // tn),
        in_specs=[pl.BlockSpec((tm, A_WIDTH), lambda i, j: (i, 0)),
                  pl.BlockSpec((tm, B_WIDTH), lambda i, j: (i, 0)),
                  pl.BlockSpec((tm, C_WIDTH), lambda i, j: (i, 0)),
                  pl.BlockSpec((A_WIDTH, tn), lambda i, j: (0, j)),
                  pl.BlockSpec((B_WIDTH, tn), lambda i, j: (A_WIDTH // B_WIDTH, j)),
                  pl.BlockSpec((C_WIDTH, tn), lambda i, j: ((A_WIDTH + B_WIDTH) // C_WIDTH, j)),
                  pl.BlockSpec((tm, tn), lambda i, j: (i, j))],
        out_specs=pl.BlockSpec((tm, tn), lambda i, j: (i, j)),
        out_shape=jax.ShapeDtypeStruct((rows, D_MODEL), F32),
        compiler_params=_cparams(2),
        name="out_proj",
    )(oa, ob, oc, w, w, w, h2d)


def _rope_angles(pos_f, n_freq):
    inv = ROPE_THETA ** (-jnp.arange(n_freq, dtype=F32) / n_freq)
    return pos_f[:, None] * inv[None, :]


def _position_tables(n_tok, bsz):
    rows = n_tok // GRID_W
    z = jnp.zeros((TAIL,), F32)
    row_f = jnp.concatenate([jnp.repeat(jnp.arange(rows, dtype=F32), GRID_W), z])
    col_f = jnp.concatenate([jnp.tile(jnp.arange(GRID_W, dtype=F32), rows), z])
    pos_f = jnp.concatenate([jnp.arange(n_tok, dtype=F32) + N_META,
                             jnp.arange(N_META, dtype=F32), jnp.zeros((TAIL - N_META,), F32)])
    ang_a = jnp.concatenate([_rope_angles(row_f, HEAD_DIM // 4), _rope_angles(col_f, HEAD_DIM // 4)], axis=-1)
    cos_a, sin_a = jnp.cos(ang_a), jnp.sin(ang_a)
    cos_a = jnp.concatenate([cos_a, cos_a], axis=-1)
    sin_a = jnp.concatenate([-sin_a, sin_a], axis=-1)
    ang_b = _rope_angles(pos_f, B_ROPE // 2)
    cos_b, sin_b = jnp.cos(ang_b), jnp.sin(ang_b)
    zb = jnp.zeros_like(cos_b)
    cos_b128 = jnp.concatenate([cos_b, cos_b, zb, zb], axis=-1)
    sin_lo = jnp.concatenate([-sin_b, zb, zb, zb], axis=-1)
    sin_hi = jnp.concatenate([zb, sin_b, zb, zb], axis=-1)
    hi = jnp.floor(pos_f / LANES)
    lo = pos_f - hi * LANES
    one = jnp.ones_like(pos_f)

    def slot_lanes(cols):
        body = jnp.stack(cols, axis=-1)
        return jnp.concatenate([jnp.zeros((lp, C_QK_DIM), F32), body,
                                jnp.zeros((lp, LANES - C_QK_DIM - len(cols)), F32)], axis=-1)

    lp = n_tok + TAIL
    pieces, rest = [], jnp.float32(LOG2E)
    for _ in range(3):
        piece = rest.astype(BF16).astype(F32)
        pieces.append(piece)
        rest = rest - piece
    q_terms = slot_lanes([col for c in pieces for col in (LANES * c * one, c * one, -LANES * hi, -lo)])
    k_terms = slot_lanes([col for c in pieces for col in (hi, lo, c * one, c * one)])
    return tuple(jnp.tile(t, (bsz, 1)) for t in (cos_a, sin_a, cos_b128, sin_lo, sin_hi, q_terms, k_terms))


def _prep_weights(w_in, w_q_b, w_kv_b, w_out, g_qn, g_kn):
    bounds = [0]
    for s in IN_SPLITS:
        bounds.append(bounds[-1] + s)
    col = lambda i: slice(bounds[i], bounds[i + 1])
    aq, ak, av, ag, bq, bkv, bpe, bg, cq, ck, cv, cg = (w_in[:, col(i)] for i in range(12))
    w_aqk = jnp.concatenate([aq, ak], axis=1).astype(BF16)
    w_plain = jnp.concatenate([av, cv], axis=1).astype(BF16)
    w_gate = jnp.concatenate([ag, bg, cg], axis=1).astype(BF16)
    w_cqk = jnp.concatenate([cq, ck], axis=1).astype(BF16)
    w_blow = jnp.concatenate([bq, bkv], axis=1).astype(BF16)
    w_bpe = jnp.concatenate([bpe, jnp.zeros((D_MODEL, LANES - B_ROPE), F32)], axis=1).astype(BF16)
    g_aqk = jnp.concatenate([jnp.tile(g_qn, A_HEADS), jnp.tile(g_kn, A_KV_HEADS)]).reshape(1, -1)
    m_aqk = jnp.concatenate([jnp.full((A_WIDTH,), HEAD_DIM ** -0.5 * LOG2E, F32),
                             jnp.ones((A_KV_WIDTH,), F32)]).reshape(1, -1)
    wq = w_q_b.astype(BF16).reshape(B_Q_RANK, B_HEADS, B_NOPE + B_ROPE)
    wq = jnp.concatenate([wq, jnp.zeros((B_Q_RANK, B_HEADS, 2 * LANES - B_NOPE - B_ROPE), BF16)], axis=-1)
    wq = wq.reshape(B_Q_RANK, B_HEADS * 2 * LANES)
    wkv = w_kv_b.astype(BF16).reshape(B_KV_RANK, B_HEADS, B_NOPE + B_VDIM)
    wkv = jnp.concatenate([wkv[:, :, :B_NOPE].reshape(B_KV_RANK, -1),
                           wkv[:, :, B_NOPE:].reshape(B_KV_RANK, -1)], axis=1)
    return dict(aqk=w_aqk, plain=w_plain, gate=w_gate, cqk=w_cqk, blow=w_blow, bpe=w_bpe,
                g_aqk=g_aqk, m_aqk=m_aqk, q_up=wq, kv_up=wkv, out=w_out.astype(BF16))


def _encode(x, meta, layers, g_final, slopes, cq_cols):
    bsz, n_tok, _ = x.shape
    lp = n_tok + TAIL
    rows = bsz * lp
    tail = jnp.concatenate([meta.astype(x.dtype), jnp.zeros((TAIL - N_META, D_MODEL), x.dtype)], axis=0)
    u, h = _embed_norm(x, tail, layers[0]["g_attn"])
    cos_a, sin_a, cos_b, sin_lo, sin_hi, q_terms, k_terms = _position_tables(n_tok, bsz)
    three = lambda a: a.reshape(bsz, lp, a.shape[-1])
    oa = ob = oc = None

    for l, p in enumerate(layers):
        lambda_init = 0.8 - 0.6 * math.exp(-0.3 * l)
        if l > 0:
            u = _rmsnorm_rows(h, p["g_attn"], BF16)
        aqk = _staggered_proj(_aqk_epilogue, u, p["aqk"], [p["g_aqk"], p["m_aqk"]], [cos_a, sin_a],
                              A_WIDTH + A_KV_WIDTH, BF16, 512, 512, True, "proj_gqa_qk")
        vt_plain = _proj(_proj_transposed_body, u, p["plain"], [], lambda tm, tn: [],
                         A_KV_WIDTH + C_WIDTH, BF16, 512, transposed=True, name="proj_v")
        gates = _proj(functools.partial(_direct_proj_body, epilogue=_gate_epilogue), u, p["gate"], [],
                      lambda tm, tn: [], A_WIDTH + B_WIDTH + C_WIDTH, F32, 512, name="proj_gate")
        cqk = _proj(functools.partial(_direct_proj_body, epilogue=_cqk_epilogue), u, p["cqk"],
                    list(cq_cols) + [q_terms, k_terms],
                    lambda tm, tn: [_col_vec_spec(tm, tn)] * 3 + [_row_table_spec(tm, tn)] * 2,
                    4 * C_QK_WIDTH, BF16, 512, out_tn=1024, name="proj_diff_qk")
        blow = _proj(_proj_plain_body, u, p["blow"], [], lambda tm, tn: [],
                     B_Q_RANK + B_KV_RANK, F32, 512, name="proj_mla_low")
        kpe = _proj(_proj_bpe_body, u, p["bpe"], [cos_b, sin_lo, sin_hi],
                    lambda tm, tn: [_row_table_spec(tm, tn)] * 3, LANES, BF16, LANES, name="proj_mla_pe")
        qb = _bq_up(blow, p["g_q_a"], p["q_up"], cos_b, sin_lo, sin_hi)
        kb, vt_b = _bkv_up(blow, p["g_kv_a"], p["kv_up"], kpe)

        gates3 = three(gates)
        oa = _attn_a(three(aqk), vt_plain, gates3, n_tok, oa)
        ob = _attn_b(three(qb), three(kb), vt_b, gates3, n_tok, ob)
        oc = _attn_c(three(cqk), vt_plain, gates3, p["lamvec"], slopes, p["g_sub"], lambda_init, n_tok, oc)
        h = _out_proj(oa.reshape(rows, -1), ob.reshape(rows, -1), oc.reshape(rows, -1), p["out"], h)

    return _final_norm(h.reshape(bsz, lp, D_MODEL), g_final, n_tok)


def kernel(x_prompt, x_sample, meta, g_attn, w_in, g_qn, g_kn, g_q_a, w_q_b, g_kv_a, w_kv_b,
           lam_q1, lam_k1, lam_q2, lam_k2, g_sub, w_out, g_final):
    layers = []
    for l in range(DEPTH):
        p = _prep_weights(w_in[l], w_q_b[l], w_kv_b[l], w_out[l], g_qn[l], g_kn[l])
        p.update(g_attn=g_attn[l], g_q_a=g_q_a[l], g_kv_a=g_kv_a[l], g_sub=g_sub[l],
                 lamvec=jnp.stack([lam_q1[l], lam_k1[l], lam_q2[l], lam_k2[l]]).astype(F32))
        layers.append(p)
    slope_h = 2.0 ** (-(jnp.arange(C_HEADS, dtype=F32) + 1.0) * 8.0 / C_HEADS)
    slopes = jnp.broadcast_to(slope_h[:, None, None], (C_HEADS, 1, LANES))
    zeros_w, ones_w = jnp.zeros((C_QK_WIDTH,), F32), jnp.ones((C_QK_WIDTH,), F32)
    cq_cols = [jnp.concatenate([C_QK_DIM ** -0.5 * LOG2E * ones_w, ones_w]).reshape(1, -1),
               jnp.concatenate([jnp.repeat(slope_h, 2 * C_QK_DIM), zeros_w]).reshape(1, -1),
               jnp.concatenate([zeros_w, ones_w]).reshape(1, -1)]
    y_prompt = _encode(x_prompt, meta, layers, g_final, slopes, cq_cols)
    y_sample = _encode(x_sample, meta, layers, g_final, slopes, cq_cols)
    return (y_prompt, y_sample)
```

```python
import functools
import math

import jax
import jax.numpy as jnp
from jax import lax
from jax.experimental import pallas as pl
from jax.experimental.pallas import tpu as pltpu

D_MODEL = 4096
DEPTH = 2
N_META = 16
GRID_W = 64
HEAD_DIM = 128
ROPE_THETA = 10000.0
NORM_EPS = 1e-6
A_HEADS = 16
A_KV_HEADS = 4
A_GROUP = A_HEADS // A_KV_HEADS
A_WIDTH = A_HEADS * HEAD_DIM
A_KV_WIDTH = A_KV_HEADS * HEAD_DIM
B_HEADS = 8
B_Q_RANK = 1024
B_KV_RANK = 512
B_NOPE = 128
B_ROPE = 64
B_VDIM = 128
B_WIDTH = B_HEADS * B_VDIM
C_HEADS = 8
C_QK_DIM = 64
C_VDIM = 2 * C_QK_DIM
C_QK_WIDTH = C_HEADS * 2 * C_QK_DIM
C_WIDTH = C_HEADS * C_VDIM
IN_SPLITS = (A_WIDTH, A_KV_WIDTH, A_KV_WIDTH, A_WIDTH,
             B_Q_RANK, B_KV_RANK, B_ROPE, B_WIDTH,
             C_QK_WIDTH, C_QK_WIDTH, C_WIDTH, C_WIDTH)

LANES = 128
TAIL = LANES
KV_CHUNK = 512
Q_SUB = 256
NEG_BIG = -1e30
LOG2E = 1.4426950408889634
A_LOOKAHEAD = 2
B_LOOKAHEAD = 2
C_LOOKAHEAD = 2
ONES_ROWS = 16
BF16 = jnp.bfloat16
F32 = jnp.float32
VMEM_LIMIT = 56 * 1024 * 1024

_NT = (((1,), (1,)), ((), ()))


def _cparams(n_axes):
    return pltpu.CompilerParams(dimension_semantics=("arbitrary",) * n_axes,
                                vmem_limit_bytes=VMEM_LIMIT)


def _row_tile(rows):
    for t in (512, 640, 256, 128):
        if rows % t == 0:
            return t
    raise ValueError(f"no row tile for {rows}")


def _tall_row_tile(rows):
    for t in (1536, 1664):
        if rows % t == 0:
            return t
    return _row_tile(rows)


def _rmsnorm_body(x_ref, g_ref, o_ref):
    x = x_ref[...]
    ms = jnp.mean(x * x, axis=-1, keepdims=True)
    o_ref[...] = ((x * lax.rsqrt(ms + NORM_EPS)) * g_ref[...]).astype(o_ref.dtype)


def _rmsnorm_rows(h2d, g, out_dtype):
    rows, d = h2d.shape
    tm = 256 if rows % 256 == 0 else 128
    return pl.pallas_call(
        _rmsnorm_body,
        grid=(rows // tm,),
        in_specs=[pl.BlockSpec((tm, d), lambda i: (i, 0)),
                  pl.BlockSpec((1, d), lambda i: (0, 0))],
        out_specs=pl.BlockSpec((tm, d), lambda i: (i, 0)),
        out_shape=jax.ShapeDtypeStruct((rows, d), out_dtype),
        compiler_params=_cparams(1),
        name="rmsnorm",
    )(h2d, g.reshape(1, d))


def _embed_norm_body(x_ref, tail_ref, g_ref, u_ref, h_ref, *, n_real):
    def emit(x):
        h_ref[...] = x
        ms = jnp.mean(x * x, axis=-1, keepdims=True)
        u_ref[...] = ((x * lax.rsqrt(ms + NORM_EPS)) * g_ref[...]).astype(u_ref.dtype)

    @pl.when(pl.program_id(1) < n_real)
    def _():
        emit(x_ref[...])

    @pl.when(pl.program_id(1) == n_real)
    def _():
        emit(tail_ref[...])


def _embed_norm(x, tail, g):
    bsz, n_tok, d = x.shape
    lp = n_tok + TAIL
    n_real = n_tok // TAIL
    block = pl.BlockSpec((None, TAIL, d), lambda b, i: (b, i, 0))
    u, h = pl.pallas_call(
        functools.partial(_embed_norm_body, n_real=n_real),
        grid=(bsz, n_real + 1),
        in_specs=[pl.BlockSpec((None, TAIL, d), lambda b, i: (b, jnp.minimum(i, n_real - 1), 0)),
                  pl.BlockSpec((TAIL, d), lambda b, i: (0, 0)),
                  pl.BlockSpec((1, d), lambda b, i: (0, 0))],
        out_specs=[block, block],
        out_shape=[jax.ShapeDtypeStruct((bsz, lp, d), BF16), jax.ShapeDtypeStruct((bsz, lp, d), x.dtype)],
        compiler_params=_cparams(2),
        name="embed_norm",
    )(x, tail, g.reshape(1, d))
    return u.reshape(bsz * lp, d), h.reshape(bsz * lp, d)


def _final_norm(h3d, g, n_tok):
    bsz, _, d = h3d.shape
    tm = 256
    return pl.pallas_call(
        _rmsnorm_body,
        grid=(bsz, n_tok // tm),
        in_specs=[pl.BlockSpec((None, tm, d), lambda b, i: (b, i, 0)),
                  pl.BlockSpec((1, d), lambda b, i: (0, 0))],
        out_specs=pl.BlockSpec((None, tm, d), lambda b, i: (b, i, 0)),
        out_shape=jax.ShapeDtypeStruct((bsz, n_tok, d), F32),
        compiler_params=_cparams(2),
        name="final_norm",
    )(h3d, g.reshape(1, d))


def _rope_pairs_64(y, cos, sin_signed):
    return y * cos + pltpu.roll(y, 64, 1) * sin_signed


def _rope_pairs_32(x, cos, sin_lo, sin_hi):
    return x * cos + pltpu.roll(x, 96, 1) * sin_lo + pltpu.roll(x, 32, 1) * sin_hi


def _staggered_proj_body(x_ref, w_ref, *rest, n_tiles, epilogue):
    *extra, o_ref, acc_ref = rest
    j = pl.program_id(1)

    def matmul():
        acc_ref[j % 2] = jnp.dot(x_ref[...], w_ref[...], preferred_element_type=F32)

    def finish_previous():
        epilogue(acc_ref.at[(j + 1) % 2], *extra, o_ref)

    @pl.when(j == 0)
    def _():
        matmul()

    @pl.when(jnp.logical_and(j > 0, j < n_tiles))
    def _():
        finish_previous()
        matmul()

    @pl.when(j == n_tiles)
    def _():
        finish_previous()


def _staggered_proj(epilogue, u, w, col_vecs, row_tables, n_out, out_dtype, tn, out_tn, tall, name):
    rows, k = u.shape
    n = w.shape[1]
    tm = _tall_row_tile(rows) if tall else _row_tile(rows)
    n_tiles = n // tn
    prev = lambda j: jnp.maximum(j - 1, 0)
    return pl.pallas_call(
        functools.partial(_staggered_proj_body, n_tiles=n_tiles, epilogue=epilogue),
        grid=(rows // tm, n_tiles + 1),
        in_specs=([pl.BlockSpec((tm, k), lambda i, j: (i, 0)),
                   pl.BlockSpec((k, tn), lambda i, j: (0, jnp.minimum(j, n_tiles - 1)))]
                  + [pl.BlockSpec((1, tn), lambda i, j: (0, prev(j)))] * len(col_vecs)
                  + [pl.BlockSpec((tm, LANES), lambda i, j: (i, 0))] * len(row_tables)),
        out_specs=pl.BlockSpec((tm, out_tn), lambda i, j: (i, prev(j))),
        out_shape=jax.ShapeDtypeStruct((rows, n_out), out_dtype),
        scratch_shapes=[pltpu.VMEM((2, tm, tn), F32)],
        compiler_params=_cparams(2),
        name=name,
    )(u, w, *col_vecs, *row_tables)


def _direct_proj_body(x_ref, w_ref, *rest, epilogue):
    *extra, o_ref = rest
    epilogue(jnp.dot(x_ref[...], w_ref[...], preferred_element_type=F32), *extra, o_ref)


def _aqk_epilogue(acc_ref, g_ref, mult_ref, cos_ref, sin_ref, o_ref):
    cos = cos_ref[...]
    sin = sin_ref[...]
    for h in range(o_ref.shape[1] // HEAD_DIM):
        sl = slice(h * HEAD_DIM, (h + 1) * HEAD_DIM)
        a = acc_ref[:, sl]
        ms = jnp.mean(a * a, axis=-1, keepdims=True)
        y = (a * lax.rsqrt(ms + NORM_EPS)) * g_ref[:, sl]
        o_ref[:, sl] = (_rope_pairs_64(y, cos, sin) * mult_ref[:, sl]).astype(o_ref.dtype)


def _gate_epilogue(acc_ref, o_ref):
    a = acc_ref[...]
    o_ref[...] = a * (1.0 / (1.0 + jnp.exp(-a)))


def _proj_plain_body(x_ref, w_ref, o_ref):
    o_ref[...] = jnp.dot(x_ref[...], w_ref[...], preferred_element_type=F32).astype(o_ref.dtype)


def _proj_transposed_body(x_ref, w_ref, o_ref):
    o_ref[...] = jnp.dot(x_ref[...], w_ref[...], preferred_element_type=F32).T.astype(o_ref.dtype)


def _cqk_epilogue(acc_ref, mult_ref, qcoef_ref, kcoef_ref, qtab_ref, ktab_ref, o_ref):
    acc = acc_ref[...] * mult_ref[...]
    lane = lax.broadcasted_iota(jnp.int32, (acc.shape[0], LANES), 1)
    low = lane < C_QK_DIM
    qtab = qtab_ref[...]
    ktab = ktab_ref[...]
    for t in range(acc.shape[1] // LANES):
        sl = slice(t * LANES, (t + 1) * LANES)
        a = acc[:, sl]
        pos_terms = qtab * qcoef_ref[:, sl] + ktab * kcoef_ref[:, sl]
        o_ref[:, (2 * t) * LANES:(2 * t + 1) * LANES] = jnp.where(low, a, pos_terms).astype(o_ref.dtype)
        o_ref[:, (2 * t + 1) * LANES:(2 * t + 2) * LANES] = (
            jnp.where(low, pltpu.roll(a, 64, 1), pos_terms).astype(o_ref.dtype))


def _proj_bpe_body(x_ref, w_ref, cos_ref, slo_ref, shi_ref, o_ref):
    acc = jnp.dot(x_ref[...], w_ref[...], preferred_element_type=F32)
    o_ref[...] = _rope_pairs_32(acc, cos_ref[...], slo_ref[...], shi_ref[...]).astype(o_ref.dtype)


def _proj(body, u, w, extras, extra_specs, n_out, out_dtype, tn, out_tn=None, transposed=False, tall=True,
          name="proj"):
    rows, k = u.shape
    n = w.shape[1]
    tm = _tall_row_tile(rows) if tall else _row_tile(rows)
    out_tn = tn if out_tn is None else out_tn
    if transposed:
        out_specs = pl.BlockSpec((out_tn, tm), lambda i, j: (j, i))
        out_shape = jax.ShapeDtypeStruct((n_out, rows), out_dtype)
    else:
        out_specs = pl.BlockSpec((tm, out_tn), lambda i, j: (i, j))
        out_shape = jax.ShapeDtypeStruct((rows, n_out), out_dtype)
    return pl.pallas_call(
        body,
        grid=(rows // tm, n // tn),
        in_specs=[pl.BlockSpec((tm, k), lambda i, j: (i, 0)),
                  pl.BlockSpec((k, tn), lambda i, j: (0, j))] + extra_specs(tm, tn),
        out_specs=out_specs,
        out_shape=out_shape,
        compiler_params=_cparams(2),
        name=name,
    )(u, w, *extras)


def _row_table_spec(tm, tn):
    return pl.BlockSpec((tm, LANES), lambda i, j: (i, 0))


def _col_vec_spec(tm, tn):
    return pl.BlockSpec((1, tn), lambda i, j: (0, j))


def _bq_up_body(c_ref, g_ref, w_ref, cos_ref, slo_ref, shi_ref, o_ref):
    c = c_ref[...]
    ms = jnp.mean(c * c, axis=-1, keepdims=True)
    u = ((c * lax.rsqrt(ms + NORM_EPS)) * g_ref[...]).astype(BF16)
    acc = jnp.dot(u, w_ref[...], preferred_element_type=F32) * ((B_NOPE + B_ROPE) ** -0.5 * LOG2E)
    cos, slo, shi = cos_ref[...], slo_ref[...], shi_ref[...]
    for h in range(B_HEADS):
        base = h * 2 * LANES
        o_ref[:, base:base + LANES] = acc[:, base:base + LANES].astype(o_ref.dtype)
        x = acc[:, base + LANES:base + 2 * LANES]
        o_ref[:, base + LANES:base + 2 * LANES] = _rope_pairs_32(x, cos, slo, shi).astype(o_ref.dtype)


def _bkv_up_body(c_ref, g_ref, w_ref, kpe_ref, k_ref, v_ref):
    c = c_ref[...]
    ms = jnp.mean(c * c, axis=-1, keepdims=True)
    u = ((c * lax.rsqrt(ms + NORM_EPS)) * g_ref[...]).astype(BF16)
    acc = jnp.dot(u, w_ref[...], preferred_element_type=F32)
    kpe = kpe_ref[...]
    for h in range(B_HEADS):
        base = h * 2 * LANES
        k_ref[:, base:base + LANES] = acc[:, h * LANES:(h + 1) * LANES].astype(k_ref.dtype)
        k_ref[:, base + LANES:base + 2 * LANES] = kpe
    v_ref[...] = acc[:, B_HEADS * LANES:].T.astype(v_ref.dtype)


def _bq_up(cb, g, w, cos, slo, shi):
    rows = cb.shape[0]
    tm = _row_tile(rows)
    n = w.shape[1]
    tab = pl.BlockSpec((tm, LANES), lambda i: (i, 0))
    return pl.pallas_call(
        _bq_up_body,
        grid=(rows // tm,),
        in_specs=[pl.BlockSpec((tm, B_Q_RANK), lambda i: (i, 0)),
                  pl.BlockSpec((1, B_Q_RANK), lambda i: (0, 0)),
                  pl.BlockSpec((B_Q_RANK, n), lambda i: (0, 0)),
                  tab, tab, tab],
        out_specs=pl.BlockSpec((tm, n), lambda i: (i, 0)),
        out_shape=jax.ShapeDtypeStruct((rows, n), BF16),
        compiler_params=_cparams(1),
        name="mla_q_up",
    )(cb, g.reshape(1, -1), w, cos, slo, shi)


def _bkv_up(cb, g, w, kpe):
    rows = cb.shape[0]
    tm = _row_tile(rows)
    n = w.shape[1]
    return pl.pallas_call(
        _bkv_up_body,
        grid=(rows // tm,),
        in_specs=[pl.BlockSpec((tm, B_KV_RANK), lambda i: (i, B_Q_RANK // B_KV_RANK)),
                  pl.BlockSpec((1, B_KV_RANK), lambda i: (0, 0)),
                  pl.BlockSpec((B_KV_RANK, n), lambda i: (0, 0)),
                  pl.BlockSpec((tm, LANES), lambda i: (i, 0))],
        out_specs=[pl.BlockSpec((tm, B_HEADS * 2 * LANES), lambda i: (i, 0)),
                   pl.BlockSpec((B_HEADS * B_VDIM, tm), lambda i: (0, i))],
        out_shape=[jax.ShapeDtypeStruct((rows, B_HEADS * 2 * LANES), BF16),
                   jax.ShapeDtypeStruct((B_HEADS * B_VDIM, rows), BF16)],
        compiler_params=_cparams(1),
        name="mla_kv_up",
    )(cb, g.reshape(1, -1), w, kpe)


def _online_softmax_sweep(k_ref, vt_ref, m_ref, acc_ref, s_ref, p_ref, a_ref, n_units, n_tok, scores, make_adjust,
                          lookahead, phases=None, first_variant=0, tail_variant=0, tail_fix=False,
                          value_head=lambda idx: 0):
    look = min(lookahead, n_units)
    n_chunks = n_tok // KV_CHUNK
    if phases is None:
        phases = [(0, n_chunks, 0, 0, False)]

    def probabilities(s, idx):
        m_prev = m_ref[idx]
        m_new = jnp.maximum(m_prev, jnp.max(s, axis=0, keepdims=True))
        m_ref[idx] = m_new
        return jnp.exp2(s - m_new).astype(BF16), jnp.exp2(m_prev - m_new)

    def accumulate(idx, p, alpha, vt_ones):
        acc_ref[idx] = alpha * acc_ref[idx] + jnp.dot(vt_ones, p, preferred_element_type=F32)

    def values_of(head, cols):
        return _with_ones_rows(vt_ref[head * HEAD_DIM:(head + 1) * HEAD_DIM, cols])

    def chunk(s_first, kc, cols, k_off, kc_after, behind, variant, variant_after, fix):
        pending = list(s_first)
        vt_ones = {}
        adjust = (lambda s, idx: s) if make_adjust is None else make_adjust(k_off, kc.shape[0], fix)
        for idx in range(n_units):
            s = pending.pop(0)
            ahead = idx + look
            if ahead < n_units:
                pending.append(scores(kc, ahead, _per_unit(variant, ahead)))
            elif kc_after is not None:
                s_ref[ahead - n_units] = scores(kc_after, ahead - n_units, _per_unit(variant_after, ahead - n_units))
            p, alpha = probabilities(adjust(s, idx), idx)
            if behind is not None:
                accumulate(*behind)
            head = value_head(idx)
            if head not in vt_ones:
                vt_ones[head] = values_of(head, cols)
            missing = vt_ones[head].shape[1] - p.shape[0]
            if missing:
                p = jnp.concatenate([p, jnp.zeros((missing, p.shape[1]), p.dtype)], axis=0)
            behind = (idx, p, alpha, vt_ones[head])
        return behind

    def chunk_step(c, carry, variant, variant_after, fix):
        off = pl.multiple_of(c * KV_CHUNK, KV_CHUNK)
        off_before = pl.multiple_of(jnp.maximum(c - 1, 0) * KV_CHUNK, KV_CHUNK)
        off_after = pl.multiple_of(jnp.minimum(c + 1, n_chunks - 1) * KV_CHUNK, KV_CHUNK)
        behind = (n_units - 1, p_ref[...], a_ref[...],
                  values_of(value_head(n_units - 1), pl.ds(off_before, KV_CHUNK)))
        _, p, alpha, _ = chunk([s_ref[j] for j in range(look)], k_ref[pl.ds(off, KV_CHUNK), :],
                               pl.ds(off, KV_CHUNK), N_META + off,
                               k_ref[pl.ds(off_after, KV_CHUNK), :], behind, variant, variant_after, fix)
        p_ref[...] = p
        a_ref[...] = alpha
        return carry

    m_ref[...] = jnp.full(m_ref.shape, NEG_BIG, F32)
    acc_ref[...] = jnp.zeros(acc_ref.shape, F32)
    p_ref[...] = jnp.zeros(p_ref.shape, BF16)
    a_ref[...] = jnp.ones(a_ref.shape, F32)
    for j in range(look):
        s_ref[j] = scores(k_ref[0:KV_CHUNK, :], j, _per_unit(first_variant, j))
    for lo, hi, variant, variant_after, fix in phases:
        lax.fori_loop(lo, hi, functools.partial(chunk_step, variant=variant, variant_after=variant_after, fix=fix), 0)
    accumulate(n_units - 1, p_ref[...], a_ref[...],
               values_of(value_head(n_units - 1), slice(n_tok - KV_CHUNK, n_tok)))
    k_tail = k_ref[n_tok:n_tok + N_META, :]
    accumulate(*chunk([scores(k_tail, j, _per_unit(tail_variant, j)) for j in range(look)], k_tail,
                      slice(n_tok, n_tok + TAIL),
                      0, None, None, tail_variant, tail_variant, tail_fix))


def _per_unit(choice, idx):
    return choice(idx) if callable(choice) else choice


def _with_ones_rows(vt):
    return jnp.concatenate([vt, jnp.ones((ONES_ROWS, vt.shape[1]), vt.dtype)], axis=0)


def _normalized(acc):
    return acc[:HEAD_DIM] * (1.0 / acc[HEAD_DIM:HEAD_DIM + 1])


def _attn_ab_body(q_ref, k_ref, vt_ref, gate_ref, *rest, group, qsub, tqs, dk, n_tok, pre_transpose, lookahead):
    units = [(g, u) for g in range(group) for u in range(qsub)]
    if pre_transpose:
        o_ref, m_ref, acc_ref, s_ref, p_ref, a_ref, qt_ref = rest[-7:]
        for idx, (g, u) in enumerate(units):
            qt_ref[idx] = q_ref[u * tqs:(u + 1) * tqs, g * dk:(g + 1) * dk].astype(F32).T.astype(BF16)
    else:
        o_ref, m_ref, acc_ref, s_ref, p_ref, a_ref = rest[-6:]

    def scores(kc, idx, variant):
        if pre_transpose:
            return jnp.dot(kc, qt_ref[idx], preferred_element_type=F32)
        g, u = units[idx]
        return lax.dot_general(kc, q_ref[u * tqs:(u + 1) * tqs, g * dk:(g + 1) * dk], _NT,
                               preferred_element_type=F32)

    _online_softmax_sweep(k_ref, vt_ref, m_ref, acc_ref, s_ref, p_ref, a_ref, len(units), n_tok, scores, None,
                          lookahead)

    for idx, (g, u) in enumerate(units):
        rows = slice(u * tqs, (u + 1) * tqs)
        cols = slice(g * HEAD_DIM, (g + 1) * HEAD_DIM)
        o = _normalized(acc_ref[idx]).T
        o_ref[rows, cols] = (o * gate_ref[rows, cols]).astype(o_ref.dtype)


def _attn_gqa_meta_body(q_ref, k_ref, vt_ref, gate_ref, dest_ref, o_ref, m_ref, acc_ref, s_ref, p_ref, a_ref, qs_ref,
                        *, group, n_tok, lookahead):
    del dest_ref
    qs_ref[...] = jnp.zeros(qs_ref.shape, qs_ref.dtype)
    for g in range(group):
        qs_ref[g * N_META:(g + 1) * N_META, :] = q_ref[0:N_META, g * HEAD_DIM:(g + 1) * HEAD_DIM]

    def scores(kc, idx, variant):
        return lax.dot_general(kc, qs_ref[...], _NT, preferred_element_type=F32)

    _online_softmax_sweep(k_ref, vt_ref, m_ref, acc_ref, s_ref, p_ref, a_ref, 1, n_tok, scores, None, lookahead)

    o = _normalized(acc_ref[0]).T
    o_ref[...] = jnp.zeros(o_ref.shape, o_ref.dtype)
    for g in range(group):
        cols = slice(g * HEAD_DIM, (g + 1) * HEAD_DIM)
        o_ref[0:N_META, cols] = (o[g * N_META:(g + 1) * N_META] * gate_ref[0:N_META, cols]).astype(o_ref.dtype)


def _attn_c_body(q_ref, k_ref, vt_ref, gate_ref, lam_ref, slope_ref, gsub_ref, *rest,
                 heads, qsub, tqs, n_tok, q_is_meta, lambda_init, lookahead):
    o_ref, m_ref, acc_ref, s_ref, p_ref, a_ref, q2_ref = rest[-7:]
    slot_w = 2 * LANES
    units = [(hd, u, c) for hd in range(heads) for u in range(qsub) for c in range(2)]
    lane = lax.broadcasted_iota(jnp.int32, (1, LANES), 1)
    flip = jnp.where(lane < C_QK_DIM, 1.0, -1.0)
    for idx, (hd, u, c) in enumerate(units):
        col = hd * slot_w + c * LANES
        q = q_ref[u * tqs:(u + 1) * tqs, col:col + LANES]
        q2_ref[0, idx] = q
        q2_ref[1, idx] = (q.astype(F32) * flip).astype(BF16)
    slope2 = [(2.0 * LOG2E) * slope_ref[hd, :, :1] for hd in range(heads)]
    tq = qsub * tqs
    n_chunks = n_tok // KV_CHUNK
    if q_is_meta:
        q_off = 0
        phases = [(0, n_chunks, 1, 1, False)]
        first_variant, tail_fix = 1, True
    else:
        assert tq % KV_CHUNK == 0 and KV_CHUNK % tqs == 0
        span = tq // KV_CHUNK
        per_chunk = KV_CHUNK // tqs
        i = pl.program_id(2)
        q_off = N_META + i * tq
        d = i * span

        def variant_in(j):
            return lambda idx: 1 if units[idx][1] < j * per_chunk else 0

        def fix_in(j):
            return lambda idx: j * per_chunk <= units[idx][1] < (j + 1) * per_chunk

        phases = ([(0, d, 0, 0, False)]
                  + [(d + j, d + j + 1, variant_in(j), variant_in(j + 1) if j + 1 < span else 1, fix_in(j))
                     for j in range(span)]
                  + [(d + span, n_chunks, 1, 1, False)])
        first_variant, tail_fix = 0, False

    def scores(kc, idx, variant):
        hd, _, c = units[idx]
        col = hd * slot_w + c * LANES
        return lax.dot_general(kc[:, col:col + LANES], q2_ref[variant, idx], _NT, preferred_element_type=F32)

    def make_adjust(k_off, n_keys, fix):
        key = lax.broadcasted_iota(jnp.int32, (n_keys, tqs), 0)
        qry = lax.broadcasted_iota(jnp.int32, (n_keys, tqs), 1)
        ahead, late = {}, {}

        def adjust(s, idx):
            hd, u, _ = units[idx]
            if _per_unit(fix, idx):
                if u not in ahead:
                    ahead[u] = jnp.maximum(((key - qry) + (k_off - q_off - u * tqs)).astype(F32), 0.0)
                if (hd, u) not in late:
                    late[hd, u] = slope2[hd] * ahead[u]
                s = s - late[hd, u]
            return s
        return adjust

    _online_softmax_sweep(k_ref, vt_ref, m_ref, acc_ref, s_ref, p_ref, a_ref, len(units), n_tok, scores, make_adjust,
                          lookahead, phases=phases, first_variant=first_variant, tail_variant=0, tail_fix=tail_fix,
                          value_head=lambda idx: units[idx][0])

    lv = lam_ref[...]
    lam = (jnp.exp(jnp.sum(lv[0:1] * lv[1:2], axis=1, keepdims=True))
           - jnp.exp(jnp.sum(lv[2:3] * lv[3:4], axis=1, keepdims=True)) + lambda_init)
    for hd in range(heads):
        cols = slice(hd * C_VDIM, (hd + 1) * C_VDIM)
        for u in range(qsub):
            rows = slice(u * tqs, (u + 1) * tqs)
            first = 2 * (hd * qsub + u)
            o = (_normalized(acc_ref[first]) - lam * _normalized(acc_ref[first + 1])).T
            ms = jnp.mean(o * o, axis=-1, keepdims=True)
            o = ((o * lax.rsqrt(ms + NORM_EPS)) * gsub_ref[...]) * (1.0 - lambda_init)
            o_ref[rows, cols] = (o * gate_ref[rows, cols]).astype(o_ref.dtype)


def _attention(body, q, q_col0, q_w, k, k_col0, k_w, vt, v_row0, gate, gate_col0, extras, extra_specs,
               out_w_total, heads, group, qsub, n_units, n_tok, name, lookahead, query_scratch, dest, v_heads=1,
               meta_units=None):
    bsz, lp, _ = q.shape
    out_w = group * HEAD_DIM
    if dest is None:
        dest = jnp.zeros((bsz, lp, out_w_total), BF16)

    def call(qsub_, tqs, n_qblk, qblk0, is_meta, dest):
        tq = qsub_ * tqs
        in_specs = [
            pl.BlockSpec((None, tq, q_w), lambda b, h, i: (b, qblk0 + i, q_col0 + h)),
            pl.BlockSpec((None, lp, k_w), lambda b, h, i: (b, 0, k_col0 + h)),
            pl.BlockSpec((v_heads * HEAD_DIM, lp), lambda b, h, i: (v_row0 + h, b)),
            pl.BlockSpec((None, tq, out_w), lambda b, h, i: (b, qblk0 + i, gate_col0 + h)),
        ] + extra_specs
        in_specs.append(pl.BlockSpec(memory_space=pl.ANY))
        args = [q, k, vt, gate] + extras + [dest]
        units = meta_units if (is_meta and meta_units) else n_units * qsub_
        qt = query_scratch(units, tqs)
        return pl.pallas_call(
            body(qsub_, tqs, is_meta),
            grid=(bsz, heads, n_qblk),
            in_specs=in_specs,
            out_specs=pl.BlockSpec((None, tq, out_w), lambda b, h, i: (b, qblk0 + i, h)),
            out_shape=jax.ShapeDtypeStruct((bsz, lp, out_w_total), BF16),
            scratch_shapes=[pltpu.VMEM((units, 1, tqs), F32),
                            pltpu.VMEM((units, HEAD_DIM + ONES_ROWS, tqs), F32),
                            pltpu.VMEM((lookahead, KV_CHUNK, tqs), F32),
                            pltpu.VMEM((KV_CHUNK, tqs), BF16), pltpu.VMEM((1, tqs), F32)]
                           + ([] if qt is None else [pltpu.VMEM(qt, BF16)]),
            input_output_aliases={len(args) - 1: 0},
            compiler_params=_cparams(3),
            name=name + ("_meta" if is_meta else ""),
        )(*args)

    main = call(qsub, Q_SUB, n_tok // (qsub * Q_SUB), 0, False, dest)
    return call(1, TAIL, 1, n_tok // TAIL, True, main)


def _attn_a(qk, vt_plain, gates, n_tok, dest):
    def body(qsub_, tqs, is_meta):
        if is_meta:
            return functools.partial(_attn_gqa_meta_body, group=A_GROUP, n_tok=n_tok, lookahead=A_LOOKAHEAD)
        return functools.partial(_attn_ab_body, group=A_GROUP, qsub=qsub_, tqs=tqs, dk=HEAD_DIM, n_tok=n_tok,
                                 pre_transpose=False, lookahead=A_LOOKAHEAD)

    return _attention(body, qk, 0, A_GROUP * HEAD_DIM, qk, A_WIDTH // HEAD_DIM, HEAD_DIM, vt_plain, 0,
                      gates, 0, [], [], A_WIDTH, A_KV_HEADS, A_GROUP, 8, A_GROUP, n_tok, "attn_gqa",
                      A_LOOKAHEAD, lambda units, tqs: (TAIL, HEAD_DIM) if units == 1 else None, dest, meta_units=1)


def _attn_b(qb, kb, vt_b, gates, n_tok, dest):
    dk = 2 * LANES

    def body(qsub_, tqs, is_meta):
        return functools.partial(_attn_ab_body, group=1, qsub=qsub_, tqs=tqs, dk=dk, n_tok=n_tok,
                                 pre_transpose=True, lookahead=B_LOOKAHEAD)

    return _attention(body, qb, 0, dk, kb, 0, dk, vt_b, 0, gates, A_WIDTH // HEAD_DIM, [], [],
                      B_WIDTH, B_HEADS, 1, 16, 1, n_tok, "attn_mla", B_LOOKAHEAD,
                      lambda units, tqs: (units, dk, tqs), dest)


def _attn_c(cqk, vt_plain, gates, lamvec, slopes, g_sub, lambda_init, n_tok, dest):
    pair = 2
    dk = pair * 2 * LANES

    def body(qsub_, tqs, is_meta):
        return functools.partial(_attn_c_body, heads=pair, qsub=qsub_, tqs=tqs, n_tok=n_tok, q_is_meta=is_meta,
                                 lambda_init=lambda_init, lookahead=C_LOOKAHEAD)

    extras = [lamvec, slopes, g_sub.reshape(1, C_VDIM)]
    extra_specs = [pl.BlockSpec((4, C_QK_DIM), lambda b, h, i: (0, 0)),
                   pl.BlockSpec((pair, 1, LANES), lambda b, h, i: (h, 0, 0)),
                   pl.BlockSpec((1, C_VDIM), lambda b, h, i: (0, 0))]
    return _attention(body, cqk, 0, dk, cqk, C_HEADS // pair, dk, vt_plain, A_KV_HEADS // pair, gates,
                      (A_WIDTH + B_WIDTH) // (pair * HEAD_DIM), extras, extra_specs,
                      C_WIDTH, C_HEADS // pair, pair, 8, 2 * pair, n_tok, "attn_diff", C_LOOKAHEAD,
                      lambda units, tqs: (2, units, tqs, LANES), dest, v_heads=pair)


def _out_proj_body(a_ref, b_ref, c_ref, wa_ref, wb_ref, wc_ref, h_ref, o_ref):
    acc = jnp.dot(a_ref[...], wa_ref[...], preferred_element_type=F32)
    acc += jnp.dot(b_ref[...], wb_ref[...], preferred_element_type=F32)
    acc += jnp.dot(c_ref[...], wc_ref[...], preferred_element_type=F32)
    o_ref[...] = h_ref[...] + acc


def _out_proj(oa, ob, oc, w, h2d):
    rows = h2d.shape[0]
    tm = _tall_row_tile(rows)
    tn = 512
    return pl.pallas_call(
        _out_proj_body,
        grid=(rows // tm, D_MODEL // tn),
        in_specs=[pl.BlockSpec((tm, A_WIDTH), lambda i, j: (i, 0)),
                  pl.BlockSpec((tm, B_WIDTH), lambda i, j: (i, 0)),
                  pl.BlockSpec((tm, C_WIDTH), lambda i, j: (i, 0)),
                  pl.BlockSpec((A_WIDTH, tn), lambda i, j: (0, j)),
                  pl.BlockSpec((B_WIDTH, tn), lambda i, j: (A_WIDTH // B_WIDTH, j)),
                  pl.BlockSpec((C_WIDTH, tn), lambda i, j: ((A_WIDTH + B_WIDTH) // C_WIDTH, j)),
                  pl.BlockSpec((tm, tn), lambda i, j: (i, j))],
        out_specs=pl.BlockSpec((tm, tn), lambda i, j: (i, j)),
        out_shape=jax.ShapeDtypeStruct((rows, D_MODEL), F32),
        compiler_params=_cparams(2),
        name="out_proj",
    )(oa, ob, oc, w, w, w, h2d)


def _rope_angles(pos_f, n_freq):
    inv = ROPE_THETA ** (-jnp.arange(n_freq, dtype=F32) / n_freq)
    return pos_f[:, None] * inv[None, :]


def _position_tables(n_tok, bsz):
    rows = n_tok // GRID_W
    z = jnp.zeros((TAIL,), F32)
    row_f = jnp.concatenate([jnp.repeat(jnp.arange(rows, dtype=F32), GRID_W), z])
    col_f = jnp.concatenate([jnp.tile(jnp.arange(GRID_W, dtype=F32), rows), z])
    pos_f = jnp.concatenate([jnp.arange(n_tok, dtype=F32) + N_META,
                             jnp.arange(N_META, dtype=F32), jnp.zeros((TAIL - N_META,), F32)])
    ang_a = jnp.concatenate([_rope_angles(row_f, HEAD_DIM // 4), _rope_angles(col_f, HEAD_DIM // 4)], axis=-1)
    cos_a, sin_a = jnp.cos(ang_a), jnp.sin(ang_a)
    cos_a = jnp.concatenate([cos_a, cos_a], axis=-1)
    sin_a = jnp.concatenate([-sin_a, sin_a], axis=-1)
    ang_b = _rope_angles(pos_f, B_ROPE // 2)
    cos_b, sin_b = jnp.cos(ang_b), jnp.sin(ang_b)
    zb = jnp.zeros_like(cos_b)
    cos_b128 = jnp.concatenate([cos_b, cos_b, zb, zb], axis=-1)
    sin_lo = jnp.concatenate([-sin_b, zb, zb, zb], axis=-1)
    sin_hi = jnp.concatenate([zb, sin_b, zb, zb], axis=-1)
    hi = jnp.floor(pos_f / LANES)
    lo = pos_f - hi * LANES
    one = jnp.ones_like(pos_f)

    def slot_lanes(cols):
        body = jnp.stack(cols, axis=-1)
        return jnp.concatenate([jnp.zeros((lp, C_QK_DIM), F32), body,
                                jnp.zeros((lp, LANES - C_QK_DIM - len(cols)), F32)], axis=-1)

    lp = n_tok + TAIL
    pieces, rest = [], jnp.float32(LOG2E)
    for _ in range(3):
        piece = rest.astype(BF16).astype(F32)
        pieces.append(piece)
        rest = rest - piece
    q_terms = slot_lanes([col for c in pieces for col in (LANES * c * one, c * one, -LANES * hi, -lo)])
    k_terms = slot_lanes([col for c in pieces for col in (hi, lo, c * one, c * one)])
    return tuple(jnp.tile(t, (bsz, 1)) for t in (cos_a, sin_a, cos_b128, sin_lo, sin_hi, q_terms, k_terms))


def _prep_weights(w_in, w_q_b, w_kv_b, w_out, g_qn, g_kn):
    bounds = [0]
    for s in IN_SPLITS:
        bounds.append(bounds[-1] + s)
    col = lambda i: slice(bounds[i], bounds[i + 1])
    aq, ak, av, ag, bq, bkv, bpe, bg, cq, ck, cv, cg = (w_in[:, col(i)] for i in range(12))
    w_aqk = jnp.concatenate([aq, ak], axis=1).astype(BF16)
    w_plain = jnp.concatenate([av, cv], axis=1).astype(BF16)
    w_gate = jnp.concatenate([ag, bg, cg], axis=1).astype(BF16)
    w_cqk = jnp.concatenate([cq, ck], axis=1).astype(BF16)
    w_blow = jnp.concatenate([bq, bkv], axis=1).astype(BF16)
    w_bpe = jnp.concatenate([bpe, jnp.zeros((D_MODEL, LANES - B_ROPE), F32)], axis=1).astype(BF16)
    g_aqk = jnp.concatenate([jnp.tile(g_qn, A_HEADS), jnp.tile(g_kn, A_KV_HEADS)]).reshape(1, -1)
    m_aqk = jnp.concatenate([jnp.full((A_WIDTH,), HEAD_DIM ** -0.5 * LOG2E, F32),
                             jnp.ones((A_KV_WIDTH,), F32)]).reshape(1, -1)
    wq = w_q_b.astype(BF16).reshape(B_Q_RANK, B_HEADS, B_NOPE + B_ROPE)
    wq = jnp.concatenate([wq, jnp.zeros((B_Q_RANK, B_HEADS, 2 * LANES - B_NOPE - B_ROPE), BF16)], axis=-1)
    wq = wq.reshape(B_Q_RANK, B_HEADS * 2 * LANES)
    wkv = w_kv_b.astype(BF16).reshape(B_KV_RANK, B_HEADS, B_NOPE + B_VDIM)
    wkv = jnp.concatenate([wkv[:, :, :B_NOPE].reshape(B_KV_RANK, -1),
                           wkv[:, :, B_NOPE:].reshape(B_KV_RANK, -1)], axis=1)
    return dict(aqk=w_aqk, plain=w_plain, gate=w_gate, cqk=w_cqk, blow=w_blow, bpe=w_bpe,
                g_aqk=g_aqk, m_aqk=m_aqk, q_up=wq, kv_up=wkv, out=w_out.astype(BF16))


def _encode(x, meta, layers, g_final, slopes, cq_cols):
    bsz, n_tok, _ = x.shape
    lp = n_tok + TAIL
    rows = bsz * lp
    tail = jnp.concatenate([meta.astype(x.dtype), jnp.zeros((TAIL - N_META, D_MODEL), x.dtype)], axis=0)
    u, h = _embed_norm(x, tail, layers[0]["g_attn"])
    cos_a, sin_a, cos_b, sin_lo, sin_hi, q_terms, k_terms = _position_tables(n_tok, bsz)
    three = lambda a: a.reshape(bsz, lp, a.shape[-1])
    oa = ob = oc = None

    for l, p in enumerate(layers):
        lambda_init = 0.8 - 0.6 * math.exp(-0.3 * l)
        if l > 0:
            u = _rmsnorm_rows(h, p["g_attn"], BF16)
        aqk = _staggered_proj(_aqk_epilogue, u, p["aqk"], [p["g_aqk"], p["m_aqk"]], [cos_a, sin_a],
                              A_WIDTH + A_KV_WIDTH, BF16, 512, 512, True, "proj_gqa_qk")
        vt_plain = _proj(_proj_transposed_body, u, p["plain"], [], lambda tm, tn: [],
                         A_KV_WIDTH + C_WIDTH, BF16, 512, transposed=True, name="proj_v")
        gates = _proj(functools.partial(_direct_proj_body, epilogue=_gate_epilogue), u, p["gate"], [],
                      lambda tm, tn: [], A_WIDTH + B_WIDTH + C_WIDTH, F32, 512, name="proj_gate")
        cqk = _proj(functools.partial(_direct_proj_body, epilogue=_cqk_epilogue), u, p["cqk"],
                    list(cq_cols) + [q_terms, k_terms],
                    lambda tm, tn: [_col_vec_spec(tm, tn)] * 3 + [_row_table_spec(tm, tn)] * 2,
                    4 * C_QK_WIDTH, BF16, 512, out_tn=1024, name="proj_diff_qk")
        blow = _proj(_proj_plain_body, u, p["blow"], [], lambda tm, tn: [],
                     B_Q_RANK + B_KV_RANK, F32, 512, name="proj_mla_low")
        kpe = _proj(_proj_bpe_body, u, p["bpe"], [cos_b, sin_lo, sin_hi],
                    lambda tm, tn: [_row_table_spec(tm, tn)] * 3, LANES, BF16, LANES, name="proj_mla_pe")
        qb = _bq_up(blow, p["g_q_a"], p["q_up"], cos_b, sin_lo, sin_hi)
        kb, vt_b = _bkv_up(blow, p["g_kv_a"], p["kv_up"], kpe)

        gates3 = three(gates)
        oa = _attn_a(three(aqk), vt_plain, gates3, n_tok, oa)
        ob = _attn_b(three(qb), three(kb), vt_b, gates3, n_tok, ob)
        oc = _attn_c(three(cqk), vt_plain, gates3, p["lamvec"], slopes, p["g_sub"], lambda_init, n_tok, oc)
        h = _out_proj(oa.reshape(rows, -1), ob.reshape(rows, -1), oc.reshape(rows, -1), p["out"], h)

    return _final_norm(h.reshape(bsz, lp, D_MODEL), g_final, n_tok)


def kernel(x_prompt, x_sample, meta, g_attn, w_in, g_qn, g_kn, g_q_a, w_q_b, g_kv_a, w_kv_b,
           lam_q1, lam_k1, lam_q2, lam_k2, g_sub, w_out, g_final):
    layers = []
    for l in range(DEPTH):
        p = _prep_weights(w_in[l], w_q_b[l], w_kv_b[l], w_out[l], g_qn[l], g_kn[l])
        p.update(g_attn=g_attn[l], g_q_a=g_q_a[l], g_kv_a=g_kv_a[l], g_sub=g_sub[l],
                 lamvec=jnp.stack([lam_q1[l], lam_k1[l], lam_q2[l], lam_k2[l]]).astype(F32))
        layers.append(p)
    slope_h = 2.0 ** (-(jnp.arange(C_HEADS, dtype=F32) + 1.0) * 8.0 / C_HEADS)
    slopes = jnp.broadcast_to(slope_h[:, None, None], (C_HEADS, 1, LANES))
    zeros_w, ones_w = jnp.zeros((C_QK_WIDTH,), F32), jnp.ones((C_QK_WIDTH,), F32)
    cq_cols = [jnp.concatenate([C_QK_DIM ** -0.5 * LOG2E * ones_w, ones_w]).reshape(1, -1),
               jnp.concatenate([jnp.repeat(slope_h, 2 * C_QK_DIM), zeros_w]).reshape(1, -1),
               jnp.concatenate([zeros_w, ones_w]).reshape(1, -1)]
    y_prompt = _encode(x_prompt, meta, layers, g_final, slopes, cq_cols)
    y_sample = _encode(x_sample, meta, layers, g_final, slopes, cq_cols)
    return (y_prompt, y_sample)
```
